```python
import jax, jax.numpy as jnp
from jax import lax
import numpy as np


D_MODEL = 1024
BATCH = 4
SEQ = 4096
DEPTH = 2
DEC_BATCH = 32
DEC_SEQ = 8
PAST_LEN = 8192
PAGE_SIZE = 128

HEAD_DIM = 64
A_HEADS = D_MODEL // (2 * HEAD_DIM)
B_HEADS = D_MODEL // (2 * HEAD_DIM)
A_WIDTH = A_HEADS * HEAD_DIM
B_WIDTH = B_HEADS * HEAD_DIM
MOBA_BLOCK = 256
MOBA_TOPK = 3
MOBA_QBLOCK = 64
ROPE_THETA = 10000.0
GLA_LOWRANK = 16
GLA_GATE_NORM = 16.0
GLA_CHUNK = 64
AB_SPLITS = [A_WIDTH, 2 * A_WIDTH, 3 * A_WIDTH, 3 * A_WIDTH + B_WIDTH, 3 * A_WIDTH + 2 * B_WIDTH,
             3 * A_WIDTH + 3 * B_WIDTH, 3 * A_WIDTH + 4 * B_WIDTH]
AB_IN = 3 * A_WIDTH + 4 * B_WIDTH + GLA_LOWRANK
D_FF = ((8 * D_MODEL // 3 + 255) // 256) * 256
N_EXPERTS = 8
MOE_TOP_K = 2
D_EXPERT = 7 * D_MODEL // 2
CM_CHUNK = 128
CM_WIDTH = 2 * D_MODEL
CM_GROUPS = 8
CM_GROUP_DIM = CM_WIDTH // CM_GROUPS
N_AB = (DEPTH + 1) // 2
N_C = DEPTH // 2
EPS = 1e-6

kernel_name = 'hybrid_moba_gla_chunkmlp_moe_step'


def _rmsnorm(x, g):
    xf = x.astype(jnp.float32)
    y = xf * lax.rsqrt(jnp.mean(xf * xf, axis=-1, keepdims=True) + EPS)
    return (y * g.astype(jnp.float32)).astype(x.dtype)


def _layernorm(x, g, b):
    xf = x.astype(jnp.float32)
    mu = jnp.mean(xf, axis=-1, keepdims=True)
    var = jnp.mean(jnp.square(xf - mu), axis=-1, keepdims=True)
    return ((xf - mu) * lax.rsqrt(var + EPS) * g.astype(jnp.float32) + b.astype(jnp.float32)).astype(x.dtype)


def _modulate(h, shift, scale):
    return h * (1 + scale) + shift


def _rope(x, pos):
    half = HEAD_DIM // 2
    inv = 1.0 / (ROPE_THETA ** (jnp.arange(half, dtype=jnp.float32) / half))
    ang = pos.astype(jnp.float32)[:, None] * inv[None, :]
    cos, sin = jnp.cos(ang), jnp.sin(ang)
    xf = x.astype(jnp.float32)
    x1, x2 = xf[..., :half], xf[..., half:]
    return jnp.concatenate([x1 * cos - x2 * sin, x2 * cos + x1 * sin], axis=-1).astype(x.dtype)


def _gather_blocks(blocks, ids):
    return jax.vmap(jax.vmap(lambda b_, i_: b_[i_]))(blocks, ids)


def _moba_attention(q, k, v, q_pos0):
    B, H, Lq, Dh = q.shape
    Lk = k.shape[2]
    n_blk = -(-Lk // MOBA_BLOCK)
    pad = n_blk * MOBA_BLOCK - Lk
    kb = jnp.pad(k, ((0, 0), (0, 0), (0, pad), (0, 0))).reshape(B, H, n_blk, MOBA_BLOCK, Dh)
    vb = jnp.pad(v, ((0, 0), (0, 0), (0, pad), (0, 0))).reshape(B, H, n_blk, MOBA_BLOCK, Dh)
    k_mean = jnp.mean(kb.astype(jnp.float32), axis=3)
    n_sel = min(MOBA_TOPK, n_blk)
    qb = MOBA_QBLOCK if Lq % MOBA_QBLOCK == 0 else Lq
    n_qb = Lq // qb
    q_blocks = q.reshape(B, H, n_qb, qb, Dh).transpose(2, 0, 1, 3, 4)
    t_blocks = (q_pos0 + jnp.arange(Lq, dtype=jnp.int32)).reshape(n_qb, qb)
    scale = HEAD_DIM ** -0.5
    blk_ids = jnp.arange(n_blk, dtype=jnp.int32)
    sel_rank = jnp.arange(n_sel, dtype=jnp.int32)
    in_blk = jnp.arange(MOBA_BLOCK, dtype=jnp.int32)

    def attend(args):
        qi, ti = args
        own = ti // MOBA_BLOCK
        gate = jnp.einsum('bhqd,bhnd->bhqn', qi.astype(jnp.float32), k_mean)
        gate = jnp.where(blk_ids[None, :] < own[:, None], gate, -jnp.inf)
        _, top_idx = lax.top_k(gate, n_sel)
        own_idx = jnp.broadcast_to(own[None, None, :, None], (B, H, qb, 1)).astype(top_idx.dtype)
        idx = jnp.concatenate([top_idx, own_idx], axis=-1)
        ok = jnp.concatenate([sel_rank[None, :] < own[:, None], jnp.ones((qb, 1), dtype=bool)], axis=-1)
        k_sel = _gather_blocks(kb, idx)
        v_sel = _gather_blocks(vb, idx)
        key_pos = idx[..., None] * MOBA_BLOCK + in_blk
        mask = ok[None, None, :, :, None] & (key_pos <= ti[None, None, :, None, None])
        s = jnp.einsum('bhqd,bhqnjd->bhqnj', qi, k_sel).astype(jnp.float32) * scale
        s = jnp.where(mask, s, -jnp.inf)
        p = jax.nn.softmax(s.reshape(B, H, qb, -1), axis=-1).reshape(s.shape)
        return jnp.einsum('bhqnj,bhqnjd->bhqd', p.astype(v_sel.dtype), v_sel)

    out = lax.map(attend, (q_blocks, t_blocks))
    return out.transpose(1, 2, 0, 3, 4).reshape(B, H, Lq, Dh)


def _gla_chunked(q, k, v, log_a, s0):
    B, H, L, Dk = q.shape
    Dv = v.shape[-1]
    C = GLA_CHUNK if L % GLA_CHUNK == 0 else L
    n = L // C

    def chunks(t):
        return t.astype(jnp.float32).reshape(B, H, n, C, t.shape[-1]).transpose(2, 0, 1, 3, 4)

    causal = jnp.tril(jnp.ones((C, C), dtype=bool))[:, :, None]

    def step(S, inp):
        qc, kc, vc, ac = inp
        b = jnp.cumsum(ac, axis=2)
        diff = b[:, :, :, None, :] - b[:, :, None, :, :]
        decay = jnp.exp(jnp.where(causal, diff, -jnp.inf))
        attn = jnp.einsum('bhid,bhjd,bhijd->bhij', qc, kc, decay)
        o = jnp.einsum('bhij,bhjv->bhiv', attn, vc) + jnp.einsum('bhid,bhdv->bhiv', qc * jnp.exp(b), S)
        b_last = b[:, :, -1, :]
        S = jnp.exp(b_last)[..., None] * S + jnp.einsum('bhjd,bhjv->bhdv', kc * jnp.exp(b_last[:, :, None, :] - b), vc)
        return S, o

    S, o = lax.scan(step, s0.astype(jnp.float32), (chunks(q), chunks(k), chunks(v), chunks(log_a)))
    return o.transpose(1, 2, 0, 3, 4).reshape(B, H, L, Dv), S


def _ab_mixer(h, pos0, li, past_k, past_v, s0, w):
    B, L, _ = h.shape
    z = h @ w['ab_w_in'][li]
    qa, ka, va, qg, kg, vg, gg, lr = jnp.split(z, AB_SPLITS, axis=-1)
    heads = lambda t: t.reshape(B, L, -1, HEAD_DIM).transpose(0, 2, 1, 3)
    pos = pos0 + jnp.arange(L, dtype=jnp.int32)
    qa = _rope(heads(qa), pos)
    ka = _rope(heads(ka), pos)
    va = heads(va)
    if past_k is None:
        k_all, v_all = ka, va
    else:
        k_all = jnp.concatenate([past_k.astype(ka.dtype), ka], axis=2)
        v_all = jnp.concatenate([past_v.astype(va.dtype), va], axis=2)
    o_a = _moba_attention(qa, k_all, v_all, pos0)
    log_a = jax.nn.log_sigmoid((lr @ w['gla_w_a2'][li] + w['gla_b_a2'][li]).astype(jnp.float32)) / GLA_GATE_NORM
    o_g, s_new = _gla_chunked(heads(qg) * (HEAD_DIM ** -0.5), heads(kg), heads(vg), heads(log_a), s0)
    o_g = _rmsnorm(o_g, w['gla_norm_g'][li]).astype(h.dtype)
    o_g = o_g.transpose(0, 2, 1, 3).reshape(B, L, B_WIDTH) * jax.nn.silu(gg)
    o_a = o_a.transpose(0, 2, 1, 3).reshape(B, L, A_WIDTH)
    out = jnp.concatenate([o_a, o_g], axis=-1) @ w['ab_w_out'][li]
    return out, ka.transpose(0, 2, 1, 3), va.transpose(0, 2, 1, 3), s_new


def _chunk_sgu(h, li, w):
    B, L, _ = h.shape
    z = jax.nn.gelu(h @ w['cm_w_in'][li])
    u, v = jnp.split(z, 2, axis=-1)
    v = _layernorm(v, w['cm_ln_g'][li], w['cm_ln_b'][li])
    n_chunks = -(-L // CM_CHUNK)
    vp = jnp.pad(v, ((0, 0), (0, n_chunks * CM_CHUNK - L), (0, 0)))
    vp = vp.reshape(B, n_chunks, CM_CHUNK, CM_GROUPS, CM_GROUP_DIM)
    w_s = jnp.tril(w['cm_w_s'][li])
    s = jnp.einsum('gts,bnsgc->bntgc', w_s, vp) + w['cm_b_s'][li].T[None, None, :, :, None]
    s = s.reshape(B, n_chunks * CM_CHUNK, CM_WIDTH)[:, :L]
    y = (u * s) @ w['cm_w_out'][li]
    start = ((L - 1) // CM_CHUNK) * CM_CHUNK
    return y, v[:, start:]


def _swiglu(h, wg, wu, wd):
    return (jax.nn.silu(h @ wg) * (h @ wu)) @ wd


def _moe(h, li, w):
    logits = (h @ w['moe_w_router'][li]).astype(jnp.float32)
    top_v, top_i = lax.top_k(logits, MOE_TOP_K)
    top_w = jax.nn.softmax(top_v, axis=-1)
    gates = jnp.sum(jax.nn.one_hot(top_i, N_EXPERTS, dtype=jnp.float32) * top_w[..., None], axis=-2).astype(h.dtype)
    y = jnp.zeros_like(h)
    for e in range(N_EXPERTS):
        y = y + gates[..., e:e + 1] * _swiglu(h, w['moe_w_gate'][li][e], w['moe_w_up'][li][e], w['moe_w_down'][li][e])
    return y


def _trunk(x, c, pos0, past, w):
    B = x.shape[0]
    k_rows, v_rows, gla_states, cm_rows = [], [], [], []
    for l in range(DEPTH):
        mod = jax.nn.silu(c) @ w['ada_w'][l] + w['ada_b'][l]
        sh1, sc1, g1, sh2, sc2, g2 = [m[:, None, :] for m in jnp.split(mod, 6, axis=-1)]
        h = _modulate(_rmsnorm(x, w['norm_mix_g'][l]), sh1, sc1)
        li = l // 2
        if l % 2 == 0:
            if past is None:
                past_k, past_v = None, None
                s0 = jnp.zeros((B, B_HEADS, HEAD_DIM, HEAD_DIM), jnp.float32)
            else:
                cache_k, cache_v, state_gla, page_table = past
                n_b = page_table.shape[0]
                past_k = cache_k[li][page_table].reshape(n_b, -1, A_HEADS, HEAD_DIM).transpose(0, 2, 1, 3)
                past_v = cache_v[li][page_table].reshape(n_b, -1, A_HEADS, HEAD_DIM).transpose(0, 2, 1, 3)
                s0 = state_gla[li]
            mix, kr, vr, s_new = _ab_mixer(h, pos0, li, past_k, past_v, s0, w)
            k_rows.append(kr)
            v_rows.append(vr)
            gla_states.append(s_new)
            x = x + g1 * mix
            h = _modulate(_rmsnorm(x, w['norm_ffn_g'][l]), sh2, sc2)
            x = x + g2 * _swiglu(h, w['ffn_w_gate'][li], w['ffn_w_up'][li], w['ffn_w_down'][li])
        else:
            mix, vrows = _chunk_sgu(h, li, w)
            cm_rows.append(vrows)
            x = x + g1 * mix
            h = _modulate(_rmsnorm(x, w['norm_ffn_g'][l]), sh2, sc2)
            x = x + g2 * _moe(h, li, w)
    y = _rmsnorm(x, w['final_norm_g'])
    return y, jnp.stack(k_rows), jnp.stack(v_rows), jnp.stack(gla_states), jnp.stack(cm_rows)


def setup_inputs(seed: int = 0) -> dict:
    key = jax.random.key(seed)
    ks = iter(jax.random.split(key, 40))
    f32 = jnp.float32

    def nrm(shape, scale):
        return jax.random.normal(next(ks), shape, f32) * scale

    n_pages = PAST_LEN // PAGE_SIZE
    n_used = DEC_BATCH * n_pages
    n_pool = n_used + max(1, n_used // 4)
    perm = jax.random.permutation(next(ks), n_pool)
    page_table = perm[:n_used].reshape(DEC_BATCH, n_pages).astype(jnp.int32)
    return {
        'x_prompt': nrm((BATCH, SEQ, D_MODEL), 1.0),
        'x_sample': nrm((DEC_BATCH, DEC_SEQ, D_MODEL), 1.0),
        'cache_k': nrm((N_AB, n_pool, PAGE_SIZE, A_HEADS, HEAD_DIM), 1.0),
        'cache_v': nrm((N_AB, n_pool, PAGE_SIZE, A_HEADS, HEAD_DIM), 1.0),
        'state_gla': nrm((N_AB, DEC_BATCH, B_HEADS, HEAD_DIM, HEAD_DIM), 0.5),
        'page_table': page_table,
        'c_prompt': nrm((BATCH, D_MODEL), 1.0),
        'c_sample': nrm((DEC_BATCH, D_MODEL), 1.0),
        'ada_w': nrm((DEPTH, D_MODEL, 6 * D_MODEL), 0.5 * D_MODEL ** -0.5),
        'ada_b': nrm((DEPTH, 6 * D_MODEL), 0.02),
        'norm_mix_g': 1.0 + nrm((DEPTH, D_MODEL), 0.02),
        'norm_ffn_g': 1.0 + nrm((DEPTH, D_MODEL), 0.02),
        'ab_w_in': nrm((N_AB, D_MODEL, AB_IN), D_MODEL ** -0.5),
        'gla_w_a2': nrm((N_AB, GLA_LOWRANK, B_WIDTH), GLA_LOWRANK ** -0.5),
        'gla_b_a2': nrm((N_AB, B_WIDTH), 0.1),
        'gla_norm_g': 1.0 + nrm((N_AB, HEAD_DIM), 0.02),
        'ab_w_out': nrm((N_AB, A_WIDTH + B_WIDTH, D_MODEL), (A_WIDTH + B_WIDTH) ** -0.5),
        'ffn_w_gate': nrm((N_AB, D_MODEL, D_FF), D_MODEL ** -0.5),
        'ffn_w_up': nrm((N_AB, D_MODEL, D_FF), D_MODEL ** -0.5),
        'ffn_w_down': nrm((N_AB, D_FF, D_MODEL), D_FF ** -0.5),
        'cm_w_in': nrm((N_C, D_MODEL, 2 * CM_WIDTH), D_MODEL ** -0.5),
        'cm_ln_g': 1.0 + nrm((N_C, CM_WIDTH), 0.02),
        'cm_ln_b': nrm((N_C, CM_WIDTH), 0.02),
        'cm_w_s': nrm((N_C, CM_GROUPS, CM_CHUNK, CM_CHUNK), CM_CHUNK ** -0.5),
        'cm_b_s': 1.0 + nrm((N_C, CM_GROUPS, CM_CHUNK), 0.1),
        'cm_w_out': nrm((N_C, CM_WIDTH, D_MODEL), CM_WIDTH ** -0.5),
        'moe_w_router': nrm((N_C, D_MODEL, N_EXPERTS), D_MODEL ** -0.5),
        'moe_w_gate': nrm((N_C, N_EXPERTS, D_MODEL, D_EXPERT), D_MODEL ** -0.5),
        'moe_w_up': nrm((N_C, N_EXPERTS, D_MODEL, D_EXPERT), D_MODEL ** -0.5),
        'moe_w_down': nrm((N_C, N_EXPERTS, D_EXPERT, D_MODEL), D_EXPERT ** -0.5),
        'final_norm_g': 1.0 + nrm((D_MODEL,), 0.02),
    }


def reference(x_prompt, x_sample, cache_k, cache_v, state_gla, page_table, c_prompt, c_sample,
              ada_w, ada_b, norm_mix_g, norm_ffn_g, ab_w_in, gla_w_a2, gla_b_a2, gla_norm_g, ab_w_out,
              ffn_w_gate, ffn_w_up, ffn_w_down, cm_w_in, cm_ln_g, cm_ln_b, cm_w_s, cm_b_s, cm_w_out,
              moe_w_router, moe_w_gate, moe_w_up, moe_w_down, final_norm_g):
    w = {
        'ada_w': ada_w, 'ada_b': ada_b, 'norm_mix_g': norm_mix_g, 'norm_ffn_g': norm_ffn_g,
        'ab_w_in': ab_w_in, 'gla_w_a2': gla_w_a2, 'gla_b_a2': gla_b_a2, 'gla_norm_g': gla_norm_g,
        'ab_w_out': ab_w_out, 'ffn_w_gate': ffn_w_gate, 'ffn_w_up': ffn_w_up, 'ffn_w_down': ffn_w_down,
        'cm_w_in': cm_w_in, 'cm_ln_g': cm_ln_g, 'cm_ln_b': cm_ln_b, 'cm_w_s': cm_w_s, 'cm_b_s': cm_b_s,
        'cm_w_out': cm_w_out, 'moe_w_router': moe_w_router, 'moe_w_gate': moe_w_gate, 'moe_w_up': moe_w_up,
        'moe_w_down': moe_w_down, 'final_norm_g': final_norm_g,
    }
    y_prompt, k_p, v_p, gla_p, cmv_p = _trunk(x_prompt, c_prompt, 0, None, w)
    past_len = page_table.shape[1] * cache_k.shape[2]
    y_sample, k_s, v_s, gla_s, cmv_s = _trunk(x_sample, c_sample, past_len,
                                              (cache_k, cache_v, state_gla, page_table), w)
    return (y_prompt, y_sample, k_p, v_p, k_s, v_s, gla_p, gla_s, cmv_p, cmv_s)
```

```python
import functools

import jax
import jax.numpy as jnp
from jax import lax
from jax.experimental import pallas as pl
from jax.experimental.pallas import tpu as pltpu

F32 = jnp.float32
BF16 = jnp.bfloat16
HIGHEST = lax.Precision.HIGHEST

HEAD_DIM = 64
HEADS_PER_GROUP = 8
HEAD_WIDTH = HEADS_PER_GROUP * HEAD_DIM
MOBA_BLOCK = 256
MOBA_TOPK = 3
ROPE_THETA = 10000.0
GLA_LOWRANK = 16
GLA_GATE_NORM = 16.0
GLA_CHUNK = 64
CM_CHUNK = 128
CM_GROUPS = 8
MOE_TOP_K = 2
EPS = 1e-6
NEG = -1e30

LANES = 128
VMEM_LIMIT_BYTES = 56 * 1024 * 1024

NT_DIMS = (((1,), (1,)), ((), ()))
TN_DIMS = (((0,), (0,)), ((), ()))


def _params(*sem):
    return pltpu.CompilerParams(dimension_semantics=sem, vmem_limit_bytes=VMEM_LIMIT_BYTES)


def _rms(x, g):
    return x * lax.rsqrt(jnp.mean(x * x, axis=-1, keepdims=True) + EPS) * g


def _silu(x):
    return x * jax.nn.sigmoid(x)


def _gelu_tanh(x):
    return 0.5 * x * (1.0 + jnp.tanh(0.7978845608028654 * (x + 0.044715 * (x * x * x))))


def _token_tiles(b, l, target):
    if l >= target:
        assert l % target == 0
        return 1, target
    bb = min(b, target // l)
    assert target % l == 0 and b % bb == 0
    return bb, l


def _mod_kernel(c_ref, w_ref, b_ref, o_ref):
    s = _silu(c_ref[...]).astype(BF16)
    o_ref[0] = jnp.dot(s, w_ref[0], preferred_element_type=F32) + b_ref[0]


def _modulation(c_all, ada_w, ada_b):
    r, d = c_all.shape
    depth, _, n = ada_w.shape
    tn = 1536
    return pl.pallas_call(
        _mod_kernel,
        grid=(depth, n // tn),
        in_specs=[
            pl.BlockSpec((r, d), lambda l, j: (0, 0)),
            pl.BlockSpec((1, d, tn), lambda l, j: (l, 0, j)),
            pl.BlockSpec((1, 1, tn), lambda l, j: (l, 0, j)),
        ],
        out_specs=pl.BlockSpec((1, r, tn), lambda l, j: (l, 0, j)),
        out_shape=jax.ShapeDtypeStruct((depth, r, n), F32),
        compiler_params=_params("arbitrary", "arbitrary"),
        name="adaln_modulation",
    )(c_all, ada_w.astype(BF16), ada_b.reshape(depth, 1, n))


def _in_proj_kernel(x_ref, mod_ref, g_ref, w_ref, wlr_ref, wa2h_ref, wa2l_ref, ba2_ref, cos_ref, sin_ref,
                    qa_ref, ka_ref, va_ref, qg_ref, kg_ref, vg_ref, gs_ref, la_ref):
    bb, tl, d = x_ref.shape
    tm = bb * tl
    hw = HEAD_WIDTH
    m = mod_ref[...]
    h = _rms(x_ref[...], g_ref[...]) * (1.0 + m[:, 1:2, :]) + m[:, 0:1, :]
    hb = h.reshape(tm, d).astype(BF16)

    def proj(c):
        return jnp.dot(hb, w_ref[:, c * hw:(c + 1) * hw], preferred_element_type=F32)

    cos = jnp.concatenate([cos_ref[...]] * (hw // LANES), axis=-1)[None]
    sin = jnp.concatenate([sin_ref[...]] * (hw // LANES), axis=-1)[None]
    lane = lax.broadcasted_iota(jnp.int32, (tm, hw), 1)
    first_half = (lane % HEAD_DIM) < (HEAD_DIM // 2)

    def rope(z):
        partner = jnp.where(first_half, pltpu.roll(z, hw - HEAD_DIM // 2, 1), pltpu.roll(z, HEAD_DIM // 2, 1))
        return z.reshape(bb, tl, hw) * cos + partner.reshape(bb, tl, hw) * sin

    scale = HEAD_DIM ** -0.5
    qa_ref[...] = rope(proj(0)) * scale
    ka_ref[...] = rope(proj(1))
    va_ref[...] = proj(2).reshape(bb, tl, hw)
    qg_ref[...] = (proj(3) * scale).reshape(bb, tl, hw)
    kg_ref[...] = proj(4).reshape(bb, tl, hw)
    vg_ref[...] = proj(5).reshape(bb, tl, hw)
    gs_ref[...] = _silu(proj(6)).reshape(bb, tl, hw)
    zlr = jnp.dot(hb, wlr_ref[...], preferred_element_type=F32)
    hi = zlr.astype(BF16)
    lo = (zlr - hi.astype(F32)).astype(BF16)
    pre = (jnp.dot(hi, wa2h_ref[...], preferred_element_type=F32)
           + jnp.dot(lo, wa2h_ref[...], preferred_element_type=F32)
           + jnp.dot(hi, wa2l_ref[...], preferred_element_type=F32)) + ba2_ref[...]
    log_sig = jnp.minimum(pre, 0.0) - jnp.log(1.0 + jnp.exp(-jnp.abs(pre)))
    la_ref[...] = (log_sig * (1.0 / GLA_GATE_NORM)).reshape(bb, tl, hw)


def _rope_tables(pos0, l):
    half = HEAD_DIM // 2
    inv = 1.0 / (ROPE_THETA ** (jnp.arange(half, dtype=F32) / half))
    ang = (pos0 + jnp.arange(l, dtype=jnp.int32)).astype(F32)[:, None] * inv[None, :]
    cos, sin = jnp.cos(ang), jnp.sin(ang)
    reps = LANES // HEAD_DIM
    cos_t = jnp.tile(jnp.concatenate([cos, cos], axis=-1), (1, reps))
    sin_t = jnp.tile(jnp.concatenate([-sin, sin], axis=-1), (1, reps))
    return cos_t, sin_t


def _in_proj(x, mod, g, w_main, w_lr, wa2_hi, wa2_lo, ba2, pos0):
    b, l, d = x.shape
    bb, tl = _token_tiles(b, l, 512 if l >= 512 else 256)
    hw = HEAD_WIDTH
    cos_t, sin_t = _rope_tables(pos0, l)
    tok = lambda i, j: (i, j, 0)
    const2 = lambda i, j: (0, 0)
    out_spec = pl.BlockSpec((bb, tl, hw), tok)
    out_sds = jax.ShapeDtypeStruct((b, l, hw), F32)
    return pl.pallas_call(
        _in_proj_kernel,
        grid=(b // bb, l // tl),
        in_specs=[
            pl.BlockSpec((bb, tl, d), tok),
            pl.BlockSpec((bb, 6, d), lambda i, j: (i, 0, 0)),
            pl.BlockSpec((1, d), const2),
            pl.BlockSpec(w_main.shape, const2),
            pl.BlockSpec(w_lr.shape, const2),
            pl.BlockSpec(wa2_hi.shape, const2),
            pl.BlockSpec(wa2_lo.shape, const2),
            pl.BlockSpec((1, hw), const2),
            pl.BlockSpec((tl, LANES), lambda i, j: (j, 0)),
            pl.BlockSpec((tl, LANES), lambda i, j: (j, 0)),
        ],
        out_specs=[out_spec] * 8,
        out_shape=[out_sds] * 8,
        compiler_params=_params("arbitrary", "arbitrary"),
        name="in_proj_rope",
    )(x, mod, g, w_main, w_lr, wa2_hi, wa2_lo, ba2, cos_t, sin_t)


def _moba_prompt_kernel(q_ref, k_ref, v_ref, o_ref, kb_sc, vb_sc, km_sc, qs_sc, m_sc, l_sc, acc_sc):
    qi = pl.program_id(2)
    lk = k_ref.shape[1]
    nblk = lk // MOBA_BLOCK
    tq = MOBA_BLOCK

    @pl.when(qi == 0)
    def _():
        k = k_ref[0]
        kb_sc[:, 0:LANES] = k.astype(BF16)
        row = lax.broadcasted_iota(jnp.int32, (lk, LANES), 0)
        col = lax.broadcasted_iota(jnp.int32, (lk, LANES), 1)
        kb_sc[:, LANES:2 * LANES] = jnp.where(row // MOBA_BLOCK == col, 1.0, 0.0).astype(BF16)
        vb_sc[...] = v_ref[0].astype(BF16)
        km_sc[...] = jnp.zeros_like(km_sc)
        km_sc[0:nblk, :] = jnp.mean(k.reshape(nblk, MOBA_BLOCK, LANES), axis=1)

    q = q_ref[0]
    lane = lax.broadcasted_iota(jnp.int32, (tq, LANES), 1)
    km = km_sc[...]
    for h in range(2):
        head_lanes = (lane < HEAD_DIM) if h == 0 else (lane >= HEAD_DIM)
        qh = jnp.where(head_lanes, q, 0.0)
        gate = lax.dot_general(qh, km, NT_DIMS, precision=HIGHEST, preferred_element_type=F32)
        gate = jnp.where(lane < qi, gate, -jnp.inf)
        cnt = jnp.zeros((tq, LANES), F32)
        for mcol in range(nblk):
            gm = gate[:, mcol:mcol + 1]
            beats = (gm > gate) | ((gm == gate) & (mcol < lane))
            cnt = cnt + jnp.where(beats, 1.0, 0.0)
        sel = ((lane < qi) & (cnt < float(MOBA_TOPK))) | (lane == qi)
        qs_sc[h, :, 0:LANES] = qh.astype(BF16)
        qs_sc[h, :, LANES:2 * LANES] = jnp.where(sel, 0.0, NEG).astype(BF16)

    m_sc[...] = jnp.full_like(m_sc, NEG)
    l_sc[...] = jnp.zeros_like(l_sc)
    acc_sc[...] = jnp.zeros_like(acc_sc)
    rowi = lax.broadcasted_iota(jnp.int32, (tq, MOBA_BLOCK), 0)
    coli = lax.broadcasted_iota(jnp.int32, (tq, MOBA_BLOCK), 1)

    def step(n, diag):
        start = pl.multiple_of(n * MOBA_BLOCK, MOBA_BLOCK)
        kblk = kb_sc[pl.ds(start, MOBA_BLOCK), :]
        vblk = vb_sc[pl.ds(start, MOBA_BLOCK), :]
        for h in range(2):
            s = lax.dot_general(qs_sc[h], kblk, NT_DIMS, preferred_element_type=F32)
            if diag:
                s = jnp.where(rowi >= coli, s, NEG)
            m_old = m_sc[h]
            m_new = jnp.maximum(m_old, jnp.max(s, axis=1, keepdims=True))
            alpha = jnp.exp(m_old - m_new)
            p = jnp.exp(s - m_new)
            l_sc[h] = alpha * l_sc[h] + jnp.sum(p, axis=1, keepdims=True)
            acc_sc[h] = alpha * acc_sc[h] + jnp.dot(p.astype(BF16), vblk, preferred_element_type=F32)
            m_sc[h] = m_new

    step(qi, True)

    def past(n, carry):
        step(n, False)
        return carry

    lax.fori_loop(0, qi, past, 0)
    o_ref[0] = jnp.where(lane < HEAD_DIM, acc_sc[0] / l_sc[0], acc_sc[1] / l_sc[1])


def _moba_prompt(qa, ka, va):
    b, l, hw = qa.shape
    tq = MOBA_BLOCK
    pairs = hw // LANES
    return pl.pallas_call(
        _moba_prompt_kernel,
        grid=(b, pairs, l // tq),
        in_specs=[
            pl.BlockSpec((1, tq, LANES), lambda i, p, t: (i, t, p)),
            pl.BlockSpec((1, l, LANES), lambda i, p, t: (i, 0, p)),
            pl.BlockSpec((1, l, LANES), lambda i, p, t: (i, 0, p)),
        ],
        out_specs=pl.BlockSpec((1, tq, LANES), lambda i, p, t: (i, t, p)),
        out_shape=jax.ShapeDtypeStruct((b, l, hw), F32),
        scratch_shapes=[
            pltpu.VMEM((l, 2 * LANES), BF16),
            pltpu.VMEM((l, LANES), BF16),
            pltpu.VMEM((LANES, LANES), F32),
            pltpu.VMEM((2, tq, 2 * LANES), BF16),
            pltpu.VMEM((2, tq, 1), F32),
            pltpu.VMEM((2, tq, 1), F32),
            pltpu.VMEM((2, tq, LANES), F32),
        ],
        compiler_params=_params("arbitrary", "arbitrary", "arbitrary"),
        name="moba_prompt",
    )(qa, ka, va)


def _moba_sample_kernel(pt_ref, q_ref, kn_ref, vn_ref, k0_ref, k1_ref, v0_ref, v1_ref, o_ref,
                        m_sc, l_sc, g_sc, acc_sc):
    del pt_ref
    j = pl.program_id(1)
    nb = pl.num_programs(1) - 1
    nq = q_ref.shape[1]
    nh = HEADS_PER_GROUP
    page = k0_ref.shape[2] // nh
    q = q_ref[0]
    lane = lax.broadcasted_iota(jnp.int32, (nq, LANES), 1)

    def heads_to_lanes(cols):
        out = jnp.zeros((nq, LANES), F32)
        for h, c in enumerate(cols):
            out = jnp.where(lane == h, c, out)
        return out

    @pl.when(j < nb)
    def _():
        ms, ls, gs, accs = [], [], [], []
        for h in range(nh):
            qh = q[:, h * HEAD_DIM:(h + 1) * HEAD_DIM]
            rows = pl.ds(h, page, stride=nh)
            kh = jnp.concatenate([k0_ref[0, 0, rows, :], k1_ref[0, 0, rows, :]], axis=0)
            vh = jnp.concatenate([v0_ref[0, 0, rows, :], v1_ref[0, 0, rows, :]], axis=0)
            kmean = jnp.mean(kh, axis=0, keepdims=True)
            gs.append(jnp.sum(qh * kmean, axis=1, keepdims=True))
            s = lax.dot_general(qh.astype(BF16), kh.astype(BF16), NT_DIMS, preferred_element_type=F32)
            mh = jnp.max(s, axis=1, keepdims=True)
            p = jnp.exp(s - mh)
            ms.append(mh)
            ls.append(jnp.sum(p, axis=1, keepdims=True))
            accs.append(jnp.dot(p.astype(BF16), vh.astype(BF16), preferred_element_type=F32))
        m_sc[j] = heads_to_lanes(ms)
        l_sc[j] = heads_to_lanes(ls)
        g_sc[j] = heads_to_lanes(gs)
        acc_sc[j] = jnp.concatenate(accs, axis=1)

    @pl.when(j == nb)
    def _():
        nblk = g_sc.shape[0]
        blk = lax.broadcasted_iota(jnp.int32, (nblk, nq, LANES), 0)
        cur = g_sc[...]
        sel = jnp.zeros((nblk, nq, LANES), jnp.bool_)
        for _ in range(min(MOBA_TOPK, nblk)):
            mx = jnp.max(cur, axis=0, keepdims=True)
            first = jnp.min(jnp.where(cur == mx, blk, nblk), axis=0, keepdims=True)
            pick = blk == first
            sel = sel | pick
            cur = jnp.where(pick, -jnp.inf, cur)
        kn = kn_ref[0]
        vn = vn_ref[0]
        rowi = lax.broadcasted_iota(jnp.int32, (nq, nq), 0)
        coli = lax.broadcasted_iota(jnp.int32, (nq, nq), 1)
        ms, ls, accs = [], [], []
        for h in range(nh):
            sl = slice(h * HEAD_DIM, (h + 1) * HEAD_DIM)
            s = lax.dot_general(q[:, sl], kn[:, sl], NT_DIMS, preferred_element_type=F32)
            s = jnp.where(rowi >= coli, s, NEG)
            mh = jnp.max(s, axis=1, keepdims=True)
            p = jnp.exp(s - mh)
            ms.append(mh)
            ls.append(jnp.sum(p, axis=1, keepdims=True))
            accs.append(jnp.dot(p, vn[:, sl], preferred_element_type=F32))
        m_own = heads_to_lanes(ms)
        l_own = heads_to_lanes(ls)
        acc_own = jnp.concatenate(accs, axis=1)
        m_all = m_sc[...]
        m_tot = jnp.maximum(jnp.max(jnp.where(sel, m_all, NEG), axis=0), m_own)
        w = jnp.where(sel, jnp.exp(m_all - m_tot[None]), 0.0)
        w_own = jnp.exp(m_own - m_tot)
        l_tot = jnp.sum(w * l_sc[...], axis=0) + w_own * l_own
        er = lax.broadcasted_iota(jnp.int32, (LANES, HEAD_WIDTH), 0)
        ec = lax.broadcasted_iota(jnp.int32, (LANES, HEAD_WIDTH), 1)
        expand = jnp.where(ec // HEAD_DIM == er, 1.0, 0.0)

        def widen(t):
            return jnp.dot(t, expand, precision=HIGHEST, preferred_element_type=F32)

        w_wide = widen(w.reshape(nblk * nq, LANES)).reshape(nblk, nq, HEAD_WIDTH)
        num = jnp.sum(w_wide * acc_sc[...], axis=0) + widen(w_own) * acc_own
        o_ref[0] = num / widen(l_tot)


def _moba_sample(qa, ka, va, cache_k, cache_v, page_table):
    b, nq, hw = qa.shape
    nh = HEADS_PER_GROUP
    n_pool, page = cache_k.shape[0], cache_k.shape[1]
    pages_per_blk = MOBA_BLOCK // page
    assert pages_per_blk == 2
    nb = page_table.shape[1] // pages_per_blk
    ck = cache_k.reshape(1, n_pool, page * nh, HEAD_DIM)
    cv = cache_v.reshape(1, n_pool, page * nh, HEAD_DIM)
    tok = lambda i, j, pt: (i, 0, 0)

    def page_spec(which):
        return pl.BlockSpec((1, 1, page * nh, HEAD_DIM),
                            lambda i, j, pt: (0, pt[i, jnp.minimum(j, nb - 1) * 2 + which], 0, 0))

    grid_spec = pltpu.PrefetchScalarGridSpec(
        num_scalar_prefetch=1,
        grid=(b, nb + 1),
        in_specs=[
            pl.BlockSpec((1, nq, hw), tok),
            pl.BlockSpec((1, nq, hw), tok),
            pl.BlockSpec((1, nq, hw), tok),
            page_spec(0), page_spec(1), page_spec(0), page_spec(1),
        ],
        out_specs=pl.BlockSpec((1, nq, hw), tok),
        scratch_shapes=[
            pltpu.VMEM((nb, nq, LANES), F32),
            pltpu.VMEM((nb, nq, LANES), F32),
            pltpu.VMEM((nb, nq, LANES), F32),
            pltpu.VMEM((nb, nq, hw), F32),
        ],
    )
    return pl.pallas_call(
        _moba_sample_kernel,
        grid_spec=grid_spec,
        out_shape=jax.ShapeDtypeStruct((b, nq, hw), F32),
        compiler_params=_params("arbitrary", "arbitrary"),
        name="moba_sample",
    )(page_table, qa, ka, va, ck, ck, cv, cv)


def _gla_kernel(q_ref, k_ref, v_ref, la_ref, gs_ref, gn_ref, s0_ref, o_ref, sout_ref, st_sc, *, chunk, mm_dtype):
    c = chunk
    n = q_ref.shape[1] // c
    hd = HEAD_DIM
    lane = lax.broadcasted_iota(jnp.int32, (c, LANES), 1)
    head0 = lane < hd
    r = lax.broadcasted_iota(jnp.int32, (c, c), 0)
    cc = lax.broadcasted_iota(jnp.int32, (c, c), 1)
    causal = r >= cc
    tri = jnp.where(causal, 1.0, 0.0)
    br = lax.broadcasted_iota(jnp.int32, (LANES, LANES), 0) // hd
    bc = lax.broadcasted_iota(jnp.int32, (LANES, LANES), 1) // hd
    same_head = br == bc
    head_mean = jnp.where(same_head, 1.0 / hd, 0.0)
    gn = gn_ref[...]

    s0 = s0_ref[0]
    zero = jnp.zeros((hd, hd), F32)
    st_sc[...] = jnp.concatenate(
        [jnp.concatenate([s0[0], zero], axis=1), jnp.concatenate([zero, s0[1]], axis=1)], axis=0)

    def body(ci, carry):
        rows = pl.ds(pl.multiple_of(ci * c, c), c)
        la = la_ref[0, rows, :]
        q = q_ref[0, rows, :]
        k = k_ref[0, rows, :]
        v = v_ref[0, rows, :]
        st = st_sc[...]
        b = jnp.dot(tri, la, precision=HIGHEST, preferred_element_type=F32)
        b_last = b[c - 1:c, :]
        b_mid = b[c // 2 - 1:c // 2, :]
        qt = q * jnp.exp(b - b_mid)
        kt = k * jnp.exp(b_mid - b)
        qts = jnp.concatenate([jnp.where(head0, qt, 0.0), jnp.where(head0, 0.0, qt)], axis=0)
        a = lax.dot_general(qts.astype(mm_dtype), kt.astype(mm_dtype), NT_DIMS, preferred_element_type=F32)
        a0 = jnp.where(causal, a[0:c], 0.0).astype(mm_dtype)
        a1 = jnp.where(causal, a[c:2 * c], 0.0).astype(mm_dtype)
        vm = v.astype(mm_dtype)
        o = jnp.where(head0, jnp.dot(a0, vm, preferred_element_type=F32),
                      jnp.dot(a1, vm, preferred_element_type=F32))
        qe = q * jnp.exp(b)
        o = o + lax.dot_general(qe.astype(mm_dtype), st.astype(mm_dtype), NT_DIMS, preferred_element_type=F32)
        kd = k * jnp.exp(b_last - b)
        upd = lax.dot_general(vm, kd.astype(mm_dtype), TN_DIMS, preferred_element_type=F32)
        st_sc[...] = st * jnp.exp(b_last) + jnp.where(same_head, upd, 0.0)
        ms = jnp.dot(o * o, head_mean, precision=HIGHEST, preferred_element_type=F32)
        o_ref[0, rows, :] = o * lax.rsqrt(ms + EPS) * gn * gs_ref[0, rows, :]
        return carry

    lax.fori_loop(0, n, body, 0)
    st = st_sc[...]
    sout_ref[0, 0] = st[0:hd, 0:hd]
    sout_ref[0, 1] = st[hd:2 * hd, hd:2 * hd]


def _gla(qg, kg, vg, la, gs, gn_pair, s0_t, chunk, mm_dtype):
    b, l, hw = qg.shape
    pairs = hw // LANES
    seq = pl.BlockSpec((1, l, LANES), lambda i, p: (i, 0, p))
    st_spec = pl.BlockSpec((1, 2, HEAD_DIM, HEAD_DIM), lambda i, p: (i, p, 0, 0))
    return pl.pallas_call(
        functools.partial(_gla_kernel, chunk=chunk, mm_dtype=mm_dtype),
        grid=(b, pairs),
        in_specs=[seq, seq, seq, seq, seq, pl.BlockSpec((1, LANES), lambda i, p: (0, 0)), st_spec],
        out_specs=[seq, st_spec],
        out_shape=[jax.ShapeDtypeStruct((b, l, hw), F32),
                   jax.ShapeDtypeStruct((b, HEADS_PER_GROUP, HEAD_DIM, HEAD_DIM), F32)],
        scratch_shapes=[pltpu.VMEM((LANES, LANES), F32)],
        compiler_params=_params("arbitrary", "arbitrary"),
        name="gla_chunked",
    )(qg, kg, vg, la, gs, gn_pair, s0_t)


def _mix_ffn_kernel(x_ref, oa_ref, og_ref, mod_ref, g_ref, wo_a_ref, wo_g_ref, wg_ref, wu_ref, wd_ref, o_ref,
                    x1_sc, h_sc, acc_sc):
    j = pl.program_id(2)
    bb, tl, d = x_ref.shape
    tm = bb * tl

    @pl.when(j == 0)
    def _():
        m = mod_ref[...]
        oa = oa_ref[...].reshape(tm, -1).astype(BF16)
        og = og_ref[...].reshape(tm, -1).astype(BF16)
        mix = (jnp.dot(oa, wo_a_ref[...], preferred_element_type=F32)
               + jnp.dot(og, wo_g_ref[...], preferred_element_type=F32))
        x1 = x_ref[...] + m[:, 2:3, :] * mix.reshape(bb, tl, d)
        h = _rms(x1, g_ref[...]) * (1.0 + m[:, 4:5, :]) + m[:, 3:4, :]
        x1_sc[...] = x1.reshape(tm, d)
        h_sc[...] = h.reshape(tm, d).astype(BF16)
        acc_sc[...] = jnp.zeros_like(acc_sc)

    h = h_sc[...]
    gate = jnp.dot(h, wg_ref[...], preferred_element_type=F32)
    up = jnp.dot(h, wu_ref[...], preferred_element_type=F32)
    act = (_silu(gate) * up).astype(BF16)
    acc_sc[...] += jnp.dot(act, wd_ref[...], preferred_element_type=F32)

    @pl.when(j == pl.num_programs(2) - 1)
    def _():
        g2 = mod_ref[...][:, 5:6, :]
        o_ref[...] = x1_sc[...].reshape(bb, tl, d) + g2 * acc_sc[...].reshape(bb, tl, d)


def _mix_ffn(x, oa, og, mod, g, wo_a, wo_g, wg, wu, wd):
    b, l, d = x.shape
    hw = oa.shape[-1]
    f = wg.shape[1]
    bb, tl = _token_tiles(b, l, 512 if l >= 512 else 256)
    tm = bb * tl
    fc = f // 2
    tok = lambda i, t, j: (i, t, 0)
    const2 = lambda i, t, j: (0, 0)
    return pl.pallas_call(
        _mix_ffn_kernel,
        grid=(b // bb, l // tl, f // fc),
        in_specs=[
            pl.BlockSpec((bb, tl, d), tok),
            pl.BlockSpec((bb, tl, hw), tok),
            pl.BlockSpec((bb, tl, hw), tok),
            pl.BlockSpec((bb, 6, d), lambda i, t, j: (i, 0, 0)),
            pl.BlockSpec((1, d), const2),
            pl.BlockSpec(wo_a.shape, const2),
            pl.BlockSpec(wo_g.shape, const2),
            pl.BlockSpec((d, fc), lambda i, t, j: (0, j)),
            pl.BlockSpec((d, fc), lambda i, t, j: (0, j)),
            pl.BlockSpec((fc, d), lambda i, t, j: (j, 0)),
        ],
        out_specs=pl.BlockSpec((bb, tl, d), tok),
        out_shape=jax.ShapeDtypeStruct((b, l, d), F32),
        scratch_shapes=[pltpu.VMEM((tm, d), F32), pltpu.VMEM((tm, d), BF16), pltpu.VMEM((tm, d), F32)],
        compiler_params=_params("arbitrary", "arbitrary", "arbitrary"),
        name="outproj_ffn",
    )(x, oa, og, mod, g, wo_a, wo_g, wg, wu, wd)


def _cm_in_kernel(x_ref, mod_ref, g_ref, w_ref, lng_ref, lnb_ref, u_ref, v_ref):
    bb, tl, d = x_ref.shape
    tm = bb * tl
    cw = u_ref.shape[-1]
    m = mod_ref[...]
    h = _rms(x_ref[...], g_ref[...]) * (1.0 + m[:, 1:2, :]) + m[:, 0:1, :]
    hb = h.reshape(tm, d).astype(BF16)
    u_ref[...] = _gelu_tanh(jnp.dot(hb, w_ref[:, 0:cw], preferred_element_type=F32)).reshape(bb, tl, cw)
    zv = _gelu_tanh(jnp.dot(hb, w_ref[:, cw:2 * cw], preferred_element_type=F32))
    mu = jnp.mean(zv, axis=-1, keepdims=True)
    zc = zv - mu
    var = jnp.mean(zc * zc, axis=-1, keepdims=True)
    v_ref[...] = (zc * lax.rsqrt(var + EPS) * lng_ref[...] + lnb_ref[...]).reshape(bb, tl, cw)


def _cm_in(x, mod, g, w, ln_g, ln_b):
    b, l, d = x.shape
    cw = w.shape[1] // 2
    bb, tl = _token_tiles(b, l, 256)
    tok = lambda i, t: (i, t, 0)
    const2 = lambda i, t: (0, 0)
    out_spec = pl.BlockSpec((bb, tl, cw), tok)
    out_sds = jax.ShapeDtypeStruct((b, l, cw), F32)
    return pl.pallas_call(
        _cm_in_kernel,
        grid=(b // bb, l // tl),
        in_specs=[
            pl.BlockSpec((bb, tl, d), tok),
            pl.BlockSpec((bb, 6, d), lambda i, t: (i, 0, 0)),
            pl.BlockSpec((1, d), const2),
            pl.BlockSpec(w.shape, const2),
            pl.BlockSpec((1, cw), const2),
            pl.BlockSpec((1, cw), const2),
        ],
        out_specs=[out_spec, out_spec],
        out_shape=[out_sds, out_sds],
        compiler_params=_params("arbitrary", "arbitrary"),
        name="cm_in_gelu_ln",
    )(x, mod, g, w, ln_g, ln_b)


def _cm_out_kernel(x_ref, u_ref, v_ref, mod_ref, g_ref, ws_ref, bs_ref, wo_ref, wr_ref,
                   x3_ref, h_ref, gates_ref, us_sc):
    bb, tl, d = x_ref.shape
    tm = bb * tl
    cw = u_ref.shape[-1]
    ngroups, cm, _ = ws_ref.shape
    gd = cw // ngroups
    u = u_ref[...].reshape(tm, cw)
    v = v_ref[...].reshape(tm, cw)
    bs = bs_ref[...]
    r = lax.broadcasted_iota(jnp.int32, (cm, cm), 0)
    c = lax.broadcasted_iota(jnp.int32, (cm, cm), 1)
    for g in range(ngroups):
        wsg = jnp.where(r >= c, ws_ref[g], 0.0).astype(BF16)
        cols = slice(g * gd, (g + 1) * gd)
        for sb in range(tm // cm):
            rows = slice(sb * cm, (sb + 1) * cm)
            s = jnp.dot(wsg, v[rows, cols].astype(BF16), preferred_element_type=F32) + bs[rows, g:g + 1]
            us_sc[rows, cols] = (u[rows, cols] * s).astype(BF16)
    y = jnp.dot(us_sc[...], wo_ref[...], preferred_element_type=F32)
    m = mod_ref[...]
    x3 = x_ref[...] + m[:, 2:3, :] * y.reshape(bb, tl, d)
    x3_ref[...] = x3
    h = (_rms(x3, g_ref[...]) * (1.0 + m[:, 4:5, :]) + m[:, 3:4, :]).reshape(tm, d)
    h_ref[...] = h.astype(BF16)
    logits = jnp.dot(h, wr_ref[...], precision=HIGHEST, preferred_element_type=F32)
    ne = logits.shape[1]
    col = lax.broadcasted_iota(jnp.int32, (tm, ne), 1)
    m1 = jnp.max(logits, axis=1, keepdims=True)
    i1 = jnp.min(jnp.where(logits == m1, col, ne), axis=1, keepdims=True)
    rest = jnp.where(col == i1, -jnp.inf, logits)
    m2 = jnp.max(rest, axis=1, keepdims=True)
    i2 = jnp.min(jnp.where(rest == m2, col, ne), axis=1, keepdims=True)
    e2 = jnp.exp(m2 - m1)
    gates_ref[...] = jnp.where(col == i1, 1.0 / (1.0 + e2), 0.0) + jnp.where(col == i2, e2 / (1.0 + e2), 0.0)


def _cm_out(x, u, v, mod, g, ws_mix, bs_rows, wo, wr, cm):
    b, l, d = x.shape
    cw = u.shape[-1]
    ne = wr.shape[1]
    bb, tl = _token_tiles(b, l, 256)
    tm = bb * tl
    tok = lambda i, t: (i, t, 0)
    flat = lambda i, t: (i * (l // tl) + t, 0)
    const2 = lambda i, t: (0, 0)
    return pl.pallas_call(
        _cm_out_kernel,
        grid=(b // bb, l // tl),
        in_specs=[
            pl.BlockSpec((bb, tl, d), tok),
            pl.BlockSpec((bb, tl, cw), tok),
            pl.BlockSpec((bb, tl, cw), tok),
            pl.BlockSpec((bb, 6, d), lambda i, t: (i, 0, 0)),
            pl.BlockSpec((1, d), const2),
            pl.BlockSpec(ws_mix.shape, lambda i, t: (0, 0, 0)),
            pl.BlockSpec((tm, bs_rows.shape[1]), const2),
            pl.BlockSpec(wo.shape, const2),
            pl.BlockSpec(wr.shape, const2),
        ],
        out_specs=[pl.BlockSpec((bb, tl, d), tok), pl.BlockSpec((tm, d), flat), pl.BlockSpec((tm, ne), flat)],
        out_shape=[jax.ShapeDtypeStruct((b, l, d), F32), jax.ShapeDtypeStruct((b * l, d), BF16),
                   jax.ShapeDtypeStruct((b * l, ne), F32)],
        scratch_shapes=[pltpu.VMEM((tm, cw), BF16)],
        compiler_params=_params("arbitrary", "arbitrary"),
        name="cm_out_router",
    )(x, u, v, mod, g, ws_mix, bs_rows, wo, wr)


def _moe_kernel(h_ref, gates_ref, wg_ref, wu_ref, wd_ref, o_ref):
    e = pl.program_id(1)
    j = pl.program_id(2)

    @pl.when((e == 0) & (j == 0))
    def _():
        o_ref[...] = jnp.zeros_like(o_ref)

    h = h_ref[...]
    gates = gates_ref[...]
    col = lax.broadcasted_iota(jnp.int32, gates.shape, 1)
    ge = jnp.sum(jnp.where(col == e, gates, 0.0), axis=1, keepdims=True)
    gate = jnp.dot(h, wg_ref[0], preferred_element_type=F32)
    up = jnp.dot(h, wu_ref[0], preferred_element_type=F32)
    act = (_silu(gate) * up * ge).astype(BF16)
    o_ref[...] += jnp.dot(act, wd_ref[0], preferred_element_type=F32)


def _moe(h, gates, wg, wu, wd, tm):
    t, d = h.shape
    ne, _, f = wg.shape
    fc = 512
    assert t % tm == 0 and f % fc == 0
    return pl.pallas_call(
        _moe_kernel,
        grid=(t // tm, ne, f // fc),
        in_specs=[
            pl.BlockSpec((tm, d), lambda i, e, j: (i, 0)),
            pl.BlockSpec((tm, ne), lambda i, e, j: (i, 0)),
            pl.BlockSpec((1, d, fc), lambda i, e, j: (e, 0, j)),
            pl.BlockSpec((1, d, fc), lambda i, e, j: (e, 0, j)),
            pl.BlockSpec((1, fc, d), lambda i, e, j: (e, j, 0)),
        ],
        out_specs=pl.BlockSpec((tm, d), lambda i, e, j: (i, 0)),
        out_shape=jax.ShapeDtypeStruct((t, d), F32),
        compiler_params=_params("arbitrary", "arbitrary", "arbitrary"),
        name="moe_experts",
    )(h, gates, wg, wu, wd)


def _final_kernel(x_ref, moe_ref, mod_ref, g_ref, o_ref):
    bb, tl, d = x_ref.shape
    g2 = mod_ref[...][:, 5:6, :]
    x4 = x_ref[...] + g2 * moe_ref[...].reshape(bb, tl, d)
    o_ref[...] = _rms(x4, g_ref[...])


def _final(x, moe, row0, mod, g):
    b, l, d = x.shape
    bb, tl = _token_tiles(b, l, 512 if l >= 512 else 256)
    tm = bb * tl
    assert row0 % tm == 0
    tok = lambda i, t: (i, t, 0)
    return pl.pallas_call(
        _final_kernel,
        grid=(b // bb, l // tl),
        in_specs=[
            pl.BlockSpec((bb, tl, d), tok),
            pl.BlockSpec((tm, d), lambda i, t: (row0 // tm + i * (l // tl) + t, 0)),
            pl.BlockSpec((bb, 6, d), lambda i, t: (i, 0, 0)),
            pl.BlockSpec((1, d), lambda i, t: (0, 0)),
        ],
        out_specs=pl.BlockSpec((bb, tl, d), tok),
        out_shape=jax.ShapeDtypeStruct((b, l, d), F32),
        compiler_params=_params("arbitrary", "arbitrary"),
        name="final_norm",
    )(x, moe, mod, g)


def _layer0(x, mod, pos0, past, w):
    b, l, d = x.shape
    hw = HEAD_WIDTH
    outs = _in_proj(x, mod, w["norm_mix_g0"], w["w_in_main"], w["w_in_lr"], w["wa2_hi"], w["wa2_lo"], w["ba2"], pos0)
    qa, ka, va, qg, kg, vg, gs, la = outs
    if past is None:
        oa = _moba_prompt(qa, ka, va)
        s0_t = jnp.zeros((b, HEADS_PER_GROUP, HEAD_DIM, HEAD_DIM), F32)
    else:
        cache_k, cache_v, state, page_table = past
        oa = _moba_sample(qa, ka, va, cache_k, cache_v, page_table)
        s0_t = jnp.swapaxes(state, -1, -2)
    chunk = GLA_CHUNK if l % GLA_CHUNK == 0 else l
    mm_dtype = BF16 if chunk % 16 == 0 else F32
    og, s_t = _gla(qg, kg, vg, la, gs, w["gla_norm_pair"], s0_t, chunk, mm_dtype)
    x2 = _mix_ffn(x, oa, og, mod, w["norm_ffn_g0"], w["wo_a"], w["wo_g"], w["ffn_wg"], w["ffn_wu"], w["ffn_wd"])
    k_rows = ka.reshape(b, l, HEADS_PER_GROUP, HEAD_DIM)
    v_rows = va.reshape(b, l, HEADS_PER_GROUP, HEAD_DIM)
    return x2, k_rows, v_rows, jnp.swapaxes(s_t, -1, -2)


def _layer1_front(x, mod, w):
    b, l, d = x.shape
    u, v = _cm_in(x, mod, w["norm_mix_g1"], w["cm_w_in"], w["cm_ln_g"], w["cm_ln_b"])
    bb, tl = _token_tiles(b, l, 256)
    tm = bb * tl
    if l >= CM_CHUNK:
        cm = CM_CHUNK
        ws_mix = w["cm_w_s"]
        bs_rows = jnp.tile(w["cm_b_s"].T, (tm // CM_CHUNK, 1))
    else:
        cm = tm
        eye = jnp.eye(tm // l, dtype=F32)
        ws_mix = jax.vmap(lambda m_: jnp.kron(eye, m_))(w["cm_w_s"][:, :l, :l])
        bs_rows = jnp.tile(w["cm_b_s"][:, :l].T, (tm // l, 1))
    x3, h, gates = _cm_out(x, u, v, mod, w["norm_ffn_g1"], ws_mix, bs_rows, w["cm_w_out"], w["moe_w_router"], cm)
    start = ((l - 1) // CM_CHUNK) * CM_CHUNK
    return x3, h, gates, v[:, start:]


def kernel(x_prompt, x_sample, cache_k, cache_v, state_gla, page_table, c_prompt, c_sample, ada_w, ada_b, norm_mix_g, norm_ffn_g, ab_w_in, gla_w_a2, gla_b_a2, gla_norm_g, ab_w_out, ffn_w_gate, ffn_w_up, ffn_w_down, cm_w_in, cm_ln_g, cm_ln_b, cm_w_s, cm_b_s, cm_w_out, moe_w_router, moe_w_gate, moe_w_up, moe_w_down, final_norm_g):
    bp, lp, d = x_prompt.shape
    bs, ls, _ = x_sample.shape
    hw = HEAD_WIDTH
    n_main = 7 * hw
    past_len = page_table.shape[1] * cache_k.shape[2]

    wa2 = jnp.pad(gla_w_a2[0], ((0, LANES - GLA_LOWRANK), (0, 0)))
    wa2_hi = wa2.astype(BF16)
    w = {
        "norm_mix_g0": norm_mix_g[0][None], "norm_mix_g1": norm_mix_g[1][None],
        "norm_ffn_g0": norm_ffn_g[0][None], "norm_ffn_g1": norm_ffn_g[1][None],
        "w_in_main": ab_w_in[0][:, :n_main].astype(BF16),
        "w_in_lr": jnp.pad(ab_w_in[0][:, n_main:], ((0, 0), (0, LANES - GLA_LOWRANK))).astype(BF16),
        "wa2_hi": wa2_hi, "wa2_lo": (wa2 - wa2_hi.astype(F32)).astype(BF16),
        "ba2": gla_b_a2[0][None],
        "gla_norm_pair": jnp.tile(gla_norm_g[0], LANES // HEAD_DIM)[None],
        "wo_a": ab_w_out[0][:hw].astype(BF16), "wo_g": ab_w_out[0][hw:].astype(BF16),
        "ffn_wg": ffn_w_gate[0].astype(BF16), "ffn_wu": ffn_w_up[0].astype(BF16), "ffn_wd": ffn_w_down[0].astype(BF16),
        "cm_w_in": cm_w_in[0].astype(BF16), "cm_ln_g": cm_ln_g[0][None], "cm_ln_b": cm_ln_b[0][None],
        "cm_w_s": cm_w_s[0], "cm_b_s": cm_b_s[0], "cm_w_out": cm_w_out[0].astype(BF16),
        "moe_w_router": moe_w_router[0],
    }

    rows = bp + bs
    rows_pad = -(-rows // 16) * 16
    c_all = jnp.pad(jnp.concatenate([c_prompt, c_sample], axis=0), ((0, rows_pad - rows), (0, 0)))
    mod_all = _modulation(c_all, ada_w, ada_b)
    mod_p = [mod_all[l, :bp].reshape(bp, 6, d) for l in range(2)]
    mod_s = [mod_all[l, bp:rows].reshape(bs, 6, d) for l in range(2)]

    x2_p, k_p, v_p, gla_p = _layer0(x_prompt, mod_p[0], 0, None, w)
    x2_s, k_s, v_s, gla_s = _layer0(x_sample, mod_s[0], past_len,
                                    (cache_k[0], cache_v[0], state_gla[0], page_table), w)

    x3_p, h_p, gates_p, cmv_p = _layer1_front(x2_p, mod_p[1], w)
    x3_s, h_s, gates_s, cmv_s = _layer1_front(x2_s, mod_s[1], w)

    h_all = jnp.concatenate([h_p, h_s], axis=0)
    gates_all = jnp.concatenate([gates_p, gates_s], axis=0)
    t_all = h_all.shape[0]
    tm_moe = next(c for c in (1280, 1024, 512, 256, 128, 64, 32, 16) if t_all % c == 0)
    moe = _moe(h_all, gates_all, moe_w_gate[0].astype(BF16), moe_w_up[0].astype(BF16),
               moe_w_down[0].astype(BF16), tm_moe)

    fg = final_norm_g[None]
    y_p = _final(x3_p, moe, 0, mod_p[1], fg)
    y_s = _final(x3_s, moe, bp * lp, mod_s[1], fg)

    return (y_p, y_s, k_p[None], v_p[None], k_s[None], v_s[None], gla_p[None], gla_s[None],
            cmv_p[None], cmv_s[None])
```

```python
import functools

import jax
import jax.numpy as jnp
from jax import lax
from jax.experimental import pallas as pl
from jax.experimental.pallas import tpu as pltpu

F32 = jnp.float32
BF16 = jnp.bfloat16
HIGHEST = lax.Precision.HIGHEST

HEAD_DIM = 64
HEADS_PER_GROUP = 8
HEAD_WIDTH = HEADS_PER_GROUP * HEAD_DIM
MOBA_BLOCK = 256
MOBA_TOPK = 3
ROPE_THETA = 10000.0
GLA_LOWRANK = 16
GLA_GATE_NORM = 16.0
GLA_CHUNK = 64
CM_CHUNK = 128
CM_GROUPS = 8
MOE_TOP_K = 2
EPS = 1e-6
NEG = -1e30

LANES = 128
VMEM_LIMIT_BYTES = 56 * 1024 * 1024

NT_DIMS = (((1,), (1,)), ((), ()))
TN_DIMS = (((0,), (0,)), ((), ()))


def _params(*sem):
    return pltpu.CompilerParams(dimension_semantics=sem, vmem_limit_bytes=VMEM_LIMIT_BYTES)


def _rms(x, g):
    return x * lax.rsqrt(jnp.mean(x * x, axis=-1, keepdims=True) + EPS) * g


def _silu(x):
    return x * jax.nn.sigmoid(x)


def _gelu_tanh(x):
    return 0.5 * x * (1.0 + jnp.tanh(0.7978845608028654 * (x + 0.044715 * (x * x * x))))


def _token_tiles(b, l, target):
    if l >= target:
        assert l % target == 0
        return 1, target
    bb = min(b, target // l)
    assert target % l == 0 and b % bb == 0
    return bb, l


def _mod_kernel(c_ref, w_ref, b_ref, o_ref):
    s = _silu(c_ref[...]).astype(BF16)
    o_ref[0] = jnp.dot(s, w_ref[0], preferred_element_type=F32) + b_ref[0]


def _modulation(c_all, ada_w, ada_b):
    r, d = c_all.shape
    depth, _, n = ada_w.shape
    tn = 1536
    return pl.pallas_call(
        _mod_kernel,
        grid=(depth, n // tn),
        in_specs=[
            pl.BlockSpec((r, d), lambda l, j: (0, 0)),
            pl.BlockSpec((1, d, tn), lambda l, j: (l, 0, j)),
            pl.BlockSpec((1, 1, tn), lambda l, j: (l, 0, j)),
        ],
        out_specs=pl.BlockSpec((1, r, tn), lambda l, j: (l, 0, j)),
        out_shape=jax.ShapeDtypeStruct((depth, r, n), F32),
        compiler_params=_params("arbitrary", "arbitrary"),
        name="adaln_modulation",
    )(c_all, ada_w.astype(BF16), ada_b.reshape(depth, 1, n))


def _in_proj_kernel(x_ref, mod_ref, g_ref, w_ref, wlr_ref, wa2h_ref, wa2l_ref, ba2_ref, cos_ref, sin_ref,
                    qa_ref, ka_ref, va_ref, qg_ref, kg_ref, vg_ref, gs_ref, la_ref):
    bb, tl, d = x_ref.shape
    tm = bb * tl
    hw = HEAD_WIDTH
    m = mod_ref[...]
    h = _rms(x_ref[...], g_ref[...]) * (1.0 + m[:, 1:2, :]) + m[:, 0:1, :]
    hb = h.reshape(tm, d).astype(BF16)

    def proj(c):
        return jnp.dot(hb, w_ref[:, c * hw:(c + 1) * hw], preferred_element_type=F32)

    cos = jnp.concatenate([cos_ref[...]] * (hw // LANES), axis=-1)[None]
    sin = jnp.concatenate([sin_ref[...]] * (hw // LANES), axis=-1)[None]
    lane = lax.broadcasted_iota(jnp.int32, (tm, hw), 1)
    first_half = (lane % HEAD_DIM) < (HEAD_DIM // 2)

    def rope(z):
        partner = jnp.where(first_half, pltpu.roll(z, hw - HEAD_DIM // 2, 1), pltpu.roll(z, HEAD_DIM // 2, 1))
        return z.reshape(bb, tl, hw) * cos + partner.reshape(bb, tl, hw) * sin

    scale = HEAD_DIM ** -0.5
    qa_ref[...] = rope(proj(0)) * scale
    ka_ref[...] = rope(proj(1))
    va_ref[...] = proj(2).reshape(bb, tl, hw)
    qg_ref[...] = (proj(3) * scale).reshape(bb, tl, hw)
    kg_ref[...] = proj(4).reshape(bb, tl, hw)
    vg_ref[...] = proj(5).reshape(bb, tl, hw)
    gs_ref[...] = _silu(proj(6)).reshape(bb, tl, hw)
    zlr = jnp.dot(hb, wlr_ref[...], preferred_element_type=F32)
    hi = zlr.astype(BF16)
    lo = (zlr - hi.astype(F32)).astype(BF16)
    pre = (jnp.dot(hi, wa2h_ref[...], preferred_element_type=F32)
           + jnp.dot(lo, wa2h_ref[...], preferred_element_type=F32)
           + jnp.dot(hi, wa2l_ref[...], preferred_element_type=F32)) + ba2_ref[...]
    log_sig = jnp.minimum(pre, 0.0) - jnp.log(1.0 + jnp.exp(-jnp.abs(pre)))
    la_ref[...] = (log_sig * (1.0 / GLA_GATE_NORM)).reshape(bb, tl, hw)


def _rope_tables(pos0, l):
    half = HEAD_DIM // 2
    inv = 1.0 / (ROPE_THETA ** (jnp.arange(half, dtype=F32) / half))
    ang = (pos0 + jnp.arange(l, dtype=jnp.int32)).astype(F32)[:, None] * inv[None, :]
    cos, sin = jnp.cos(ang), jnp.sin(ang)
    reps = LANES // HEAD_DIM
    cos_t = jnp.tile(jnp.concatenate([cos, cos], axis=-1), (1, reps))
    sin_t = jnp.tile(jnp.concatenate([-sin, sin], axis=-1), (1, reps))
    return cos_t, sin_t


def _in_proj(x, mod, g, w_main, w_lr, wa2_hi, wa2_lo, ba2, pos0):
    b, l, d = x.shape
    bb, tl = _token_tiles(b, l, 512 if l >= 512 else 256)
    hw = HEAD_WIDTH
    cos_t, sin_t = _rope_tables(pos0, l)
    tok = lambda i, j: (i, j, 0)
    const2 = lambda i, j: (0, 0)
    out_spec = pl.BlockSpec((bb, tl, hw), tok)
    out_sds = jax.ShapeDtypeStruct((b, l, hw), F32)
    return pl.pallas_call(
        _in_proj_kernel,
        grid=(b // bb, l // tl),
        in_specs=[
            pl.BlockSpec((bb, tl, d), tok),
            pl.BlockSpec((bb, 6, d), lambda i, j: (i, 0, 0)),
            pl.BlockSpec((1, d), const2),
            pl.BlockSpec(w_main.shape, const2),
            pl.BlockSpec(w_lr.shape, const2),
            pl.BlockSpec(wa2_hi.shape, const2),
            pl.BlockSpec(wa2_lo.shape, const2),
            pl.BlockSpec((1, hw), const2),
            pl.BlockSpec((tl, LANES), lambda i, j: (j, 0)),
            pl.BlockSpec((tl, LANES), lambda i, j: (j, 0)),
        ],
        out_specs=[out_spec] * 8,
        out_shape=[out_sds] * 8,
        compiler_params=_params("arbitrary", "arbitrary"),
        name="in_proj_rope",
    )(x, mod, g, w_main, w_lr, wa2_hi, wa2_lo, ba2, cos_t, sin_t)


def _moba_prompt_kernel(q_ref, k_ref, v_ref, o_ref, kb_sc, vb_sc, km_sc, qs_sc, m_sc, acc_sc):
    ti = pl.program_id(2)
    lk = k_ref.shape[1]
    nblk = lk // MOBA_BLOCK
    nblk8 = -(-nblk // 8) * 8
    tq = q_ref.shape[1]
    nsub = tq // MOBA_BLOCK

    @pl.when(ti == 0)
    def _():
        k = k_ref[0]
        v = v_ref[0]
        row = lax.broadcasted_iota(jnp.int32, (lk, LANES), 0)
        col = lax.broadcasted_iota(jnp.int32, (lk, LANES), 1)
        kb_sc[:, 0:LANES] = k.astype(BF16)
        kb_sc[:, LANES:2 * LANES] = jnp.where(row // MOBA_BLOCK == col, 1.0, 0.0).astype(BF16)
        vb_sc[0] = jnp.where(col < HEAD_DIM, v, 1.0).astype(BF16)
        vb_sc[1] = jnp.where(col >= HEAD_DIM, v, 1.0).astype(BF16)
        km_sc[...] = jnp.zeros_like(km_sc)
        km_sc[0:nblk, :] = jnp.mean(k.reshape(nblk, MOBA_BLOCK, LANES), axis=1)

    q = q_ref[0]
    lane = lax.broadcasted_iota(jnp.int32, (tq, LANES), 1)
    blk = lax.broadcasted_iota(jnp.int32, (nblk8, tq), 0)
    own = ti * nsub + lax.broadcasted_iota(jnp.int32, (nblk8, tq), 1) // MOBA_BLOCK
    past_blk = blk < own
    km = km_sc[...]
    for h in range(2):
        head_lanes = (lane < HEAD_DIM) if h == 0 else (lane >= HEAD_DIM)
        qh = jnp.where(head_lanes, q, 0.0)
        gate = lax.dot_general(km, qh, NT_DIMS, precision=HIGHEST, preferred_element_type=F32)[0:nblk8]
        gate = jnp.where(past_blk, gate, -jnp.inf)
        cnt = jnp.zeros((nblk8, tq), F32)
        for mrow in range(nblk):
            gm = gate[mrow:mrow + 1, :]
            beats = (gm > gate) | ((gm == gate) & (mrow < blk))
            cnt = cnt + jnp.where(beats, 1.0, 0.0)
        sel = (past_blk & (cnt < float(MOBA_TOPK))) | (blk == own)
        bias_t = jnp.concatenate(
            [jnp.where(sel, 0.0, NEG), jnp.full((LANES - nblk8, tq), NEG, F32)], axis=0)
        qs_sc[h, :, 0:LANES] = qh.astype(BF16)
        qs_sc[h, :, LANES:2 * LANES] = bias_t.T.astype(BF16)

    m_sc[...] = jnp.full_like(m_sc, NEG)
    acc_sc[...] = jnp.zeros_like(acc_sc)
    rowi = lax.broadcasted_iota(jnp.int32, (tq, MOBA_BLOCK), 0)
    coli = lax.broadcasted_iota(jnp.int32, (tq, MOBA_BLOCK), 1)

    def step(n, own_sub):
        start = pl.multiple_of(n * MOBA_BLOCK, MOBA_BLOCK)
        kblk = kb_sc[pl.ds(start, MOBA_BLOCK), :]
        for h in range(2):
            s = lax.dot_general(qs_sc[h], kblk, NT_DIMS, preferred_element_type=F32)
            if own_sub is not None:
                in_sub = rowi // MOBA_BLOCK == own_sub
                s = jnp.where(in_sub & (rowi - own_sub * MOBA_BLOCK < coli), NEG, s)
            m_old = m_sc[h]
            row_max = jnp.max(jnp.maximum(s[:, 0:LANES], s[:, LANES:2 * LANES]), axis=1, keepdims=True)
            m_new = jnp.maximum(m_old, row_max)
            alpha = jnp.exp(m_old - m_new)
            p = jnp.exp(s - jnp.concatenate([m_new, m_new], axis=1))
            pv = jnp.dot(p.astype(BF16), vb_sc[h, pl.ds(start, MOBA_BLOCK), :], preferred_element_type=F32)
            acc_sc[h] = alpha * acc_sc[h] + pv
            m_sc[h] = m_new

    def past(n, carry):
        step(n, None)
        return carry

    lax.fori_loop(0, ti * nsub, past, 0)
    for sub in range(nsub):
        step(ti * nsub + sub, sub)
    a0 = acc_sc[0]
    a1 = acc_sc[1]
    o_ref[0] = jnp.where(lane < HEAD_DIM, a0 / pltpu.roll(a0, HEAD_DIM, 1), a1 / pltpu.roll(a1, HEAD_DIM, 1))


def _moba_prompt(qa, ka, va):
    b, l, hw = qa.shape
    tq = 2 * MOBA_BLOCK if l % (2 * MOBA_BLOCK) == 0 else MOBA_BLOCK
    pairs = hw // LANES
    return pl.pallas_call(
        _moba_prompt_kernel,
        grid=(b, pairs, l // tq),
        in_specs=[
            pl.BlockSpec((1, tq, LANES), lambda i, p, t: (i, t, p)),
            pl.BlockSpec((1, l, LANES), lambda i, p, t: (i, 0, p)),
            pl.BlockSpec((1, l, LANES), lambda i, p, t: (i, 0, p)),
        ],
        out_specs=pl.BlockSpec((1, tq, LANES), lambda i, p, t: (i, t, p)),
        out_shape=jax.ShapeDtypeStruct((b, l, hw), F32),
        scratch_shapes=[
            pltpu.VMEM((l, 2 * LANES), BF16),
            pltpu.VMEM((2, l, LANES), BF16),
            pltpu.VMEM((LANES, LANES), F32),
            pltpu.VMEM((2, tq, 2 * LANES), BF16),
            pltpu.VMEM((2, tq, LANES), F32),
            pltpu.VMEM((2, tq, LANES), F32),
        ],
        compiler_params=_params("arbitrary", "arbitrary", "arbitrary"),
        name="moba_prompt",
    )(qa, ka, va)


def _moba_sample_kernel(pt_ref, q_ref, kn_ref, vn_ref, k0_ref, k1_ref, v0_ref, v1_ref, o_ref,
                        q_sc, m_sc, l_sc, g_sc, acc_sc):
    del pt_ref
    j = pl.program_id(1)
    nb = pl.num_programs(1) - 1
    nq = q_ref.shape[1]
    nh = HEADS_PER_GROUP
    nr = nh * nq
    hw = HEAD_WIDTH
    keys = 2 * k0_ref.shape[2]
    row_w = lax.broadcasted_iota(jnp.int32, (nr, hw), 0)
    lane_w = lax.broadcasted_iota(jnp.int32, (nr, hw), 1)
    own_head = (row_w // nq) == (lane_w // HEAD_DIM)
    lane = lax.broadcasted_iota(jnp.int32, (nr, LANES), 1)

    def to_queries(t):
        return jnp.sum(jnp.where(own_head, t, 0.0).reshape(nh, nq, hw), axis=0)

    @pl.when(j == 0)
    def _():
        qbd = jnp.where(own_head, jnp.concatenate([q_ref[0]] * nh, axis=0), 0.0)
        hi = qbd.astype(BF16)
        q_sc[0:nr] = hi
        q_sc[nr:2 * nr] = (qbd - hi.astype(F32)).astype(BF16)
        m_sc[...] = jnp.zeros_like(m_sc)
        l_sc[...] = jnp.zeros_like(l_sc)
        g_sc[...] = jnp.zeros_like(g_sc)

    def scores(kt):
        s2 = jnp.dot(q_sc[...], kt, preferred_element_type=F32)
        return s2[0:nr] + s2[nr:2 * nr]

    @pl.when(j < nb)
    def _():
        kt = jnp.concatenate([k0_ref[0, 0], k1_ref[0, 0]], axis=1).astype(BF16)
        vt = jnp.concatenate([v0_ref[0, 0], v1_ref[0, 0]], axis=1).astype(BF16)
        s = scores(kt)
        gate = jnp.sum(s, axis=1, keepdims=True) * (1.0 / keys)
        mx = jnp.max(s, axis=1, keepdims=True)
        p = jnp.exp(s - mx)
        den = jnp.sum(p, axis=1, keepdims=True)
        pv = lax.dot_general(p.astype(BF16), vt, NT_DIMS, preferred_element_type=F32)
        acc_sc[j] = to_queries(pv)
        m_sc[...] = jnp.where(lane == j, mx, m_sc[...])
        l_sc[...] = jnp.where(lane == j, den, l_sc[...])
        g_sc[...] = jnp.where(lane == j, gate, g_sc[...])

    @pl.when(j == nb)
    def _():
        cur = jnp.where(lane < nb, g_sc[...], -jnp.inf)
        sel = jnp.zeros((nr, LANES), jnp.bool_)
        for _ in range(MOBA_TOPK):
            top = jnp.max(cur, axis=1, keepdims=True)
            first = jnp.min(jnp.where(cur == top, lane, LANES), axis=1, keepdims=True)
            pick = lane == first
            sel = sel | pick
            cur = jnp.where(pick, -jnp.inf, cur)
        pad = jnp.zeros((nq, hw), F32)
        kn = jnp.concatenate([kn_ref[0], pad], axis=0)
        vn = jnp.concatenate([vn_ref[0], pad], axis=0)
        s2 = lax.dot_general(q_sc[...], kn.astype(BF16), NT_DIMS, preferred_element_type=F32)
        s_own = s2[0:nr] + s2[nr:2 * nr]
        qpos = lax.broadcasted_iota(jnp.int32, (nr, 2 * nq), 0) % nq
        kpos = lax.broadcasted_iota(jnp.int32, (nr, 2 * nq), 1)
        s_own = jnp.where(kpos <= qpos, s_own, NEG)
        m_own = jnp.max(s_own, axis=1, keepdims=True)
        p_own = jnp.exp(s_own - m_own)
        l_own = jnp.sum(p_own, axis=1, keepdims=True)
        acc_own = to_queries(jnp.dot(p_own, vn, preferred_element_type=F32))
        m_all = m_sc[...]
        m_tot = jnp.maximum(jnp.max(jnp.where(sel, m_all, NEG), axis=1, keepdims=True), m_own)
        w = jnp.where(sel, jnp.exp(m_all - m_tot), 0.0)
        w_own = jnp.exp(m_own - m_tot)
        l_tot = jnp.sum(w * l_sc[...], axis=1, keepdims=True) + w_own * l_own

        def widen(col):
            return to_queries(jnp.broadcast_to(col, (nr, hw)))

        num = widen(w_own) * acc_own
        for n in range(nb):
            num = num + widen(w[:, n:n + 1]) * acc_sc[n]
        o_ref[0] = num / widen(l_tot)


def _moba_sample(qa, ka, va, cache_k, cache_v, page_table):
    b, nq, hw = qa.shape
    nh = HEADS_PER_GROUP
    n_pool, page = cache_k.shape[0], cache_k.shape[1]
    pages_per_blk = MOBA_BLOCK // page
    assert pages_per_blk == 2 and page == LANES
    nb = page_table.shape[1] // pages_per_blk
    assert MOBA_TOPK <= nb <= LANES
    ck = cache_k.transpose(0, 2, 3, 1).reshape(1, n_pool, hw, page)
    cv = cache_v.transpose(0, 2, 3, 1).reshape(1, n_pool, hw, page)
    tok = lambda i, j, pt: (i, 0, 0)

    def page_spec(which):
        return pl.BlockSpec((1, 1, hw, page),
                            lambda i, j, pt: (0, pt[i, jnp.minimum(j, nb - 1) * 2 + which], 0, 0))

    grid_spec = pltpu.PrefetchScalarGridSpec(
        num_scalar_prefetch=1,
        grid=(b, nb + 1),
        in_specs=[
            pl.BlockSpec((1, nq, hw), tok),
            pl.BlockSpec((1, nq, hw), tok),
            pl.BlockSpec((1, nq, hw), tok),
            page_spec(0), page_spec(1), page_spec(0), page_spec(1),
        ],
        out_specs=pl.BlockSpec((1, nq, hw), tok),
        scratch_shapes=[
            pltpu.VMEM((2 * nh * nq, hw), BF16),
            pltpu.VMEM((nh * nq, LANES), F32),
            pltpu.VMEM((nh * nq, LANES), F32),
            pltpu.VMEM((nh * nq, LANES), F32),
            pltpu.VMEM((nb, nq, hw), F32),
        ],
    )
    return pl.pallas_call(
        _moba_sample_kernel,
        grid_spec=grid_spec,
        out_shape=jax.ShapeDtypeStruct((b, nq, hw), F32),
        compiler_params=_params("arbitrary", "arbitrary"),
        name="moba_sample",
    )(page_table, qa, ka, va, ck, ck, cv, cv)


def _gla_kernel(q_ref, k_ref, v_ref, la_ref, gs_ref, gn_ref, s0_ref, o_ref, sout_ref, st_sc, *, chunk, mm_dtype):
    c = chunk
    n = q_ref.shape[1] // c
    hd = HEAD_DIM
    lane = lax.broadcasted_iota(jnp.int32, (c, LANES), 1)
    head0 = lane < hd
    r = lax.broadcasted_iota(jnp.int32, (c, c), 0)
    cc = lax.broadcasted_iota(jnp.int32, (c, c), 1)
    causal = r >= cc
    tri = jnp.where(causal, 1.0, 0.0)
    br = lax.broadcasted_iota(jnp.int32, (LANES, LANES), 0) // hd
    bc = lax.broadcasted_iota(jnp.int32, (LANES, LANES), 1) // hd
    same_head = br == bc
    head_mean = jnp.where(same_head, 1.0 / hd, 0.0)
    gn = gn_ref[...]

    s0 = s0_ref[0]
    zero = jnp.zeros((hd, hd), F32)
    st_sc[...] = jnp.concatenate(
        [jnp.concatenate([s0[0], zero], axis=1), jnp.concatenate([zero, s0[1]], axis=1)], axis=0)

    def body(ci, carry):
        rows = pl.ds(pl.multiple_of(ci * c, c), c)
        la = la_ref[0, rows, :]
        q = q_ref[0, rows, :]
        k = k_ref[0, rows, :]
        v = v_ref[0, rows, :]
        st = st_sc[...]
        b = jnp.dot(tri, la, precision=HIGHEST, preferred_element_type=F32)
        b_last = b[c - 1:c, :]
        b_mid = b[c // 2 - 1:c // 2, :]
        qt = q * jnp.exp(b - b_mid)
        kt = k * jnp.exp(b_mid - b)
        qts = jnp.concatenate([jnp.where(head0, qt, 0.0), jnp.where(head0, 0.0, qt)], axis=0)
        a = lax.dot_general(qts.astype(mm_dtype), kt.astype(mm_dtype), NT_DIMS, preferred_element_type=F32)
        a0 = jnp.where(causal, a[0:c], 0.0).astype(mm_dtype)
        a1 = jnp.where(causal, a[c:2 * c], 0.0).astype(mm_dtype)
        vm = v.astype(mm_dtype)
        o = jnp.where(head0, jnp.dot(a0, vm, preferred_element_type=F32),
                      jnp.dot(a1, vm, preferred_element_type=F32))
        qe = q * jnp.exp(b)
        o = o + lax.dot_general(qe.astype(mm_dtype), st.astype(mm_dtype), NT_DIMS, preferred_element_type=F32)
        kd = k * jnp.exp(b_last - b)
        upd = lax.dot_general(vm, kd.astype(mm_dtype), TN_DIMS, preferred_element_type=F32)
        st_sc[...] = st * jnp.exp(b_last) + jnp.where(same_head, upd, 0.0)
        ms = jnp.dot(o * o, head_mean, precision=HIGHEST, preferred_element_type=F32)
        o_ref[0, rows, :] = o * lax.rsqrt(ms + EPS) * gn * gs_ref[0, rows, :]
        return carry

    lax.fori_loop(0, n, body, 0)
    st = st_sc[...]
    sout_ref[0, 0] = st[0:hd, 0:hd]
    sout_ref[0, 1] = st[hd:2 * hd, hd:2 * hd]


def _gla(qg, kg, vg, la, gs, gn_pair, s0_t, chunk, mm_dtype):
    b, l, hw = qg.shape
    pairs = hw // LANES
    seq = pl.BlockSpec((1, l, LANES), lambda i, p: (i, 0, p))
    st_spec = pl.BlockSpec((1, 2, HEAD_DIM, HEAD_DIM), lambda i, p: (i, p, 0, 0))
    return pl.pallas_call(
        functools.partial(_gla_kernel, chunk=chunk, mm_dtype=mm_dtype),
        grid=(b, pairs),
        in_specs=[seq, seq, seq, seq, seq, pl.BlockSpec((1, LANES), lambda i, p: (0, 0)), st_spec],
        out_specs=[seq, st_spec],
        out_shape=[jax.ShapeDtypeStruct((b, l, hw), F32),
                   jax.ShapeDtypeStruct((b, HEADS_PER_GROUP, HEAD_DIM, HEAD_DIM), F32)],
        scratch_shapes=[pltpu.VMEM((LANES, LANES), F32)],
        compiler_params=_params("arbitrary", "arbitrary"),
        name="gla_chunked",
    )(qg, kg, vg, la, gs, gn_pair, s0_t)


def _mix_ffn_kernel(x_ref, oa_ref, og_ref, mod_ref, g_ref, wo_a_ref, wo_g_ref, wg_ref, wu_ref, wd_ref, o_ref,
                    x1_sc, h_sc, acc_sc):
    j = pl.program_id(2)
    bb, tl, d = x_ref.shape
    tm = bb * tl

    @pl.when(j == 0)
    def _():
        m = mod_ref[...]
        oa = oa_ref[...].reshape(tm, -1).astype(BF16)
        og = og_ref[...].reshape(tm, -1).astype(BF16)
        mix = (jnp.dot(oa, wo_a_ref[...], preferred_element_type=F32)
               + jnp.dot(og, wo_g_ref[...], preferred_element_type=F32))
        x1 = x_ref[...] + m[:, 2:3, :] * mix.reshape(bb, tl, d)
        h = _rms(x1, g_ref[...]) * (1.0 + m[:, 4:5, :]) + m[:, 3:4, :]
        x1_sc[...] = x1.reshape(tm, d)
        h_sc[...] = h.reshape(tm, d).astype(BF16)
        acc_sc[...] = jnp.zeros_like(acc_sc)

    h = h_sc[...]
    gate = jnp.dot(h, wg_ref[...], preferred_element_type=F32)
    up = jnp.dot(h, wu_ref[...], preferred_element_type=F32)
    act = (_silu(gate) * up).astype(BF16)
    acc_sc[...] += jnp.dot(act, wd_ref[...], preferred_element_type=F32)

    @pl.when(j == pl.num_programs(2) - 1)
    def _():
        g2 = mod_ref[...][:, 5:6, :]
        o_ref[...] = x1_sc[...].reshape(bb, tl, d) + g2 * acc_sc[...].reshape(bb, tl, d)


def _mix_ffn(x, oa, og, mod, g, wo_a, wo_g, wg, wu, wd):
    b, l, d = x.shape
    hw = oa.shape[-1]
    f = wg.shape[1]
    bb, tl = _token_tiles(b, l, 512 if l >= 512 else 256)
    tm = bb * tl
    fc = f // 2
    tok = lambda i, t, j: (i, t, 0)
    const2 = lambda i, t, j: (0, 0)
    return pl.pallas_call(
        _mix_ffn_kernel,
        grid=(b // bb, l // tl, f // fc),
        in_specs=[
            pl.BlockSpec((bb, tl, d), tok),
            pl.BlockSpec((bb, tl, hw), tok),
            pl.BlockSpec((bb, tl, hw), tok),
            pl.BlockSpec((bb, 6, d), lambda i, t, j: (i, 0, 0)),
            pl.BlockSpec((1, d), const2),
            pl.BlockSpec(wo_a.shape, const2),
            pl.BlockSpec(wo_g.shape, const2),
            pl.BlockSpec((d, fc), lambda i, t, j: (0, j)),
            pl.BlockSpec((d, fc), lambda i, t, j: (0, j)),
            pl.BlockSpec((fc, d), lambda i, t, j: (j, 0)),
        ],
        out_specs=pl.BlockSpec((bb, tl, d), tok),
        out_shape=jax.ShapeDtypeStruct((b, l, d), F32),
        scratch_shapes=[pltpu.VMEM((tm, d), F32), pltpu.VMEM((tm, d), BF16), pltpu.VMEM((tm, d), F32)],
        compiler_params=_params("arbitrary", "arbitrary", "arbitrary"),
        name="outproj_ffn",
    )(x, oa, og, mod, g, wo_a, wo_g, wg, wu, wd)


def _cm_in_kernel(x_ref, mod_ref, g_ref, w_ref, lng_ref, lnb_ref, u_ref, v_ref):
    bb, tl, d = x_ref.shape
    tm = bb * tl
    cw = u_ref.shape[-1]
    m = mod_ref[...]
    h = _rms(x_ref[...], g_ref[...]) * (1.0 + m[:, 1:2, :]) + m[:, 0:1, :]
    hb = h.reshape(tm, d).astype(BF16)
    u_ref[...] = _gelu_tanh(jnp.dot(hb, w_ref[:, 0:cw], preferred_element_type=F32)).reshape(bb, tl, cw)
    zv = _gelu_tanh(jnp.dot(hb, w_ref[:, cw:2 * cw], preferred_element_type=F32))
    mu = jnp.mean(zv, axis=-1, keepdims=True)
    zc = zv - mu
    var = jnp.mean(zc * zc, axis=-1, keepdims=True)
    v_ref[...] = (zc * lax.rsqrt(var + EPS) * lng_ref[...] + lnb_ref[...]).reshape(bb, tl, cw)


def _cm_in(x, mod, g, w, ln_g, ln_b):
    b, l, d = x.shape
    cw = w.shape[1] // 2
    bb, tl = _token_tiles(b, l, 256)
    tok = lambda i, t: (i, t, 0)
    const2 = lambda i, t: (0, 0)
    out_spec = pl.BlockSpec((bb, tl, cw), tok)
    out_sds = jax.ShapeDtypeStruct((b, l, cw), F32)
    return pl.pallas_call(
        _cm_in_kernel,
        grid=(b // bb, l // tl),
        in_specs=[
            pl.BlockSpec((bb, tl, d), tok),
            pl.BlockSpec((bb, 6, d), lambda i, t: (i, 0, 0)),
            pl.BlockSpec((1, d), const2),
            pl.BlockSpec(w.shape, const2),
            pl.BlockSpec((1, cw), const2),
            pl.BlockSpec((1, cw), const2),
        ],
        out_specs=[out_spec, out_spec],
        out_shape=[out_sds, out_sds],
        compiler_params=_params("arbitrary", "arbitrary"),
        name="cm_in_gelu_ln",
    )(x, mod, g, w, ln_g, ln_b)


def _cm_out_kernel(x_ref, u_ref, v_ref, mod_ref, g_ref, ws_ref, bs_ref, wo_ref, wr_ref,
                   x3_ref, h_ref, gates_ref, us_sc):
    bb, tl, d = x_ref.shape
    tm = bb * tl
    cw = u_ref.shape[-1]
    ngroups, cm, _ = ws_ref.shape
    gd = cw // ngroups
    u = u_ref[...].reshape(tm, cw)
    v = v_ref[...].reshape(tm, cw)
    bs = bs_ref[...]
    r = lax.broadcasted_iota(jnp.int32, (cm, cm), 0)
    c = lax.broadcasted_iota(jnp.int32, (cm, cm), 1)
    for g in range(ngroups):
        wsg = jnp.where(r >= c, ws_ref[g], 0.0).astype(BF16)
        cols = slice(g * gd, (g + 1) * gd)
        for sb in range(tm // cm):
            rows = slice(sb * cm, (sb + 1) * cm)
            s = jnp.dot(wsg, v[rows, cols].astype(BF16), preferred_element_type=F32) + bs[rows, g:g + 1]
            us_sc[rows, cols] = (u[rows, cols] * s).astype(BF16)
    y = jnp.dot(us_sc[...], wo_ref[...], preferred_element_type=F32)
    m = mod_ref[...]
    x3 = x_ref[...] + m[:, 2:3, :] * y.reshape(bb, tl, d)
    x3_ref[...] = x3
    h = (_rms(x3, g_ref[...]) * (1.0 + m[:, 4:5, :]) + m[:, 3:4, :]).reshape(tm, d)
    h_ref[...] = h.astype(BF16)
    logits = jnp.dot(h, wr_ref[...], precision=HIGHEST, preferred_element_type=F32)
    ne = logits.shape[1]
    col = lax.broadcasted_iota(jnp.int32, (tm, ne), 1)
    m1 = jnp.max(logits, axis=1, keepdims=True)
    i1 = jnp.min(jnp.where(logits == m1, col, ne), axis=1, keepdims=True)
    rest = jnp.where(col == i1, -jnp.inf, logits)
    m2 = jnp.max(rest, axis=1, keepdims=True)
    i2 = jnp.min(jnp.where(rest == m2, col, ne), axis=1, keepdims=True)
    e2 = jnp.exp(m2 - m1)
    gates_ref[...] = jnp.where(col == i1, 1.0 / (1.0 + e2), 0.0) + jnp.where(col == i2, e2 / (1.0 + e2), 0.0)


def _cm_out(x, u, v, mod, g, ws_mix, bs_rows, wo, wr, cm):
    b, l, d = x.shape
    cw = u.shape[-1]
    ne = wr.shape[1]
    bb, tl = _token_tiles(b, l, 256)
    tm = bb * tl
    tok = lambda i, t: (i, t, 0)
    flat = lambda i, t: (i * (l // tl) + t, 0)
    const2 = lambda i, t: (0, 0)
    return pl.pallas_call(
        _cm_out_kernel,
        grid=(b // bb, l // tl),
        in_specs=[
            pl.BlockSpec((bb, tl, d), tok),
            pl.BlockSpec((bb, tl, cw), tok),
            pl.BlockSpec((bb, tl, cw), tok),
            pl.BlockSpec((bb, 6, d), lambda i, t: (i, 0, 0)),
            pl.BlockSpec((1, d), const2),
            pl.BlockSpec(ws_mix.shape, lambda i, t: (0, 0, 0)),
            pl.BlockSpec((tm, bs_rows.shape[1]), const2),
            pl.BlockSpec(wo.shape, const2),
            pl.BlockSpec(wr.shape, const2),
        ],
        out_specs=[pl.BlockSpec((bb, tl, d), tok), pl.BlockSpec((tm, d), flat), pl.BlockSpec((tm, ne), flat)],
        out_shape=[jax.ShapeDtypeStruct((b, l, d), F32), jax.ShapeDtypeStruct((b * l, d), BF16),
                   jax.ShapeDtypeStruct((b * l, ne), F32)],
        scratch_shapes=[pltpu.VMEM((tm, cw), BF16)],
        compiler_params=_params("arbitrary", "arbitrary"),
        name="cm_out_router",
    )(x, u, v, mod, g, ws_mix, bs_rows, wo, wr)


def _moe_kernel(h_ref, gates_ref, wg_ref, wu_ref, wd_ref, o_ref):
    e = pl.program_id(1)
    j = pl.program_id(2)

    @pl.when((e == 0) & (j == 0))
    def _():
        o_ref[...] = jnp.zeros_like(o_ref)

    h = h_ref[...]
    gates = gates_ref[...]
    col = lax.broadcasted_iota(jnp.int32, gates.shape, 1)
    ge = jnp.sum(jnp.where(col == e, gates, 0.0), axis=1, keepdims=True)
    gate = jnp.dot(h, wg_ref[0], preferred_element_type=F32)
    up = jnp.dot(h, wu_ref[0], preferred_element_type=F32)
    act = (_silu(gate) * up * ge).astype(BF16)
    o_ref[...] += jnp.dot(act, wd_ref[0], preferred_element_type=F32)


def _moe(h, gates, wg, wu, wd, tm):
    t, d = h.shape
    ne, _, f = wg.shape
    fc = 512
    assert t % tm == 0 and f % fc == 0
    return pl.pallas_call(
        _moe_kernel,
        grid=(t // tm, ne, f // fc),
        in_specs=[
            pl.BlockSpec((tm, d), lambda i, e, j: (i, 0)),
            pl.BlockSpec((tm, ne), lambda i, e, j: (i, 0)),
            pl.BlockSpec((1, d, fc), lambda i, e, j: (e, 0, j)),
            pl.BlockSpec((1, d, fc), lambda i, e, j: (e, 0, j)),
            pl.BlockSpec((1, fc, d), lambda i, e, j: (e, j, 0)),
        ],
        out_specs=pl.BlockSpec((tm, d), lambda i, e, j: (i, 0)),
        out_shape=jax.ShapeDtypeStruct((t, d), F32),
        compiler_params=_params("arbitrary", "arbitrary", "arbitrary"),
        name="moe_experts",
    )(h, gates, wg, wu, wd)


def _final_kernel(x_ref, moe_ref, mod_ref, g_ref, o_ref):
    bb, tl, d = x_ref.shape
    g2 = mod_ref[...][:, 5:6, :]
    x4 = x_ref[...] + g2 * moe_ref[...].reshape(bb, tl, d)
    o_ref[...] = _rms(x4, g_ref[...])


def _final(x, moe, row0, mod, g):
    b, l, d = x.shape
    bb, tl = _token_tiles(b, l, 512 if l >= 512 else 256)
    tm = bb * tl
    assert row0 % tm == 0
    tok = lambda i, t: (i, t, 0)
    return pl.pallas_call(
        _final_kernel,
        grid=(b // bb, l // tl),
        in_specs=[
            pl.BlockSpec((bb, tl, d), tok),
            pl.BlockSpec((tm, d), lambda i, t: (row0 // tm + i * (l // tl) + t, 0)),
            pl.BlockSpec((bb, 6, d), lambda i, t: (i, 0, 0)),
            pl.BlockSpec((1, d), lambda i, t: (0, 0)),
        ],
        out_specs=pl.BlockSpec((bb, tl, d), tok),
        out_shape=jax.ShapeDtypeStruct((b, l, d), F32),
        compiler_params=_params("arbitrary", "arbitrary"),
        name="final_norm",
    )(x, moe, mod, g)


def _layer0(x, mod, pos0, past, w):
    b, l, d = x.shape
    hw = HEAD_WIDTH
    outs = _in_proj(x, mod, w["norm_mix_g0"], w["w_in_main"], w["w_in_lr"], w["wa2_hi"], w["wa2_lo"], w["ba2"], pos0)
    qa, ka, va, qg, kg, vg, gs, la = outs
    if past is None:
        oa = _moba_prompt(qa, ka, va)
        s0_t = jnp.zeros((b, HEADS_PER_GROUP, HEAD_DIM, HEAD_DIM), F32)
    else:
        cache_k, cache_v, state, page_table = past
        oa = _moba_sample(qa, ka, va, cache_k, cache_v, page_table)
        s0_t = jnp.swapaxes(state, -1, -2)
    chunk = GLA_CHUNK if l % GLA_CHUNK == 0 else l
    mm_dtype = BF16 if chunk % 16 == 0 else F32
    og, s_t = _gla(qg, kg, vg, la, gs, w["gla_norm_pair"], s0_t, chunk, mm_dtype)
    x2 = _mix_ffn(x, oa, og, mod, w["norm_ffn_g0"], w["wo_a"], w["wo_g"], w["ffn_wg"], w["ffn_wu"], w["ffn_wd"])
    k_rows = ka.reshape(b, l, HEADS_PER_GROUP, HEAD_DIM)
    v_rows = va.reshape(b, l, HEADS_PER_GROUP, HEAD_DIM)
    return x2, k_rows, v_rows, jnp.swapaxes(s_t, -1, -2)


def _layer1_front(x, mod, w):
    b, l, d = x.shape
    u, v = _cm_in(x, mod, w["norm_mix_g1"], w["cm_w_in"], w["cm_ln_g"], w["cm_ln_b"])
    bb, tl = _token_tiles(b, l, 256)
    tm = bb * tl
    if l >= CM_CHUNK:
        cm = CM_CHUNK
        ws_mix = w["cm_w_s"]
        bs_rows = jnp.tile(w["cm_b_s"].T, (tm // CM_CHUNK, 1))
    else:
        cm = tm
        eye = jnp.eye(tm // l, dtype=F32)
        ws_mix = jax.vmap(lambda m_: jnp.kron(eye, m_))(w["cm_w_s"][:, :l, :l])
        bs_rows = jnp.tile(w["cm_b_s"][:, :l].T, (tm // l, 1))
    x3, h, gates = _cm_out(x, u, v, mod, w["norm_ffn_g1"], ws_mix, bs_rows, w["cm_w_out"], w["moe_w_router"], cm)
    start = ((l - 1) // CM_CHUNK) * CM_CHUNK
    return x3, h, gates, v[:, start:]


def kernel(x_prompt, x_sample, cache_k, cache_v, state_gla, page_table, c_prompt, c_sample, ada_w, ada_b, norm_mix_g, norm_ffn_g, ab_w_in, gla_w_a2, gla_b_a2, gla_norm_g, ab_w_out, ffn_w_gate, ffn_w_up, ffn_w_down, cm_w_in, cm_ln_g, cm_ln_b, cm_w_s, cm_b_s, cm_w_out, moe_w_router, moe_w_gate, moe_w_up, moe_w_down, final_norm_g):
    bp, lp, d = x_prompt.shape
    bs, ls, _ = x_sample.shape
    hw = HEAD_WIDTH
    n_main = 7 * hw
    past_len = page_table.shape[1] * cache_k.shape[2]

    wa2 = jnp.pad(gla_w_a2[0], ((0, LANES - GLA_LOWRANK), (0, 0)))
    wa2_hi = wa2.astype(BF16)
    w = {
        "norm_mix_g0": norm_mix_g[0][None], "norm_mix_g1": norm_mix_g[1][None],
        "norm_ffn_g0": norm_ffn_g[0][None], "norm_ffn_g1": norm_ffn_g[1][None],
        "w_in_main": ab_w_in[0][:, :n_main].astype(BF16),
        "w_in_lr": jnp.pad(ab_w_in[0][:, n_main:], ((0, 0), (0, LANES - GLA_LOWRANK))).astype(BF16),
        "wa2_hi": wa2_hi, "wa2_lo": (wa2 - wa2_hi.astype(F32)).astype(BF16),
        "ba2": gla_b_a2[0][None],
        "gla_norm_pair": jnp.tile(gla_norm_g[0], LANES // HEAD_DIM)[None],
        "wo_a": ab_w_out[0][:hw].astype(BF16), "wo_g": ab_w_out[0][hw:].astype(BF16),
        "ffn_wg": ffn_w_gate[0].astype(BF16), "ffn_wu": ffn_w_up[0].astype(BF16), "ffn_wd": ffn_w_down[0].astype(BF16),
        "cm_w_in": cm_w_in[0].astype(BF16), "cm_ln_g": cm_ln_g[0][None], "cm_ln_b": cm_ln_b[0][None],
        "cm_w_s": cm_w_s[0], "cm_b_s": cm_b_s[0], "cm_w_out": cm_w_out[0].astype(BF16),
        "moe_w_router": moe_w_router[0],
    }

    rows = bp + bs
    rows_pad = -(-rows // 16) * 16
    c_all = jnp.pad(jnp.concatenate([c_prompt, c_sample], axis=0), ((0, rows_pad - rows), (0, 0)))
    mod_all = _modulation(c_all, ada_w, ada_b)
    mod_p = [mod_all[l, :bp].reshape(bp, 6, d) for l in range(2)]
    mod_s = [mod_all[l, bp:rows].reshape(bs, 6, d) for l in range(2)]

    x2_p, k_p, v_p, gla_p = _layer0(x_prompt, mod_p[0], 0, None, w)
    x2_s, k_s, v_s, gla_s = _layer0(x_sample, mod_s[0], past_len,
                                    (cache_k[0], cache_v[0], state_gla[0], page_table), w)

    x3_p, h_p, gates_p, cmv_p = _layer1_front(x2_p, mod_p[1], w)
    x3_s, h_s, gates_s, cmv_s = _layer1_front(x2_s, mod_s[1], w)

    h_all = jnp.concatenate([h_p, h_s], axis=0)
    gates_all = jnp.concatenate([gates_p, gates_s], axis=0)
    t_all = h_all.shape[0]
    tm_moe = next(c for c in (1280, 1024, 512, 256, 128, 64, 32, 16) if t_all % c == 0)
    moe = _moe(h_all, gates_all, moe_w_gate[0].astype(BF16), moe_w_up[0].astype(BF16),
               moe_w_down[0].astype(BF16), tm_moe)

    fg = final_norm_g[None]
    y_p = _final(x3_p, moe, 0, mod_p[1], fg)
    y_s = _final(x3_s, moe, bp * lp, mod_s[1], fg)

    return (y_p, y_s, k_p[None], v_p[None], k_s[None], v_s[None], gla_p[None], gla_s[None],
            cmv_p[None], cmv_s[None])
```

```python
import functools

import jax
import jax.numpy as jnp
from jax import lax
from jax.experimental import pallas as pl
from jax.experimental.pallas import tpu as pltpu

F32 = jnp.float32
BF16 = jnp.bfloat16
HIGHEST = lax.Precision.HIGHEST

HEAD_DIM = 64
HEADS_PER_GROUP = 8
HEAD_WIDTH = HEADS_PER_GROUP * HEAD_DIM
MOBA_BLOCK = 256
MOBA_TOPK = 3
ROPE_THETA = 10000.0
GLA_LOWRANK = 16
GLA_GATE_NORM = 16.0
GLA_CHUNK = 64
CM_CHUNK = 128
CM_GROUPS = 8
MOE_TOP_K = 2
EPS = 1e-6
NEG = -1e30

LANES = 128
VMEM_LIMIT_BYTES = 56 * 1024 * 1024

NT_DIMS = (((1,), (1,)), ((), ()))
TN_DIMS = (((0,), (0,)), ((), ()))


def _params(*sem):
    return pltpu.CompilerParams(dimension_semantics=sem, vmem_limit_bytes=VMEM_LIMIT_BYTES)


def _rms(x, g):
    return x * lax.rsqrt(jnp.mean(x * x, axis=-1, keepdims=True) + EPS) * g


def _silu(x):
    return x * jax.nn.sigmoid(x)


def _mm(a, w):
    if w.dtype == F32:
        return jnp.dot(a.astype(F32), w, precision=HIGHEST, preferred_element_type=F32)
    return jnp.dot(a.astype(BF16), w, preferred_element_type=F32)


def _gelu_tanh(x):
    return 0.5 * x * (1.0 + jnp.tanh(0.7978845608028654 * (x + 0.044715 * (x * x * x))))


def _token_tiles(b, l, target):
    if l >= target:
        assert l % target == 0
        return 1, target
    bb = min(b, target // l)
    assert target % l == 0 and b % bb == 0
    return bb, l


def _mod_kernel(c_ref, w_ref, b_ref, o_ref):
    s = _silu(c_ref[...])
    r = s.shape[0]
    s_hi = s.astype(BF16)
    s_lo = (s - s_hi.astype(F32)).astype(BF16)
    w = w_ref[0]
    w_hi = w.astype(BF16)
    w_lo = (w - w_hi.astype(F32)).astype(BF16)
    both = jnp.dot(jnp.concatenate([s_hi, s_lo], axis=0), w_hi, preferred_element_type=F32)
    o_ref[0] = both[0:r] + both[r:2 * r] + jnp.dot(s_hi, w_lo, preferred_element_type=F32) + b_ref[0]


def _modulation(c_all, ada_w, ada_b):
    r, d = c_all.shape
    depth, _, n = ada_w.shape
    tn = 1536
    return pl.pallas_call(
        _mod_kernel,
        grid=(depth, n // tn),
        in_specs=[
            pl.BlockSpec((r, d), lambda l, j: (0, 0)),
            pl.BlockSpec((1, d, tn), lambda l, j: (l, 0, j)),
            pl.BlockSpec((1, 1, tn), lambda l, j: (l, 0, j)),
        ],
        out_specs=pl.BlockSpec((1, r, tn), lambda l, j: (l, 0, j)),
        out_shape=jax.ShapeDtypeStruct((depth, r, n), F32),
        compiler_params=_params("arbitrary", "arbitrary"),
        name="adaln_modulation",
    )(c_all, ada_w, ada_b.reshape(depth, 1, n))


def _in_proj_kernel(x_ref, mod_ref, g_ref, w_ref, wlr_ref, wa2h_ref, wa2l_ref, ba2_ref, cos_ref, sin_ref,
                    qa_ref, ka_ref, va_ref, qg_ref, kg_ref, vg_ref, gs_ref, la_ref):
    bb, tl, d = x_ref.shape
    tm = bb * tl
    hw = HEAD_WIDTH
    m = mod_ref[...]
    h = _rms(x_ref[...], g_ref[...]) * (1.0 + m[:, 1:2, :]) + m[:, 0:1, :]
    hb = h.reshape(tm, d).astype(w_ref.dtype)

    def proj(c):
        return _mm(hb, w_ref[:, c * hw:(c + 1) * hw])

    cos = jnp.concatenate([cos_ref[...]] * (hw // LANES), axis=-1)[None]
    sin = jnp.concatenate([sin_ref[...]] * (hw // LANES), axis=-1)[None]
    lane = lax.broadcasted_iota(jnp.int32, (tm, hw), 1)
    first_half = (lane % HEAD_DIM) < (HEAD_DIM // 2)

    def rope(z):
        partner = jnp.where(first_half, pltpu.roll(z, hw - HEAD_DIM // 2, 1), pltpu.roll(z, HEAD_DIM // 2, 1))
        return z.reshape(bb, tl, hw) * cos + partner.reshape(bb, tl, hw) * sin

    scale = HEAD_DIM ** -0.5
    qa_ref[...] = rope(proj(0)) * scale
    ka_ref[...] = rope(proj(1))
    va_ref[...] = proj(2).reshape(bb, tl, hw)
    qg_ref[...] = (proj(3) * scale).reshape(bb, tl, hw)
    kg_ref[...] = proj(4).reshape(bb, tl, hw)
    vg_ref[...] = proj(5).reshape(bb, tl, hw)
    gs_ref[...] = _silu(proj(6)).reshape(bb, tl, hw)
    zlr = _mm(hb, wlr_ref[...])
    hi = zlr.astype(BF16)
    lo = (zlr - hi.astype(F32)).astype(BF16)
    pre = (jnp.dot(hi, wa2h_ref[...], preferred_element_type=F32)
           + jnp.dot(lo, wa2h_ref[...], preferred_element_type=F32)
           + jnp.dot(hi, wa2l_ref[...], preferred_element_type=F32)) + ba2_ref[...]
    log_sig = jnp.minimum(pre, 0.0) - jnp.log(1.0 + jnp.exp(-jnp.abs(pre)))
    la_ref[...] = (log_sig * (1.0 / GLA_GATE_NORM)).reshape(bb, tl, hw)


def _rope_tables(pos0, l):
    half = HEAD_DIM // 2
    inv = 1.0 / (ROPE_THETA ** (jnp.arange(half, dtype=F32) / half))
    ang = (pos0 + jnp.arange(l, dtype=jnp.int32)).astype(F32)[:, None] * inv[None, :]
    cos, sin = jnp.cos(ang), jnp.sin(ang)
    reps = LANES // HEAD_DIM
    cos_t = jnp.tile(jnp.concatenate([cos, cos], axis=-1), (1, reps))
    sin_t = jnp.tile(jnp.concatenate([-sin, sin], axis=-1), (1, reps))
    return cos_t, sin_t


def _in_proj(x, mod, g, w_main, w_lr, wa2_hi, wa2_lo, ba2, pos0):
    b, l, d = x.shape
    bb, tl = _token_tiles(b, l, 512 if l >= 512 else 256)
    hw = HEAD_WIDTH
    cos_t, sin_t = _rope_tables(pos0, l)
    tok = lambda i, j: (i, j, 0)
    const2 = lambda i, j: (0, 0)
    out_spec = pl.BlockSpec((bb, tl, hw), tok)
    out_sds = jax.ShapeDtypeStruct((b, l, hw), F32)
    return pl.pallas_call(
        _in_proj_kernel,
        grid=(b // bb, l // tl),
        in_specs=[
            pl.BlockSpec((bb, tl, d), tok),
            pl.BlockSpec((bb, 6, d), lambda i, j: (i, 0, 0)),
            pl.BlockSpec((1, d), const2),
            pl.BlockSpec(w_main.shape, const2),
            pl.BlockSpec(w_lr.shape, const2),
            pl.BlockSpec(wa2_hi.shape, const2),
            pl.BlockSpec(wa2_lo.shape, const2),
            pl.BlockSpec((1, hw), const2),
            pl.BlockSpec((tl, LANES), lambda i, j: (j, 0)),
            pl.BlockSpec((tl, LANES), lambda i, j: (j, 0)),
        ],
        out_specs=[out_spec] * 8,
        out_shape=[out_sds] * 8,
        compiler_params=_params("arbitrary", "arbitrary"),
        name="in_proj_rope",
    )(x, mod, g, w_main, w_lr, wa2_hi, wa2_lo, ba2, cos_t, sin_t)


def _moba_prompt_kernel(q_ref, k_ref, v_ref, o_ref, kb_sc, vb_sc, km_sc, qs_sc, m_sc, acc_sc):
    ti = pl.program_id(2)
    lk = k_ref.shape[1]
    nblk = lk // MOBA_BLOCK
    nblk8 = -(-nblk // 8) * 8
    tq = q_ref.shape[1]
    nsub = tq // MOBA_BLOCK

    @pl.when(ti == 0)
    def _():
        k = k_ref[0]
        v = v_ref[0]
        row = lax.broadcasted_iota(jnp.int32, (lk, LANES), 0)
        col = lax.broadcasted_iota(jnp.int32, (lk, LANES), 1)
        kb_sc[:, 0:LANES] = k.astype(BF16)
        kb_sc[:, LANES:2 * LANES] = jnp.where(row // MOBA_BLOCK == col, 1.0, 0.0).astype(BF16)
        vb_sc[0] = jnp.where(col < HEAD_DIM, v, 1.0).astype(BF16)
        vb_sc[1] = jnp.where(col >= HEAD_DIM, v, 1.0).astype(BF16)
        km_sc[...] = jnp.zeros_like(km_sc)
        km_sc[0:nblk, :] = jnp.mean(k.reshape(nblk, MOBA_BLOCK, LANES), axis=1)

    q = q_ref[0]
    lane = lax.broadcasted_iota(jnp.int32, (tq, LANES), 1)
    blk = lax.broadcasted_iota(jnp.int32, (nblk8, tq), 0)
    own = ti * nsub + lax.broadcasted_iota(jnp.int32, (nblk8, tq), 1) // MOBA_BLOCK
    past_blk = blk < own
    km = km_sc[...]
    for h in range(2):
        head_lanes = (lane < HEAD_DIM) if h == 0 else (lane >= HEAD_DIM)
        qh = jnp.where(head_lanes, q, 0.0)
        gate = lax.dot_general(km, qh, NT_DIMS, precision=HIGHEST, preferred_element_type=F32)[0:nblk8]
        gate = jnp.where(past_blk, gate, -jnp.inf)
        cnt = jnp.zeros((nblk8, tq), F32)
        for mrow in range(nblk):
            gm = gate[mrow:mrow + 1, :]
            beats = (gm > gate) | ((gm == gate) & (mrow < blk))
            cnt = cnt + jnp.where(beats, 1.0, 0.0)
        sel = (past_blk & (cnt < float(MOBA_TOPK))) | (blk == own)
        bias_t = jnp.concatenate(
            [jnp.where(sel, 0.0, NEG), jnp.full((LANES - nblk8, tq), NEG, F32)], axis=0)
        qs_sc[h, :, 0:LANES] = qh.astype(BF16)
        qs_sc[h, :, LANES:2 * LANES] = bias_t.T.astype(BF16)

    m_sc[...] = jnp.full_like(m_sc, NEG)
    acc_sc[...] = jnp.zeros_like(acc_sc)
    rowi = lax.broadcasted_iota(jnp.int32, (tq, MOBA_BLOCK), 0)
    coli = lax.broadcasted_iota(jnp.int32, (tq, MOBA_BLOCK), 1)

    def step(n, own_sub):
        start = pl.multiple_of(n * MOBA_BLOCK, MOBA_BLOCK)
        kblk = kb_sc[pl.ds(start, MOBA_BLOCK), :]
        for h in range(2):
            s = lax.dot_general(qs_sc[h], kblk, NT_DIMS, preferred_element_type=F32)
            if own_sub is not None:
                in_sub = rowi // MOBA_BLOCK == own_sub
                s = jnp.where(in_sub & (rowi - own_sub * MOBA_BLOCK < coli), NEG, s)
            m_old = m_sc[h]
            row_max = jnp.max(jnp.maximum(s[:, 0:LANES], s[:, LANES:2 * LANES]), axis=1, keepdims=True)
            m_new = jnp.maximum(m_old, row_max)
            alpha = jnp.exp(m_old - m_new)
            p = jnp.exp(s - jnp.concatenate([m_new, m_new], axis=1))
            pv = jnp.dot(p.astype(BF16), vb_sc[h, pl.ds(start, MOBA_BLOCK), :], preferred_element_type=F32)
            acc_sc[h] = alpha * acc_sc[h] + pv
            m_sc[h] = m_new

    def past(n, carry):
        step(n, None)
        return carry

    lax.fori_loop(0, ti * nsub, past, 0)
    for sub in range(nsub):
        step(ti * nsub + sub, sub)
    a0 = acc_sc[0]
    a1 = acc_sc[1]
    o_ref[0] = jnp.where(lane < HEAD_DIM, a0 / pltpu.roll(a0, HEAD_DIM, 1), a1 / pltpu.roll(a1, HEAD_DIM, 1))


def _moba_prompt(qa, ka, va):
    b, l, hw = qa.shape
    tq = 2 * MOBA_BLOCK if l % (2 * MOBA_BLOCK) == 0 else MOBA_BLOCK
    pairs = hw // LANES
    return pl.pallas_call(
        _moba_prompt_kernel,
        grid=(b, pairs, l // tq),
        in_specs=[
            pl.BlockSpec((1, tq, LANES), lambda i, p, t: (i, t, p)),
            pl.BlockSpec((1, l, LANES), lambda i, p, t: (i, 0, p)),
            pl.BlockSpec((1, l, LANES), lambda i, p, t: (i, 0, p)),
        ],
        out_specs=pl.BlockSpec((1, tq, LANES), lambda i, p, t: (i, t, p)),
        out_shape=jax.ShapeDtypeStruct((b, l, hw), F32),
        scratch_shapes=[
            pltpu.VMEM((l, 2 * LANES), BF16),
            pltpu.VMEM((2, l, LANES), BF16),
            pltpu.VMEM((LANES, LANES), F32),
            pltpu.VMEM((2, tq, 2 * LANES), BF16),
            pltpu.VMEM((2, tq, LANES), F32),
            pltpu.VMEM((2, tq, LANES), F32),
        ],
        compiler_params=_params("arbitrary", "arbitrary", "arbitrary"),
        name="moba_prompt",
    )(qa, ka, va)


def _moba_sample_kernel(pt_ref, q_ref, kn_ref, vn_ref, k0_ref, k1_ref, v0_ref, v1_ref, o_ref,
                        q_sc, m_sc, l_sc, g_sc, acc_sc):
    del pt_ref
    j = pl.program_id(1)
    nb = pl.num_programs(1) - 1
    nq = q_ref.shape[1]
    nh = HEADS_PER_GROUP
    nr = nh * nq
    hw = HEAD_WIDTH
    keys = 2 * k0_ref.shape[2]
    row_w = lax.broadcasted_iota(jnp.int32, (nr, hw), 0)
    lane_w = lax.broadcasted_iota(jnp.int32, (nr, hw), 1)
    own_head = (row_w // nq) == (lane_w // HEAD_DIM)
    lane = lax.broadcasted_iota(jnp.int32, (nr, LANES), 1)

    def to_queries(t):
        return jnp.sum(jnp.where(own_head, t, 0.0).reshape(nh, nq, hw), axis=0)

    @pl.when(j == 0)
    def _():
        qbd = jnp.where(own_head, jnp.concatenate([q_ref[0]] * nh, axis=0), 0.0)
        hi = qbd.astype(BF16)
        q_sc[0:nr] = hi
        q_sc[nr:2 * nr] = (qbd - hi.astype(F32)).astype(BF16)
        m_sc[...] = jnp.zeros_like(m_sc)
        l_sc[...] = jnp.zeros_like(l_sc)
        g_sc[...] = jnp.zeros_like(g_sc)

    def scores(kt):
        s2 = jnp.dot(q_sc[...], kt, preferred_element_type=F32)
        return s2[0:nr] + s2[nr:2 * nr]

    @pl.when(j < nb)
    def _():
        kt = jnp.concatenate([k0_ref[0, 0], k1_ref[0, 0]], axis=1).astype(BF16)
        vt = jnp.concatenate([v0_ref[0, 0], v1_ref[0, 0]], axis=1).astype(BF16)
        s = scores(kt)
        gate = jnp.sum(s, axis=1, keepdims=True) * (1.0 / keys)
        mx = jnp.max(s, axis=1, keepdims=True)
        p = jnp.exp(s - mx)
        den = jnp.sum(p, axis=1, keepdims=True)
        pv = lax.dot_general(p.astype(BF16), vt, NT_DIMS, preferred_element_type=F32)
        acc_sc[j] = to_queries(pv)
        m_sc[...] = jnp.where(lane == j, mx, m_sc[...])
        l_sc[...] = jnp.where(lane == j, den, l_sc[...])
        g_sc[...] = jnp.where(lane == j, gate, g_sc[...])

    @pl.when(j == nb)
    def _():
        cur = jnp.where(lane < nb, g_sc[...], -jnp.inf)
        sel = jnp.zeros((nr, LANES), jnp.bool_)
        for _ in range(MOBA_TOPK):
            top = jnp.max(cur, axis=1, keepdims=True)
            first = jnp.min(jnp.where(cur == top, lane, LANES), axis=1, keepdims=True)
            pick = lane == first
            sel = sel | pick
            cur = jnp.where(pick, -jnp.inf, cur)
        pad = jnp.zeros((nq, hw), F32)
        kn = jnp.concatenate([kn_ref[0], pad], axis=0)
        vn = jnp.concatenate([vn_ref[0], pad], axis=0)
        s2 = lax.dot_general(q_sc[...], kn.astype(BF16), NT_DIMS, preferred_element_type=F32)
        s_own = s2[0:nr] + s2[nr:2 * nr]
        qpos = lax.broadcasted_iota(jnp.int32, (nr, 2 * nq), 0) % nq
        kpos = lax.broadcasted_iota(jnp.int32, (nr, 2 * nq), 1)
        s_own = jnp.where(kpos <= qpos, s_own, NEG)
        m_own = jnp.max(s_own, axis=1, keepdims=True)
        p_own = jnp.exp(s_own - m_own)
        l_own = jnp.sum(p_own, axis=1, keepdims=True)
        acc_own = to_queries(jnp.dot(p_own, vn, preferred_element_type=F32))
        m_all = m_sc[...]
        m_tot = jnp.maximum(jnp.max(jnp.where(sel, m_all, NEG), axis=1, keepdims=True), m_own)
        w = jnp.where(sel, jnp.exp(m_all - m_tot), 0.0)
        w_own = jnp.exp(m_own - m_tot)
        l_tot = jnp.sum(w * l_sc[...], axis=1, keepdims=True) + w_own * l_own

        def widen(col):
            return to_queries(jnp.broadcast_to(col, (nr, hw)))

        num = widen(w_own) * acc_own
        for n in range(nb):
            num = num + widen(w[:, n:n + 1]) * acc_sc[n]
        o_ref[0] = num / widen(l_tot)


def _moba_sample(qa, ka, va, cache_k, cache_v, page_table):
    b, nq, hw = qa.shape
    nh = HEADS_PER_GROUP
    n_pool, page = cache_k.shape[0], cache_k.shape[1]
    pages_per_blk = MOBA_BLOCK // page
    assert pages_per_blk == 2 and page == LANES
    nb = page_table.shape[1] // pages_per_blk
    assert MOBA_TOPK <= nb <= LANES
    ck = cache_k.transpose(0, 2, 3, 1).reshape(1, n_pool, hw, page)
    cv = cache_v.transpose(0, 2, 3, 1).reshape(1, n_pool, hw, page)
    tok = lambda i, j, pt: (i, 0, 0)

    def page_spec(which):
        return pl.BlockSpec((1, 1, hw, page),
                            lambda i, j, pt: (0, pt[i, jnp.minimum(j, nb - 1) * 2 + which], 0, 0))

    grid_spec = pltpu.PrefetchScalarGridSpec(
        num_scalar_prefetch=1,
        grid=(b, nb + 1),
        in_specs=[
            pl.BlockSpec((1, nq, hw), tok),
            pl.BlockSpec((1, nq, hw), tok),
            pl.BlockSpec((1, nq, hw), tok),
            page_spec(0), page_spec(1), page_spec(0), page_spec(1),
        ],
        out_specs=pl.BlockSpec((1, nq, hw), tok),
        scratch_shapes=[
            pltpu.VMEM((2 * nh * nq, hw), BF16),
            pltpu.VMEM((nh * nq, LANES), F32),
            pltpu.VMEM((nh * nq, LANES), F32),
            pltpu.VMEM((nh * nq, LANES), F32),
            pltpu.VMEM((nb, nq, hw), F32),
        ],
    )
    return pl.pallas_call(
        _moba_sample_kernel,
        grid_spec=grid_spec,
        out_shape=jax.ShapeDtypeStruct((b, nq, hw), F32),
        compiler_params=_params("arbitrary", "arbitrary"),
        name="moba_sample",
    )(page_table, qa, ka, va, ck, ck, cv, cv)


def _gla_kernel(q_ref, k_ref, v_ref, la_ref, gs_ref, gn_ref, s0_ref, o_ref, sout_ref, st_sc, *, chunk, mm_dtype):
    c = chunk
    n = q_ref.shape[1] // c
    hd = HEAD_DIM
    lane = lax.broadcasted_iota(jnp.int32, (c, LANES), 1)
    head0 = lane < hd
    r = lax.broadcasted_iota(jnp.int32, (c, c), 0)
    cc = lax.broadcasted_iota(jnp.int32, (c, c), 1)
    causal = r >= cc
    tri = jnp.where(causal, 1.0, 0.0)
    br = lax.broadcasted_iota(jnp.int32, (LANES, LANES), 0) // hd
    bc = lax.broadcasted_iota(jnp.int32, (LANES, LANES), 1) // hd
    same_head = br == bc
    head_mean = jnp.where(same_head, 1.0 / hd, 0.0)
    gn = gn_ref[...]

    s0 = s0_ref[0]
    zero = jnp.zeros((hd, hd), F32)
    st_sc[...] = jnp.concatenate(
        [jnp.concatenate([s0[0], zero], axis=1), jnp.concatenate([zero, s0[1]], axis=1)], axis=0)

    def body(ci, carry):
        rows = pl.ds(pl.multiple_of(ci * c, c), c)
        la = la_ref[0, rows, :]
        q = q_ref[0, rows, :]
        k = k_ref[0, rows, :]
        v = v_ref[0, rows, :]
        st = st_sc[...]
        b = jnp.dot(tri, la, precision=HIGHEST, preferred_element_type=F32)
        b_last = b[c - 1:c, :]
        b_mid = b[c // 2 - 1:c // 2, :]
        qt = q * jnp.exp(b - b_mid)
        kt = k * jnp.exp(b_mid - b)
        qts = jnp.concatenate([jnp.where(head0, qt, 0.0), jnp.where(head0, 0.0, qt)], axis=0)
        a = lax.dot_general(qts.astype(mm_dtype), kt.astype(mm_dtype), NT_DIMS, preferred_element_type=F32)
        a0 = jnp.where(causal, a[0:c], 0.0).astype(mm_dtype)
        a1 = jnp.where(causal, a[c:2 * c], 0.0).astype(mm_dtype)
        vm = v.astype(mm_dtype)
        o = jnp.where(head0, jnp.dot(a0, vm, preferred_element_type=F32),
                      jnp.dot(a1, vm, preferred_element_type=F32))
        qe = q * jnp.exp(b)
        o = o + lax.dot_general(qe.astype(mm_dtype), st.astype(mm_dtype), NT_DIMS, preferred_element_type=F32)
        kd = k * jnp.exp(b_last - b)
        upd = lax.dot_general(vm, kd.astype(mm_dtype), TN_DIMS, preferred_element_type=F32)
        st_sc[...] = st * jnp.exp(b_last) + jnp.where(same_head, upd, 0.0)
        ms = jnp.dot(o * o, head_mean, precision=HIGHEST, preferred_element_type=F32)
        o_ref[0, rows, :] = o * lax.rsqrt(ms + EPS) * gn * gs_ref[0, rows, :]
        return carry

    lax.fori_loop(0, n, body, 0)
    st = st_sc[...]
    sout_ref[0, 0] = st[0:hd, 0:hd]
    sout_ref[0, 1] = st[hd:2 * hd, hd:2 * hd]


def _gla(qg, kg, vg, la, gs, gn_pair, s0_t, chunk, mm_dtype):
    b, l, hw = qg.shape
    pairs = hw // LANES
    seq = pl.BlockSpec((1, l, LANES), lambda i, p: (i, 0, p))
    st_spec = pl.BlockSpec((1, 2, HEAD_DIM, HEAD_DIM), lambda i, p: (i, p, 0, 0))
    return pl.pallas_call(
        functools.partial(_gla_kernel, chunk=chunk, mm_dtype=mm_dtype),
        grid=(b, pairs),
        in_specs=[seq, seq, seq, seq, seq, pl.BlockSpec((1, LANES), lambda i, p: (0, 0)), st_spec],
        out_specs=[seq, st_spec],
        out_shape=[jax.ShapeDtypeStruct((b, l, hw), F32),
                   jax.ShapeDtypeStruct((b, HEADS_PER_GROUP, HEAD_DIM, HEAD_DIM), F32)],
        scratch_shapes=[pltpu.VMEM((LANES, LANES), F32)],
        compiler_params=_params("arbitrary", "arbitrary"),
        name="gla_chunked",
    )(qg, kg, vg, la, gs, gn_pair, s0_t)


def _mix_ffn_kernel(x_ref, oa_ref, og_ref, mod_ref, g_ref, wo_a_ref, wo_g_ref, wg_ref, wu_ref, wd_ref, o_ref,
                    x1_sc, h_sc, acc_sc):
    j = pl.program_id(2)
    bb, tl, d = x_ref.shape
    tm = bb * tl

    @pl.when(j == 0)
    def _():
        m = mod_ref[...]
        mix = (_mm(oa_ref[...].reshape(tm, -1), wo_a_ref[...])
               + _mm(og_ref[...].reshape(tm, -1), wo_g_ref[...]))
        x1 = x_ref[...] + m[:, 2:3, :] * mix.reshape(bb, tl, d)
        h = _rms(x1, g_ref[...]) * (1.0 + m[:, 4:5, :]) + m[:, 3:4, :]
        x1_sc[...] = x1.reshape(tm, d)
        h_sc[...] = h.reshape(tm, d).astype(h_sc.dtype)
        acc_sc[...] = jnp.zeros_like(acc_sc)

    h = h_sc[...]
    acc_sc[...] += _mm(_silu(_mm(h, wg_ref[...])) * _mm(h, wu_ref[...]), wd_ref[...])

    @pl.when(j == pl.num_programs(2) - 1)
    def _():
        g2 = mod_ref[...][:, 5:6, :]
        o_ref[...] = x1_sc[...].reshape(bb, tl, d) + g2 * acc_sc[...].reshape(bb, tl, d)


def _mix_ffn(x, oa, og, mod, g, wo_a, wo_g, wg, wu, wd):
    b, l, d = x.shape
    hw = oa.shape[-1]
    f = wg.shape[1]
    bb, tl = _token_tiles(b, l, 512 if l >= 512 else 256)
    tm = bb * tl
    fc = f // 2 if wg.dtype == BF16 else 256
    tok = lambda i, t, j: (i, t, 0)
    const2 = lambda i, t, j: (0, 0)
    return pl.pallas_call(
        _mix_ffn_kernel,
        grid=(b // bb, l // tl, f // fc),
        in_specs=[
            pl.BlockSpec((bb, tl, d), tok),
            pl.BlockSpec((bb, tl, hw), tok),
            pl.BlockSpec((bb, tl, hw), tok),
            pl.BlockSpec((bb, 6, d), lambda i, t, j: (i, 0, 0)),
            pl.BlockSpec((1, d), const2),
            pl.BlockSpec(wo_a.shape, const2),
            pl.BlockSpec(wo_g.shape, const2),
            pl.BlockSpec((d, fc), lambda i, t, j: (0, j)),
            pl.BlockSpec((d, fc), lambda i, t, j: (0, j)),
            pl.BlockSpec((fc, d), lambda i, t, j: (j, 0)),
        ],
        out_specs=pl.BlockSpec((bb, tl, d), tok),
        out_shape=jax.ShapeDtypeStruct((b, l, d), F32),
        scratch_shapes=[pltpu.VMEM((tm, d), F32), pltpu.VMEM((tm, d), wg.dtype), pltpu.VMEM((tm, d), F32)],
        compiler_params=_params("arbitrary", "arbitrary", "arbitrary"),
        name="outproj_ffn",
    )(x, oa, og, mod, g, wo_a, wo_g, wg, wu, wd)


def _cm_in_kernel(x_ref, mod_ref, g_ref, w_ref, lng_ref, lnb_ref, u_ref, v_ref):
    bb, tl, d = x_ref.shape
    tm = bb * tl
    cw = u_ref.shape[-1]
    m = mod_ref[...]
    h = _rms(x_ref[...], g_ref[...]) * (1.0 + m[:, 1:2, :]) + m[:, 0:1, :]
    hb = h.reshape(tm, d).astype(w_ref.dtype)
    u_ref[...] = _gelu_tanh(_mm(hb, w_ref[:, 0:cw])).reshape(bb, tl, cw)
    zv = _gelu_tanh(_mm(hb, w_ref[:, cw:2 * cw]))
    mu = jnp.mean(zv, axis=-1, keepdims=True)
    zc = zv - mu
    var = jnp.mean(zc * zc, axis=-1, keepdims=True)
    v_ref[...] = (zc * lax.rsqrt(var + EPS) * lng_ref[...] + lnb_ref[...]).reshape(bb, tl, cw)


def _cm_in(x, mod, g, w, ln_g, ln_b):
    b, l, d = x.shape
    cw = w.shape[1] // 2
    bb, tl = _token_tiles(b, l, 256)
    tok = lambda i, t: (i, t, 0)
    const2 = lambda i, t: (0, 0)
    out_spec = pl.BlockSpec((bb, tl, cw), tok)
    out_sds = jax.ShapeDtypeStruct((b, l, cw), F32)
    return pl.pallas_call(
        _cm_in_kernel,
        grid=(b // bb, l // tl),
        in_specs=[
            pl.BlockSpec((bb, tl, d), tok),
            pl.BlockSpec((bb, 6, d), lambda i, t: (i, 0, 0)),
            pl.BlockSpec((1, d), const2),
            pl.BlockSpec(w.shape, const2),
            pl.BlockSpec((1, cw), const2),
            pl.BlockSpec((1, cw), const2),
        ],
        out_specs=[out_spec, out_spec],
        out_shape=[out_sds, out_sds],
        compiler_params=_params("arbitrary", "arbitrary"),
        name="cm_in_gelu_ln",
    )(x, mod, g, w, ln_g, ln_b)


def _cm_out_kernel(x_ref, u_ref, v_ref, mod_ref, g_ref, ws_ref, bs_ref, wo_ref, wr_ref,
                   x3_ref, h_ref, gates_ref, us_sc):
    bb, tl, d = x_ref.shape
    tm = bb * tl
    cw = u_ref.shape[-1]
    ngroups, cm, _ = ws_ref.shape
    gd = cw // ngroups
    u = u_ref[...].reshape(tm, cw)
    v = v_ref[...].reshape(tm, cw)
    bs = bs_ref[...]
    mm_dtype = wo_ref.dtype
    r = lax.broadcasted_iota(jnp.int32, (cm, cm), 0)
    c = lax.broadcasted_iota(jnp.int32, (cm, cm), 1)
    for g in range(ngroups):
        wsg = jnp.where(r >= c, ws_ref[g], 0.0).astype(mm_dtype)
        cols = slice(g * gd, (g + 1) * gd)
        for sb in range(tm // cm):
            rows = slice(sb * cm, (sb + 1) * cm)
            s = _mm(wsg, v[rows, cols].astype(mm_dtype)) + bs[rows, g:g + 1]
            us_sc[rows, cols] = (u[rows, cols] * s).astype(mm_dtype)
    y = _mm(us_sc[...], wo_ref[...])
    m = mod_ref[...]
    x3 = x_ref[...] + m[:, 2:3, :] * y.reshape(bb, tl, d)
    x3_ref[...] = x3
    h = (_rms(x3, g_ref[...]) * (1.0 + m[:, 4:5, :]) + m[:, 3:4, :]).reshape(tm, d)
    h_ref[...] = h.astype(BF16)
    logits = jnp.dot(h, wr_ref[...], precision=HIGHEST, preferred_element_type=F32)
    ne = logits.shape[1]
    col = lax.broadcasted_iota(jnp.int32, (tm, ne), 1)
    m1 = jnp.max(logits, axis=1, keepdims=True)
    i1 = jnp.min(jnp.where(logits == m1, col, ne), axis=1, keepdims=True)
    rest = jnp.where(col == i1, -jnp.inf, logits)
    m2 = jnp.max(rest, axis=1, keepdims=True)
    i2 = jnp.min(jnp.where(rest == m2, col, ne), axis=1, keepdims=True)
    e2 = jnp.exp(m2 - m1)
    gates_ref[...] = jnp.where(col == i1, 1.0 / (1.0 + e2), 0.0) + jnp.where(col == i2, e2 / (1.0 + e2), 0.0)


def _cm_out(x, u, v, mod, g, ws_mix, bs_rows, wo, wr, cm):
    b, l, d = x.shape
    cw = u.shape[-1]
    ne = wr.shape[1]
    bb, tl = _token_tiles(b, l, 256)
    tm = bb * tl
    tok = lambda i, t: (i, t, 0)
    flat = lambda i, t: (i * (l // tl) + t, 0)
    const2 = lambda i, t: (0, 0)
    return pl.pallas_call(
        _cm_out_kernel,
        grid=(b // bb, l // tl),
        in_specs=[
            pl.BlockSpec((bb, tl, d), tok),
            pl.BlockSpec((bb, tl, cw), tok),
            pl.BlockSpec((bb, tl, cw), tok),
            pl.BlockSpec((bb, 6, d), lambda i, t: (i, 0, 0)),
            pl.BlockSpec((1, d), const2),
            pl.BlockSpec(ws_mix.shape, lambda i, t: (0, 0, 0)),
            pl.BlockSpec((tm, bs_rows.shape[1]), const2),
            pl.BlockSpec(wo.shape, const2),
            pl.BlockSpec(wr.shape, const2),
        ],
        out_specs=[pl.BlockSpec((bb, tl, d), tok), pl.BlockSpec((tm, d), flat), pl.BlockSpec((tm, ne), flat)],
        out_shape=[jax.ShapeDtypeStruct((b, l, d), F32), jax.ShapeDtypeStruct((b * l, d), BF16),
                   jax.ShapeDtypeStruct((b * l, ne), F32)],
        scratch_shapes=[pltpu.VMEM((tm, cw), wo.dtype)],
        compiler_params=_params("arbitrary", "arbitrary"),
        name="cm_out_router",
    )(x, u, v, mod, g, ws_mix, bs_rows, wo, wr)


MOE_SUB = 256
MOE_ALIGN = 16
MOE_WIN_SMALL = 128
MOE_WIN_FULL = MOE_SUB + MOE_ALIGN
MOE_ROW_CHUNK = 512
MOE_BLOCK_SUBS = 8


def _moe_routing(gates, nb, nsub):
    t, ne = gates.shape
    routed = (gates > 0.0).astype(jnp.int32).reshape(nb, nsub, MOE_SUB, ne)
    rank = jnp.cumsum(routed, axis=2) - 1
    cnt = jnp.sum(routed, axis=2)
    off = jnp.cumsum(cnt, axis=1) - cnt
    total = jnp.sum(cnt, axis=1)
    start = (off // MOE_ALIGN) * MOE_ALIGN
    shift = off - start
    win_row = jnp.where(routed > 0, rank + shift[:, :, None, :], -1)
    win_row = win_row.transpose(0, 3, 1, 2).reshape(nb, ne, nsub * MOE_SUB)
    used = jnp.where(cnt > 0, shift + cnt, 0)
    scalars = jnp.concatenate([start.transpose(0, 2, 1).reshape(-1), used.transpose(0, 2, 1).reshape(-1),
                               total.reshape(-1)]).astype(jnp.int32)
    return win_row, scalars


def _moe_kernel(sc_ref, h_ref, gates_ref, row_ref, wg_ref, wu_ref, wd_ref, o_ref, x_sc, y_sc):
    b = pl.program_id(0)
    e = pl.program_id(1)
    j = pl.program_id(2)
    ne = pl.num_programs(1)
    tb = h_ref.shape[0]
    nsub = tb // MOE_SUB
    cap = x_sc.shape[0]
    n_be = pl.num_programs(0) * ne
    seg = (b * ne + e) * nsub
    total = sc_ref[2 * n_be * nsub + b * ne + e]

    def windows(fn):
        for s in range(nsub):
            start = pl.multiple_of(sc_ref[seg + s], MOE_ALIGN)
            used = sc_ref[n_be * nsub + seg + s]
            tok = slice(s * MOE_SUB, (s + 1) * MOE_SUB)
            win_row = row_ref[0, pl.ds(e, 1), tok]

            def run(rows, start=start, tok=tok, win_row=win_row):
                ridx = lax.broadcasted_iota(jnp.int32, (rows, MOE_SUB), 0)
                fn(pl.ds(start, rows), tok, jnp.where(ridx == win_row, 1.0, 0.0).astype(BF16))

            pl.when((used > 0) & (used <= MOE_WIN_SMALL))(functools.partial(run, MOE_WIN_SMALL))
            pl.when(used > MOE_WIN_SMALL)(functools.partial(run, MOE_WIN_FULL))

    @pl.when((e == 0) & (j == 0))
    def _():
        o_ref[...] = jnp.zeros_like(o_ref)

    @pl.when(j == 0)
    def _():
        def clear(c, carry):
            rows = pl.ds(pl.multiple_of(c * MOE_ROW_CHUNK, MOE_ROW_CHUNK), MOE_ROW_CHUNK)
            x_sc[rows, :] = jnp.zeros((MOE_ROW_CHUNK, x_sc.shape[1]), BF16)
            y_sc[rows, :] = jnp.zeros((MOE_ROW_CHUNK, y_sc.shape[1]), F32)
            return carry

        lax.fori_loop(0, jnp.minimum((total + MOE_WIN_FULL + MOE_ROW_CHUNK - 1) // MOE_ROW_CHUNK,
                                     cap // MOE_ROW_CHUNK), clear, 0)

        def dispatch(rows, tok, onehot):
            picked = jnp.dot(onehot, h_ref[tok, :], preferred_element_type=F32).astype(BF16)
            x_sc[rows, :] = x_sc[rows, :] + picked

        windows(dispatch)

    def ffn(r0, nrows):
        rows = pl.ds(pl.multiple_of(r0, MOE_WIN_SMALL), nrows)
        x = x_sc[rows, :]
        gate = jnp.dot(x, wg_ref[0], preferred_element_type=F32)
        up = jnp.dot(x, wu_ref[0], preferred_element_type=F32)
        act = (_silu(gate) * up).astype(BF16)
        y_sc[rows, :] += jnp.dot(act, wd_ref[0], preferred_element_type=F32)

    nfull = total // MOE_ROW_CHUNK
    rem = total - nfull * MOE_ROW_CHUNK
    tail = nfull * MOE_ROW_CHUNK

    def full_chunk(c, carry):
        ffn(c * MOE_ROW_CHUNK, MOE_ROW_CHUNK)
        return carry

    lax.fori_loop(0, nfull, full_chunk, 0)
    take_256 = (rem > 128) & (rem <= 384)
    pl.when(rem > 384)(lambda: ffn(tail, 512))
    pl.when(take_256)(lambda: ffn(tail, 256))
    pl.when((rem > 0) & (rem <= 128))(lambda: ffn(tail, 128))
    pl.when((rem > 256) & (rem <= 384))(lambda: ffn(tail + 256, 128))

    @pl.when(j == pl.num_programs(2) - 1)
    def _():
        gates = gates_ref[...]
        col = lax.broadcasted_iota(jnp.int32, gates.shape, 1)
        ge = jnp.sum(jnp.where(col == e, gates, 0.0), axis=1, keepdims=True)

        def combine(rows, tok, onehot):
            back = lax.dot_general(onehot, y_sc[rows, :].astype(BF16), TN_DIMS, preferred_element_type=F32)
            o_ref[tok, :] = (o_ref[tok, :].astype(F32) + ge[tok] * back).astype(o_ref.dtype)

        windows(combine)


def _moe(h, gates, wg, wu, wd, tb):
    t, d = h.shape
    ne, _, f = wg.shape
    fc = 512
    assert t % tb == 0 and tb % MOE_SUB == 0 and f % fc == 0
    nb, nsub = t // tb, tb // MOE_SUB
    win_row, scalars = _moe_routing(gates, nb, nsub)
    cap = -(-(tb + MOE_WIN_FULL) // MOE_ROW_CHUNK) * MOE_ROW_CHUNK
    grid_spec = pltpu.PrefetchScalarGridSpec(
        num_scalar_prefetch=1,
        grid=(nb, ne, f // fc),
        in_specs=[
            pl.BlockSpec((tb, d), lambda i, e, j, sc: (i, 0)),
            pl.BlockSpec((tb, ne), lambda i, e, j, sc: (i, 0)),
            pl.BlockSpec((1, ne, tb), lambda i, e, j, sc: (i, 0, 0)),
            pl.BlockSpec((1, d, fc), lambda i, e, j, sc: (e, 0, j)),
            pl.BlockSpec((1, d, fc), lambda i, e, j, sc: (e, 0, j)),
            pl.BlockSpec((1, fc, d), lambda i, e, j, sc: (e, j, 0)),
        ],
        out_specs=pl.BlockSpec((tb, d), lambda i, e, j, sc: (i, 0)),
        scratch_shapes=[pltpu.VMEM((cap, d), BF16), pltpu.VMEM((cap, d), F32)],
    )
    return pl.pallas_call(
        _moe_kernel,
        grid_spec=grid_spec,
        out_shape=jax.ShapeDtypeStruct((t, d), BF16),
        compiler_params=_params("arbitrary", "arbitrary", "arbitrary"),
        name="moe_experts",
    )(scalars, h, gates, win_row, wg, wu, wd)


def _final_kernel(x_ref, moe_ref, mod_ref, g_ref, o_ref):
    bb, tl, d = x_ref.shape
    g2 = mod_ref[...][:, 5:6, :]
    x4 = x_ref[...] + g2 * moe_ref[...].reshape(bb, tl, d)
    o_ref[...] = _rms(x4, g_ref[...])


def _final(x, moe, row0, mod, g):
    b, l, d = x.shape
    bb, tl = _token_tiles(b, l, 512 if l >= 512 else 256)
    tm = bb * tl
    assert row0 % tm == 0
    tok = lambda i, t: (i, t, 0)
    return pl.pallas_call(
        _final_kernel,
        grid=(b // bb, l // tl),
        in_specs=[
            pl.BlockSpec((bb, tl, d), tok),
            pl.BlockSpec((tm, d), lambda i, t: (row0 // tm + i * (l // tl) + t, 0)),
            pl.BlockSpec((bb, 6, d), lambda i, t: (i, 0, 0)),
            pl.BlockSpec((1, d), lambda i, t: (0, 0)),
        ],
        out_specs=pl.BlockSpec((bb, tl, d), tok),
        out_shape=jax.ShapeDtypeStruct((b, l, d), F32),
        compiler_params=_params("arbitrary", "arbitrary"),
        name="final_norm",
    )(x, moe, mod, g)


def _layer0(x, mod, pos0, past, w):
    b, l, d = x.shape
    hw = HEAD_WIDTH
    outs = _in_proj(x, mod, w["norm_mix_g0"], w["w_in_main"], w["w_in_lr"], w["wa2_hi"], w["wa2_lo"], w["ba2"], pos0)
    qa, ka, va, qg, kg, vg, gs, la = outs
    if past is None:
        oa = _moba_prompt(qa, ka, va)
        s0_t = jnp.zeros((b, HEADS_PER_GROUP, HEAD_DIM, HEAD_DIM), F32)
    else:
        cache_k, cache_v, state, page_table = past
        oa = _moba_sample(qa, ka, va, cache_k, cache_v, page_table)
        s0_t = jnp.swapaxes(state, -1, -2)
    chunk = GLA_CHUNK if l % GLA_CHUNK == 0 else l
    mm_dtype = BF16 if chunk % 16 == 0 else F32
    og, s_t = _gla(qg, kg, vg, la, gs, w["gla_norm_pair"], s0_t, chunk, mm_dtype)
    x2 = _mix_ffn(x, oa, og, mod, w["norm_ffn_g0"], w["wo_a"], w["wo_g"], w["ffn_wg"], w["ffn_wu"], w["ffn_wd"])
    k_rows = ka.reshape(b, l, HEADS_PER_GROUP, HEAD_DIM)
    v_rows = va.reshape(b, l, HEADS_PER_GROUP, HEAD_DIM)
    return x2, k_rows, v_rows, jnp.swapaxes(s_t, -1, -2)


def _layer1_front(x, mod, w):
    b, l, d = x.shape
    u, v = _cm_in(x, mod, w["norm_mix_g1"], w["cm_w_in"], w["cm_ln_g"], w["cm_ln_b"])
    bb, tl = _token_tiles(b, l, 256)
    tm = bb * tl
    if l >= CM_CHUNK:
        cm = CM_CHUNK
        ws_mix = w["cm_w_s"]
        bs_rows = jnp.tile(w["cm_b_s"].T, (tm // CM_CHUNK, 1))
    else:
        cm = tm
        eye = jnp.eye(tm // l, dtype=F32)
        ws_mix = jax.vmap(lambda m_: jnp.kron(eye, m_))(w["cm_w_s"][:, :l, :l])
        bs_rows = jnp.tile(w["cm_b_s"][:, :l].T, (tm // l, 1))
    x3, h, gates = _cm_out(x, u, v, mod, w["norm_ffn_g1"], ws_mix, bs_rows, w["cm_w_out"], w["moe_w_router"], cm)
    start = ((l - 1) // CM_CHUNK) * CM_CHUNK
    return x3, h, gates, v[:, start:]


def kernel(x_prompt, x_sample, cache_k, cache_v, state_gla, page_table, c_prompt, c_sample, ada_w, ada_b, norm_mix_g, norm_ffn_g, ab_w_in, gla_w_a2, gla_b_a2, gla_norm_g, ab_w_out, ffn_w_gate, ffn_w_up, ffn_w_down, cm_w_in, cm_ln_g, cm_ln_b, cm_w_s, cm_b_s, cm_w_out, moe_w_router, moe_w_gate, moe_w_up, moe_w_down, final_norm_g):
    bp, lp, d = x_prompt.shape
    bs, ls, _ = x_sample.shape
    hw = HEAD_WIDTH
    n_main = 7 * hw
    past_len = page_table.shape[1] * cache_k.shape[2]

    wa2 = jnp.pad(gla_w_a2[0], ((0, LANES - GLA_LOWRANK), (0, 0)))
    wa2_hi = wa2.astype(BF16)
    shared = {
        "norm_mix_g0": norm_mix_g[0][None], "norm_mix_g1": norm_mix_g[1][None],
        "norm_ffn_g0": norm_ffn_g[0][None], "norm_ffn_g1": norm_ffn_g[1][None],
        "wa2_hi": wa2_hi, "wa2_lo": (wa2 - wa2_hi.astype(F32)).astype(BF16),
        "ba2": gla_b_a2[0][None],
        "gla_norm_pair": jnp.tile(gla_norm_g[0], LANES // HEAD_DIM)[None],
        "cm_ln_g": cm_ln_g[0][None], "cm_ln_b": cm_ln_b[0][None],
        "cm_w_s": cm_w_s[0], "cm_b_s": cm_b_s[0],
        "moe_w_router": moe_w_router[0],
    }
    mats = {
        "w_in_main": ab_w_in[0][:, :n_main],
        "w_in_lr": jnp.pad(ab_w_in[0][:, n_main:], ((0, 0), (0, LANES - GLA_LOWRANK))),
        "wo_a": ab_w_out[0][:hw], "wo_g": ab_w_out[0][hw:],
        "ffn_wg": ffn_w_gate[0], "ffn_wu": ffn_w_up[0], "ffn_wd": ffn_w_down[0],
        "cm_w_in": cm_w_in[0], "cm_w_out": cm_w_out[0],
    }
    w = dict(shared, **{k: v.astype(BF16) for k, v in mats.items()})
    w_sample = dict(shared, **mats)

    rows = bp + bs
    rows_pad = -(-rows // 16) * 16
    c_all = jnp.pad(jnp.concatenate([c_prompt, c_sample], axis=0), ((0, rows_pad - rows), (0, 0)))
    mod_all = _modulation(c_all, ada_w, ada_b)
    mod_p = [mod_all[l, :bp].reshape(bp, 6, d) for l in range(2)]
    mod_s = [mod_all[l, bp:rows].reshape(bs, 6, d) for l in range(2)]

    x2_p, k_p, v_p, gla_p = _layer0(x_prompt, mod_p[0], 0, None, w)
    x2_s, k_s, v_s, gla_s = _layer0(x_sample, mod_s[0], past_len,
                                    (cache_k[0], cache_v[0], state_gla[0], page_table), w_sample)

    x3_p, h_p, gates_p, cmv_p = _layer1_front(x2_p, mod_p[1], w)
    x3_s, h_s, gates_s, cmv_s = _layer1_front(x2_s, mod_s[1], w_sample)

    t_all = bp * lp + bs * ls
    subs = -(-t_all // MOE_SUB)
    n_blocks = -(-subs // MOE_BLOCK_SUBS)
    tb = -(-subs // n_blocks) * MOE_SUB
    pad = n_blocks * tb - t_all
    h_all = jnp.concatenate([h_p, h_s, jnp.zeros((pad, d), BF16)], axis=0)
    gates_all = jnp.concatenate([gates_p, gates_s, jnp.zeros((pad, gates_p.shape[1]), F32)], axis=0)
    moe = _moe(h_all, gates_all, moe_w_gate[0].astype(BF16), moe_w_up[0].astype(BF16),
               moe_w_down[0].astype(BF16), tb)

    fg = final_norm_g[None]
    y_p = _final(x3_p, moe, 0, mod_p[1], fg)
    y_s = _final(x3_s, moe, bp * lp, mod_s[1], fg)

    return (y_p, y_s, k_p[None], v_p[None], k_s[None], v_s[None], gla_p[None], gla_s[None],
            cmv_p[None], cmv_s[None])
```

```python
import functools

import jax
import jax.numpy as jnp
from jax import lax
from jax.experimental import pallas as pl
from jax.experimental.pallas import tpu as pltpu

F32 = jnp.float32
BF16 = jnp.bfloat16
HIGHEST = lax.Precision.HIGHEST

HEAD_DIM = 64
HEADS_PER_GROUP = 8
HEAD_WIDTH = HEADS_PER_GROUP * HEAD_DIM
MOBA_BLOCK = 256
MOBA_TOPK = 3
ROPE_THETA = 10000.0
GLA_LOWRANK = 16
GLA_GATE_NORM = 16.0
GLA_CHUNK = 64
CM_CHUNK = 128
CM_GROUPS = 8
MOE_TOP_K = 2
EPS = 1e-6
NEG = -1e30

LANES = 128
VMEM_LIMIT_BYTES = 56 * 1024 * 1024

NT_DIMS = (((1,), (1,)), ((), ()))
TN_DIMS = (((0,), (0,)), ((), ()))


def _params(*sem):
    return pltpu.CompilerParams(dimension_semantics=sem, vmem_limit_bytes=VMEM_LIMIT_BYTES)


def _rms(x, g):
    return x * lax.rsqrt(jnp.mean(x * x, axis=-1, keepdims=True) + EPS) * g


def _silu(x):
    return x * jax.nn.sigmoid(x)


def _mm(a, w):
    if w.dtype == F32:
        return jnp.dot(a.astype(F32), w, precision=HIGHEST, preferred_element_type=F32)
    return jnp.dot(a.astype(BF16), w, preferred_element_type=F32)


def _gelu_tanh(x):
    return 0.5 * x * (1.0 + jnp.tanh(0.7978845608028654 * (x + 0.044715 * (x * x * x))))


def _token_tiles(b, l, target):
    if l >= target:
        assert l % target == 0
        return 1, target
    bb = min(b, target // l)
    assert target % l == 0 and b % bb == 0
    return bb, l


def _mod_kernel(c_ref, w_ref, b_ref, o_ref):
    s = _silu(c_ref[...])
    r = s.shape[0]
    s_hi = s.astype(BF16)
    s_lo = (s - s_hi.astype(F32)).astype(BF16)
    w = w_ref[0]
    w_hi = w.astype(BF16)
    w_lo = (w - w_hi.astype(F32)).astype(BF16)
    both = jnp.dot(jnp.concatenate([s_hi, s_lo], axis=0), w_hi, preferred_element_type=F32)
    o_ref[0] = both[0:r] + both[r:2 * r] + jnp.dot(s_hi, w_lo, preferred_element_type=F32) + b_ref[0]


def _modulation(c_all, ada_w, ada_b):
    r, d = c_all.shape
    depth, _, n = ada_w.shape
    tn = 1536
    return pl.pallas_call(
        _mod_kernel,
        grid=(depth, n // tn),
        in_specs=[
            pl.BlockSpec((r, d), lambda l, j: (0, 0)),
            pl.BlockSpec((1, d, tn), lambda l, j: (l, 0, j)),
            pl.BlockSpec((1, 1, tn), lambda l, j: (l, 0, j)),
        ],
        out_specs=pl.BlockSpec((1, r, tn), lambda l, j: (l, 0, j)),
        out_shape=jax.ShapeDtypeStruct((depth, r, n), F32),
        compiler_params=_params("arbitrary", "arbitrary"),
        name="adaln_modulation",
    )(c_all, ada_w, ada_b.reshape(depth, 1, n))


def _in_proj_kernel(x_ref, mod_ref, g_ref, w_ref, wlr_ref, wa2h_ref, wa2l_ref, ba2_ref, cos_ref, sin_ref,
                    qa_ref, ka_ref, va_ref, qg_ref, kg_ref, vg_ref, gs_ref, la_ref):
    bb, tl, d = x_ref.shape
    tm = bb * tl
    hw = HEAD_WIDTH
    m = mod_ref[...]
    h = _rms(x_ref[...], g_ref[...]) * (1.0 + m[:, 1:2, :]) + m[:, 0:1, :]
    hb = h.reshape(tm, d).astype(w_ref.dtype)

    def proj(c):
        return _mm(hb, w_ref[:, c * hw:(c + 1) * hw])

    cos = jnp.concatenate([cos_ref[...]] * (hw // LANES), axis=-1)[None]
    sin = jnp.concatenate([sin_ref[...]] * (hw // LANES), axis=-1)[None]
    lane = lax.broadcasted_iota(jnp.int32, (tm, hw), 1)
    first_half = (lane % HEAD_DIM) < (HEAD_DIM // 2)

    def rope(z):
        partner = jnp.where(first_half, pltpu.roll(z, hw - HEAD_DIM // 2, 1), pltpu.roll(z, HEAD_DIM // 2, 1))
        return z.reshape(bb, tl, hw) * cos + partner.reshape(bb, tl, hw) * sin

    scale = HEAD_DIM ** -0.5
    qa_ref[...] = rope(proj(0)) * scale
    ka_ref[...] = rope(proj(1))
    va_ref[...] = proj(2).reshape(bb, tl, hw)
    qg_ref[...] = (proj(3) * scale).reshape(bb, tl, hw)
    kg_ref[...] = proj(4).reshape(bb, tl, hw)
    vg_ref[...] = proj(5).reshape(bb, tl, hw)
    gs_ref[...] = _silu(proj(6)).reshape(bb, tl, hw)
    zlr = _mm(hb, wlr_ref[...])
    hi = zlr.astype(BF16)
    lo = (zlr - hi.astype(F32)).astype(BF16)
    pre = (jnp.dot(hi, wa2h_ref[...], preferred_element_type=F32)
           + jnp.dot(lo, wa2h_ref[...], preferred_element_type=F32)
           + jnp.dot(hi, wa2l_ref[...], preferred_element_type=F32)) + ba2_ref[...]
    log_sig = jnp.minimum(pre, 0.0) - jnp.log(1.0 + jnp.exp(-jnp.abs(pre)))
    la_ref[...] = (log_sig * (1.0 / GLA_GATE_NORM)).reshape(bb, tl, hw)


def _rope_tables(pos0, l):
    half = HEAD_DIM // 2
    inv = 1.0 / (ROPE_THETA ** (jnp.arange(half, dtype=F32) / half))
    ang = (pos0 + jnp.arange(l, dtype=jnp.int32)).astype(F32)[:, None] * inv[None, :]
    cos, sin = jnp.cos(ang), jnp.sin(ang)
    reps = LANES // HEAD_DIM
    cos_t = jnp.tile(jnp.concatenate([cos, cos], axis=-1), (1, reps))
    sin_t = jnp.tile(jnp.concatenate([-sin, sin], axis=-1), (1, reps))
    return cos_t, sin_t


def _in_proj(x, mod, g, w_main, w_lr, wa2_hi, wa2_lo, ba2, pos0):
    b, l, d = x.shape
    bb, tl = _token_tiles(b, l, 512 if l >= 512 else 256)
    hw = HEAD_WIDTH
    cos_t, sin_t = _rope_tables(pos0, l)
    tok = lambda i, j: (i, j, 0)
    const2 = lambda i, j: (0, 0)
    out_spec = pl.BlockSpec((bb, tl, hw), tok)
    out_sds = jax.ShapeDtypeStruct((b, l, hw), F32)
    return pl.pallas_call(
        _in_proj_kernel,
        grid=(b // bb, l // tl),
        in_specs=[
            pl.BlockSpec((bb, tl, d), tok),
            pl.BlockSpec((bb, 6, d), lambda i, j: (i, 0, 0)),
            pl.BlockSpec((1, d), const2),
            pl.BlockSpec(w_main.shape, const2),
            pl.BlockSpec(w_lr.shape, const2),
            pl.BlockSpec(wa2_hi.shape, const2),
            pl.BlockSpec(wa2_lo.shape, const2),
            pl.BlockSpec((1, hw), const2),
            pl.BlockSpec((tl, LANES), lambda i, j: (j, 0)),
            pl.BlockSpec((tl, LANES), lambda i, j: (j, 0)),
        ],
        out_specs=[out_spec] * 8,
        out_shape=[out_sds] * 8,
        compiler_params=_params("arbitrary", "arbitrary"),
        name="in_proj_rope",
    )(x, mod, g, w_main, w_lr, wa2_hi, wa2_lo, ba2, cos_t, sin_t)


def _moba_prompt_kernel(q_ref, k_ref, v_ref, o_ref, kb_sc, vb_sc, km_sc, qs_sc, s_sc, m_sc, acc_sc):
    ti = pl.program_id(2)
    lk = k_ref.shape[1]
    nblk = lk // MOBA_BLOCK
    nblk8 = -(-nblk // 8) * 8
    tq = q_ref.shape[1]
    nsub = tq // MOBA_BLOCK

    @pl.when(ti == 0)
    def _():
        k = k_ref[0]
        v = v_ref[0]
        row = lax.broadcasted_iota(jnp.int32, (lk, LANES), 0)
        col = lax.broadcasted_iota(jnp.int32, (lk, LANES), 1)
        kb_sc[:, 0:LANES] = k.astype(BF16)
        kb_sc[:, LANES:2 * LANES] = jnp.where(row // MOBA_BLOCK == col, 1.0, 0.0).astype(BF16)
        vb_sc[0] = jnp.where(col < HEAD_DIM, v, 1.0).astype(BF16)
        vb_sc[1] = jnp.where(col >= HEAD_DIM, v, 1.0).astype(BF16)
        km_sc[...] = jnp.zeros_like(km_sc)
        km_sc[0:nblk, :] = jnp.mean(k.reshape(nblk, MOBA_BLOCK, LANES), axis=1)

    q = q_ref[0]
    lane = lax.broadcasted_iota(jnp.int32, (tq, LANES), 1)
    blk = lax.broadcasted_iota(jnp.int32, (nblk8, tq), 0)
    own = ti * nsub + lax.broadcasted_iota(jnp.int32, (nblk8, tq), 1) // MOBA_BLOCK
    past_blk = blk < own
    km = km_sc[...]
    for h in range(2):
        head_lanes = (lane < HEAD_DIM) if h == 0 else (lane >= HEAD_DIM)
        qh = jnp.where(head_lanes, q, 0.0)
        gate = lax.dot_general(km, qh, NT_DIMS, precision=HIGHEST, preferred_element_type=F32)[0:nblk8]
        gate = jnp.where(past_blk, gate, -jnp.inf)
        cnt = jnp.zeros((nblk8, tq), F32)
        for mrow in range(nblk):
            gm = gate[mrow:mrow + 1, :]
            beats = (gm > gate) | ((gm == gate) & (mrow < blk))
            cnt = cnt + jnp.where(beats, 1.0, 0.0)
        sel = (past_blk & (cnt < float(MOBA_TOPK))) | (blk == own)
        bias_t = jnp.concatenate(
            [jnp.where(sel, 0.0, NEG), jnp.full((LANES - nblk8, tq), NEG, F32)], axis=0)
        qs_sc[h, :, 0:LANES] = qh.astype(BF16)
        qs_sc[h, :, LANES:2 * LANES] = bias_t.T.astype(BF16)

    m_sc[...] = jnp.full_like(m_sc, NEG)
    acc_sc[...] = jnp.zeros_like(acc_sc)
    rowi = lax.broadcasted_iota(jnp.int32, (tq, MOBA_BLOCK), 0)
    coli = lax.broadcasted_iota(jnp.int32, (tq, MOBA_BLOCK), 1)

    def scores(n, slot):
        start = pl.multiple_of(n * MOBA_BLOCK, MOBA_BLOCK)
        kblk = kb_sc[pl.ds(start, MOBA_BLOCK), :]
        for h in range(2):
            s_sc[slot, h] = lax.dot_general(qs_sc[h], kblk, NT_DIMS, preferred_element_type=F32)

    def consume(n, slot, own_sub):
        start = pl.multiple_of(n * MOBA_BLOCK, MOBA_BLOCK)
        for h in range(2):
            s = s_sc[slot, h]
            if own_sub is not None:
                in_sub = rowi // MOBA_BLOCK == own_sub
                s = jnp.where(in_sub & (rowi - own_sub * MOBA_BLOCK < coli), NEG, s)
            m_old = m_sc[h]
            row_max = jnp.max(jnp.maximum(s[:, 0:LANES], s[:, LANES:2 * LANES]), axis=1, keepdims=True)
            m_new = jnp.maximum(m_old, row_max)
            alpha = jnp.exp(m_old - m_new)
            p = jnp.exp(s - jnp.concatenate([m_new, m_new], axis=1))
            pv = jnp.dot(p.astype(BF16), vb_sc[h, pl.ds(start, MOBA_BLOCK), :], preferred_element_type=F32)
            acc_sc[h] = alpha * acc_sc[h] + pv
            m_sc[h] = m_new

    def past_pair(i, carry):
        scores(2 * i + 1, 1)
        consume(2 * i, 0, None)
        scores(2 * i + 2, 0)
        consume(2 * i + 1, 1, None)
        return carry

    scores(0, 0)
    lax.fori_loop(0, ti, past_pair, 0)
    scores(2 * ti + 1, 1)
    consume(2 * ti, 0, 0)
    consume(2 * ti + 1, 1, 1)
    a0 = acc_sc[0]
    a1 = acc_sc[1]
    o_ref[0] = jnp.where(lane < HEAD_DIM, a0 / pltpu.roll(a0, HEAD_DIM, 1), a1 / pltpu.roll(a1, HEAD_DIM, 1))


def _moba_prompt(qa, ka, va):
    b, l, hw = qa.shape
    tq = 2 * MOBA_BLOCK
    assert l % tq == 0
    pairs = hw // LANES
    return pl.pallas_call(
        _moba_prompt_kernel,
        grid=(b, pairs, l // tq),
        in_specs=[
            pl.BlockSpec((1, tq, LANES), lambda i, p, t: (i, t, p)),
            pl.BlockSpec((1, l, LANES), lambda i, p, t: (i, 0, p)),
            pl.BlockSpec((1, l, LANES), lambda i, p, t: (i, 0, p)),
        ],
        out_specs=pl.BlockSpec((1, tq, LANES), lambda i, p, t: (i, t, p)),
        out_shape=jax.ShapeDtypeStruct((b, l, hw), F32),
        scratch_shapes=[
            pltpu.VMEM((l, 2 * LANES), BF16),
            pltpu.VMEM((2, l, LANES), BF16),
            pltpu.VMEM((LANES, LANES), F32),
            pltpu.VMEM((2, tq, 2 * LANES), BF16),
            pltpu.VMEM((2, 2, tq, MOBA_BLOCK), F32),
            pltpu.VMEM((2, tq, LANES), F32),
            pltpu.VMEM((2, tq, LANES), F32),
        ],
        compiler_params=_params("arbitrary", "arbitrary", "arbitrary"),
        name="moba_prompt",
    )(qa, ka, va)


def _moba_sample_kernel(pt_ref, q_ref, kn_ref, vn_ref, *refs, blocks_per_step):
    del pt_ref
    npages = 2 * blocks_per_step
    k_refs, v_refs = refs[0:npages], refs[npages:2 * npages]
    o_ref, q_sc, m_sc, l_sc, g_sc, acc_sc = refs[2 * npages:]
    j = pl.program_id(1)
    last = pl.num_programs(1) - 1
    nb = acc_sc.shape[0]
    nq = q_ref.shape[1]
    nh = HEADS_PER_GROUP
    nr = nh * nq
    hw = HEAD_WIDTH
    keys = 2 * k_refs[0].shape[3]
    row_w = lax.broadcasted_iota(jnp.int32, (nr, hw), 0)
    lane_w = lax.broadcasted_iota(jnp.int32, (nr, hw), 1)
    own_head = (row_w // nq) == (lane_w // HEAD_DIM)
    lane = lax.broadcasted_iota(jnp.int32, (nr, LANES), 1)

    def to_queries(t):
        return jnp.sum(jnp.where(own_head, t, 0.0).reshape(nh, nq, hw), axis=0)

    @pl.when(j == 0)
    def _():
        qbd = jnp.where(own_head, jnp.concatenate([q_ref[0]] * nh, axis=0), 0.0)
        hi = qbd.astype(BF16)
        q_sc[0:nr] = hi
        q_sc[nr:2 * nr] = (qbd - hi.astype(F32)).astype(BF16)
        m_sc[...] = jnp.zeros_like(m_sc)
        l_sc[...] = jnp.zeros_like(l_sc)
        g_sc[...] = jnp.zeros_like(g_sc)

    def scores(kt):
        s2 = jnp.dot(q_sc[...], kt, preferred_element_type=F32)
        return s2[0:nr] + s2[nr:2 * nr]

    @pl.when(j < last)
    def _():
        m_all, l_all, g_all = m_sc[...], l_sc[...], g_sc[...]
        for i in range(blocks_per_step):
            n = j * blocks_per_step + i
            kt = jnp.concatenate([k_refs[2 * i][0, 0], k_refs[2 * i + 1][0, 0]], axis=1).astype(BF16)
            vt = jnp.concatenate([v_refs[2 * i][0, 0], v_refs[2 * i + 1][0, 0]], axis=1).astype(BF16)
            s = scores(kt)
            gate = jnp.sum(s, axis=1, keepdims=True) * (1.0 / keys)
            mx = jnp.max(s, axis=1, keepdims=True)
            p = jnp.exp(s - mx)
            den = jnp.sum(p, axis=1, keepdims=True)
            pv = lax.dot_general(p.astype(BF16), vt, NT_DIMS, preferred_element_type=F32)
            acc_sc[n] = to_queries(pv)
            m_all = jnp.where(lane == n, mx, m_all)
            l_all = jnp.where(lane == n, den, l_all)
            g_all = jnp.where(lane == n, gate, g_all)
        m_sc[...], l_sc[...], g_sc[...] = m_all, l_all, g_all

    @pl.when(j == last)
    def _():
        cur = jnp.where(lane < nb, g_sc[...], -jnp.inf)
        sel = jnp.zeros((nr, LANES), jnp.bool_)
        for _ in range(MOBA_TOPK):
            top = jnp.max(cur, axis=1, keepdims=True)
            first = jnp.min(jnp.where(cur == top, lane, LANES), axis=1, keepdims=True)
            pick = lane == first
            sel = sel | pick
            cur = jnp.where(pick, -jnp.inf, cur)
        pad = jnp.zeros((nq, hw), F32)
        kn = jnp.concatenate([kn_ref[0], pad], axis=0)
        vn = jnp.concatenate([vn_ref[0], pad], axis=0)
        s2 = lax.dot_general(q_sc[...], kn.astype(BF16), NT_DIMS, preferred_element_type=F32)
        s_own = s2[0:nr] + s2[nr:2 * nr]
        qpos = lax.broadcasted_iota(jnp.int32, (nr, 2 * nq), 0) % nq
        kpos = lax.broadcasted_iota(jnp.int32, (nr, 2 * nq), 1)
        s_own = jnp.where(kpos <= qpos, s_own, NEG)
        m_own = jnp.max(s_own, axis=1, keepdims=True)
        p_own = jnp.exp(s_own - m_own)
        l_own = jnp.sum(p_own, axis=1, keepdims=True)
        acc_own = to_queries(jnp.dot(p_own, vn, preferred_element_type=F32))
        m_all = m_sc[...]
        m_tot = jnp.maximum(jnp.max(jnp.where(sel, m_all, NEG), axis=1, keepdims=True), m_own)
        w = jnp.where(sel, jnp.exp(m_all - m_tot), 0.0)
        w_own = jnp.exp(m_own - m_tot)
        l_tot = jnp.sum(w * l_sc[...], axis=1, keepdims=True) + w_own * l_own

        def widen(col):
            return to_queries(jnp.broadcast_to(col, (nr, hw)))

        num = widen(w_own) * acc_own
        for n in range(nb):
            num = num + widen(w[:, n:n + 1]) * acc_sc[n]
        o_ref[0] = num / widen(l_tot)


def _moba_sample(qa, ka, va, cache_k, cache_v, page_table):
    b, nq, hw = qa.shape
    nh = HEADS_PER_GROUP
    n_pool, page = cache_k.shape[0], cache_k.shape[1]
    pages_per_blk = MOBA_BLOCK // page
    assert pages_per_blk == 2 and page == LANES
    nb = page_table.shape[1] // pages_per_blk
    assert MOBA_TOPK <= nb <= LANES
    ck = cache_k.transpose(0, 2, 3, 1).reshape(1, n_pool, hw, page)
    cv = cache_v.transpose(0, 2, 3, 1).reshape(1, n_pool, hw, page)
    tok = lambda i, j, pt: (i, 0, 0)
    per_step = next(g for g in (4, 2, 1) if nb % g == 0)
    nsteps = nb // per_step

    def page_spec(p):
        return pl.BlockSpec((1, 1, hw, page),
                            lambda i, j, pt: (0, pt[i, jnp.minimum(j, nsteps - 1) * 2 * per_step + p], 0, 0))

    pages = [page_spec(p) for p in range(2 * per_step)]
    grid_spec = pltpu.PrefetchScalarGridSpec(
        num_scalar_prefetch=1,
        grid=(b, nsteps + 1),
        in_specs=[
            pl.BlockSpec((1, nq, hw), tok),
            pl.BlockSpec((1, nq, hw), tok),
            pl.BlockSpec((1, nq, hw), tok),
        ] + pages + pages,
        out_specs=pl.BlockSpec((1, nq, hw), tok),
        scratch_shapes=[
            pltpu.VMEM((2 * nh * nq, hw), BF16),
            pltpu.VMEM((nh * nq, LANES), F32),
            pltpu.VMEM((nh * nq, LANES), F32),
            pltpu.VMEM((nh * nq, LANES), F32),
            pltpu.VMEM((nb, nq, hw), F32),
        ],
    )
    return pl.pallas_call(
        functools.partial(_moba_sample_kernel, blocks_per_step=per_step),
        grid_spec=grid_spec,
        out_shape=jax.ShapeDtypeStruct((b, nq, hw), F32),
        compiler_params=_params("arbitrary", "arbitrary"),
        name="moba_sample",
    )(page_table, qa, ka, va, *([ck] * (2 * per_step)), *([cv] * (2 * per_step)))


def _gla_kernel(q_ref, k_ref, v_ref, la_ref, gs_ref, gn_ref, s0_ref, o_ref, sout_ref, st_sc, *, chunk, mm_dtype):
    c = chunk
    n = q_ref.shape[1] // c
    hd = HEAD_DIM
    lane = lax.broadcasted_iota(jnp.int32, (c, LANES), 1)
    head0 = lane < hd
    r = lax.broadcasted_iota(jnp.int32, (c, c), 0)
    cc = lax.broadcasted_iota(jnp.int32, (c, c), 1)
    causal = r >= cc
    tri = jnp.where(causal, 1.0, 0.0)
    br = lax.broadcasted_iota(jnp.int32, (LANES, LANES), 0) // hd
    bc = lax.broadcasted_iota(jnp.int32, (LANES, LANES), 1) // hd
    same_head = br == bc
    head_mean = jnp.where(same_head, 1.0 / hd, 0.0)
    gn = gn_ref[...]

    s0 = s0_ref[0]
    zero = jnp.zeros((hd, hd), F32)
    st_sc[...] = jnp.concatenate(
        [jnp.concatenate([s0[0], zero], axis=1), jnp.concatenate([zero, s0[1]], axis=1)], axis=0)

    def split_dot(a, b_mat, stack_axis):
        hi = a.astype(BF16)
        lo = (a - hi.astype(F32)).astype(BF16)
        both = jnp.concatenate([hi, lo], axis=stack_axis)
        if stack_axis == 1:
            r = jnp.dot(b_mat, both, preferred_element_type=F32)
            return r[:, 0:LANES] + r[:, LANES:2 * LANES]
        r = jnp.dot(both, b_mat, preferred_element_type=F32)
        return r[0:c] + r[c:2 * c]

    def chunk(ci):
        rows = pl.ds(pl.multiple_of(ci * c, c), c)
        la = la_ref[0, rows, :]
        q = q_ref[0, rows, :]
        k = k_ref[0, rows, :]
        v = v_ref[0, rows, :]
        if mm_dtype == BF16:
            b = split_dot(la, tri.astype(BF16), 1)
        else:
            b = jnp.dot(tri, la, precision=HIGHEST, preferred_element_type=F32)
        st = st_sc[...]
        b_last = b[c - 1:c, :]
        b_mid = b[c // 2 - 1:c // 2, :]
        qt = q * jnp.exp(b - b_mid)
        kt = k * jnp.exp(b_mid - b)
        qts = jnp.concatenate([jnp.where(head0, qt, 0.0), jnp.where(head0, 0.0, qt)], axis=0)
        a = lax.dot_general(qts.astype(mm_dtype), kt.astype(mm_dtype), NT_DIMS, preferred_element_type=F32)
        a0 = jnp.where(causal, a[0:c], 0.0).astype(mm_dtype)
        a1 = jnp.where(causal, a[c:2 * c], 0.0).astype(mm_dtype)
        vm = v.astype(mm_dtype)
        o = jnp.where(head0, jnp.dot(a0, vm, preferred_element_type=F32),
                      jnp.dot(a1, vm, preferred_element_type=F32))
        qe = q * jnp.exp(b)
        o = o + lax.dot_general(qe.astype(mm_dtype), st.astype(mm_dtype), NT_DIMS, preferred_element_type=F32)
        kd = k * jnp.exp(b_last - b)
        upd = lax.dot_general(vm, kd.astype(mm_dtype), TN_DIMS, preferred_element_type=F32)
        st_sc[...] = st * jnp.exp(b_last) + jnp.where(same_head, upd, 0.0)
        if mm_dtype == BF16:
            ms = split_dot(o * o, head_mean.astype(BF16), 0)
        else:
            ms = jnp.dot(o * o, head_mean, precision=HIGHEST, preferred_element_type=F32)
        o_ref[0, rows, :] = o * lax.rsqrt(ms + EPS) * gn * gs_ref[0, rows, :]

    unroll = 4 if n % 4 == 0 else 1

    def body(i, carry):
        for u in range(unroll):
            chunk(i * unroll + u)
        return carry

    lax.fori_loop(0, n // unroll, body, 0)
    st = st_sc[...]
    sout_ref[0, 0] = st[0:hd, 0:hd]
    sout_ref[0, 1] = st[hd:2 * hd, hd:2 * hd]


def _gla(qg, kg, vg, la, gs, gn_pair, s0_t, chunk, mm_dtype):
    b, l, hw = qg.shape
    pairs = hw // LANES
    seq = pl.BlockSpec((1, l, LANES), lambda i, p: (i, 0, p))
    st_spec = pl.BlockSpec((1, 2, HEAD_DIM, HEAD_DIM), lambda i, p: (i, p, 0, 0))
    return pl.pallas_call(
        functools.partial(_gla_kernel, chunk=chunk, mm_dtype=mm_dtype),
        grid=(b, pairs),
        in_specs=[seq, seq, seq, seq, seq, pl.BlockSpec((1, LANES), lambda i, p: (0, 0)), st_spec],
        out_specs=[seq, st_spec],
        out_shape=[jax.ShapeDtypeStruct((b, l, hw), F32),
                   jax.ShapeDtypeStruct((b, HEADS_PER_GROUP, HEAD_DIM, HEAD_DIM), F32)],
        scratch_shapes=[pltpu.VMEM((LANES, LANES), F32)],
        compiler_params=_params("arbitrary", "arbitrary"),
        name="gla_chunked",
    )(qg, kg, vg, la, gs, gn_pair, s0_t)


def _mix_ffn_kernel(x_ref, oa_ref, og_ref, mod_ref, g_ref, wo_a_ref, wo_g_ref, wg_ref, wu_ref, wd_ref, o_ref,
                    x1_sc, h_sc, acc_sc):
    j = pl.program_id(2)
    bb, tl, d = x_ref.shape
    tm = bb * tl

    @pl.when(j == 0)
    def _():
        m = mod_ref[...]
        mix = (_mm(oa_ref[...].reshape(tm, -1), wo_a_ref[...])
               + _mm(og_ref[...].reshape(tm, -1), wo_g_ref[...]))
        x1 = x_ref[...] + m[:, 2:3, :] * mix.reshape(bb, tl, d)
        h = _rms(x1, g_ref[...]) * (1.0 + m[:, 4:5, :]) + m[:, 3:4, :]
        x1_sc[...] = x1.reshape(tm, d)
        h_sc[...] = h.reshape(tm, d).astype(h_sc.dtype)
        acc_sc[...] = jnp.zeros_like(acc_sc)

    h = h_sc[...]
    acc_sc[...] += _mm(_silu(_mm(h, wg_ref[...])) * _mm(h, wu_ref[...]), wd_ref[...])

    @pl.when(j == pl.num_programs(2) - 1)
    def _():
        g2 = mod_ref[...][:, 5:6, :]
        o_ref[...] = x1_sc[...].reshape(bb, tl, d) + g2 * acc_sc[...].reshape(bb, tl, d)


def _mix_ffn(x, oa, og, mod, g, wo_a, wo_g, wg, wu, wd):
    b, l, d = x.shape
    hw = oa.shape[-1]
    f = wg.shape[1]
    bb, tl = _token_tiles(b, l, 512 if l >= 512 else 256)
    tm = bb * tl
    fc = f // 2 if wg.dtype == BF16 else 256
    tok = lambda i, t, j: (i, t, 0)
    const2 = lambda i, t, j: (0, 0)
    return pl.pallas_call(
        _mix_ffn_kernel,
        grid=(b // bb, l // tl, f // fc),
        in_specs=[
            pl.BlockSpec((bb, tl, d), tok),
            pl.BlockSpec((bb, tl, hw), tok),
            pl.BlockSpec((bb, tl, hw), tok),
            pl.BlockSpec((bb, 6, d), lambda i, t, j: (i, 0, 0)),
            pl.BlockSpec((1, d), const2),
            pl.BlockSpec(wo_a.shape, const2),
            pl.BlockSpec(wo_g.shape, const2),
            pl.BlockSpec((d, fc), lambda i, t, j: (0, j)),
            pl.BlockSpec((d, fc), lambda i, t, j: (0, j)),
            pl.BlockSpec((fc, d), lambda i, t, j: (j, 0)),
        ],
        out_specs=pl.BlockSpec((bb, tl, d), tok),
        out_shape=jax.ShapeDtypeStruct((b, l, d), F32),
        scratch_shapes=[pltpu.VMEM((tm, d), F32), pltpu.VMEM((tm, d), wg.dtype), pltpu.VMEM((tm, d), F32)],
        compiler_params=_params("arbitrary", "arbitrary", "arbitrary"),
        name="outproj_ffn",
    )(x, oa, og, mod, g, wo_a, wo_g, wg, wu, wd)


def _cm_in_kernel(x_ref, mod_ref, g_ref, w_ref, lng_ref, lnb_ref, u_ref, v_ref):
    bb, tl, d = x_ref.shape
    tm = bb * tl
    cw = u_ref.shape[-1]
    m = mod_ref[...]
    h = _rms(x_ref[...], g_ref[...]) * (1.0 + m[:, 1:2, :]) + m[:, 0:1, :]
    hb = h.reshape(tm, d).astype(w_ref.dtype)
    u_ref[...] = _gelu_tanh(_mm(hb, w_ref[:, 0:cw])).reshape(bb, tl, cw)
    zv = _gelu_tanh(_mm(hb, w_ref[:, cw:2 * cw]))
    mu = jnp.mean(zv, axis=-1, keepdims=True)
    zc = zv - mu
    var = jnp.mean(zc * zc, axis=-1, keepdims=True)
    v_ref[...] = (zc * lax.rsqrt(var + EPS) * lng_ref[...] + lnb_ref[...]).reshape(bb, tl, cw)


def _cm_in(x, mod, g, w, ln_g, ln_b):
    b, l, d = x.shape
    cw = w.shape[1] // 2
    bb, tl = _token_tiles(b, l, 256)
    tok = lambda i, t: (i, t, 0)
    const2 = lambda i, t: (0, 0)
    out_spec = pl.BlockSpec((bb, tl, cw), tok)
    out_sds = jax.ShapeDtypeStruct((b, l, cw), F32)
    return pl.pallas_call(
        _cm_in_kernel,
        grid=(b // bb, l // tl),
        in_specs=[
            pl.BlockSpec((bb, tl, d), tok),
            pl.BlockSpec((bb, 6, d), lambda i, t: (i, 0, 0)),
            pl.BlockSpec((1, d), const2),
            pl.BlockSpec(w.shape, const2),
            pl.BlockSpec((1, cw), const2),
            pl.BlockSpec((1, cw), const2),
        ],
        out_specs=[out_spec, out_spec],
        out_shape=[out_sds, out_sds],
        compiler_params=_params("arbitrary", "arbitrary"),
        name="cm_in_gelu_ln",
    )(x, mod, g, w, ln_g, ln_b)


def _cm_out_kernel(x_ref, u_ref, v_ref, mod_ref, g_ref, ws_ref, bs_ref, wo_ref, wr_ref,
                   x3_ref, h_ref, gates_ref, us_sc):
    bb, tl, d = x_ref.shape
    tm = bb * tl
    cw = u_ref.shape[-1]
    ngroups, cm, _ = ws_ref.shape
    gd = cw // ngroups
    u = u_ref[...].reshape(tm, cw)
    v = v_ref[...].reshape(tm, cw)
    bs = bs_ref[...]
    mm_dtype = wo_ref.dtype
    r = lax.broadcasted_iota(jnp.int32, (cm, cm), 0)
    c = lax.broadcasted_iota(jnp.int32, (cm, cm), 1)
    for g in range(ngroups):
        wsg = jnp.where(r >= c, ws_ref[g], 0.0).astype(mm_dtype)
        cols = slice(g * gd, (g + 1) * gd)
        for sb in range(tm // cm):
            rows = slice(sb * cm, (sb + 1) * cm)
            s = _mm(wsg, v[rows, cols].astype(mm_dtype)) + bs[rows, g:g + 1]
            us_sc[rows, cols] = (u[rows, cols] * s).astype(mm_dtype)
    y = _mm(us_sc[...], wo_ref[...])
    m = mod_ref[...]
    x3 = x_ref[...] + m[:, 2:3, :] * y.reshape(bb, tl, d)
    x3_ref[...] = x3
    h = (_rms(x3, g_ref[...]) * (1.0 + m[:, 4:5, :]) + m[:, 3:4, :]).reshape(tm, d)
    h_ref[...] = h.astype(BF16)
    wr = wr_ref[...]
    if mm_dtype == BF16:
        h_hi = h.astype(BF16)
        h_lo = (h - h_hi.astype(F32)).astype(BF16)
        wr_hi = wr.astype(BF16)
        wr_lo = (wr - wr_hi.astype(F32)).astype(BF16)
        both = jnp.dot(jnp.concatenate([h_hi, h_lo], axis=0), wr_hi, preferred_element_type=F32)
        logits = both[0:tm] + both[tm:2 * tm] + jnp.dot(h_hi, wr_lo, preferred_element_type=F32)
    else:
        logits = jnp.dot(h, wr, precision=HIGHEST, preferred_element_type=F32)
    ne = logits.shape[1]
    col = lax.broadcasted_iota(jnp.int32, (tm, ne), 1)
    m1 = jnp.max(logits, axis=1, keepdims=True)
    i1 = jnp.min(jnp.where(logits == m1, col, ne), axis=1, keepdims=True)
    rest = jnp.where(col == i1, -jnp.inf, logits)
    m2 = jnp.max(rest, axis=1, keepdims=True)
    i2 = jnp.min(jnp.where(rest == m2, col, ne), axis=1, keepdims=True)
    e2 = jnp.exp(m2 - m1)
    gates_ref[...] = jnp.where(col == i1, 1.0 / (1.0 + e2), 0.0) + jnp.where(col == i2, e2 / (1.0 + e2), 0.0)


def _cm_out(x, u, v, mod, g, ws_mix, bs_rows, wo, wr, cm):
    b, l, d = x.shape
    cw = u.shape[-1]
    ne = wr.shape[1]
    bb, tl = _token_tiles(b, l, 256)
    tm = bb * tl
    tok = lambda i, t: (i, t, 0)
    flat = lambda i, t: (i * (l // tl) + t, 0)
    const2 = lambda i, t: (0, 0)
    return pl.pallas_call(
        _cm_out_kernel,
        grid=(b // bb, l // tl),
        in_specs=[
            pl.BlockSpec((bb, tl, d), tok),
            pl.BlockSpec((bb, tl, cw), tok),
            pl.BlockSpec((bb, tl, cw), tok),
            pl.BlockSpec((bb, 6, d), lambda i, t: (i, 0, 0)),
            pl.BlockSpec((1, d), const2),
            pl.BlockSpec(ws_mix.shape, lambda i, t: (0, 0, 0)),
            pl.BlockSpec((tm, bs_rows.shape[1]), const2),
            pl.BlockSpec(wo.shape, const2),
            pl.BlockSpec(wr.shape, const2),
        ],
        out_specs=[pl.BlockSpec((bb, tl, d), tok), pl.BlockSpec((tm, d), flat), pl.BlockSpec((tm, ne), flat)],
        out_shape=[jax.ShapeDtypeStruct((b, l, d), F32), jax.ShapeDtypeStruct((b * l, d), BF16),
                   jax.ShapeDtypeStruct((b * l, ne), F32)],
        scratch_shapes=[pltpu.VMEM((tm, cw), wo.dtype)],
        compiler_params=_params("arbitrary", "arbitrary"),
        name="cm_out_router",
    )(x, u, v, mod, g, ws_mix, bs_rows, wo, wr)


MOE_SUB = 256
MOE_ALIGN = 16
MOE_WIN_SMALL = 128
MOE_WIN_FULL = MOE_SUB + MOE_ALIGN
MOE_ROW_CHUNK = 512
MOE_BLOCK_SUBS = 8


def _moe_routing(gates, nb, nsub):
    t, ne = gates.shape
    routed = (gates > 0.0).astype(jnp.int32).reshape(nb, nsub, MOE_SUB, ne)
    rank = jnp.cumsum(routed, axis=2) - 1
    cnt = jnp.sum(routed, axis=2)
    off = jnp.cumsum(cnt, axis=1) - cnt
    total = jnp.sum(cnt, axis=1)
    start = (off // MOE_ALIGN) * MOE_ALIGN
    shift = off - start
    win_row = jnp.where(routed > 0, rank + shift[:, :, None, :], -1)
    win_row = win_row.transpose(0, 3, 1, 2).reshape(nb, ne, nsub * MOE_SUB)
    used = jnp.where(cnt > 0, shift + cnt, 0)
    scalars = jnp.concatenate([start.transpose(0, 2, 1).reshape(-1), used.transpose(0, 2, 1).reshape(-1),
                               total.reshape(-1)]).astype(jnp.int32)
    return win_row, scalars


def _moe_kernel(sc_ref, h_ref, gates_ref, row_ref, wg_ref, wu_ref, wd_ref, o_ref, x_sc, y_sc):
    b = pl.program_id(0)
    e = pl.program_id(1)
    j = pl.program_id(2)
    ne = pl.num_programs(1)
    tb = h_ref.shape[0]
    nsub = tb // MOE_SUB
    cap = x_sc.shape[0]
    n_be = pl.num_programs(0) * ne
    seg = (b * ne + e) * nsub
    total = sc_ref[2 * n_be * nsub + b * ne + e]

    def windows(fn):
        for s in range(nsub):
            start = pl.multiple_of(sc_ref[seg + s], MOE_ALIGN)
            used = sc_ref[n_be * nsub + seg + s]
            tok = slice(s * MOE_SUB, (s + 1) * MOE_SUB)
            win_row = row_ref[0, pl.ds(e, 1), tok]

            def run(rows, start=start, tok=tok, win_row=win_row):
                ridx = lax.broadcasted_iota(jnp.int32, (rows, MOE_SUB), 0)
                fn(pl.ds(start, rows), tok, jnp.where(ridx == win_row, 1.0, 0.0).astype(BF16))

            pl.when((used > 0) & (used <= MOE_WIN_SMALL))(functools.partial(run, MOE_WIN_SMALL))
            pl.when(used > MOE_WIN_SMALL)(functools.partial(run, MOE_WIN_FULL))

    @pl.when((e == 0) & (j == 0))
    def _():
        o_ref[...] = jnp.zeros_like(o_ref)

    @pl.when(j == 0)
    def _():
        def clear(c, carry):
            rows = pl.ds(pl.multiple_of(c * MOE_ROW_CHUNK, MOE_ROW_CHUNK), MOE_ROW_CHUNK)
            x_sc[rows, :] = jnp.zeros((MOE_ROW_CHUNK, x_sc.shape[1]), BF16)
            y_sc[rows, :] = jnp.zeros((MOE_ROW_CHUNK, y_sc.shape[1]), F32)
            return carry

        lax.fori_loop(0, jnp.minimum((total + MOE_WIN_FULL + MOE_ROW_CHUNK - 1) // MOE_ROW_CHUNK,
                                     cap // MOE_ROW_CHUNK), clear, 0)

        def dispatch(rows, tok, onehot):
            picked = jnp.dot(onehot, h_ref[tok, :], preferred_element_type=F32).astype(BF16)
            x_sc[rows, :] = x_sc[rows, :] + picked

        windows(dispatch)

    def ffn(r0, nrows):
        rows = pl.ds(pl.multiple_of(r0, MOE_WIN_SMALL), nrows)
        x = x_sc[rows, :]
        gate = jnp.dot(x, wg_ref[0], preferred_element_type=F32)
        up = jnp.dot(x, wu_ref[0], preferred_element_type=F32)
        act = (_silu(gate) * up).astype(BF16)
        y_sc[rows, :] += jnp.dot(act, wd_ref[0], preferred_element_type=F32)

    nfull = total // MOE_ROW_CHUNK
    rem = total - nfull * MOE_ROW_CHUNK
    tail = nfull * MOE_ROW_CHUNK

    def full_chunk(c, carry):
        ffn(c * MOE_ROW_CHUNK, MOE_ROW_CHUNK)
        return carry

    lax.fori_loop(0, nfull, full_chunk, 0)
    take_256 = (rem > 128) & (rem <= 384)
    pl.when(rem > 384)(lambda: ffn(tail, 512))
    pl.when(take_256)(lambda: ffn(tail, 256))
    pl.when((rem > 0) & (rem <= 128))(lambda: ffn(tail, 128))
    pl.when((rem > 256) & (rem <= 384))(lambda: ffn(tail + 256, 128))

    @pl.when(j == pl.num_programs(2) - 1)
    def _():
        gates = gates_ref[...]
        col = lax.broadcasted_iota(jnp.int32, gates.shape, 1)
        ge = jnp.sum(jnp.where(col == e, gates, 0.0), axis=1, keepdims=True)

        def combine(rows, tok, onehot):
            back = lax.dot_general(onehot, y_sc[rows, :].astype(BF16), TN_DIMS, preferred_element_type=F32)
            o_ref[tok, :] = (o_ref[tok, :].astype(F32) + ge[tok] * back).astype(o_ref.dtype)

        windows(combine)


def _moe(h, gates, wg, wu, wd, tb):
    t, d = h.shape
    ne, _, f = wg.shape
    fc = 512
    assert t % tb == 0 and tb % MOE_SUB == 0 and f % fc == 0
    nb, nsub = t // tb, tb // MOE_SUB
    win_row, scalars = _moe_routing(gates, nb, nsub)
    cap = -(-(tb + MOE_WIN_FULL) // MOE_ROW_CHUNK) * MOE_ROW_CHUNK
    grid_spec = pltpu.PrefetchScalarGridSpec(
        num_scalar_prefetch=1,
        grid=(nb, ne, f // fc),
        in_specs=[
            pl.BlockSpec((tb, d), lambda i, e, j, sc: (i, 0)),
            pl.BlockSpec((tb, ne), lambda i, e, j, sc: (i, 0)),
            pl.BlockSpec((1, ne, tb), lambda i, e, j, sc: (i, 0, 0)),
            pl.BlockSpec((1, d, fc), lambda i, e, j, sc: (e, 0, j)),
            pl.BlockSpec((1, d, fc), lambda i, e, j, sc: (e, 0, j)),
            pl.BlockSpec((1, fc, d), lambda i, e, j, sc: (e, j, 0)),
        ],
        out_specs=pl.BlockSpec((tb, d), lambda i, e, j, sc: (i, 0)),
        scratch_shapes=[pltpu.VMEM((cap, d), BF16), pltpu.VMEM((cap, d), F32)],
    )
    return pl.pallas_call(
        _moe_kernel,
        grid_spec=grid_spec,
        out_shape=jax.ShapeDtypeStruct((t, d), BF16),
        compiler_params=_params("arbitrary", "arbitrary", "arbitrary"),
        name="moe_experts",
    )(scalars, h, gates, win_row, wg, wu, wd)


def _final_kernel(x_ref, moe_ref, mod_ref, g_ref, o_ref):
    bb, tl, d = x_ref.shape
    g2 = mod_ref[...][:, 5:6, :]
    x4 = x_ref[...] + g2 * moe_ref[...].reshape(bb, tl, d)
    o_ref[...] = _rms(x4, g_ref[...])


def _final(x, moe, row0, mod, g):
    b, l, d = x.shape
    bb, tl = _token_tiles(b, l, 512 if l >= 512 else 256)
    tm = bb * tl
    assert row0 % tm == 0
    tok = lambda i, t: (i, t, 0)
    return pl.pallas_call(
        _final_kernel,
        grid=(b // bb, l // tl),
        in_specs=[
            pl.BlockSpec((bb, tl, d), tok),
            pl.BlockSpec((tm, d), lambda i, t: (row0 // tm + i * (l // tl) + t, 0)),
            pl.BlockSpec((bb, 6, d), lambda i, t: (i, 0, 0)),
            pl.BlockSpec((1, d), lambda i, t: (0, 0)),
        ],
        out_specs=pl.BlockSpec((bb, tl, d), tok),
        out_shape=jax.ShapeDtypeStruct((b, l, d), F32),
        compiler_params=_params("arbitrary", "arbitrary"),
        name="final_norm",
    )(x, moe, mod, g)


def _layer0(x, mod, pos0, past, w):
    b, l, d = x.shape
    hw = HEAD_WIDTH
    outs = _in_proj(x, mod, w["norm_mix_g0"], w["w_in_main"], w["w_in_lr"], w["wa2_hi"], w["wa2_lo"], w["ba2"], pos0)
    qa, ka, va, qg, kg, vg, gs, la = outs
    if past is None:
        oa = _moba_prompt(qa, ka, va)
        s0_t = jnp.zeros((b, HEADS_PER_GROUP, HEAD_DIM, HEAD_DIM), F32)
    else:
        cache_k, cache_v, state, page_table = past
        oa = _moba_sample(qa, ka, va, cache_k, cache_v, page_table)
        s0_t = jnp.swapaxes(state, -1, -2)
    chunk = GLA_CHUNK if l % GLA_CHUNK == 0 else l
    mm_dtype = BF16 if chunk % 16 == 0 else F32
    og, s_t = _gla(qg, kg, vg, la, gs, w["gla_norm_pair"], s0_t, chunk, mm_dtype)
    x2 = _mix_ffn(x, oa, og, mod, w["norm_ffn_g0"], w["wo_a"], w["wo_g"], w["ffn_wg"], w["ffn_wu"], w["ffn_wd"])
    k_rows = ka.reshape(b, l, HEADS_PER_GROUP, HEAD_DIM)
    v_rows = va.reshape(b, l, HEADS_PER_GROUP, HEAD_DIM)
    return x2, k_rows, v_rows, jnp.swapaxes(s_t, -1, -2)


def _layer1_front(x, mod, w):
    b, l, d = x.shape
    u, v = _cm_in(x, mod, w["norm_mix_g1"], w["cm_w_in"], w["cm_ln_g"], w["cm_ln_b"])
    bb, tl = _token_tiles(b, l, 256)
    tm = bb * tl
    if l >= CM_CHUNK:
        cm = CM_CHUNK
        ws_mix = w["cm_w_s"]
        bs_rows = jnp.tile(w["cm_b_s"].T, (tm // CM_CHUNK, 1))
    else:
        cm = tm
        eye = jnp.eye(tm // l, dtype=F32)
        ws_mix = jax.vmap(lambda m_: jnp.kron(eye, m_))(w["cm_w_s"][:, :l, :l])
        bs_rows = jnp.tile(w["cm_b_s"][:, :l].T, (tm // l, 1))
    x3, h, gates = _cm_out(x, u, v, mod, w["norm_ffn_g1"], ws_mix, bs_rows, w["cm_w_out"], w["moe_w_router"], cm)
    start = ((l - 1) // CM_CHUNK) * CM_CHUNK
    return x3, h, gates, v[:, start:]


def kernel(x_prompt, x_sample, cache_k, cache_v, state_gla, page_table, c_prompt, c_sample, ada_w, ada_b, norm_mix_g, norm_ffn_g, ab_w_in, gla_w_a2, gla_b_a2, gla_norm_g, ab_w_out, ffn_w_gate, ffn_w_up, ffn_w_down, cm_w_in, cm_ln_g, cm_ln_b, cm_w_s, cm_b_s, cm_w_out, moe_w_router, moe_w_gate, moe_w_up, moe_w_down, final_norm_g):
    bp, lp, d = x_prompt.shape
    bs, ls, _ = x_sample.shape
    hw = HEAD_WIDTH
    n_main = 7 * hw
    past_len = page_table.shape[1] * cache_k.shape[2]

    wa2 = jnp.pad(gla_w_a2[0], ((0, LANES - GLA_LOWRANK), (0, 0)))
    wa2_hi = wa2.astype(BF16)
    shared = {
        "norm_mix_g0": norm_mix_g[0][None], "norm_mix_g1": norm_mix_g[1][None],
        "norm_ffn_g0": norm_ffn_g[0][None], "norm_ffn_g1": norm_ffn_g[1][None],
        "wa2_hi": wa2_hi, "wa2_lo": (wa2 - wa2_hi.astype(F32)).astype(BF16),
        "ba2": gla_b_a2[0][None],
        "gla_norm_pair": jnp.tile(gla_norm_g[0], LANES // HEAD_DIM)[None],
        "cm_ln_g": cm_ln_g[0][None], "cm_ln_b": cm_ln_b[0][None],
        "cm_w_s": cm_w_s[0], "cm_b_s": cm_b_s[0],
        "moe_w_router": moe_w_router[0],
    }
    mats = {
        "w_in_main": ab_w_in[0][:, :n_main],
        "w_in_lr": jnp.pad(ab_w_in[0][:, n_main:], ((0, 0), (0, LANES - GLA_LOWRANK))),
        "wo_a": ab_w_out[0][:hw], "wo_g": ab_w_out[0][hw:],
        "ffn_wg": ffn_w_gate[0], "ffn_wu": ffn_w_up[0], "ffn_wd": ffn_w_down[0],
        "cm_w_in": cm_w_in[0], "cm_w_out": cm_w_out[0],
    }
    w = dict(shared, **{k: v.astype(BF16) for k, v in mats.items()})
    w_sample = dict(shared, **mats)

    rows = bp + bs
    rows_pad = -(-rows // 16) * 16
    c_all = jnp.pad(jnp.concatenate([c_prompt, c_sample], axis=0), ((0, rows_pad - rows), (0, 0)))
    mod_all = _modulation(c_all, ada_w, ada_b)
    mod_p = [mod_all[l, :bp].reshape(bp, 6, d) for l in range(2)]
    mod_s = [mod_all[l, bp:rows].reshape(bs, 6, d) for l in range(2)]

    x2_p, k_p, v_p, gla_p = _layer0(x_prompt, mod_p[0], 0, None, w)
    x2_s, k_s, v_s, gla_s = _layer0(x_sample, mod_s[0], past_len,
                                    (cache_k[0], cache_v[0], state_gla[0], page_table), w_sample)

    x3_p, h_p, gates_p, cmv_p = _layer1_front(x2_p, mod_p[1], w)
    x3_s, h_s, gates_s, cmv_s = _layer1_front(x2_s, mod_s[1], w_sample)

    t_all = bp * lp + bs * ls
    subs = -(-t_all // MOE_SUB)
    n_blocks = -(-subs // MOE_BLOCK_SUBS)
    tb = -(-subs // n_blocks) * MOE_SUB
    pad = n_blocks * tb - t_all
    h_all = jnp.concatenate([h_p, h_s, jnp.zeros((pad, d), BF16)], axis=0)
    gates_all = jnp.concatenate([gates_p, gates_s, jnp.zeros((pad, gates_p.shape[1]), F32)], axis=0)
    moe = _moe(h_all, gates_all, moe_w_gate[0].astype(BF16), moe_w_up[0].astype(BF16),
               moe_w_down[0].astype(BF16), tb)

    fg = final_norm_g[None]
    y_p = _final(x3_p, moe, 0, mod_p[1], fg)
    y_s = _final(x3_s, moe, bp * lp, mod_s[1], fg)

    return (y_p, y_s, k_p[None], v_p[None], k_s[None], v_s[None], gla_p[None], gla_s[None],
            cmv_p[None], cmv_s[None])
```

```python
import functools

import jax
import jax.numpy as jnp
from jax import lax
from jax.experimental import pallas as pl
from jax.experimental.pallas import tpu as pltpu

F32 = jnp.float32
BF16 = jnp.bfloat16
HIGHEST = lax.Precision.HIGHEST

HEAD_DIM = 64
HEADS_PER_GROUP = 8
HEAD_WIDTH = HEADS_PER_GROUP * HEAD_DIM
MOBA_BLOCK = 256
MOBA_TOPK = 3
ROPE_THETA = 10000.0
GLA_LOWRANK = 16
GLA_GATE_NORM = 16.0
GLA_CHUNK = 64
CM_CHUNK = 128
CM_GROUPS = 8
MOE_TOP_K = 2
EPS = 1e-6
NEG = -1e30

LANES = 128
VMEM_LIMIT_BYTES = 56 * 1024 * 1024

NT_DIMS = (((1,), (1,)), ((), ()))
TN_DIMS = (((0,), (0,)), ((), ()))


def _params(*sem):
    return pltpu.CompilerParams(dimension_semantics=sem, vmem_limit_bytes=VMEM_LIMIT_BYTES)


def _rms(x, g):
    return x * lax.rsqrt(jnp.mean(x * x, axis=-1, keepdims=True) + EPS) * g


def _silu(x):
    return x * jax.nn.sigmoid(x)


def _mm(a, w):
    if w.dtype == F32:
        return jnp.dot(a.astype(F32), w, precision=HIGHEST, preferred_element_type=F32)
    return jnp.dot(a.astype(BF16), w, preferred_element_type=F32)


def _gelu_tanh(x):
    c = 0.7978845608028654
    return x * (0.5 + 0.5 * jnp.tanh(x * (c + (c * 0.044715) * (x * x))))


def _token_tiles(b, l, target):
    if l >= target:
        assert l % target == 0
        return 1, target
    bb = min(b, target // l)
    assert target % l == 0 and b % bb == 0
    return bb, l


def _mod_kernel(c_ref, w_ref, b_ref, o_ref):
    s = _silu(c_ref[...])
    r = s.shape[0]
    s_hi = s.astype(BF16)
    s_lo = (s - s_hi.astype(F32)).astype(BF16)
    w = w_ref[0]
    w_hi = w.astype(BF16)
    w_lo = (w - w_hi.astype(F32)).astype(BF16)
    both = jnp.dot(jnp.concatenate([s_hi, s_lo], axis=0), w_hi, preferred_element_type=F32)
    o_ref[0] = both[0:r] + both[r:2 * r] + jnp.dot(s_hi, w_lo, preferred_element_type=F32) + b_ref[0]


def _modulation(c_all, ada_w, ada_b):
    r, d = c_all.shape
    depth, _, n = ada_w.shape
    tn = 1536
    return pl.pallas_call(
        _mod_kernel,
        grid=(depth, n // tn),
        in_specs=[
            pl.BlockSpec((r, d), lambda l, j: (0, 0)),
            pl.BlockSpec((1, d, tn), lambda l, j: (l, 0, j)),
            pl.BlockSpec((1, 1, tn), lambda l, j: (l, 0, j)),
        ],
        out_specs=pl.BlockSpec((1, r, tn), lambda l, j: (l, 0, j)),
        out_shape=jax.ShapeDtypeStruct((depth, r, n), F32),
        compiler_params=_params("arbitrary", "arbitrary"),
        name="adaln_modulation",
    )(c_all, ada_w, ada_b.reshape(depth, 1, n))


def _in_proj_kernel(x_ref, mod_ref, g_ref, w_ref, wlr_ref, wa2h_ref, wa2l_ref, ba2_ref, cos_ref, sin_ref,
                    qa_ref, ka_ref, va_ref, qg_ref, kg_ref, vg_ref, gs_ref, la_ref):
    bb, tl, d = x_ref.shape
    tm = bb * tl
    hw = HEAD_WIDTH
    m = mod_ref[...]
    h = _rms(x_ref[...], g_ref[...]) * (1.0 + m[:, 1:2, :]) + m[:, 0:1, :]
    hb = h.reshape(tm, d).astype(w_ref.dtype)

    def proj(c):
        return _mm(hb, w_ref[:, c * hw:(c + 1) * hw])

    cos = jnp.concatenate([cos_ref[...]] * (hw // LANES), axis=-1)[None]
    sin = jnp.concatenate([sin_ref[...]] * (hw // LANES), axis=-1)[None]
    lane = lax.broadcasted_iota(jnp.int32, (tm, hw), 1)
    first_half = (lane % HEAD_DIM) < (HEAD_DIM // 2)

    def rope(z):
        partner = jnp.where(first_half, pltpu.roll(z, hw - HEAD_DIM // 2, 1), pltpu.roll(z, HEAD_DIM // 2, 1))
        return z.reshape(bb, tl, hw) * cos + partner.reshape(bb, tl, hw) * sin

    scale = HEAD_DIM ** -0.5
    qa_ref[...] = rope(proj(0)) * scale
    ka_ref[...] = rope(proj(1))
    va_ref[...] = proj(2).reshape(bb, tl, hw)
    qg_ref[...] = (proj(3) * scale).reshape(bb, tl, hw)
    kg_ref[...] = proj(4).reshape(bb, tl, hw)
    vg_ref[...] = proj(5).reshape(bb, tl, hw)
    gs_ref[...] = _silu(proj(6)).reshape(bb, tl, hw)
    zlr = _mm(hb, wlr_ref[...])
    hi = zlr.astype(BF16)
    lo = (zlr - hi.astype(F32)).astype(BF16)
    pre = (jnp.dot(hi, wa2h_ref[...], preferred_element_type=F32)
           + jnp.dot(lo, wa2h_ref[...], preferred_element_type=F32)
           + jnp.dot(hi, wa2l_ref[...], preferred_element_type=F32)) + ba2_ref[...]
    log_sig = jnp.minimum(pre, 0.0) - jnp.log(1.0 + jnp.exp(-jnp.abs(pre)))
    la_ref[...] = (log_sig * (1.0 / GLA_GATE_NORM)).reshape(bb, tl, hw)


def _rope_tables(pos0, l):
    half = HEAD_DIM // 2
    inv = 1.0 / (ROPE_THETA ** (jnp.arange(half, dtype=F32) / half))
    ang = (pos0 + jnp.arange(l, dtype=jnp.int32)).astype(F32)[:, None] * inv[None, :]
    cos, sin = jnp.cos(ang), jnp.sin(ang)
    reps = LANES // HEAD_DIM
    cos_t = jnp.tile(jnp.concatenate([cos, cos], axis=-1), (1, reps))
    sin_t = jnp.tile(jnp.concatenate([-sin, sin], axis=-1), (1, reps))
    return cos_t, sin_t


def _in_proj(x, mod, g, w_main, w_lr, wa2_hi, wa2_lo, ba2, pos0):
    b, l, d = x.shape
    bb, tl = _token_tiles(b, l, 512 if l >= 512 else 256)
    hw = HEAD_WIDTH
    cos_t, sin_t = _rope_tables(pos0, l)
    tok = lambda i, j: (i, j, 0)
    const2 = lambda i, j: (0, 0)
    out_spec = pl.BlockSpec((bb, tl, hw), tok)
    out_sds = jax.ShapeDtypeStruct((b, l, hw), F32)
    return pl.pallas_call(
        _in_proj_kernel,
        grid=(b // bb, l // tl),
        in_specs=[
            pl.BlockSpec((bb, tl, d), tok),
            pl.BlockSpec((bb, 6, d), lambda i, j: (i, 0, 0)),
            pl.BlockSpec((1, d), const2),
            pl.BlockSpec(w_main.shape, const2),
            pl.BlockSpec(w_lr.shape, const2),
            pl.BlockSpec(wa2_hi.shape, const2),
            pl.BlockSpec(wa2_lo.shape, const2),
            pl.BlockSpec((1, hw), const2),
            pl.BlockSpec((tl, LANES), lambda i, j: (j, 0)),
            pl.BlockSpec((tl, LANES), lambda i, j: (j, 0)),
        ],
        out_specs=[out_spec] * 8,
        out_shape=[out_sds] * 8,
        compiler_params=_params("arbitrary", "arbitrary"),
        name="in_proj_rope",
    )(x, mod, g, w_main, w_lr, wa2_hi, wa2_lo, ba2, cos_t, sin_t)


def _moba_prompt_kernel(q_ref, k_ref, v_ref, o_ref, kb_sc, vb_sc, km_sc, qs_sc, s_sc, p_sc, m_sc, acc_sc):
    ti = pl.program_id(2)
    lk = k_ref.shape[1]
    nblk = lk // MOBA_BLOCK
    nblk8 = -(-nblk // 8) * 8
    tq = q_ref.shape[1]
    nsub = tq // MOBA_BLOCK

    @pl.when(ti == 0)
    def _():
        k = k_ref[0]
        v = v_ref[0]
        row = lax.broadcasted_iota(jnp.int32, (lk, LANES), 0)
        col = lax.broadcasted_iota(jnp.int32, (lk, LANES), 1)
        kb_sc[:, 0:LANES] = k.astype(BF16)
        kb_sc[:, LANES:2 * LANES] = jnp.where(row // MOBA_BLOCK == col, 1.0, 0.0).astype(BF16)
        vb_sc[0] = jnp.where(col < HEAD_DIM, v, 1.0).astype(BF16)
        vb_sc[1] = jnp.where(col >= HEAD_DIM, v, 1.0).astype(BF16)
        km_sc[...] = jnp.zeros_like(km_sc)
        km_sc[0:nblk, :] = jnp.mean(k.reshape(nblk, MOBA_BLOCK, LANES), axis=1)

    q = q_ref[0]
    lane = lax.broadcasted_iota(jnp.int32, (tq, LANES), 1)
    blk = lax.broadcasted_iota(jnp.int32, (nblk8, tq), 0)
    own = ti * nsub + lax.broadcasted_iota(jnp.int32, (nblk8, tq), 1) // MOBA_BLOCK
    past_blk = blk < own
    km = km_sc[...]
    for h in range(2):
        head_lanes = (lane < HEAD_DIM) if h == 0 else (lane >= HEAD_DIM)
        qh = jnp.where(head_lanes, q, 0.0)
        gate = lax.dot_general(km, qh, NT_DIMS, precision=HIGHEST, preferred_element_type=F32)[0:nblk8]
        gate = jnp.where(past_blk, gate, -jnp.inf)
        cnt = jnp.zeros((nblk8, tq), F32)
        for mrow in range(nblk):
            gm = gate[mrow:mrow + 1, :]
            beats = (gm > gate) | ((gm == gate) & (mrow < blk))
            cnt = cnt + jnp.where(beats, 1.0, 0.0)
        sel = (past_blk & (cnt < float(MOBA_TOPK))) | (blk == own)
        bias_t = jnp.concatenate(
            [jnp.where(sel, 0.0, NEG), jnp.full((LANES - nblk8, tq), NEG, F32)], axis=0)
        qs_sc[h, :, 0:LANES] = qh.astype(BF16)
        qs_sc[h, :, LANES:2 * LANES] = bias_t.T.astype(BF16)

    m_sc[...] = jnp.full_like(m_sc, NEG)
    acc_sc[...] = jnp.zeros_like(acc_sc)
    rc = 128
    rowi = lax.broadcasted_iota(jnp.int32, (rc, MOBA_BLOCK), 0)
    coli = lax.broadcasted_iota(jnp.int32, (rc, MOBA_BLOCK), 1)

    def scores(n, slot):
        start = pl.multiple_of(n * MOBA_BLOCK, MOBA_BLOCK)
        kblk = kb_sc[pl.ds(start, MOBA_BLOCK), :]
        for h in range(2):
            s_sc[slot, h] = lax.dot_general(qs_sc[h], kblk, NT_DIMS, preferred_element_type=F32)

    def consume(n, slot, own_sub):
        start = pl.multiple_of(n * MOBA_BLOCK, MOBA_BLOCK)
        for h in range(2):
            for r0 in range(0, tq, rc):
                rows = slice(r0, r0 + rc)
                s = s_sc[slot, h, rows, :]
                if own_sub is not None and r0 // MOBA_BLOCK == own_sub:
                    s = jnp.where(rowi + (r0 - own_sub * MOBA_BLOCK) < coli, NEG, s)
                m_old = m_sc[h, rows, :]
                row_max = jnp.max(jnp.maximum(s[:, 0:LANES], s[:, LANES:2 * LANES]), axis=1, keepdims=True)
                m_new = jnp.maximum(m_old, row_max)
                p_sc[h, rows, :] = jnp.exp(s - jnp.concatenate([m_new, m_new], axis=1)).astype(BF16)
                acc_sc[h, rows, :] = jnp.exp(m_old - m_new) * acc_sc[h, rows, :]
                m_sc[h, rows, :] = m_new
            acc_sc[h] += jnp.dot(p_sc[h], vb_sc[h, pl.ds(start, MOBA_BLOCK), :], preferred_element_type=F32)

    def past_pair(i, carry):
        scores(2 * i + 1, 1)
        consume(2 * i, 0, None)
        scores(2 * i + 2, 0)
        consume(2 * i + 1, 1, None)
        return carry

    scores(0, 0)
    lax.fori_loop(0, ti, past_pair, 0)
    scores(2 * ti + 1, 1)
    consume(2 * ti, 0, 0)
    consume(2 * ti + 1, 1, 1)
    a0 = acc_sc[0]
    a1 = acc_sc[1]
    o_ref[0] = jnp.where(lane < HEAD_DIM, a0 / pltpu.roll(a0, HEAD_DIM, 1), a1 / pltpu.roll(a1, HEAD_DIM, 1))


def _moba_prompt(qa, ka, va):
    b, l, hw = qa.shape
    tq = 2 * MOBA_BLOCK
    assert l % tq == 0
    pairs = hw // LANES
    return pl.pallas_call(
        _moba_prompt_kernel,
        grid=(b, pairs, l // tq),
        in_specs=[
            pl.BlockSpec((1, tq, LANES), lambda i, p, t: (i, t, p)),
            pl.BlockSpec((1, l, LANES), lambda i, p, t: (i, 0, p)),
            pl.BlockSpec((1, l, LANES), lambda i, p, t: (i, 0, p)),
        ],
        out_specs=pl.BlockSpec((1, tq, LANES), lambda i, p, t: (i, t, p)),
        out_shape=jax.ShapeDtypeStruct((b, l, hw), F32),
        scratch_shapes=[
            pltpu.VMEM((l, 2 * LANES), BF16),
            pltpu.VMEM((2, l, LANES), BF16),
            pltpu.VMEM((LANES, LANES), F32),
            pltpu.VMEM((2, tq, 2 * LANES), BF16),
            pltpu.VMEM((2, 2, tq, MOBA_BLOCK), F32),
            pltpu.VMEM((2, tq, MOBA_BLOCK), BF16),
            pltpu.VMEM((2, tq, LANES), F32),
            pltpu.VMEM((2, tq, LANES), F32),
        ],
        compiler_params=_params("arbitrary", "arbitrary", "arbitrary"),
        name="moba_prompt",
    )(qa, ka, va)


def _moba_sample_kernel(pt_ref, q_ref, kn_ref, vn_ref, *refs, blocks_per_step):
    del pt_ref
    npages = 2 * blocks_per_step
    k_refs, v_refs = refs[0:npages], refs[npages:2 * npages]
    o_ref, q_sc, m_sc, l_sc, g_sc, acc_sc = refs[2 * npages:]
    j = pl.program_id(1)
    last = pl.num_programs(1) - 1
    nb = acc_sc.shape[0]
    nq = q_ref.shape[1]
    nh = HEADS_PER_GROUP
    nr = nh * nq
    hw = HEAD_WIDTH
    keys = 2 * k_refs[0].shape[3]
    row_w = lax.broadcasted_iota(jnp.int32, (nr, hw), 0)
    lane_w = lax.broadcasted_iota(jnp.int32, (nr, hw), 1)
    own_head = (row_w // nq) == (lane_w // HEAD_DIM)
    lane = lax.broadcasted_iota(jnp.int32, (nr, LANES), 1)

    def to_queries(t):
        return jnp.sum(jnp.where(own_head, t, 0.0).reshape(nh, nq, hw), axis=0)

    @pl.when(j == 0)
    def _():
        qbd = jnp.where(own_head, jnp.concatenate([q_ref[0]] * nh, axis=0), 0.0)
        hi = qbd.astype(BF16)
        q_sc[0:nr] = hi
        q_sc[nr:2 * nr] = (qbd - hi.astype(F32)).astype(BF16)
        m_sc[...] = jnp.zeros_like(m_sc)
        l_sc[...] = jnp.zeros_like(l_sc)
        g_sc[...] = jnp.zeros_like(g_sc)

    def scores(kt):
        s2 = jnp.dot(q_sc[...], kt, preferred_element_type=F32)
        return s2[0:nr] + s2[nr:2 * nr]

    @pl.when(j < last)
    def _():
        m_all, l_all, g_all = m_sc[...], l_sc[...], g_sc[...]
        for i in range(blocks_per_step):
            n = j * blocks_per_step + i
            kt = jnp.concatenate([k_refs[2 * i][0, 0], k_refs[2 * i + 1][0, 0]], axis=1).astype(BF16)
            vt = jnp.concatenate([v_refs[2 * i][0, 0], v_refs[2 * i + 1][0, 0]], axis=1).astype(BF16)
            s = scores(kt)
            gate = jnp.sum(s, axis=1, keepdims=True) * (1.0 / keys)
            mx = jnp.max(s, axis=1, keepdims=True)
            p = jnp.exp(s - mx)
            den = jnp.sum(p, axis=1, keepdims=True)
            pv = lax.dot_general(p.astype(BF16), vt, NT_DIMS, preferred_element_type=F32)
            acc_sc[n] = to_queries(pv)
            m_all = jnp.where(lane == n, mx, m_all)
            l_all = jnp.where(lane == n, den, l_all)
            g_all = jnp.where(lane == n, gate, g_all)
        m_sc[...], l_sc[...], g_sc[...] = m_all, l_all, g_all

    @pl.when(j == last)
    def _():
        cur = jnp.where(lane < nb, g_sc[...], -jnp.inf)
        sel = jnp.zeros((nr, LANES), jnp.bool_)
        for _ in range(MOBA_TOPK):
            top = jnp.max(cur, axis=1, keepdims=True)
            first = jnp.min(jnp.where(cur == top, lane, LANES), axis=1, keepdims=True)
            pick = lane == first
            sel = sel | pick
            cur = jnp.where(pick, -jnp.inf, cur)
        pad = jnp.zeros((nq, hw), F32)
        kn = jnp.concatenate([kn_ref[0], pad], axis=0)
        vn = jnp.concatenate([vn_ref[0], pad], axis=0)
        s2 = lax.dot_general(q_sc[...], kn.astype(BF16), NT_DIMS, preferred_element_type=F32)
        s_own = s2[0:nr] + s2[nr:2 * nr]
        qpos = lax.broadcasted_iota(jnp.int32, (nr, 2 * nq), 0) % nq
        kpos = lax.broadcasted_iota(jnp.int32, (nr, 2 * nq), 1)
        s_own = jnp.where(kpos <= qpos, s_own, NEG)
        m_own = jnp.max(s_own, axis=1, keepdims=True)
        p_own = jnp.exp(s_own - m_own)
        l_own = jnp.sum(p_own, axis=1, keepdims=True)
        acc_own = to_queries(jnp.dot(p_own, vn, preferred_element_type=F32))
        m_all = m_sc[...]
        m_tot = jnp.maximum(jnp.max(jnp.where(sel, m_all, NEG), axis=1, keepdims=True), m_own)
        w = jnp.where(sel, jnp.exp(m_all - m_tot), 0.0)
        w_own = jnp.exp(m_own - m_tot)
        l_tot = jnp.sum(w * l_sc[...], axis=1, keepdims=True) + w_own * l_own

        def widen(col):
            return to_queries(jnp.broadcast_to(col, (nr, hw)))

        num = widen(w_own) * acc_own
        for n in range(nb):
            num = num + widen(w[:, n:n + 1]) * acc_sc[n]
        o_ref[0] = num / widen(l_tot)


def _moba_sample(qa, ka, va, cache_k, cache_v, page_table):
    b, nq, hw = qa.shape
    nh = HEADS_PER_GROUP
    n_pool, page = cache_k.shape[0], cache_k.shape[1]
    pages_per_blk = MOBA_BLOCK // page
    assert pages_per_blk == 2 and page == LANES
    nb = page_table.shape[1] // pages_per_blk
    assert MOBA_TOPK <= nb <= LANES
    ck = cache_k.transpose(0, 2, 3, 1).reshape(1, n_pool, hw, page)
    cv = cache_v.transpose(0, 2, 3, 1).reshape(1, n_pool, hw, page)
    tok = lambda i, j, pt: (i, 0, 0)
    per_step = next(g for g in (4, 2, 1) if nb % g == 0)
    nsteps = nb // per_step

    def page_spec(p):
        return pl.BlockSpec((1, 1, hw, page),
                            lambda i, j, pt: (0, pt[i, jnp.minimum(j, nsteps - 1) * 2 * per_step + p], 0, 0))

    pages = [page_spec(p) for p in range(2 * per_step)]
    grid_spec = pltpu.PrefetchScalarGridSpec(
        num_scalar_prefetch=1,
        grid=(b, nsteps + 1),
        in_specs=[
            pl.BlockSpec((1, nq, hw), tok),
            pl.BlockSpec((1, nq, hw), tok),
            pl.BlockSpec((1, nq, hw), tok),
        ] + pages + pages,
        out_specs=pl.BlockSpec((1, nq, hw), tok),
        scratch_shapes=[
            pltpu.VMEM((2 * nh * nq, hw), BF16),
            pltpu.VMEM((nh * nq, LANES), F32),
            pltpu.VMEM((nh * nq, LANES), F32),
            pltpu.VMEM((nh * nq, LANES), F32),
            pltpu.VMEM((nb, nq, hw), F32),
        ],
    )
    return pl.pallas_call(
        functools.partial(_moba_sample_kernel, blocks_per_step=per_step),
        grid_spec=grid_spec,
        out_shape=jax.ShapeDtypeStruct((b, nq, hw), F32),
        compiler_params=_params("arbitrary", "arbitrary"),
        name="moba_sample",
    )(page_table, qa, ka, va, *([ck] * (2 * per_step)), *([cv] * (2 * per_step)))


def _gla_kernel(q_ref, k_ref, v_ref, la_ref, gs_ref, gn_ref, s0_ref, o_ref, sout_ref, st_sc, *, chunk, mm_dtype):
    c = chunk
    n = q_ref.shape[1] // c
    npair = q_ref.shape[2] // LANES
    hd = HEAD_DIM
    lane = lax.broadcasted_iota(jnp.int32, (c, LANES), 1)
    head0 = lane < hd
    r = lax.broadcasted_iota(jnp.int32, (c, c), 0)
    cc = lax.broadcasted_iota(jnp.int32, (c, c), 1)
    causal = r >= cc
    tri = jnp.where(causal, 1.0, 0.0)
    br = lax.broadcasted_iota(jnp.int32, (LANES, LANES), 0) // hd
    bc = lax.broadcasted_iota(jnp.int32, (LANES, LANES), 1) // hd
    same_head = br == bc
    head_mean = jnp.where(same_head, 1.0 / hd, 0.0)
    gn = gn_ref[...]

    zero = jnp.zeros((hd, hd), F32)
    for p in range(npair):
        st_sc[p] = jnp.concatenate([jnp.concatenate([s0_ref[0, 2 * p], zero], axis=1),
                                    jnp.concatenate([zero, s0_ref[0, 2 * p + 1]], axis=1)], axis=0)

    def split_dot(a, b_mat, stack_axis):
        hi = a.astype(BF16)
        lo = (a - hi.astype(F32)).astype(BF16)
        both = jnp.concatenate([hi, lo], axis=stack_axis)
        if stack_axis == 1:
            r = jnp.dot(b_mat, both, preferred_element_type=F32)
            return r[:, 0:LANES] + r[:, LANES:2 * LANES]
        r = jnp.dot(both, b_mat, preferred_element_type=F32)
        return r[0:c] + r[c:2 * c]

    def chunk(ci, p):
        rows = pl.ds(pl.multiple_of(ci * c, c), c)
        lanes = slice(p * LANES, (p + 1) * LANES)
        la = la_ref[0, rows, lanes]
        q = q_ref[0, rows, lanes]
        k = k_ref[0, rows, lanes]
        v = v_ref[0, rows, lanes]
        if mm_dtype == BF16:
            b = split_dot(la, tri.astype(BF16), 1)
        else:
            b = jnp.dot(tri, la, precision=HIGHEST, preferred_element_type=F32)
        st = st_sc[p]
        b_last = b[c - 1:c, :]
        b_mid = b[c // 2 - 1:c // 2, :]
        qt = q * jnp.exp(b - b_mid)
        kt = k * jnp.exp(b_mid - b)
        qts = jnp.concatenate([jnp.where(head0, qt, 0.0), jnp.where(head0, 0.0, qt)], axis=0)
        a = lax.dot_general(qts.astype(mm_dtype), kt.astype(mm_dtype), NT_DIMS, preferred_element_type=F32)
        a0 = jnp.where(causal, a[0:c], 0.0).astype(mm_dtype)
        a1 = jnp.where(causal, a[c:2 * c], 0.0).astype(mm_dtype)
        vm = v.astype(mm_dtype)
        o = jnp.where(head0, jnp.dot(a0, vm, preferred_element_type=F32),
                      jnp.dot(a1, vm, preferred_element_type=F32))
        qe = q * jnp.exp(b)
        o = o + lax.dot_general(qe.astype(mm_dtype), st.astype(mm_dtype), NT_DIMS, preferred_element_type=F32)
        kd = k * jnp.exp(b_last - b)
        upd = lax.dot_general(vm, kd.astype(mm_dtype), TN_DIMS, preferred_element_type=F32)
        st_sc[p] = st * jnp.exp(b_last) + jnp.where(same_head, upd, 0.0)
        if mm_dtype == BF16:
            ms = split_dot(o * o, head_mean.astype(BF16), 0)
        else:
            ms = jnp.dot(o * o, head_mean, precision=HIGHEST, preferred_element_type=F32)
        o_ref[0, rows, lanes] = o * lax.rsqrt(ms + EPS) * gn * gs_ref[0, rows, lanes]

    unroll = 4 if n % 4 == 0 else 1

    def body(i, carry):
        for u in range(unroll):
            for p in range(npair):
                chunk(i * unroll + u, p)
        return carry

    lax.fori_loop(0, n // unroll, body, 0)
    for p in range(npair):
        st = st_sc[p]
        sout_ref[0, 2 * p] = st[0:hd, 0:hd]
        sout_ref[0, 2 * p + 1] = st[hd:2 * hd, hd:2 * hd]


def _gla(qg, kg, vg, la, gs, gn_pair, s0_t, chunk, mm_dtype):
    b, l, hw = qg.shape
    pairs = hw // LANES
    npair = pairs if l <= GLA_CHUNK else 1
    seq = pl.BlockSpec((1, l, npair * LANES), lambda i, p: (i, 0, p))
    st_spec = pl.BlockSpec((1, 2 * npair, HEAD_DIM, HEAD_DIM), lambda i, p: (i, p, 0, 0))
    return pl.pallas_call(
        functools.partial(_gla_kernel, chunk=chunk, mm_dtype=mm_dtype),
        grid=(b, pairs // npair),
        in_specs=[seq, seq, seq, seq, seq, pl.BlockSpec((1, LANES), lambda i, p: (0, 0)), st_spec],
        out_specs=[seq, st_spec],
        out_shape=[jax.ShapeDtypeStruct((b, l, hw), F32),
                   jax.ShapeDtypeStruct((b, HEADS_PER_GROUP, HEAD_DIM, HEAD_DIM), F32)],
        scratch_shapes=[pltpu.VMEM((npair, LANES, LANES), F32)],
        compiler_params=_params("arbitrary", "arbitrary"),
        name="gla_chunked",
    )(qg, kg, vg, la, gs, gn_pair, s0_t)


def _mix_ffn_kernel(x_ref, oa_ref, og_ref, mod_ref, g_ref, wo_a_ref, wo_g_ref, wg_ref, wu_ref, wd_ref, o_ref,
                    x1_sc, h_sc, acc_sc):
    j = pl.program_id(2)
    bb, tl, d = x_ref.shape
    tm = bb * tl

    @pl.when(j == 0)
    def _():
        m = mod_ref[...]
        mix = (_mm(oa_ref[...].reshape(tm, -1), wo_a_ref[...])
               + _mm(og_ref[...].reshape(tm, -1), wo_g_ref[...]))
        x1 = x_ref[...] + m[:, 2:3, :] * mix.reshape(bb, tl, d)
        h = _rms(x1, g_ref[...]) * (1.0 + m[:, 4:5, :]) + m[:, 3:4, :]
        x1_sc[...] = x1.reshape(tm, d)
        h_sc[...] = h.reshape(tm, d).astype(h_sc.dtype)
        acc_sc[...] = jnp.zeros_like(acc_sc)

    h = h_sc[...]
    acc_sc[...] += _mm(_silu(_mm(h, wg_ref[...])) * _mm(h, wu_ref[...]), wd_ref[...])

    @pl.when(j == pl.num_programs(2) - 1)
    def _():
        g2 = mod_ref[...][:, 5:6, :]
        o_ref[...] = x1_sc[...].reshape(bb, tl, d) + g2 * acc_sc[...].reshape(bb, tl, d)


def _mix_ffn(x, oa, og, mod, g, wo_a, wo_g, wg, wu, wd):
    b, l, d = x.shape
    hw = oa.shape[-1]
    f = wg.shape[1]
    bb, tl = _token_tiles(b, l, 512 if l >= 512 else 256)
    tm = bb * tl
    fc = f // 2 if wg.dtype == BF16 else 256
    tok = lambda i, t, j: (i, t, 0)
    const2 = lambda i, t, j: (0, 0)
    return pl.pallas_call(
        _mix_ffn_kernel,
        grid=(b // bb, l // tl, f // fc),
        in_specs=[
            pl.BlockSpec((bb, tl, d), tok),
            pl.BlockSpec((bb, tl, hw), tok),
            pl.BlockSpec((bb, tl, hw), tok),
            pl.BlockSpec((bb, 6, d), lambda i, t, j: (i, 0, 0)),
            pl.BlockSpec((1, d), const2),
            pl.BlockSpec(wo_a.shape, const2),
            pl.BlockSpec(wo_g.shape, const2),
            pl.BlockSpec((d, fc), lambda i, t, j: (0, j)),
            pl.BlockSpec((d, fc), lambda i, t, j: (0, j)),
            pl.BlockSpec((fc, d), lambda i, t, j: (j, 0)),
        ],
        out_specs=pl.BlockSpec((bb, tl, d), tok),
        out_shape=jax.ShapeDtypeStruct((b, l, d), F32),
        scratch_shapes=[pltpu.VMEM((tm, d), F32), pltpu.VMEM((tm, d), wg.dtype), pltpu.VMEM((tm, d), F32)],
        compiler_params=_params("arbitrary", "arbitrary", "arbitrary"),
        name="outproj_ffn",
    )(x, oa, og, mod, g, wo_a, wo_g, wg, wu, wd)


def _cm_in_kernel(x_ref, mod_ref, g_ref, w_ref, lng_ref, lnb_ref, u_ref, v_ref):
    bb, tl, d = x_ref.shape
    tm = bb * tl
    cw = u_ref.shape[-1]
    m = mod_ref[...]
    h = _rms(x_ref[...], g_ref[...]) * (1.0 + m[:, 1:2, :]) + m[:, 0:1, :]
    hb = h.reshape(tm, d).astype(w_ref.dtype)
    u_ref[...] = _gelu_tanh(_mm(hb, w_ref[:, 0:cw])).reshape(bb, tl, cw)
    zv = _gelu_tanh(_mm(hb, w_ref[:, cw:2 * cw]))
    mu = jnp.mean(zv, axis=-1, keepdims=True)
    zc = zv - mu
    var = jnp.mean(zc * zc, axis=-1, keepdims=True)
    v_ref[...] = (zc * lax.rsqrt(var + EPS) * lng_ref[...] + lnb_ref[...]).reshape(bb, tl, cw)


def _cm_in(x, mod, g, w, ln_g, ln_b):
    b, l, d = x.shape
    cw = w.shape[1] // 2
    bb, tl = _token_tiles(b, l, 512 if l >= 512 else 256)
    tok = lambda i, t: (i, t, 0)
    const2 = lambda i, t: (0, 0)
    out_spec = pl.BlockSpec((bb, tl, cw), tok)
    out_sds = jax.ShapeDtypeStruct((b, l, cw), F32)
    return pl.pallas_call(
        _cm_in_kernel,
        grid=(b // bb, l // tl),
        in_specs=[
            pl.BlockSpec((bb, tl, d), tok),
            pl.BlockSpec((bb, 6, d), lambda i, t: (i, 0, 0)),
            pl.BlockSpec((1, d), const2),
            pl.BlockSpec(w.shape, const2),
            pl.BlockSpec((1, cw), const2),
            pl.BlockSpec((1, cw), const2),
        ],
        out_specs=[out_spec, out_spec],
        out_shape=[out_sds, out_sds],
        compiler_params=_params("arbitrary", "arbitrary"),
        name="cm_in_gelu_ln",
    )(x, mod, g, w, ln_g, ln_b)


def _cm_out_kernel(x_ref, u_ref, v_ref, mod_ref, g_ref, ws_ref, bs_ref, wo_ref, wr_ref,
                   x3_ref, h_ref, gates_ref, rank_ref, cnt_ref, us_sc):
    bb, tl, d = x_ref.shape
    tm = bb * tl
    cw = u_ref.shape[-1]
    ngroups, cm, _ = ws_ref.shape
    gd = cw // ngroups
    u = u_ref[...].reshape(tm, cw)
    v = v_ref[...].reshape(tm, cw)
    bs = bs_ref[...]
    mm_dtype = wo_ref.dtype
    r = lax.broadcasted_iota(jnp.int32, (cm, cm), 0)
    c = lax.broadcasted_iota(jnp.int32, (cm, cm), 1)
    for g in range(ngroups):
        wsg = jnp.where(r >= c, ws_ref[g], 0.0).astype(mm_dtype)
        cols = slice(g * gd, (g + 1) * gd)
        for sb in range(tm // cm):
            rows = slice(sb * cm, (sb + 1) * cm)
            s = _mm(wsg, v[rows, cols].astype(mm_dtype)) + bs[rows, g:g + 1]
            us_sc[rows, cols] = (u[rows, cols] * s).astype(mm_dtype)
    y = _mm(us_sc[...], wo_ref[...])
    m = mod_ref[...]
    x3 = x_ref[...] + m[:, 2:3, :] * y.reshape(bb, tl, d)
    x3_ref[...] = x3
    h = (_rms(x3, g_ref[...]) * (1.0 + m[:, 4:5, :]) + m[:, 3:4, :]).reshape(tm, d)
    h_ref[...] = h.astype(BF16)
    wr = wr_ref[...]
    if mm_dtype == BF16:
        h_hi = h.astype(BF16)
        h_lo = (h - h_hi.astype(F32)).astype(BF16)
        wr_hi = wr.astype(BF16)
        wr_lo = (wr - wr_hi.astype(F32)).astype(BF16)
        both = jnp.dot(jnp.concatenate([h_hi, h_lo], axis=0), wr_hi, preferred_element_type=F32)
        logits = both[0:tm] + both[tm:2 * tm] + jnp.dot(h_hi, wr_lo, preferred_element_type=F32)
    else:
        logits = jnp.dot(h, wr, precision=HIGHEST, preferred_element_type=F32)
    ne = logits.shape[1]
    col = lax.broadcasted_iota(jnp.int32, (tm, ne), 1)
    m1 = jnp.max(logits, axis=1, keepdims=True)
    i1 = jnp.min(jnp.where(logits == m1, col, ne), axis=1, keepdims=True)
    rest = jnp.where(col == i1, -jnp.inf, logits)
    m2 = jnp.max(rest, axis=1, keepdims=True)
    i2 = jnp.min(jnp.where(rest == m2, col, ne), axis=1, keepdims=True)
    e2 = jnp.exp(m2 - m1)
    gates = jnp.where(col == i1, 1.0 / (1.0 + e2), 0.0) + jnp.where(col == i2, e2 / (1.0 + e2), 0.0)
    gates_ref[...] = gates
    routed = gates > 0.0
    tr = lax.broadcasted_iota(jnp.int32, (tm, tm), 0)
    tc = lax.broadcasted_iota(jnp.int32, (tm, tm), 1)
    upto = jnp.dot(jnp.where(tr >= tc, 1.0, 0.0).astype(BF16), jnp.where(routed, 1.0, 0.0).astype(BF16),
                   preferred_element_type=F32)
    rank_ref[...] = jnp.where(routed, upto - 1.0, -1.0)
    cnt_ref[0] = upto[tm - 1:tm, :]


def _cm_out(x, u, v, mod, g, ws_mix, bs_rows, wo, wr, cm):
    b, l, d = x.shape
    cw = u.shape[-1]
    ne = wr.shape[1]
    bb, tl = _token_tiles(b, l, 256)
    tm = bb * tl
    tok = lambda i, t: (i, t, 0)
    flat = lambda i, t: (i * (l // tl) + t, 0)
    const2 = lambda i, t: (0, 0)
    return pl.pallas_call(
        _cm_out_kernel,
        grid=(b // bb, l // tl),
        in_specs=[
            pl.BlockSpec((bb, tl, d), tok),
            pl.BlockSpec((bb, tl, cw), tok),
            pl.BlockSpec((bb, tl, cw), tok),
            pl.BlockSpec((bb, 6, d), lambda i, t: (i, 0, 0)),
            pl.BlockSpec((1, d), const2),
            pl.BlockSpec(ws_mix.shape, lambda i, t: (0, 0, 0)),
            pl.BlockSpec((tm, bs_rows.shape[1]), const2),
            pl.BlockSpec(wo.shape, const2),
            pl.BlockSpec(wr.shape, const2),
        ],
        out_specs=[pl.BlockSpec((bb, tl, d), tok), pl.BlockSpec((tm, d), flat), pl.BlockSpec((tm, ne), flat),
                   pl.BlockSpec((tm, ne), flat),
                   pl.BlockSpec((1, 1, ne), lambda i, t: (i * (l // tl) + t, 0, 0))],
        out_shape=[jax.ShapeDtypeStruct((b, l, d), F32), jax.ShapeDtypeStruct((b * l, d), BF16),
                   jax.ShapeDtypeStruct((b * l, ne), F32), jax.ShapeDtypeStruct((b * l, ne), F32),
                   jax.ShapeDtypeStruct((b * l // tm, 1, ne), F32)],
        scratch_shapes=[pltpu.VMEM((tm, cw), wo.dtype)],
        compiler_params=_params("arbitrary", "arbitrary"),
        name="cm_out_router",
    )(x, u, v, mod, g, ws_mix, bs_rows, wo, wr)


MOE_SUB = 256
MOE_ALIGN = 16
MOE_WIN_SMALL = 128
MOE_WIN_FULL = MOE_SUB + MOE_ALIGN
MOE_ROW_CHUNK = 512
MOE_BLOCK_SUBS = 8
MOE_HIDDEN_CHUNK = 512


def _moe_routing(rank, cnt, nb, nsub):
    ne = rank.shape[1]
    rank = rank.astype(jnp.int32).reshape(nb, nsub, MOE_SUB, ne)
    cnt = cnt.astype(jnp.int32).reshape(nb, nsub, ne)
    off = jnp.cumsum(cnt, axis=1) - cnt
    total = jnp.sum(cnt, axis=1)
    start = (off // MOE_ALIGN) * MOE_ALIGN
    shift = off - start
    win_row = jnp.where(rank >= 0, rank + shift[:, :, None, :], -1)
    win_row = win_row.transpose(0, 3, 1, 2).reshape(nb, ne, nsub * MOE_SUB)
    used = jnp.where(cnt > 0, shift + cnt, 0)
    scalars = jnp.concatenate([start.transpose(0, 2, 1).reshape(-1), used.transpose(0, 2, 1).reshape(-1),
                               total.reshape(-1)]).astype(jnp.int32)
    return win_row, scalars


def _moe_kernel(sc_ref, h_ref, gates_ref, row_ref, wg_ref, wu_ref, wd_ref, o_ref, x_sc, y_sc):
    b = pl.program_id(0)
    e = pl.program_id(1)
    j = pl.program_id(2)
    ne = pl.num_programs(1)
    tb = h_ref.shape[0]
    nsub = tb // MOE_SUB
    cap = x_sc.shape[0]
    n_be = pl.num_programs(0) * ne
    seg = (b * ne + e) * nsub
    total = sc_ref[2 * n_be * nsub + b * ne + e]

    def windows(fn):
        for s in range(nsub):
            start = pl.multiple_of(sc_ref[seg + s], MOE_ALIGN)
            used = sc_ref[n_be * nsub + seg + s]
            tok = slice(s * MOE_SUB, (s + 1) * MOE_SUB)
            win_row = row_ref[0, pl.ds(e, 1), tok]

            def run(rows, start=start, tok=tok, win_row=win_row):
                ridx = lax.broadcasted_iota(jnp.int32, (rows, MOE_SUB), 0)
                fn(pl.ds(start, rows), tok, jnp.where(ridx == win_row, 1.0, 0.0).astype(BF16))

            pl.when((used > 0) & (used <= MOE_WIN_SMALL))(functools.partial(run, MOE_WIN_SMALL))
            pl.when(used > MOE_WIN_SMALL)(functools.partial(run, MOE_WIN_FULL))

    @pl.when((e == 0) & (j == 0))
    def _():
        o_ref[...] = jnp.zeros_like(o_ref)

    @pl.when(j == 0)
    def _():
        def clear(c, carry):
            rows = pl.ds(pl.multiple_of(c * MOE_ROW_CHUNK, MOE_ROW_CHUNK), MOE_ROW_CHUNK)
            x_sc[rows, :] = jnp.zeros((MOE_ROW_CHUNK, x_sc.shape[1]), BF16)
            y_sc[rows, :] = jnp.zeros((MOE_ROW_CHUNK, y_sc.shape[1]), F32)
            return carry

        lax.fori_loop(0, jnp.minimum((total + MOE_WIN_FULL + MOE_ROW_CHUNK - 1) // MOE_ROW_CHUNK,
                                     cap // MOE_ROW_CHUNK), clear, 0)

        def dispatch(rows, tok, onehot):
            picked = jnp.dot(onehot, h_ref[tok, :], preferred_element_type=F32).astype(BF16)
            x_sc[rows, :] = x_sc[rows, :] + picked

        windows(dispatch)

    def ffn(r0, nrows):
        rows = pl.ds(pl.multiple_of(r0, MOE_WIN_SMALL), nrows)
        x = x_sc[rows, :]
        gate = jnp.dot(x, wg_ref[0].astype(BF16), preferred_element_type=F32)
        up = jnp.dot(x, wu_ref[0].astype(BF16), preferred_element_type=F32)
        act = (_silu(gate) * up).astype(BF16)
        y_sc[rows, :] += jnp.dot(act, wd_ref[0].astype(BF16), preferred_element_type=F32)

    nfull = total // MOE_ROW_CHUNK
    rem = total - nfull * MOE_ROW_CHUNK
    tail = nfull * MOE_ROW_CHUNK

    def full_chunk(c, carry):
        ffn(c * MOE_ROW_CHUNK, MOE_ROW_CHUNK)
        return carry

    lax.fori_loop(0, nfull, full_chunk, 0)
    take_256 = (rem > 128) & (rem <= 384)
    pl.when(rem > 384)(lambda: ffn(tail, 512))
    pl.when(take_256)(lambda: ffn(tail, 256))
    pl.when((rem > 0) & (rem <= 128))(lambda: ffn(tail, 128))
    pl.when((rem > 256) & (rem <= 384))(lambda: ffn(tail + 256, 128))

    @pl.when(j == pl.num_programs(2) - 1)
    def _():
        gates = gates_ref[...]
        col = lax.broadcasted_iota(jnp.int32, gates.shape, 1)
        ge = jnp.sum(jnp.where(col == e, gates, 0.0), axis=1, keepdims=True)

        def combine(rows, tok, onehot):
            back = lax.dot_general(onehot, y_sc[rows, :].astype(BF16), TN_DIMS, preferred_element_type=F32)
            o_ref[tok, :] = (o_ref[tok, :].astype(F32) + ge[tok] * back).astype(o_ref.dtype)

        windows(combine)


def _moe(h, gates, rank, cnt, wg, wu, wd, tb):
    t, d = h.shape
    ne, _, f = wg.shape
    fc = MOE_HIDDEN_CHUNK
    assert t % tb == 0 and tb % MOE_SUB == 0 and f % fc == 0
    nb, nsub = t // tb, tb // MOE_SUB
    win_row, scalars = _moe_routing(rank, cnt, nb, nsub)
    cap = -(-(tb + MOE_WIN_FULL) // MOE_ROW_CHUNK) * MOE_ROW_CHUNK
    grid_spec = pltpu.PrefetchScalarGridSpec(
        num_scalar_prefetch=1,
        grid=(nb, ne, f // fc),
        in_specs=[
            pl.BlockSpec((tb, d), lambda i, e, j, sc: (i, 0)),
            pl.BlockSpec((tb, ne), lambda i, e, j, sc: (i, 0)),
            pl.BlockSpec((1, ne, tb), lambda i, e, j, sc: (i, 0, 0)),
            pl.BlockSpec((1, d, fc), lambda i, e, j, sc: (e, 0, j)),
            pl.BlockSpec((1, d, fc), lambda i, e, j, sc: (e, 0, j)),
            pl.BlockSpec((1, fc, d), lambda i, e, j, sc: (e, j, 0)),
        ],
        out_specs=pl.BlockSpec((tb, d), lambda i, e, j, sc: (i, 0)),
        scratch_shapes=[pltpu.VMEM((cap, d), BF16), pltpu.VMEM((cap, d), F32)],
    )
    return pl.pallas_call(
        _moe_kernel,
        grid_spec=grid_spec,
        out_shape=jax.ShapeDtypeStruct((t, d), BF16),
        compiler_params=_params("arbitrary", "arbitrary", "arbitrary"),
        name="moe_experts",
    )(scalars, h, gates, win_row, wg, wu, wd)


def _final_kernel(x_ref, moe_ref, mod_ref, g_ref, o_ref):
    bb, tl, d = x_ref.shape
    g2 = mod_ref[...][:, 5:6, :]
    x4 = x_ref[...] + g2 * moe_ref[...].reshape(bb, tl, d)
    o_ref[...] = _rms(x4, g_ref[...])


def _final(x, moe, row0, mod, g):
    b, l, d = x.shape
    bb, tl = _token_tiles(b, l, 512 if l >= 512 else 256)
    tm = bb * tl
    assert row0 % tm == 0
    tok = lambda i, t: (i, t, 0)
    return pl.pallas_call(
        _final_kernel,
        grid=(b // bb, l // tl),
        in_specs=[
            pl.BlockSpec((bb, tl, d), tok),
            pl.BlockSpec((tm, d), lambda i, t: (row0 // tm + i * (l // tl) + t, 0)),
            pl.BlockSpec((bb, 6, d), lambda i, t: (i, 0, 0)),
            pl.BlockSpec((1, d), lambda i, t: (0, 0)),
        ],
        out_specs=pl.BlockSpec((bb, tl, d), tok),
        out_shape=jax.ShapeDtypeStruct((b, l, d), F32),
        compiler_params=_params("arbitrary", "arbitrary"),
        name="final_norm",
    )(x, moe, mod, g)


def _layer0(x, mod, pos0, past, w):
    b, l, d = x.shape
    hw = HEAD_WIDTH
    outs = _in_proj(x, mod, w["norm_mix_g0"], w["w_in_main"], w["w_in_lr"], w["wa2_hi"], w["wa2_lo"], w["ba2"], pos0)
    qa, ka, va, qg, kg, vg, gs, la = outs
    if past is None:
        oa = _moba_prompt(qa, ka, va)
        s0_t = jnp.zeros((b, HEADS_PER_GROUP, HEAD_DIM, HEAD_DIM), F32)
    else:
        cache_k, cache_v, state, page_table = past
        oa = _moba_sample(qa, ka, va, cache_k, cache_v, page_table)
        s0_t = jnp.swapaxes(state, -1, -2)
    chunk = GLA_CHUNK if l % GLA_CHUNK == 0 else l
    mm_dtype = BF16 if chunk % 16 == 0 else F32
    og, s_t = _gla(qg, kg, vg, la, gs, w["gla_norm_pair"], s0_t, chunk, mm_dtype)
    x2 = _mix_ffn(x, oa, og, mod, w["norm_ffn_g0"], w["wo_a"], w["wo_g"], w["ffn_wg"], w["ffn_wu"], w["ffn_wd"])
    k_rows = ka.reshape(b, l, HEADS_PER_GROUP, HEAD_DIM)
    v_rows = va.reshape(b, l, HEADS_PER_GROUP, HEAD_DIM)
    return x2, k_rows, v_rows, jnp.swapaxes(s_t, -1, -2)


def _layer1_front(x, mod, w):
    b, l, d = x.shape
    u, v = _cm_in(x, mod, w["norm_mix_g1"], w["cm_w_in"], w["cm_ln_g"], w["cm_ln_b"])
    bb, tl = _token_tiles(b, l, 256)
    tm = bb * tl
    if l >= CM_CHUNK:
        cm = CM_CHUNK
        ws_mix = w["cm_w_s"]
        bs_rows = jnp.tile(w["cm_b_s"].T, (tm // CM_CHUNK, 1))
    else:
        cm = tm
        seq_id = jnp.arange(tm, dtype=jnp.int32) // l
        same_seq = (seq_id[:, None] == seq_id[None, :]).astype(F32)
        ws_mix = jnp.tile(w["cm_w_s"][:, :l, :l], (1, tm // l, tm // l)) * same_seq[None]
        bs_rows = jnp.tile(w["cm_b_s"][:, :l].T, (tm // l, 1))
    assert tm == MOE_SUB
    x3, h, gates, rank, cnt = _cm_out(x, u, v, mod, w["norm_ffn_g1"], ws_mix, bs_rows, w["cm_w_out"],
                                      w["moe_w_router"], cm)
    start = ((l - 1) // CM_CHUNK) * CM_CHUNK
    return x3, (h, gates, rank, cnt), v[:, start:]


def kernel(x_prompt, x_sample, cache_k, cache_v, state_gla, page_table, c_prompt, c_sample, ada_w, ada_b, norm_mix_g, norm_ffn_g, ab_w_in, gla_w_a2, gla_b_a2, gla_norm_g, ab_w_out, ffn_w_gate, ffn_w_up, ffn_w_down, cm_w_in, cm_ln_g, cm_ln_b, cm_w_s, cm_b_s, cm_w_out, moe_w_router, moe_w_gate, moe_w_up, moe_w_down, final_norm_g):
    bp, lp, d = x_prompt.shape
    bs, ls, _ = x_sample.shape
    hw = HEAD_WIDTH
    n_main = 7 * hw
    past_len = page_table.shape[1] * cache_k.shape[2]

    wa2 = jnp.pad(gla_w_a2[0], ((0, LANES - GLA_LOWRANK), (0, 0)))
    wa2_hi = wa2.astype(BF16)
    shared = {
        "norm_mix_g0": norm_mix_g[0][None], "norm_mix_g1": norm_mix_g[1][None],
        "norm_ffn_g0": norm_ffn_g[0][None], "norm_ffn_g1": norm_ffn_g[1][None],
        "wa2_hi": wa2_hi, "wa2_lo": (wa2 - wa2_hi.astype(F32)).astype(BF16),
        "ba2": gla_b_a2[0][None],
        "gla_norm_pair": jnp.tile(gla_norm_g[0], LANES // HEAD_DIM)[None],
        "cm_ln_g": cm_ln_g[0][None], "cm_ln_b": cm_ln_b[0][None],
        "cm_w_s": cm_w_s[0], "cm_b_s": cm_b_s[0],
        "moe_w_router": moe_w_router[0],
    }
    mats = {
        "w_in_main": ab_w_in[0][:, :n_main],
        "w_in_lr": jnp.pad(ab_w_in[0][:, n_main:], ((0, 0), (0, LANES - GLA_LOWRANK))),
        "wo_a": ab_w_out[0][:hw], "wo_g": ab_w_out[0][hw:],
        "ffn_wg": ffn_w_gate[0], "ffn_wu": ffn_w_up[0], "ffn_wd": ffn_w_down[0],
        "cm_w_in": cm_w_in[0], "cm_w_out": cm_w_out[0],
    }
    w = dict(shared, **{k: v.astype(BF16) for k, v in mats.items()})
    w_sample = dict(shared, **mats)

    rows = bp + bs
    rows_pad = -(-rows // 16) * 16
    c_all = jnp.pad(jnp.concatenate([c_prompt, c_sample], axis=0), ((0, rows_pad - rows), (0, 0)))
    mod_all = _modulation(c_all, ada_w, ada_b)
    mod_p = [mod_all[l, :bp].reshape(bp, 6, d) for l in range(2)]
    mod_s = [mod_all[l, bp:rows].reshape(bs, 6, d) for l in range(2)]

    x2_p, k_p, v_p, gla_p = _layer0(x_prompt, mod_p[0], 0, None, w)
    x2_s, k_s, v_s, gla_s = _layer0(x_sample, mod_s[0], past_len,
                                    (cache_k[0], cache_v[0], state_gla[0], page_table), w_sample)

    x3_p, route_p, cmv_p = _layer1_front(x2_p, mod_p[1], w)
    x3_s, route_s, cmv_s = _layer1_front(x2_s, mod_s[1], w_sample)

    t_all = bp * lp + bs * ls
    subs = -(-t_all // MOE_SUB)
    n_blocks = -(-subs // MOE_BLOCK_SUBS)
    tb = -(-subs // n_blocks) * MOE_SUB
    pad = n_blocks * tb - t_all
    ne = moe_w_gate.shape[1]
    pads = (jnp.zeros((pad, d), BF16), jnp.zeros((pad, ne), F32), jnp.full((pad, ne), -1.0, F32),
            jnp.zeros((pad // MOE_SUB, 1, ne), F32))
    h_all, gates_all, rank_all, cnt_all = (jnp.concatenate(parts, axis=0) for parts in zip(route_p, route_s, pads))
    moe = _moe(h_all, gates_all, rank_all, cnt_all, moe_w_gate[0], moe_w_up[0], moe_w_down[0], tb)

    fg = final_norm_g[None]
    y_p = _final(x3_p, moe, 0, mod_p[1], fg)
    y_s = _final(x3_s, moe, bp * lp, mod_s[1], fg)

    return (y_p, y_s, k_p[None], v_p[None], k_s[None], v_s[None], gla_p[None], gla_s[None],
            cmv_p[None], cmv_s[None])
```

```python
import functools

import jax
import jax.numpy as jnp
from jax import lax
from jax.experimental import pallas as pl
from jax.experimental.pallas import tpu as pltpu

F32 = jnp.float32
BF16 = jnp.bfloat16
HIGHEST = lax.Precision.HIGHEST

HEAD_DIM = 64
HEADS_PER_GROUP = 8
HEAD_WIDTH = HEADS_PER_GROUP * HEAD_DIM
MOBA_BLOCK = 256
MOBA_TOPK = 3
ROPE_THETA = 10000.0
GLA_LOWRANK = 16
GLA_GATE_NORM = 16.0
GLA_CHUNK = 64
CM_CHUNK = 128
CM_GROUPS = 8
MOE_TOP_K = 2
EPS = 1e-6
NEG = -1e30

LANES = 128
VMEM_LIMIT_BYTES = 56 * 1024 * 1024

NT_DIMS = (((1,), (1,)), ((), ()))
TN_DIMS = (((0,), (0,)), ((), ()))


def _params(*sem):
    return pltpu.CompilerParams(dimension_semantics=sem, vmem_limit_bytes=VMEM_LIMIT_BYTES)


def _rms(x, g):
    return x * lax.rsqrt(jnp.mean(x * x, axis=-1, keepdims=True) + EPS) * g


def _silu(x):
    return x * jax.nn.sigmoid(x)


def _mm(a, w):
    if w.dtype == F32:
        return jnp.dot(a.astype(F32), w, precision=HIGHEST, preferred_element_type=F32)
    return jnp.dot(a.astype(BF16), w, preferred_element_type=F32)


def _gelu_tanh(x):
    c = 0.7978845608028654
    return x * (0.5 + 0.5 * jnp.tanh(x * (c + (c * 0.044715) * (x * x))))


def _token_tiles(b, l, target):
    if l >= target:
        assert l % target == 0
        return 1, target
    bb = min(b, target // l)
    assert target % l == 0 and b % bb == 0
    return bb, l


def _mod_kernel(c_ref, w_ref, b_ref, o_ref):
    s = _silu(c_ref[...])
    r = s.shape[0]
    s_hi = s.astype(BF16)
    s_lo = (s - s_hi.astype(F32)).astype(BF16)
    w = w_ref[0]
    w_hi = w.astype(BF16)
    w_lo = (w - w_hi.astype(F32)).astype(BF16)
    both = jnp.dot(jnp.concatenate([s_hi, s_lo], axis=0), w_hi, preferred_element_type=F32)
    o_ref[0] = both[0:r] + both[r:2 * r] + jnp.dot(s_hi, w_lo, preferred_element_type=F32) + b_ref[0]


def _modulation(c_all, ada_w, ada_b):
    r, d = c_all.shape
    depth, _, n = ada_w.shape
    tn = 1536
    return pl.pallas_call(
        _mod_kernel,
        grid=(depth, n // tn),
        in_specs=[
            pl.BlockSpec((r, d), lambda l, j: (0, 0)),
            pl.BlockSpec((1, d, tn), lambda l, j: (l, 0, j)),
            pl.BlockSpec((1, 1, tn), lambda l, j: (l, 0, j)),
        ],
        out_specs=pl.BlockSpec((1, r, tn), lambda l, j: (l, 0, j)),
        out_shape=jax.ShapeDtypeStruct((depth, r, n), F32),
        compiler_params=_params("arbitrary", "arbitrary"),
        name="adaln_modulation",
    )(c_all, ada_w, ada_b.reshape(depth, 1, n))


def _in_proj_kernel(x_ref, mod_ref, g_ref, w_ref, wlr_ref, wa2h_ref, wa2l_ref, ba2_ref, cos_ref, sin_ref,
                    qa_ref, ka_ref, va_ref, qg_ref, kg_ref, vg_ref, gs_ref, la_ref):
    bb, tl, d = x_ref.shape
    tm = bb * tl
    hw = HEAD_WIDTH
    m = mod_ref[...]
    h = _rms(x_ref[...], g_ref[...]) * (1.0 + m[:, 1:2, :]) + m[:, 0:1, :]
    hb = h.reshape(tm, d).astype(w_ref.dtype)

    def proj(c):
        return _mm(hb, w_ref[:, c * hw:(c + 1) * hw])

    cos = jnp.concatenate([cos_ref[...]] * (hw // LANES), axis=-1)[None]
    sin = jnp.concatenate([sin_ref[...]] * (hw // LANES), axis=-1)[None]
    lane = lax.broadcasted_iota(jnp.int32, (tm, hw), 1)
    first_half = (lane % HEAD_DIM) < (HEAD_DIM // 2)

    def rope(z):
        partner = jnp.where(first_half, pltpu.roll(z, hw - HEAD_DIM // 2, 1), pltpu.roll(z, HEAD_DIM // 2, 1))
        return z.reshape(bb, tl, hw) * cos + partner.reshape(bb, tl, hw) * sin

    scale = HEAD_DIM ** -0.5
    qa_ref[...] = rope(proj(0)) * scale
    ka_ref[...] = rope(proj(1))
    va_ref[...] = proj(2).reshape(bb, tl, hw)
    qg_ref[...] = (proj(3) * scale).reshape(bb, tl, hw)
    kg_ref[...] = proj(4).reshape(bb, tl, hw)
    vg_ref[...] = proj(5).reshape(bb, tl, hw)
    gs_ref[...] = _silu(proj(6)).reshape(bb, tl, hw)
    zlr = _mm(hb, wlr_ref[...])
    hi = zlr.astype(BF16)
    lo = (zlr - hi.astype(F32)).astype(BF16)
    pre = (jnp.dot(hi, wa2h_ref[...], preferred_element_type=F32)
           + jnp.dot(lo, wa2h_ref[...], preferred_element_type=F32)
           + jnp.dot(hi, wa2l_ref[...], preferred_element_type=F32)) + ba2_ref[...]
    log_sig = jnp.minimum(pre, 0.0) - jnp.log(1.0 + jnp.exp(-jnp.abs(pre)))
    la_ref[...] = (log_sig * (1.0 / GLA_GATE_NORM)).reshape(bb, tl, hw)


def _rope_tables(pos0, l):
    half = HEAD_DIM // 2
    inv = 1.0 / (ROPE_THETA ** (jnp.arange(half, dtype=F32) / half))
    ang = (pos0 + jnp.arange(l, dtype=jnp.int32)).astype(F32)[:, None] * inv[None, :]
    cos, sin = jnp.cos(ang), jnp.sin(ang)
    reps = LANES // HEAD_DIM
    cos_t = jnp.tile(jnp.concatenate([cos, cos], axis=-1), (1, reps))
    sin_t = jnp.tile(jnp.concatenate([-sin, sin], axis=-1), (1, reps))
    return cos_t, sin_t


def _in_proj(x, mod, g, w_main, w_lr, wa2_hi, wa2_lo, ba2, pos0):
    b, l, d = x.shape
    bb, tl = _token_tiles(b, l, 512 if l >= 512 else 256)
    hw = HEAD_WIDTH
    cos_t, sin_t = _rope_tables(pos0, l)
    tok = lambda i, j: (i, j, 0)
    const2 = lambda i, j: (0, 0)
    out_spec = pl.BlockSpec((bb, tl, hw), tok)
    out_sds = jax.ShapeDtypeStruct((b, l, hw), F32)
    return pl.pallas_call(
        _in_proj_kernel,
        grid=(b // bb, l // tl),
        in_specs=[
            pl.BlockSpec((bb, tl, d), tok),
            pl.BlockSpec((bb, 6, d), lambda i, j: (i, 0, 0)),
            pl.BlockSpec((1, d), const2),
            pl.BlockSpec(w_main.shape, const2),
            pl.BlockSpec(w_lr.shape, const2),
            pl.BlockSpec(wa2_hi.shape, const2),
            pl.BlockSpec(wa2_lo.shape, const2),
            pl.BlockSpec((1, hw), const2),
            pl.BlockSpec((tl, LANES), lambda i, j: (j, 0)),
            pl.BlockSpec((tl, LANES), lambda i, j: (j, 0)),
        ],
        out_specs=[out_spec] * 8,
        out_shape=[out_sds] * 8,
        compiler_params=_params("arbitrary", "arbitrary"),
        name="in_proj_rope",
    )(x, mod, g, w_main, w_lr, wa2_hi, wa2_lo, ba2, cos_t, sin_t)


def _moba_prompt_kernel(q_ref, k_ref, v_ref, o_ref, kb_sc, vb_sc, km_sc, qs_sc, s_sc, p_sc, m_sc, acc_sc):
    ti = pl.program_id(2)
    lk = k_ref.shape[1]
    nblk = lk // MOBA_BLOCK
    nblk8 = -(-nblk // 8) * 8
    tq = q_ref.shape[1]
    nsub = tq // MOBA_BLOCK

    @pl.when(ti == 0)
    def _():
        k = k_ref[0]
        v = v_ref[0]
        row = lax.broadcasted_iota(jnp.int32, (lk, LANES), 0)
        col = lax.broadcasted_iota(jnp.int32, (lk, LANES), 1)
        kb_sc[:, 0:LANES] = k.astype(BF16)
        kb_sc[:, LANES:2 * LANES] = jnp.where(row // MOBA_BLOCK == col, 1.0, 0.0).astype(BF16)
        vb_sc[0] = jnp.where(col < HEAD_DIM, v, 1.0).astype(BF16)
        vb_sc[1] = jnp.where(col >= HEAD_DIM, v, 1.0).astype(BF16)
        km_sc[...] = jnp.zeros_like(km_sc)
        km_sc[0:nblk, :] = jnp.mean(k.reshape(nblk, MOBA_BLOCK, LANES), axis=1)

    q = q_ref[0]
    lane = lax.broadcasted_iota(jnp.int32, (tq, LANES), 1)
    blk = lax.broadcasted_iota(jnp.int32, (nblk8, tq), 0)
    own = ti * nsub + lax.broadcasted_iota(jnp.int32, (nblk8, tq), 1) // MOBA_BLOCK
    past_blk = blk < own
    km = km_sc[...]
    for h in range(2):
        head_lanes = (lane < HEAD_DIM) if h == 0 else (lane >= HEAD_DIM)
        qh = jnp.where(head_lanes, q, 0.0)
        gate = lax.dot_general(km, qh, NT_DIMS, precision=HIGHEST, preferred_element_type=F32)[0:nblk8]
        gate = jnp.where(past_blk, gate, -jnp.inf)
        cnt = jnp.zeros((nblk8, tq), F32)
        for mrow in range(nblk):
            gm = gate[mrow:mrow + 1, :]
            beats = (gm > gate) | ((gm == gate) & (mrow < blk))
            cnt = cnt + jnp.where(beats, 1.0, 0.0)
        sel = (past_blk & (cnt < float(MOBA_TOPK))) | (blk == own)
        bias_t = jnp.concatenate(
            [jnp.where(sel, 0.0, NEG), jnp.full((LANES - nblk8, tq), NEG, F32)], axis=0)
        qs_sc[h, :, 0:LANES] = qh.astype(BF16)
        qs_sc[h, :, LANES:2 * LANES] = bias_t.T.astype(BF16)

    m_sc[...] = jnp.full_like(m_sc, NEG)
    acc_sc[...] = jnp.zeros_like(acc_sc)
    rc = 128
    rowi = lax.broadcasted_iota(jnp.int32, (rc, MOBA_BLOCK), 0)
    coli = lax.broadcasted_iota(jnp.int32, (rc, MOBA_BLOCK), 1)

    def scores(n, slot):
        start = pl.multiple_of(n * MOBA_BLOCK, MOBA_BLOCK)
        kblk = kb_sc[pl.ds(start, MOBA_BLOCK), :]
        for h in range(2):
            s_sc[slot, h] = lax.dot_general(qs_sc[h], kblk, NT_DIMS, preferred_element_type=F32)

    def consume(n, slot, own_sub):
        start = pl.multiple_of(n * MOBA_BLOCK, MOBA_BLOCK)
        for h in range(2):
            for r0 in range(0, tq, rc):
                rows = slice(r0, r0 + rc)
                s = s_sc[slot, h, rows, :]
                if own_sub is not None and r0 // MOBA_BLOCK == own_sub:
                    s = jnp.where(rowi + (r0 - own_sub * MOBA_BLOCK) < coli, NEG, s)
                m_old = m_sc[h, rows, :]
                row_max = jnp.max(jnp.maximum(s[:, 0:LANES], s[:, LANES:2 * LANES]), axis=1, keepdims=True)
                m_new = jnp.maximum(m_old, row_max)
                p_sc[h, rows, :] = jnp.exp(s - jnp.concatenate([m_new, m_new], axis=1)).astype(BF16)
                acc_sc[h, rows, :] = jnp.exp(m_old - m_new) * acc_sc[h, rows, :]
                m_sc[h, rows, :] = m_new
            acc_sc[h] += jnp.dot(p_sc[h], vb_sc[h, pl.ds(start, MOBA_BLOCK), :], preferred_element_type=F32)

    def past_pair(i, carry):
        scores(2 * i + 1, 1)
        consume(2 * i, 0, None)
        scores(2 * i + 2, 0)
        consume(2 * i + 1, 1, None)
        return carry

    scores(0, 0)
    lax.fori_loop(0, ti, past_pair, 0)
    scores(2 * ti + 1, 1)
    consume(2 * ti, 0, 0)
    consume(2 * ti + 1, 1, 1)
    a0 = acc_sc[0]
    a1 = acc_sc[1]
    o_ref[0] = jnp.where(lane < HEAD_DIM, a0 / pltpu.roll(a0, HEAD_DIM, 1), a1 / pltpu.roll(a1, HEAD_DIM, 1))


def _moba_prompt(qa, ka, va):
    b, l, hw = qa.shape
    tq = 2 * MOBA_BLOCK
    assert l % tq == 0
    pairs = hw // LANES
    return pl.pallas_call(
        _moba_prompt_kernel,
        grid=(b, pairs, l // tq),
        in_specs=[
            pl.BlockSpec((1, tq, LANES), lambda i, p, t: (i, t, p)),
            pl.BlockSpec((1, l, LANES), lambda i, p, t: (i, 0, p)),
            pl.BlockSpec((1, l, LANES), lambda i, p, t: (i, 0, p)),
        ],
        out_specs=pl.BlockSpec((1, tq, LANES), lambda i, p, t: (i, t, p)),
        out_shape=jax.ShapeDtypeStruct((b, l, hw), F32),
        scratch_shapes=[
            pltpu.VMEM((l, 2 * LANES), BF16),
            pltpu.VMEM((2, l, LANES), BF16),
            pltpu.VMEM((LANES, LANES), F32),
            pltpu.VMEM((2, tq, 2 * LANES), BF16),
            pltpu.VMEM((2, 2, tq, MOBA_BLOCK), F32),
            pltpu.VMEM((2, tq, MOBA_BLOCK), BF16),
            pltpu.VMEM((2, tq, LANES), F32),
            pltpu.VMEM((2, tq, LANES), F32),
        ],
        compiler_params=_params("arbitrary", "arbitrary", "arbitrary"),
        name="moba_prompt",
    )(qa, ka, va)


def _moba_sample_kernel(pt_ref, q_ref, kn_ref, vn_ref, *refs, blocks_per_step):
    del pt_ref
    npages = 2 * blocks_per_step
    k_refs, v_refs = refs[0:npages], refs[npages:2 * npages]
    o_ref, q_sc, m_sc, l_sc, g_sc, acc_sc = refs[2 * npages:]
    j = pl.program_id(1)
    last = pl.num_programs(1) - 1
    nb = acc_sc.shape[0]
    nq = q_ref.shape[1]
    nh = HEADS_PER_GROUP
    nr = nh * nq
    hw = HEAD_WIDTH
    keys = 2 * k_refs[0].shape[3]
    row_w = lax.broadcasted_iota(jnp.int32, (nr, hw), 0)
    lane_w = lax.broadcasted_iota(jnp.int32, (nr, hw), 1)
    own_head = (row_w // nq) == (lane_w // HEAD_DIM)
    lane = lax.broadcasted_iota(jnp.int32, (nr, LANES), 1)

    def to_queries(t):
        return jnp.sum(jnp.where(own_head, t, 0.0).reshape(nh, nq, hw), axis=0)

    @pl.when(j == 0)
    def _():
        qbd = jnp.where(own_head, jnp.concatenate([q_ref[0]] * nh, axis=0), 0.0)
        hi = qbd.astype(BF16)
        q_sc[0:nr] = hi
        q_sc[nr:2 * nr] = (qbd - hi.astype(F32)).astype(BF16)
        m_sc[...] = jnp.zeros_like(m_sc)
        l_sc[...] = jnp.zeros_like(l_sc)
        g_sc[...] = jnp.zeros_like(g_sc)

    def scores(kt):
        s2 = jnp.dot(q_sc[...], kt, preferred_element_type=F32)
        return s2[0:nr] + s2[nr:2 * nr]

    @pl.when(j < last)
    def _():
        m_all, l_all, g_all = m_sc[...], l_sc[...], g_sc[...]
        for i in range(blocks_per_step):
            n = j * blocks_per_step + i
            kt = jnp.concatenate([k_refs[2 * i][0, 0], k_refs[2 * i + 1][0, 0]], axis=1).astype(BF16)
            vt = jnp.concatenate([v_refs[2 * i][0, 0], v_refs[2 * i + 1][0, 0]], axis=1).astype(BF16)
            s = scores(kt)
            gate = jnp.sum(s, axis=1, keepdims=True) * (1.0 / keys)
            mx = jnp.max(s, axis=1, keepdims=True)
            p = jnp.exp(s - mx)
            den = jnp.sum(p, axis=1, keepdims=True)
            pv = lax.dot_general(p.astype(BF16), vt, NT_DIMS, preferred_element_type=F32)
            acc_sc[n] = to_queries(pv)
            m_all = jnp.where(lane == n, mx, m_all)
            l_all = jnp.where(lane == n, den, l_all)
            g_all = jnp.where(lane == n, gate, g_all)
        m_sc[...], l_sc[...], g_sc[...] = m_all, l_all, g_all

    @pl.when(j == last)
    def _():
        cur = jnp.where(lane < nb, g_sc[...], -jnp.inf)
        sel = jnp.zeros((nr, LANES), jnp.bool_)
        for _ in range(MOBA_TOPK):
            top = jnp.max(cur, axis=1, keepdims=True)
            first = jnp.min(jnp.where(cur == top, lane, LANES), axis=1, keepdims=True)
            pick = lane == first
            sel = sel | pick
            cur = jnp.where(pick, -jnp.inf, cur)
        pad = jnp.zeros((nq, hw), F32)
        kn = jnp.concatenate([kn_ref[0], pad], axis=0)
        vn = jnp.concatenate([vn_ref[0], pad], axis=0)
        s2 = lax.dot_general(q_sc[...], kn.astype(BF16), NT_DIMS, preferred_element_type=F32)
        s_own = s2[0:nr] + s2[nr:2 * nr]
        qpos = lax.broadcasted_iota(jnp.int32, (nr, 2 * nq), 0) % nq
        kpos = lax.broadcasted_iota(jnp.int32, (nr, 2 * nq), 1)
        s_own = jnp.where(kpos <= qpos, s_own, NEG)
        m_own = jnp.max(s_own, axis=1, keepdims=True)
        p_own = jnp.exp(s_own - m_own)
        l_own = jnp.sum(p_own, axis=1, keepdims=True)
        acc_own = to_queries(jnp.dot(p_own, vn, preferred_element_type=F32))
        m_all = m_sc[...]
        m_tot = jnp.maximum(jnp.max(jnp.where(sel, m_all, NEG), axis=1, keepdims=True), m_own)
        w = jnp.where(sel, jnp.exp(m_all - m_tot), 0.0)
        w_own = jnp.exp(m_own - m_tot)
        l_tot = jnp.sum(w * l_sc[...], axis=1, keepdims=True) + w_own * l_own

        def widen(col):
            return to_queries(jnp.broadcast_to(col, (nr, hw)))

        num = widen(w_own) * acc_own
        for n in range(nb):
            num = num + widen(w[:, n:n + 1]) * acc_sc[n]
        o_ref[0] = num / widen(l_tot)


def _moba_sample(qa, ka, va, cache_k, cache_v, page_table):
    b, nq, hw = qa.shape
    nh = HEADS_PER_GROUP
    n_pool, page = cache_k.shape[0], cache_k.shape[1]
    pages_per_blk = MOBA_BLOCK // page
    assert pages_per_blk == 2 and page == LANES
    nb = page_table.shape[1] // pages_per_blk
    assert MOBA_TOPK <= nb <= LANES
    ck = cache_k.transpose(0, 2, 3, 1).reshape(1, n_pool, hw, page)
    cv = cache_v.transpose(0, 2, 3, 1).reshape(1, n_pool, hw, page)
    tok = lambda i, j, pt: (i, 0, 0)
    per_step = next(g for g in (4, 2, 1) if nb % g == 0)
    nsteps = nb // per_step

    def page_spec(p):
        return pl.BlockSpec((1, 1, hw, page),
                            lambda i, j, pt: (0, pt[i, jnp.minimum(j, nsteps - 1) * 2 * per_step + p], 0, 0))

    pages = [page_spec(p) for p in range(2 * per_step)]
    grid_spec = pltpu.PrefetchScalarGridSpec(
        num_scalar_prefetch=1,
        grid=(b, nsteps + 1),
        in_specs=[
            pl.BlockSpec((1, nq, hw), tok),
            pl.BlockSpec((1, nq, hw), tok),
            pl.BlockSpec((1, nq, hw), tok),
        ] + pages + pages,
        out_specs=pl.BlockSpec((1, nq, hw), tok),
        scratch_shapes=[
            pltpu.VMEM((2 * nh * nq, hw), BF16),
            pltpu.VMEM((nh * nq, LANES), F32),
            pltpu.VMEM((nh * nq, LANES), F32),
            pltpu.VMEM((nh * nq, LANES), F32),
            pltpu.VMEM((nb, nq, hw), F32),
        ],
    )
    return pl.pallas_call(
        functools.partial(_moba_sample_kernel, blocks_per_step=per_step),
        grid_spec=grid_spec,
        out_shape=jax.ShapeDtypeStruct((b, nq, hw), F32),
        compiler_params=_params("arbitrary", "arbitrary"),
        name="moba_sample",
    )(page_table, qa, ka, va, *([ck] * (2 * per_step)), *([cv] * (2 * per_step)))


def _gla_kernel(q_ref, k_ref, v_ref, la_ref, gs_ref, gn_ref, s0_ref, o_ref, sout_ref, st_sc, *, chunk, mm_dtype):
    c = chunk
    n = q_ref.shape[1] // c
    npair = q_ref.shape[2] // LANES
    hd = HEAD_DIM
    lane = lax.broadcasted_iota(jnp.int32, (c, LANES), 1)
    head0 = lane < hd
    r = lax.broadcasted_iota(jnp.int32, (c, c), 0)
    cc = lax.broadcasted_iota(jnp.int32, (c, c), 1)
    causal = r >= cc
    tri = jnp.where(causal, 1.0, 0.0)
    br = lax.broadcasted_iota(jnp.int32, (LANES, LANES), 0) // hd
    bc = lax.broadcasted_iota(jnp.int32, (LANES, LANES), 1) // hd
    same_head = br == bc
    head_mean = jnp.where(same_head, 1.0 / hd, 0.0)
    gn = gn_ref[...]

    zero = jnp.zeros((hd, hd), F32)
    for p in range(npair):
        st_sc[p] = jnp.concatenate([jnp.concatenate([s0_ref[0, 2 * p], zero], axis=1),
                                    jnp.concatenate([zero, s0_ref[0, 2 * p + 1]], axis=1)], axis=0)

    def split_dot(a, b_mat, stack_axis):
        hi = a.astype(BF16)
        lo = (a - hi.astype(F32)).astype(BF16)
        both = jnp.concatenate([hi, lo], axis=stack_axis)
        if stack_axis == 1:
            r = jnp.dot(b_mat, both, preferred_element_type=F32)
            return r[:, 0:LANES] + r[:, LANES:2 * LANES]
        r = jnp.dot(both, b_mat, preferred_element_type=F32)
        return r[0:c] + r[c:2 * c]

    def chunk(ci, p):
        rows = pl.ds(pl.multiple_of(ci * c, c), c)
        lanes = slice(p * LANES, (p + 1) * LANES)
        la = la_ref[0, rows, lanes]
        q = q_ref[0, rows, lanes]
        k = k_ref[0, rows, lanes]
        v = v_ref[0, rows, lanes]
        if mm_dtype == BF16:
            b = split_dot(la, tri.astype(BF16), 1)
        else:
            b = jnp.dot(tri, la, precision=HIGHEST, preferred_element_type=F32)
        st = st_sc[p]
        b_last = b[c - 1:c, :]
        b_mid = b[c // 2 - 1:c // 2, :]
        qt = q * jnp.exp(b - b_mid)
        kt = k * jnp.exp(b_mid - b)
        qts = jnp.concatenate([jnp.where(head0, qt, 0.0), jnp.where(head0, 0.0, qt)], axis=0)
        a = lax.dot_general(qts.astype(mm_dtype), kt.astype(mm_dtype), NT_DIMS, preferred_element_type=F32)
        a0 = jnp.where(causal, a[0:c], 0.0).astype(mm_dtype)
        a1 = jnp.where(causal, a[c:2 * c], 0.0).astype(mm_dtype)
        vm = v.astype(mm_dtype)
        o = jnp.where(head0, jnp.dot(a0, vm, preferred_element_type=F32),
                      jnp.dot(a1, vm, preferred_element_type=F32))
        qe = q * jnp.exp(b)
        o = o + lax.dot_general(qe.astype(mm_dtype), st.astype(mm_dtype), NT_DIMS, preferred_element_type=F32)
        kd = k * jnp.exp(b_last - b)
        upd = lax.dot_general(vm, kd.astype(mm_dtype), TN_DIMS, preferred_element_type=F32)
        st_sc[p] = st * jnp.exp(b_last) + jnp.where(same_head, upd, 0.0)
        if mm_dtype == BF16:
            ms = split_dot(o * o, head_mean.astype(BF16), 0)
        else:
            ms = jnp.dot(o * o, head_mean, precision=HIGHEST, preferred_element_type=F32)
        o_ref[0, rows, lanes] = o * lax.rsqrt(ms + EPS) * gn * gs_ref[0, rows, lanes]

    unroll = next(u for u in (8, 4, 2, 1) if n % u == 0)

    def body(i, carry):
        for u in range(unroll):
            for p in range(npair):
                chunk(i * unroll + u, p)
        return carry

    lax.fori_loop(0, n // unroll, body, 0)
    for p in range(npair):
        st = st_sc[p]
        sout_ref[0, 2 * p] = st[0:hd, 0:hd]
        sout_ref[0, 2 * p + 1] = st[hd:2 * hd, hd:2 * hd]


def _gla(qg, kg, vg, la, gs, gn_pair, s0_t, chunk, mm_dtype):
    b, l, hw = qg.shape
    pairs = hw // LANES
    npair = pairs if l <= GLA_CHUNK else 1
    seq = pl.BlockSpec((1, l, npair * LANES), lambda i, p: (i, 0, p))
    st_spec = pl.BlockSpec((1, 2 * npair, HEAD_DIM, HEAD_DIM), lambda i, p: (i, p, 0, 0))
    return pl.pallas_call(
        functools.partial(_gla_kernel, chunk=chunk, mm_dtype=mm_dtype),
        grid=(b, pairs // npair),
        in_specs=[seq, seq, seq, seq, seq, pl.BlockSpec((1, LANES), lambda i, p: (0, 0)), st_spec],
        out_specs=[seq, st_spec],
        out_shape=[jax.ShapeDtypeStruct((b, l, hw), F32),
                   jax.ShapeDtypeStruct((b, HEADS_PER_GROUP, HEAD_DIM, HEAD_DIM), F32)],
        scratch_shapes=[pltpu.VMEM((npair, LANES, LANES), F32)],
        compiler_params=_params("arbitrary", "arbitrary"),
        name="gla_chunked",
    )(qg, kg, vg, la, gs, gn_pair, s0_t)


def _mix_ffn_kernel(x_ref, oa_ref, og_ref, mod_ref, g_ref, wo_a_ref, wo_g_ref, wg_ref, wu_ref, wd_ref, o_ref,
                    x1_sc, h_sc, acc_sc):
    j = pl.program_id(2)
    bb, tl, d = x_ref.shape
    tm = bb * tl

    @pl.when(j == 0)
    def _():
        m = mod_ref[...]
        mix = (_mm(oa_ref[...].reshape(tm, -1), wo_a_ref[...])
               + _mm(og_ref[...].reshape(tm, -1), wo_g_ref[...]))
        x1 = x_ref[...] + m[:, 2:3, :] * mix.reshape(bb, tl, d)
        h = _rms(x1, g_ref[...]) * (1.0 + m[:, 4:5, :]) + m[:, 3:4, :]
        x1_sc[...] = x1.reshape(tm, d)
        h_sc[...] = h.reshape(tm, d).astype(h_sc.dtype)
        acc_sc[...] = jnp.zeros_like(acc_sc)

    h = h_sc[...]
    acc_sc[...] += _mm(_silu(_mm(h, wg_ref[...])) * _mm(h, wu_ref[...]), wd_ref[...])

    @pl.when(j == pl.num_programs(2) - 1)
    def _():
        g2 = mod_ref[...][:, 5:6, :]
        o_ref[...] = x1_sc[...].reshape(bb, tl, d) + g2 * acc_sc[...].reshape(bb, tl, d)


def _mix_ffn(x, oa, og, mod, g, wo_a, wo_g, wg, wu, wd):
    b, l, d = x.shape
    hw = oa.shape[-1]
    f = wg.shape[1]
    bb, tl = _token_tiles(b, l, 512 if l >= 512 else 256)
    tm = bb * tl
    fc = f // 2 if wg.dtype == BF16 else 256
    tok = lambda i, t, j: (i, t, 0)
    const2 = lambda i, t, j: (0, 0)
    return pl.pallas_call(
        _mix_ffn_kernel,
        grid=(b // bb, l // tl, f // fc),
        in_specs=[
            pl.BlockSpec((bb, tl, d), tok),
            pl.BlockSpec((bb, tl, hw), tok),
            pl.BlockSpec((bb, tl, hw), tok),
            pl.BlockSpec((bb, 6, d), lambda i, t, j: (i, 0, 0)),
            pl.BlockSpec((1, d), const2),
            pl.BlockSpec(wo_a.shape, const2),
            pl.BlockSpec(wo_g.shape, const2),
            pl.BlockSpec((d, fc), lambda i, t, j: (0, j)),
            pl.BlockSpec((d, fc), lambda i, t, j: (0, j)),
            pl.BlockSpec((fc, d), lambda i, t, j: (j, 0)),
        ],
        out_specs=pl.BlockSpec((bb, tl, d), tok),
        out_shape=jax.ShapeDtypeStruct((b, l, d), F32),
        scratch_shapes=[pltpu.VMEM((tm, d), F32), pltpu.VMEM((tm, d), wg.dtype), pltpu.VMEM((tm, d), F32)],
        compiler_params=_params("arbitrary", "arbitrary", "arbitrary"),
        name="outproj_ffn",
    )(x, oa, og, mod, g, wo_a, wo_g, wg, wu, wd)


def _cm_in_kernel(x_ref, mod_ref, g_ref, w_ref, lng_ref, lnb_ref, u_ref, v_ref):
    bb, tl, d = x_ref.shape
    tm = bb * tl
    cw = u_ref.shape[-1]
    m = mod_ref[...]
    h = _rms(x_ref[...], g_ref[...]) * (1.0 + m[:, 1:2, :]) + m[:, 0:1, :]
    hb = h.reshape(tm, d).astype(w_ref.dtype)
    u_ref[...] = _gelu_tanh(_mm(hb, w_ref[:, 0:cw])).reshape(bb, tl, cw)
    zv = _gelu_tanh(_mm(hb, w_ref[:, cw:2 * cw]))
    mu = jnp.mean(zv, axis=-1, keepdims=True)
    zc = zv - mu
    var = jnp.mean(zc * zc, axis=-1, keepdims=True)
    v_ref[...] = (zc * lax.rsqrt(var + EPS) * lng_ref[...] + lnb_ref[...]).reshape(bb, tl, cw)


def _cm_in(x, mod, g, w, ln_g, ln_b):
    b, l, d = x.shape
    cw = w.shape[1] // 2
    bb, tl = _token_tiles(b, l, 512 if l >= 512 else 256)
    tok = lambda i, t: (i, t, 0)
    const2 = lambda i, t: (0, 0)
    out_spec = pl.BlockSpec((bb, tl, cw), tok)
    out_sds = jax.ShapeDtypeStruct((b, l, cw), F32)
    return pl.pallas_call(
        _cm_in_kernel,
        grid=(b // bb, l // tl),
        in_specs=[
            pl.BlockSpec((bb, tl, d), tok),
            pl.BlockSpec((bb, 6, d), lambda i, t: (i, 0, 0)),
            pl.BlockSpec((1, d), const2),
            pl.BlockSpec(w.shape, const2),
            pl.BlockSpec((1, cw), const2),
            pl.BlockSpec((1, cw), const2),
        ],
        out_specs=[out_spec, out_spec],
        out_shape=[out_sds, out_sds],
        compiler_params=_params("arbitrary", "arbitrary"),
        name="cm_in_gelu_ln",
    )(x, mod, g, w, ln_g, ln_b)


def _cm_out_kernel(x_ref, u_ref, v_ref, mod_ref, g_ref, ws_ref, bs_ref, wo_ref, wr_ref,
                   x3_ref, h_ref, gates_ref, rank_ref, cnt_ref, us_sc):
    bb, tl, d = x_ref.shape
    tm = bb * tl
    cw = u_ref.shape[-1]
    ngroups, cm, _ = ws_ref.shape
    gd = cw // ngroups
    u = u_ref[...].reshape(tm, cw)
    v = v_ref[...].reshape(tm, cw)
    bs = bs_ref[...]
    mm_dtype = wo_ref.dtype
    r = lax.broadcasted_iota(jnp.int32, (cm, cm), 0)
    c = lax.broadcasted_iota(jnp.int32, (cm, cm), 1)
    for g in range(ngroups):
        wsg = jnp.where(r >= c, ws_ref[g], 0.0).astype(mm_dtype)
        cols = slice(g * gd, (g + 1) * gd)
        for sb in range(tm // cm):
            rows = slice(sb * cm, (sb + 1) * cm)
            s = _mm(wsg, v[rows, cols].astype(mm_dtype)) + bs[rows, g:g + 1]
            us_sc[rows, cols] = (u[rows, cols] * s).astype(mm_dtype)
    y = _mm(us_sc[...], wo_ref[...])
    m = mod_ref[...]
    x3 = x_ref[...] + m[:, 2:3, :] * y.reshape(bb, tl, d)
    x3_ref[...] = x3
    h = (_rms(x3, g_ref[...]) * (1.0 + m[:, 4:5, :]) + m[:, 3:4, :]).reshape(tm, d)
    h_ref[...] = h.astype(BF16)
    wr = wr_ref[...]
    if mm_dtype == BF16:
        h_hi = h.astype(BF16)
        h_lo = (h - h_hi.astype(F32)).astype(BF16)
        wr_hi = wr.astype(BF16)
        wr_lo = (wr - wr_hi.astype(F32)).astype(BF16)
        both = jnp.dot(jnp.concatenate([h_hi, h_lo], axis=0), wr_hi, preferred_element_type=F32)
        logits = both[0:tm] + both[tm:2 * tm] + jnp.dot(h_hi, wr_lo, preferred_element_type=F32)
    else:
        logits = jnp.dot(h, wr, precision=HIGHEST, preferred_element_type=F32)
    ne = logits.shape[1]
    col = lax.broadcasted_iota(jnp.int32, (tm, ne), 1)
    m1 = jnp.max(logits, axis=1, keepdims=True)
    i1 = jnp.min(jnp.where(logits == m1, col, ne), axis=1, keepdims=True)
    rest = jnp.where(col == i1, -jnp.inf, logits)
    m2 = jnp.max(rest, axis=1, keepdims=True)
    i2 = jnp.min(jnp.where(rest == m2, col, ne), axis=1, keepdims=True)
    e2 = jnp.exp(m2 - m1)
    gates = jnp.where(col == i1, 1.0 / (1.0 + e2), 0.0) + jnp.where(col == i2, e2 / (1.0 + e2), 0.0)
    gates_ref[...] = gates
    routed = gates > 0.0
    tr = lax.broadcasted_iota(jnp.int32, (tm, tm), 0)
    tc = lax.broadcasted_iota(jnp.int32, (tm, tm), 1)
    upto = jnp.dot(jnp.where(tr >= tc, 1.0, 0.0).astype(BF16), jnp.where(routed, 1.0, 0.0).astype(BF16),
                   preferred_element_type=F32)
    rank_ref[...] = jnp.where(routed, upto - 1.0, -1.0)
    cnt_ref[0] = upto[tm - 1:tm, :]


def _cm_out(x, u, v, mod, g, ws_mix, bs_rows, wo, wr, cm):
    b, l, d = x.shape
    cw = u.shape[-1]
    ne = wr.shape[1]
    bb, tl = _token_tiles(b, l, 256)
    tm = bb * tl
    tok = lambda i, t: (i, t, 0)
    flat = lambda i, t: (i * (l // tl) + t, 0)
    const2 = lambda i, t: (0, 0)
    return pl.pallas_call(
        _cm_out_kernel,
        grid=(b // bb, l // tl),
        in_specs=[
            pl.BlockSpec((bb, tl, d), tok),
            pl.BlockSpec((bb, tl, cw), tok),
            pl.BlockSpec((bb, tl, cw), tok),
            pl.BlockSpec((bb, 6, d), lambda i, t: (i, 0, 0)),
            pl.BlockSpec((1, d), const2),
            pl.BlockSpec(ws_mix.shape, lambda i, t: (0, 0, 0)),
            pl.BlockSpec((tm, bs_rows.shape[1]), const2),
            pl.BlockSpec(wo.shape, const2),
            pl.BlockSpec(wr.shape, const2),
        ],
        out_specs=[pl.BlockSpec((bb, tl, d), tok), pl.BlockSpec((tm, d), flat), pl.BlockSpec((tm, ne), flat),
                   pl.BlockSpec((tm, ne), flat),
                   pl.BlockSpec((1, 1, ne), lambda i, t: (i * (l // tl) + t, 0, 0))],
        out_shape=[jax.ShapeDtypeStruct((b, l, d), F32), jax.ShapeDtypeStruct((b * l, d), BF16),
                   jax.ShapeDtypeStruct((b * l, ne), F32), jax.ShapeDtypeStruct((b * l, ne), F32),
                   jax.ShapeDtypeStruct((b * l // tm, 1, ne), F32)],
        scratch_shapes=[pltpu.VMEM((tm, cw), wo.dtype)],
        compiler_params=_params("arbitrary", "arbitrary"),
        name="cm_out_router",
    )(x, u, v, mod, g, ws_mix, bs_rows, wo, wr)


MOE_SUB = 256
MOE_ALIGN = 16
MOE_WIN_SMALL = 128
MOE_WIN_FULL = MOE_SUB + MOE_ALIGN
MOE_ROW_CHUNK = 512
MOE_FFN_ROWS = 640
MOE_BLOCK_SUBS = 9
MOE_HIDDEN_CHUNK = 512


def _moe_routing(rank, cnt, nb, nsub):
    ne = rank.shape[1]
    rank = rank.astype(jnp.int32).reshape(nb, nsub, MOE_SUB, ne)
    cnt = cnt.astype(jnp.int32).reshape(nb, nsub, ne)
    off = jnp.cumsum(cnt, axis=1) - cnt
    total = jnp.sum(cnt, axis=1)
    start = (off // MOE_ALIGN) * MOE_ALIGN
    shift = off - start
    win_row = jnp.where(rank >= 0, rank + shift[:, :, None, :], -1)
    win_row = win_row.transpose(0, 3, 1, 2).reshape(nb, ne, nsub * MOE_SUB)
    used = jnp.where(cnt > 0, shift + cnt, 0)
    scalars = jnp.concatenate([start.transpose(0, 2, 1).reshape(-1), used.transpose(0, 2, 1).reshape(-1),
                               total.reshape(-1)]).astype(jnp.int32)
    return win_row, scalars


def _moe_kernel(sc_ref, h_ref, gates_ref, row_ref, wg_ref, wu_ref, wd_ref, o_ref, x_sc, y_sc):
    b = pl.program_id(0)
    e = pl.program_id(1)
    j = pl.program_id(2)
    ne = pl.num_programs(1)
    tb = h_ref.shape[0]
    nsub = tb // MOE_SUB
    cap = x_sc.shape[0]
    n_be = pl.num_programs(0) * ne
    seg = (b * ne + e) * nsub
    total = sc_ref[2 * n_be * nsub + b * ne + e]

    def windows(fn):
        for s in range(nsub):
            start = pl.multiple_of(sc_ref[seg + s], MOE_ALIGN)
            used = sc_ref[n_be * nsub + seg + s]
            tok = slice(s * MOE_SUB, (s + 1) * MOE_SUB)
            win_row = row_ref[0, pl.ds(e, 1), tok]

            def run(rows, start=start, tok=tok, win_row=win_row):
                ridx = lax.broadcasted_iota(jnp.int32, (rows, MOE_SUB), 0)
                fn(pl.ds(start, rows), tok, jnp.where(ridx == win_row, 1.0, 0.0).astype(BF16))

            pl.when((used > 0) & (used <= MOE_WIN_SMALL))(functools.partial(run, MOE_WIN_SMALL))
            pl.when(used > MOE_WIN_SMALL)(functools.partial(run, MOE_WIN_FULL))

    @pl.when((e == 0) & (j == 0))
    def _():
        o_ref[...] = jnp.zeros_like(o_ref)

    @pl.when(j == 0)
    def _():
        def clear(c, carry):
            rows = pl.ds(pl.multiple_of(c * MOE_ROW_CHUNK, MOE_ROW_CHUNK), MOE_ROW_CHUNK)
            x_sc[rows, :] = jnp.zeros((MOE_ROW_CHUNK, x_sc.shape[1]), BF16)
            y_sc[rows, :] = jnp.zeros((MOE_ROW_CHUNK, y_sc.shape[1]), F32)
            return carry

        lax.fori_loop(0, jnp.minimum((total + MOE_WIN_FULL + MOE_ROW_CHUNK - 1) // MOE_ROW_CHUNK,
                                     cap // MOE_ROW_CHUNK), clear, 0)

        def dispatch(rows, tok, onehot):
            picked = jnp.dot(onehot, h_ref[tok, :], preferred_element_type=F32).astype(BF16)
            x_sc[rows, :] = x_sc[rows, :] + picked

        windows(dispatch)

    def ffn(r0, nrows):
        rows = pl.ds(pl.multiple_of(r0, MOE_WIN_SMALL), nrows)
        x = x_sc[rows, :]
        gate = jnp.dot(x, wg_ref[0], preferred_element_type=F32)
        up = jnp.dot(x, wu_ref[0], preferred_element_type=F32)
        act = (_silu(gate) * up).astype(BF16)
        y_sc[rows, :] += jnp.dot(act, wd_ref[0], preferred_element_type=F32)

    nfull = (jnp.maximum(total - 512, 0) + MOE_FFN_ROWS - 1) // MOE_FFN_ROWS
    rem = jnp.maximum(total - nfull * MOE_FFN_ROWS, 0)
    tail = nfull * MOE_FFN_ROWS

    def full_chunk(c, carry):
        ffn(c * MOE_FFN_ROWS, MOE_FFN_ROWS)
        return carry

    lax.fori_loop(0, nfull, full_chunk, 0)
    take_256 = (rem > 128) & (rem <= 384)
    pl.when(rem > 384)(lambda: ffn(tail, 512))
    pl.when(take_256)(lambda: ffn(tail, 256))
    pl.when((rem > 0) & (rem <= 128))(lambda: ffn(tail, 128))
    pl.when((rem > 256) & (rem <= 384))(lambda: ffn(tail + 256, 128))

    @pl.when(j == pl.num_programs(2) - 1)
    def _():
        gates = gates_ref[...]
        col = lax.broadcasted_iota(jnp.int32, gates.shape, 1)
        ge = jnp.sum(jnp.where(col == e, gates, 0.0), axis=1, keepdims=True)

        def combine(rows, tok, onehot):
            back = lax.dot_general(onehot, y_sc[rows, :].astype(BF16), TN_DIMS, preferred_element_type=F32)
            o_ref[tok, :] = (o_ref[tok, :].astype(F32) + ge[tok] * back).astype(o_ref.dtype)

        windows(combine)


def _moe(h, gates, rank, cnt, wg, wu, wd, tb):
    t, d = h.shape
    ne, _, f = wg.shape
    fc = MOE_HIDDEN_CHUNK
    assert t % tb == 0 and tb % MOE_SUB == 0 and f % fc == 0
    nb, nsub = t // tb, tb // MOE_SUB
    win_row, scalars = _moe_routing(rank, cnt, nb, nsub)
    cap = -(-(tb + MOE_WIN_FULL) // MOE_ROW_CHUNK) * MOE_ROW_CHUNK
    grid_spec = pltpu.PrefetchScalarGridSpec(
        num_scalar_prefetch=1,
        grid=(nb, ne, f // fc),
        in_specs=[
            pl.BlockSpec((tb, d), lambda i, e, j, sc: (i, 0)),
            pl.BlockSpec((tb, ne), lambda i, e, j, sc: (i, 0)),
            pl.BlockSpec((1, ne, tb), lambda i, e, j, sc: (i, 0, 0)),
            pl.BlockSpec((1, d, fc), lambda i, e, j, sc: (e, 0, j)),
            pl.BlockSpec((1, d, fc), lambda i, e, j, sc: (e, 0, j)),
            pl.BlockSpec((1, fc, d), lambda i, e, j, sc: (e, j, 0)),
        ],
        out_specs=pl.BlockSpec((tb, d), lambda i, e, j, sc: (i, 0)),
        scratch_shapes=[pltpu.VMEM((cap, d), BF16), pltpu.VMEM((cap, d), F32)],
    )
    return pl.pallas_call(
        _moe_kernel,
        grid_spec=grid_spec,
        out_shape=jax.ShapeDtypeStruct((t, d), BF16),
        compiler_params=_params("arbitrary", "arbitrary", "arbitrary"),
        name="moe_experts",
    )(scalars, h, gates, win_row, wg, wu, wd)


def _final_kernel(x_ref, moe_ref, mod_ref, g_ref, o_ref):
    bb, tl, d = x_ref.shape
    g2 = mod_ref[...][:, 5:6, :]
    x4 = x_ref[...] + g2 * moe_ref[...].reshape(bb, tl, d)
    o_ref[...] = _rms(x4, g_ref[...])


def _final(x, moe, row0, mod, g):
    b, l, d = x.shape
    bb, tl = _token_tiles(b, l, 512 if l >= 512 else 256)
    tm = bb * tl
    assert row0 % tm == 0
    tok = lambda i, t: (i, t, 0)
    return pl.pallas_call(
        _final_kernel,
        grid=(b // bb, l // tl),
        in_specs=[
            pl.BlockSpec((bb, tl, d), tok),
            pl.BlockSpec((tm, d), lambda i, t: (row0 // tm + i * (l // tl) + t, 0)),
            pl.BlockSpec((bb, 6, d), lambda i, t: (i, 0, 0)),
            pl.BlockSpec((1, d), lambda i, t: (0, 0)),
        ],
        out_specs=pl.BlockSpec((bb, tl, d), tok),
        out_shape=jax.ShapeDtypeStruct((b, l, d), F32),
        compiler_params=_params("arbitrary", "arbitrary"),
        name="final_norm",
    )(x, moe, mod, g)


def _layer0(x, mod, pos0, past, w):
    b, l, d = x.shape
    hw = HEAD_WIDTH
    outs = _in_proj(x, mod, w["norm_mix_g0"], w["w_in_main"], w["w_in_lr"], w["wa2_hi"], w["wa2_lo"], w["ba2"], pos0)
    qa, ka, va, qg, kg, vg, gs, la = outs
    if past is None:
        oa = _moba_prompt(qa, ka, va)
        s0_t = jnp.zeros((b, HEADS_PER_GROUP, HEAD_DIM, HEAD_DIM), F32)
    else:
        cache_k, cache_v, state, page_table = past
        oa = _moba_sample(qa, ka, va, cache_k, cache_v, page_table)
        s0_t = jnp.swapaxes(state, -1, -2)
    chunk = GLA_CHUNK if l % GLA_CHUNK == 0 else l
    mm_dtype = BF16 if chunk % 16 == 0 else F32
    og, s_t = _gla(qg, kg, vg, la, gs, w["gla_norm_pair"], s0_t, chunk, mm_dtype)
    x2 = _mix_ffn(x, oa, og, mod, w["norm_ffn_g0"], w["wo_a"], w["wo_g"], w["ffn_wg"], w["ffn_wu"], w["ffn_wd"])
    k_rows = ka.reshape(b, l, HEADS_PER_GROUP, HEAD_DIM)
    v_rows = va.reshape(b, l, HEADS_PER_GROUP, HEAD_DIM)
    return x2, k_rows, v_rows, jnp.swapaxes(s_t, -1, -2)


def _layer1_front(x, mod, w):
    b, l, d = x.shape
    u, v = _cm_in(x, mod, w["norm_mix_g1"], w["cm_w_in"], w["cm_ln_g"], w["cm_ln_b"])
    bb, tl = _token_tiles(b, l, 256)
    tm = bb * tl
    if l >= CM_CHUNK:
        cm = CM_CHUNK
        ws_mix = w["cm_w_s"]
        bs_rows = jnp.tile(w["cm_b_s"].T, (tm // CM_CHUNK, 1))
    else:
        cm = tm
        pos = jnp.arange(tm, dtype=jnp.int32)
        same_seq = (pos[:, None] // l == pos[None, :] // l).astype(F32)
        place = (pos[:, None] % l == jnp.arange(l, dtype=jnp.int32)[None, :]).astype(F32)
        ws_mix = jnp.einsum("ra,gab,cb->grc", place, w["cm_w_s"][:, :l, :l], place,
                            precision=HIGHEST) * same_seq[None]
        bs_rows = jnp.tile(w["cm_b_s"][:, :l].T, (tm // l, 1))
    assert tm == MOE_SUB
    x3, h, gates, rank, cnt = _cm_out(x, u, v, mod, w["norm_ffn_g1"], ws_mix, bs_rows, w["cm_w_out"],
                                      w["moe_w_router"], cm)
    start = ((l - 1) // CM_CHUNK) * CM_CHUNK
    return x3, (h, gates, rank, cnt), v[:, start:]


def kernel(x_prompt, x_sample, cache_k, cache_v, state_gla, page_table, c_prompt, c_sample, ada_w, ada_b, norm_mix_g, norm_ffn_g, ab_w_in, gla_w_a2, gla_b_a2, gla_norm_g, ab_w_out, ffn_w_gate, ffn_w_up, ffn_w_down, cm_w_in, cm_ln_g, cm_ln_b, cm_w_s, cm_b_s, cm_w_out, moe_w_router, moe_w_gate, moe_w_up, moe_w_down, final_norm_g):
    bp, lp, d = x_prompt.shape
    bs, ls, _ = x_sample.shape
    hw = HEAD_WIDTH
    n_main = 7 * hw
    past_len = page_table.shape[1] * cache_k.shape[2]

    wa2 = jnp.pad(gla_w_a2[0], ((0, LANES - GLA_LOWRANK), (0, 0)))
    wa2_hi = wa2.astype(BF16)
    shared = {
        "norm_mix_g0": norm_mix_g[0][None], "norm_mix_g1": norm_mix_g[1][None],
        "norm_ffn_g0": norm_ffn_g[0][None], "norm_ffn_g1": norm_ffn_g[1][None],
        "wa2_hi": wa2_hi, "wa2_lo": (wa2 - wa2_hi.astype(F32)).astype(BF16),
        "ba2": gla_b_a2[0][None],
        "gla_norm_pair": jnp.tile(gla_norm_g[0], LANES // HEAD_DIM)[None],
        "cm_ln_g": cm_ln_g[0][None], "cm_ln_b": cm_ln_b[0][None],
        "cm_w_s": cm_w_s[0], "cm_b_s": cm_b_s[0],
        "moe_w_router": moe_w_router[0],
    }
    mats = {
        "w_in_main": ab_w_in[0][:, :n_main],
        "w_in_lr": jnp.pad(ab_w_in[0][:, n_main:], ((0, 0), (0, LANES - GLA_LOWRANK))),
        "wo_a": ab_w_out[0][:hw], "wo_g": ab_w_out[0][hw:],
        "ffn_wg": ffn_w_gate[0], "ffn_wu": ffn_w_up[0], "ffn_wd": ffn_w_down[0],
        "cm_w_in": cm_w_in[0], "cm_w_out": cm_w_out[0],
    }
    w = dict(shared, **{k: v.astype(BF16) for k, v in mats.items()})
    w_sample = dict(shared, **mats)

    rows = bp + bs
    rows_pad = -(-rows // 16) * 16
    c_all = jnp.pad(jnp.concatenate([c_prompt, c_sample], axis=0), ((0, rows_pad - rows), (0, 0)))
    mod_all = _modulation(c_all, ada_w, ada_b)
    mod_p = [mod_all[l, :bp].reshape(bp, 6, d) for l in range(2)]
    mod_s = [mod_all[l, bp:rows].reshape(bs, 6, d) for l in range(2)]

    x2_p, k_p, v_p, gla_p = _layer0(x_prompt, mod_p[0], 0, None, w)
    x2_s, k_s, v_s, gla_s = _layer0(x_sample, mod_s[0], past_len,
                                    (cache_k[0], cache_v[0], state_gla[0], page_table), w_sample)

    x3_p, route_p, cmv_p = _layer1_front(x2_p, mod_p[1], w)
    x3_s, route_s, cmv_s = _layer1_front(x2_s, mod_s[1], w_sample)

    t_all = bp * lp + bs * ls
    subs = -(-t_all // MOE_SUB)
    n_blocks = -(-subs // MOE_BLOCK_SUBS)
    tb = -(-subs // n_blocks) * MOE_SUB
    pad = n_blocks * tb - t_all
    ne = moe_w_gate.shape[1]
    pads = (jnp.zeros((pad, d), BF16), jnp.zeros((pad, ne), F32), jnp.full((pad, ne), -1.0, F32),
            jnp.zeros((pad // MOE_SUB, 1, ne), F32))
    h_all, gates_all, rank_all, cnt_all = (jnp.concatenate(parts, axis=0) for parts in zip(route_p, route_s, pads))
    moe = _moe(h_all, gates_all, rank_all, cnt_all, moe_w_gate[0].astype(BF16), moe_w_up[0].astype(BF16),
               moe_w_down[0].astype(BF16), tb)

    fg = final_norm_g[None]
    y_p = _final(x3_p, moe, 0, mod_p[1], fg)
    y_s = _final(x3_s, moe, bp * lp, mod_s[1], fg)

    return (y_p, y_s, k_p[None], v_p[None], k_s[None], v_s[None], gla_p[None], gla_s[None],
            cmv_p[None], cmv_s[None])
```

```python
import functools

import jax
import jax.numpy as jnp
from jax import lax
from jax.experimental import pallas as pl
from jax.experimental.pallas import tpu as pltpu

F32 = jnp.float32
BF16 = jnp.bfloat16
HIGHEST = lax.Precision.HIGHEST

HEAD_DIM = 64
HEADS_PER_GROUP = 8
HEAD_WIDTH = HEADS_PER_GROUP * HEAD_DIM
MOBA_BLOCK = 256
MOBA_TOPK = 3
ROPE_THETA = 10000.0
GLA_LOWRANK = 16
GLA_GATE_NORM = 16.0
GLA_CHUNK = 64
CM_CHUNK = 128
CM_GROUPS = 8
MOE_TOP_K = 2
EPS = 1e-6
NEG = -1e30

LANES = 128
FFN_COL_CHUNK = 1024
VMEM_LIMIT_BYTES = 56 * 1024 * 1024

NT_DIMS = (((1,), (1,)), ((), ()))
TN_DIMS = (((0,), (0,)), ((), ()))


def _params(*sem):
    return pltpu.CompilerParams(dimension_semantics=sem, vmem_limit_bytes=VMEM_LIMIT_BYTES)


def _rms(x, g):
    return x * lax.rsqrt(jnp.mean(x * x, axis=-1, keepdims=True) + EPS) * g


def _silu(x):
    return x * jax.nn.sigmoid(x)


def _mm(a, w):
    if w.dtype == F32:
        return jnp.dot(a.astype(F32), w, precision=HIGHEST, preferred_element_type=F32)
    return jnp.dot(a.astype(BF16), w, preferred_element_type=F32)


def _gelu_tanh(x):
    c = 0.7978845608028654
    return x * (0.5 + 0.5 * jnp.tanh(x * (c + (c * 0.044715) * (x * x))))


def _token_tiles(b, l, target):
    if l >= target:
        assert l % target == 0
        return 1, target
    bb = min(b, target // l)
    assert target % l == 0 and b % bb == 0
    return bb, l


def _mod_kernel(c_ref, w_ref, b_ref, o_ref):
    s = _silu(c_ref[...])
    r = s.shape[0]
    s_hi = s.astype(BF16)
    s_lo = (s - s_hi.astype(F32)).astype(BF16)
    w = w_ref[0]
    w_hi = w.astype(BF16)
    w_lo = (w - w_hi.astype(F32)).astype(BF16)
    both = jnp.dot(jnp.concatenate([s_hi, s_lo], axis=0), w_hi, preferred_element_type=F32)
    o_ref[0] = both[0:r] + both[r:2 * r] + jnp.dot(s_hi, w_lo, preferred_element_type=F32) + b_ref[0]


def _modulation(c_all, ada_w, ada_b):
    r, d = c_all.shape
    depth, _, n = ada_w.shape
    tn = 1536
    return pl.pallas_call(
        _mod_kernel,
        grid=(depth, n // tn),
        in_specs=[
            pl.BlockSpec((r, d), lambda l, j: (0, 0)),
            pl.BlockSpec((1, d, tn), lambda l, j: (l, 0, j)),
            pl.BlockSpec((1, 1, tn), lambda l, j: (l, 0, j)),
        ],
        out_specs=pl.BlockSpec((1, r, tn), lambda l, j: (l, 0, j)),
        out_shape=jax.ShapeDtypeStruct((depth, r, n), F32),
        compiler_params=_params("arbitrary", "arbitrary"),
        name="adaln_modulation",
    )(c_all, ada_w, ada_b.reshape(depth, 1, n))


def _in_proj_kernel(x_ref, mod_ref, g_ref, w_ref, wlr_ref, wa2h_ref, wa2l_ref, ba2_ref, cos_ref, sin_ref,
                    qa_ref, ka_ref, va_ref, qg_ref, kg_ref, vg_ref, gs_ref, la_ref):
    bb, tl, d = x_ref.shape
    tm = bb * tl
    hw = HEAD_WIDTH
    m = mod_ref[...]
    h = _rms(x_ref[...], g_ref[...]) * (1.0 + m[:, 1:2, :]) + m[:, 0:1, :]
    hb = h.reshape(tm, d).astype(w_ref.dtype)

    def proj(c):
        return _mm(hb, w_ref[:, c * hw:(c + 1) * hw])

    cos = jnp.concatenate([cos_ref[...]] * (hw // LANES), axis=-1)[None]
    sin = jnp.concatenate([sin_ref[...]] * (hw // LANES), axis=-1)[None]
    lane = lax.broadcasted_iota(jnp.int32, (tm, hw), 1)
    first_half = (lane % HEAD_DIM) < (HEAD_DIM // 2)

    def rope(z):
        partner = jnp.where(first_half, pltpu.roll(z, hw - HEAD_DIM // 2, 1), pltpu.roll(z, HEAD_DIM // 2, 1))
        return z.reshape(bb, tl, hw) * cos + partner.reshape(bb, tl, hw) * sin

    scale = HEAD_DIM ** -0.5
    qa_ref[...] = rope(proj(0)) * scale
    ka_ref[...] = rope(proj(1))
    va_ref[...] = proj(2).reshape(bb, tl, hw)
    qg_ref[...] = (proj(3) * scale).reshape(bb, tl, hw)
    kg_ref[...] = proj(4).reshape(bb, tl, hw)
    vg_ref[...] = proj(5).reshape(bb, tl, hw)
    gs_ref[...] = _silu(proj(6)).reshape(bb, tl, hw)
    zlr = _mm(hb, wlr_ref[...])
    hi = zlr.astype(BF16)
    lo = (zlr - hi.astype(F32)).astype(BF16)
    pre = (jnp.dot(hi, wa2h_ref[...], preferred_element_type=F32)
           + jnp.dot(lo, wa2h_ref[...], preferred_element_type=F32)
           + jnp.dot(hi, wa2l_ref[...], preferred_element_type=F32)) + ba2_ref[...]
    log_sig = jnp.minimum(pre, 0.0) - jnp.log(1.0 + jnp.exp(-jnp.abs(pre)))
    la_ref[...] = (log_sig * (1.0 / GLA_GATE_NORM)).reshape(bb, tl, hw)


def _rope_tables(pos0, l):
    half = HEAD_DIM // 2
    inv = 1.0 / (ROPE_THETA ** (jnp.arange(half, dtype=F32) / half))
    ang = (pos0 + jnp.arange(l, dtype=jnp.int32)).astype(F32)[:, None] * inv[None, :]
    cos, sin = jnp.cos(ang), jnp.sin(ang)
    reps = LANES // HEAD_DIM
    cos_t = jnp.tile(jnp.concatenate([cos, cos], axis=-1), (1, reps))
    sin_t = jnp.tile(jnp.concatenate([-sin, sin], axis=-1), (1, reps))
    return cos_t, sin_t


def _in_proj(x, mod, g, w_main, w_lr, wa2_hi, wa2_lo, ba2, pos0):
    b, l, d = x.shape
    bb, tl = _token_tiles(b, l, 512 if l >= 512 else 256)
    hw = HEAD_WIDTH
    cos_t, sin_t = _rope_tables(pos0, l)
    tok = lambda i, j: (i, j, 0)
    const2 = lambda i, j: (0, 0)
    out_spec = pl.BlockSpec((bb, tl, hw), tok)
    out_sds = jax.ShapeDtypeStruct((b, l, hw), F32)
    return pl.pallas_call(
        _in_proj_kernel,
        grid=(b // bb, l // tl),
        in_specs=[
            pl.BlockSpec((bb, tl, d), tok),
            pl.BlockSpec((bb, 6, d), lambda i, j: (i, 0, 0)),
            pl.BlockSpec((1, d), const2),
            pl.BlockSpec((d, 7 * hw), const2),
            pl.BlockSpec(w_lr.shape, const2),
            pl.BlockSpec(wa2_hi.shape, const2),
            pl.BlockSpec(wa2_lo.shape, const2),
            pl.BlockSpec((1, hw), const2),
            pl.BlockSpec((tl, LANES), lambda i, j: (j, 0)),
            pl.BlockSpec((tl, LANES), lambda i, j: (j, 0)),
        ],
        out_specs=[out_spec] * 8,
        out_shape=[out_sds] * 8,
        compiler_params=_params("arbitrary", "arbitrary"),
        name="in_proj_rope",
    )(x, mod, g, w_main, w_lr, wa2_hi, wa2_lo, ba2, cos_t, sin_t)


def _moba_prompt_kernel(q_ref, k_ref, v_ref, o_ref, kb_sc, vb_sc, km_sc, qs_sc, s_sc, p_sc, m_sc, acc_sc):
    ti = pl.program_id(2)
    lk = k_ref.shape[1]
    nblk = lk // MOBA_BLOCK
    nblk8 = -(-nblk // 8) * 8
    tq = q_ref.shape[1]
    nsub = tq // MOBA_BLOCK

    @pl.when(ti == 0)
    def _():
        k = k_ref[0]
        v = v_ref[0]
        row = lax.broadcasted_iota(jnp.int32, (lk, LANES), 0)
        col = lax.broadcasted_iota(jnp.int32, (lk, LANES), 1)
        kb_sc[:, 0:LANES] = k.astype(BF16)
        kb_sc[:, LANES:2 * LANES] = jnp.where(row // MOBA_BLOCK == col, 1.0, 0.0).astype(BF16)
        vb_sc[0] = jnp.where(col < HEAD_DIM, v, 1.0).astype(BF16)
        vb_sc[1] = jnp.where(col >= HEAD_DIM, v, 1.0).astype(BF16)
        km_sc[...] = jnp.zeros_like(km_sc)
        km_sc[0:nblk, :] = jnp.mean(k.reshape(nblk, MOBA_BLOCK, LANES), axis=1)

    q = q_ref[0]
    lane = lax.broadcasted_iota(jnp.int32, (tq, LANES), 1)
    blk = lax.broadcasted_iota(jnp.int32, (nblk8, tq), 0)
    own = ti * nsub + lax.broadcasted_iota(jnp.int32, (nblk8, tq), 1) // MOBA_BLOCK
    past_blk = blk < own
    km = km_sc[...]
    for h in range(2):
        head_lanes = (lane < HEAD_DIM) if h == 0 else (lane >= HEAD_DIM)
        qh = jnp.where(head_lanes, q, 0.0)
        gate = lax.dot_general(km, qh, NT_DIMS, precision=HIGHEST, preferred_element_type=F32)[0:nblk8]
        gate = jnp.where(past_blk, gate, -jnp.inf)
        cnt = jnp.zeros((nblk8, tq), F32)
        for mrow in range(nblk):
            gm = gate[mrow:mrow + 1, :]
            beats = (gm > gate) | ((gm == gate) & (mrow < blk))
            cnt = cnt + jnp.where(beats, 1.0, 0.0)
        sel = (past_blk & (cnt < float(MOBA_TOPK))) | (blk == own)
        bias_t = jnp.concatenate(
            [jnp.where(sel, 0.0, NEG), jnp.full((LANES - nblk8, tq), NEG, F32)], axis=0)
        qs_sc[h, :, 0:LANES] = qh.astype(BF16)
        qs_sc[h, :, LANES:2 * LANES] = bias_t.T.astype(BF16)

    m_sc[...] = jnp.full_like(m_sc, NEG)
    acc_sc[...] = jnp.zeros_like(acc_sc)
    rc = 128
    rowi = lax.broadcasted_iota(jnp.int32, (rc, MOBA_BLOCK), 0)
    coli = lax.broadcasted_iota(jnp.int32, (rc, MOBA_BLOCK), 1)

    def scores(n, slot):
        start = pl.multiple_of(n * MOBA_BLOCK, MOBA_BLOCK)
        kblk = kb_sc[pl.ds(start, MOBA_BLOCK), :]
        for h in range(2):
            s_sc[slot, h] = lax.dot_general(qs_sc[h], kblk, NT_DIMS, preferred_element_type=F32)

    def consume(n, slot, own_sub):
        start = pl.multiple_of(n * MOBA_BLOCK, MOBA_BLOCK)
        for h in range(2):
            for r0 in range(0, tq, rc):
                rows = slice(r0, r0 + rc)
                s = s_sc[slot, h, rows, :]
                if own_sub is not None and r0 // MOBA_BLOCK == own_sub:
                    s = jnp.where(rowi + (r0 - own_sub * MOBA_BLOCK) < coli, NEG, s)
                m_old = m_sc[h, rows, :]
                row_max = jnp.max(jnp.maximum(s[:, 0:LANES], s[:, LANES:2 * LANES]), axis=1, keepdims=True)
                m_new = jnp.maximum(m_old, row_max)
                p_sc[h, rows, :] = jnp.exp(s - jnp.concatenate([m_new, m_new], axis=1)).astype(BF16)
                acc_sc[h, rows, :] = jnp.exp(m_old - m_new) * acc_sc[h, rows, :]
                m_sc[h, rows, :] = m_new
            acc_sc[h] += jnp.dot(p_sc[h], vb_sc[h, pl.ds(start, MOBA_BLOCK), :], preferred_element_type=F32)

    def past_pair(i, carry):
        scores(2 * i + 1, 1)
        consume(2 * i, 0, None)
        scores(2 * i + 2, 0)
        consume(2 * i + 1, 1, None)
        return carry

    scores(0, 0)
    lax.fori_loop(0, ti, past_pair, 0)
    scores(2 * ti + 1, 1)
    consume(2 * ti, 0, 0)
    consume(2 * ti + 1, 1, 1)
    a0 = acc_sc[0]
    a1 = acc_sc[1]
    o_ref[0] = jnp.where(lane < HEAD_DIM, a0 / pltpu.roll(a0, HEAD_DIM, 1), a1 / pltpu.roll(a1, HEAD_DIM, 1))


def _moba_prompt(qa, ka, va):
    b, l, hw = qa.shape
    tq = 2 * MOBA_BLOCK
    assert l % tq == 0
    pairs = hw // LANES
    return pl.pallas_call(
        _moba_prompt_kernel,
        grid=(b, pairs, l // tq),
        in_specs=[
            pl.BlockSpec((1, tq, LANES), lambda i, p, t: (i, t, p)),
            pl.BlockSpec((1, l, LANES), lambda i, p, t: (i, 0, p)),
            pl.BlockSpec((1, l, LANES), lambda i, p, t: (i, 0, p)),
        ],
        out_specs=pl.BlockSpec((1, tq, LANES), lambda i, p, t: (i, t, p)),
        out_shape=jax.ShapeDtypeStruct((b, l, hw), F32),
        scratch_shapes=[
            pltpu.VMEM((l, 2 * LANES), BF16),
            pltpu.VMEM((2, l, LANES), BF16),
            pltpu.VMEM((LANES, LANES), F32),
            pltpu.VMEM((2, tq, 2 * LANES), BF16),
            pltpu.VMEM((2, 2, tq, MOBA_BLOCK), F32),
            pltpu.VMEM((2, tq, MOBA_BLOCK), BF16),
            pltpu.VMEM((2, tq, LANES), F32),
            pltpu.VMEM((2, tq, LANES), F32),
        ],
        compiler_params=_params("arbitrary", "arbitrary", "arbitrary"),
        name="moba_prompt",
    )(qa, ka, va)


def _moba_sample_kernel(pt_ref, q_ref, kn_ref, vn_ref, *refs, blocks_per_step):
    del pt_ref
    npages = 2 * blocks_per_step
    k_refs, v_refs = refs[0:npages], refs[npages:2 * npages]
    o_ref, q_sc, m_sc, l_sc, g_sc, acc_sc = refs[2 * npages:]
    j = pl.program_id(1)
    last = pl.num_programs(1) - 1
    nb = acc_sc.shape[0]
    nq = q_ref.shape[1]
    nh = HEADS_PER_GROUP
    nr = nh * nq
    hw = HEAD_WIDTH
    keys = 2 * k_refs[0].shape[3]
    row_w = lax.broadcasted_iota(jnp.int32, (nr, hw), 0)
    lane_w = lax.broadcasted_iota(jnp.int32, (nr, hw), 1)
    own_head = (row_w // nq) == (lane_w // HEAD_DIM)
    lane = lax.broadcasted_iota(jnp.int32, (nr, LANES), 1)

    def to_queries(t):
        return jnp.sum(jnp.where(own_head, t, 0.0).reshape(nh, nq, hw), axis=0)

    @pl.when(j == 0)
    def _():
        qbd = jnp.where(own_head, jnp.concatenate([q_ref[0]] * nh, axis=0), 0.0)
        hi = qbd.astype(BF16)
        q_sc[0:nr] = hi
        q_sc[nr:2 * nr] = (qbd - hi.astype(F32)).astype(BF16)
        m_sc[...] = jnp.zeros_like(m_sc)
        l_sc[...] = jnp.zeros_like(l_sc)
        g_sc[...] = jnp.zeros_like(g_sc)

    def scores(kt):
        s2 = jnp.dot(q_sc[...], kt, preferred_element_type=F32)
        return s2[0:nr] + s2[nr:2 * nr]

    @pl.when(j < last)
    def _():
        m_all, l_all, g_all = m_sc[...], l_sc[...], g_sc[...]
        for i in range(blocks_per_step):
            n = j * blocks_per_step + i
            kt = jnp.concatenate([k_refs[2 * i][0, 0], k_refs[2 * i + 1][0, 0]], axis=1).astype(BF16)
            vt = jnp.concatenate([v_refs[2 * i][0, 0], v_refs[2 * i + 1][0, 0]], axis=1).astype(BF16)
            s = scores(kt)
            gate = jnp.sum(s, axis=1, keepdims=True) * (1.0 / keys)
            mx = jnp.max(s, axis=1, keepdims=True)
            p = jnp.exp(s - mx)
            den = jnp.sum(p, axis=1, keepdims=True)
            pv = lax.dot_general(p.astype(BF16), vt, NT_DIMS, preferred_element_type=F32)
            acc_sc[n] = to_queries(pv)
            m_all = jnp.where(lane == n, mx, m_all)
            l_all = jnp.where(lane == n, den, l_all)
            g_all = jnp.where(lane == n, gate, g_all)
        m_sc[...], l_sc[...], g_sc[...] = m_all, l_all, g_all

    @pl.when(j == last)
    def _():
        cur = jnp.where(lane < nb, g_sc[...], -jnp.inf)
        sel = jnp.zeros((nr, LANES), jnp.bool_)
        for _ in range(MOBA_TOPK):
            top = jnp.max(cur, axis=1, keepdims=True)
            first = jnp.min(jnp.where(cur == top, lane, LANES), axis=1, keepdims=True)
            pick = lane == first
            sel = sel | pick
            cur = jnp.where(pick, -jnp.inf, cur)
        pad = jnp.zeros((nq, hw), F32)
        kn = jnp.concatenate([kn_ref[0], pad], axis=0)
        vn = jnp.concatenate([vn_ref[0], pad], axis=0)
        s2 = lax.dot_general(q_sc[...], kn.astype(BF16), NT_DIMS, preferred_element_type=F32)
        s_own = s2[0:nr] + s2[nr:2 * nr]
        qpos = lax.broadcasted_iota(jnp.int32, (nr, 2 * nq), 0) % nq
        kpos = lax.broadcasted_iota(jnp.int32, (nr, 2 * nq), 1)
        s_own = jnp.where(kpos <= qpos, s_own, NEG)
        m_own = jnp.max(s_own, axis=1, keepdims=True)
        p_own = jnp.exp(s_own - m_own)
        l_own = jnp.sum(p_own, axis=1, keepdims=True)
        acc_own = to_queries(jnp.dot(p_own, vn, preferred_element_type=F32))
        m_all = m_sc[...]
        m_tot = jnp.maximum(jnp.max(jnp.where(sel, m_all, NEG), axis=1, keepdims=True), m_own)
        w = jnp.where(sel, jnp.exp(m_all - m_tot), 0.0)
        w_own = jnp.exp(m_own - m_tot)
        l_tot = jnp.sum(w * l_sc[...], axis=1, keepdims=True) + w_own * l_own

        def widen(col):
            return to_queries(jnp.broadcast_to(col, (nr, hw)))

        num = widen(w_own) * acc_own
        for n in range(nb):
            num = num + widen(w[:, n:n + 1]) * acc_sc[n]
        o_ref[0] = num / widen(l_tot)


def _moba_sample(qa, ka, va, cache_k, cache_v, page_table):
    b, nq, hw = qa.shape
    nh = HEADS_PER_GROUP
    n_pool, page = cache_k.shape[0], cache_k.shape[1]
    pages_per_blk = MOBA_BLOCK // page
    assert pages_per_blk == 2 and page == LANES
    nb = page_table.shape[1] // pages_per_blk
    assert MOBA_TOPK <= nb <= LANES
    ck = cache_k.transpose(0, 2, 3, 1).reshape(1, n_pool, hw, page)
    cv = cache_v.transpose(0, 2, 3, 1).reshape(1, n_pool, hw, page)
    tok = lambda i, j, pt: (i, 0, 0)
    per_step = next(g for g in (4, 2, 1) if nb % g == 0)
    nsteps = nb // per_step

    def page_spec(p):
        return pl.BlockSpec((1, 1, hw, page),
                            lambda i, j, pt: (0, pt[i, jnp.minimum(j, nsteps - 1) * 2 * per_step + p], 0, 0))

    pages = [page_spec(p) for p in range(2 * per_step)]
    grid_spec = pltpu.PrefetchScalarGridSpec(
        num_scalar_prefetch=1,
        grid=(b, nsteps + 1),
        in_specs=[
            pl.BlockSpec((1, nq, hw), tok),
            pl.BlockSpec((1, nq, hw), tok),
            pl.BlockSpec((1, nq, hw), tok),
        ] + pages + pages,
        out_specs=pl.BlockSpec((1, nq, hw), tok),
        scratch_shapes=[
            pltpu.VMEM((2 * nh * nq, hw), BF16),
            pltpu.VMEM((nh * nq, LANES), F32),
            pltpu.VMEM((nh * nq, LANES), F32),
            pltpu.VMEM((nh * nq, LANES), F32),
            pltpu.VMEM((nb, nq, hw), F32),
        ],
    )
    return pl.pallas_call(
        functools.partial(_moba_sample_kernel, blocks_per_step=per_step),
        grid_spec=grid_spec,
        out_shape=jax.ShapeDtypeStruct((b, nq, hw), F32),
        compiler_params=_params("arbitrary", "arbitrary"),
        name="moba_sample",
    )(page_table, qa, ka, va, *([ck] * (2 * per_step)), *([cv] * (2 * per_step)))


def _gla_kernel(q_ref, k_ref, v_ref, la_ref, gs_ref, gn_ref, s0_ref, o_ref, sout_ref, st_sc, *, chunk, mm_dtype):
    c = chunk
    n = q_ref.shape[1] // c
    npair = q_ref.shape[2] // LANES
    hd = HEAD_DIM
    lane = lax.broadcasted_iota(jnp.int32, (c, LANES), 1)
    head0 = lane < hd
    r = lax.broadcasted_iota(jnp.int32, (c, c), 0)
    cc = lax.broadcasted_iota(jnp.int32, (c, c), 1)
    causal = r >= cc
    tri = jnp.where(causal, 1.0, 0.0)
    br = lax.broadcasted_iota(jnp.int32, (LANES, LANES), 0) // hd
    bc = lax.broadcasted_iota(jnp.int32, (LANES, LANES), 1) // hd
    same_head = br == bc
    head_mean = jnp.where(same_head, 1.0 / hd, 0.0)
    gn = gn_ref[...]

    zero = jnp.zeros((hd, hd), F32)
    for p in range(npair):
        st_sc[p] = jnp.concatenate([jnp.concatenate([s0_ref[0, 2 * p], zero], axis=1),
                                    jnp.concatenate([zero, s0_ref[0, 2 * p + 1]], axis=1)], axis=0)

    def split_dot(a, b_mat, stack_axis):
        hi = a.astype(BF16)
        lo = (a - hi.astype(F32)).astype(BF16)
        both = jnp.concatenate([hi, lo], axis=stack_axis)
        if stack_axis == 1:
            r = jnp.dot(b_mat, both, preferred_element_type=F32)
            return r[:, 0:LANES] + r[:, LANES:2 * LANES]
        r = jnp.dot(both, b_mat, preferred_element_type=F32)
        return r[0:c] + r[c:2 * c]

    def chunk(ci, p):
        rows = pl.ds(pl.multiple_of(ci * c, c), c)
        lanes = slice(p * LANES, (p + 1) * LANES)
        la = la_ref[0, rows, lanes]
        q = q_ref[0, rows, lanes]
        k = k_ref[0, rows, lanes]
        v = v_ref[0, rows, lanes]
        if mm_dtype == BF16:
            b = split_dot(la, tri.astype(BF16), 1)
        else:
            b = jnp.dot(tri, la, precision=HIGHEST, preferred_element_type=F32)
        st = st_sc[p]
        b_last = b[c - 1:c, :]
        b_mid = b[c // 2 - 1:c // 2, :]
        qt = q * jnp.exp(b - b_mid)
        kt = k * jnp.exp(b_mid - b)
        qts = jnp.concatenate([jnp.where(head0, qt, 0.0), jnp.where(head0, 0.0, qt)], axis=0)
        a = lax.dot_general(qts.astype(mm_dtype), kt.astype(mm_dtype), NT_DIMS, preferred_element_type=F32)
        a0 = jnp.where(causal, a[0:c], 0.0).astype(mm_dtype)
        a1 = jnp.where(causal, a[c:2 * c], 0.0).astype(mm_dtype)
        vm = v.astype(mm_dtype)
        o = jnp.where(head0, jnp.dot(a0, vm, preferred_element_type=F32),
                      jnp.dot(a1, vm, preferred_element_type=F32))
        qe = q * jnp.exp(b)
        o = o + lax.dot_general(qe.astype(mm_dtype), st.astype(mm_dtype), NT_DIMS, preferred_element_type=F32)
        kd = k * jnp.exp(b_last - b)
        upd = lax.dot_general(vm, kd.astype(mm_dtype), TN_DIMS, preferred_element_type=F32)
        st_sc[p] = st * jnp.exp(b_last) + jnp.where(same_head, upd, 0.0)
        if mm_dtype == BF16:
            ms = split_dot(o * o, head_mean.astype(BF16), 0)
        else:
            ms = jnp.dot(o * o, head_mean, precision=HIGHEST, preferred_element_type=F32)
        o_ref[0, rows, lanes] = o * lax.rsqrt(ms + EPS) * gn * gs_ref[0, rows, lanes]

    unroll = next(u for u in (8, 4, 2, 1) if n % u == 0)

    def body(i, carry):
        for u in range(unroll):
            for p in range(npair):
                chunk(i * unroll + u, p)
        return carry

    lax.fori_loop(0, n // unroll, body, 0)
    for p in range(npair):
        st = st_sc[p]
        sout_ref[0, 2 * p] = st[0:hd, 0:hd]
        sout_ref[0, 2 * p + 1] = st[hd:2 * hd, hd:2 * hd]


def _gla(qg, kg, vg, la, gs, gn_pair, s0_t, chunk, mm_dtype):
    b, l, hw = qg.shape
    pairs = hw // LANES
    npair = pairs if l <= GLA_CHUNK else 1
    seq = pl.BlockSpec((1, l, npair * LANES), lambda i, p: (i, 0, p))
    st_spec = pl.BlockSpec((1, 2 * npair, HEAD_DIM, HEAD_DIM), lambda i, p: (i, p, 0, 0))
    return pl.pallas_call(
        functools.partial(_gla_kernel, chunk=chunk, mm_dtype=mm_dtype),
        grid=(b, pairs // npair),
        in_specs=[seq, seq, seq, seq, seq, pl.BlockSpec((1, LANES), lambda i, p: (0, 0)), st_spec],
        out_specs=[seq, st_spec],
        out_shape=[jax.ShapeDtypeStruct((b, l, hw), F32),
                   jax.ShapeDtypeStruct((b, HEADS_PER_GROUP, HEAD_DIM, HEAD_DIM), F32)],
        scratch_shapes=[pltpu.VMEM((npair, LANES, LANES), F32)],
        compiler_params=_params("arbitrary", "arbitrary"),
        name="gla_chunked",
    )(qg, kg, vg, la, gs, gn_pair, s0_t)


def _mix_ffn_kernel(x_ref, oa_ref, og_ref, mod_ref, g_ref, wo_a_ref, wo_g_ref, wg_ref, wu_ref, wd_ref, o_ref,
                    x1_sc, h_sc, acc_sc):
    j = pl.program_id(2)
    bb, tl, d = x_ref.shape
    tm = bb * tl

    @pl.when(j == 0)
    def _():
        m = mod_ref[...]
        mix = (_mm(oa_ref[...].reshape(tm, -1), wo_a_ref[...])
               + _mm(og_ref[...].reshape(tm, -1), wo_g_ref[...]))
        x1 = x_ref[...] + m[:, 2:3, :] * mix.reshape(bb, tl, d)
        h = _rms(x1, g_ref[...]) * (1.0 + m[:, 4:5, :]) + m[:, 3:4, :]
        x1_sc[...] = x1.reshape(tm, d)
        h_sc[...] = h.reshape(tm, d).astype(h_sc.dtype)
        acc_sc[...] = jnp.zeros_like(acc_sc)

    h = h_sc[...]
    fc = wg_ref.shape[1]
    for c0 in range(0, fc, FFN_COL_CHUNK):
        cols = slice(c0, min(c0 + FFN_COL_CHUNK, fc))
        acc_sc[...] += _mm(_silu(_mm(h, wg_ref[:, cols])) * _mm(h, wu_ref[:, cols]), wd_ref[cols, :])

    @pl.when(j == pl.num_programs(2) - 1)
    def _():
        g2 = mod_ref[...][:, 5:6, :]
        o_ref[...] = x1_sc[...].reshape(bb, tl, d) + g2 * acc_sc[...].reshape(bb, tl, d)


def _mix_ffn(x, oa, og, mod, g, wo_a, wo_g, wg, wu, wd):
    b, l, d = x.shape
    hw = oa.shape[-1]
    f = wg.shape[1]
    bb, tl = _token_tiles(b, l, 512 if l >= 512 else 256)
    tm = bb * tl
    fc = f if wg.dtype == BF16 else 256
    once = dict(pipeline_mode=pl.Buffered(1)) if fc == f else {}
    tok = lambda i, t, j: (i, t, 0)
    const2 = lambda i, t, j: (0, 0)
    return pl.pallas_call(
        _mix_ffn_kernel,
        grid=(b // bb, l // tl, f // fc),
        in_specs=[
            pl.BlockSpec((bb, tl, d), tok),
            pl.BlockSpec((bb, tl, hw), tok),
            pl.BlockSpec((bb, tl, hw), tok),
            pl.BlockSpec((bb, 6, d), lambda i, t, j: (i, 0, 0)),
            pl.BlockSpec((1, d), const2),
            pl.BlockSpec(wo_a.shape, const2),
            pl.BlockSpec(wo_g.shape, const2),
            pl.BlockSpec((d, fc), lambda i, t, j: (0, j), **once),
            pl.BlockSpec((d, fc), lambda i, t, j: (0, j), **once),
            pl.BlockSpec((fc, d), lambda i, t, j: (j, 0), **once),
        ],
        out_specs=pl.BlockSpec((bb, tl, d), tok),
        out_shape=jax.ShapeDtypeStruct((b, l, d), F32),
        scratch_shapes=[pltpu.VMEM((tm, d), F32), pltpu.VMEM((tm, d), wg.dtype), pltpu.VMEM((tm, d), F32)],
        compiler_params=_params("arbitrary", "arbitrary", "arbitrary"),
        name="outproj_ffn",
    )(x, oa, og, mod, g, wo_a, wo_g, wg, wu, wd)


def _cm_in_kernel(x_ref, mod_ref, g_ref, w_ref, lng_ref, lnb_ref, u_ref, v_ref):
    bb, tl, d = x_ref.shape
    tm = bb * tl
    cw = u_ref.shape[-1]
    m = mod_ref[...]
    h = _rms(x_ref[...], g_ref[...]) * (1.0 + m[:, 1:2, :]) + m[:, 0:1, :]
    hb = h.reshape(tm, d).astype(w_ref.dtype)
    u_ref[...] = _gelu_tanh(_mm(hb, w_ref[:, 0:cw])).reshape(bb, tl, cw)
    zv = _gelu_tanh(_mm(hb, w_ref[:, cw:2 * cw]))
    mu = jnp.mean(zv, axis=-1, keepdims=True)
    zc = zv - mu
    var = jnp.mean(zc * zc, axis=-1, keepdims=True)
    v_ref[...] = (zc * lax.rsqrt(var + EPS) * lng_ref[...] + lnb_ref[...]).reshape(bb, tl, cw)


def _cm_in(x, mod, g, w, ln_g, ln_b):
    b, l, d = x.shape
    cw = w.shape[1] // 2
    bb, tl = _token_tiles(b, l, 512 if l >= 512 else 256)
    tok = lambda i, t: (i, t, 0)
    const2 = lambda i, t: (0, 0)
    out_spec = pl.BlockSpec((bb, tl, cw), tok)
    out_sds = jax.ShapeDtypeStruct((b, l, cw), F32)
    return pl.pallas_call(
        _cm_in_kernel,
        grid=(b // bb, l // tl),
        in_specs=[
            pl.BlockSpec((bb, tl, d), tok),
            pl.BlockSpec((bb, 6, d), lambda i, t: (i, 0, 0)),
            pl.BlockSpec((1, d), const2),
            pl.BlockSpec(w.shape, const2),
            pl.BlockSpec((1, cw), const2),
            pl.BlockSpec((1, cw), const2),
        ],
        out_specs=[out_spec, out_spec],
        out_shape=[out_sds, out_sds],
        compiler_params=_params("arbitrary", "arbitrary"),
        name="cm_in_gelu_ln",
    )(x, mod, g, w, ln_g, ln_b)


def _cm_out_kernel(x_ref, u_ref, v_ref, mod_ref, g_ref, ws_ref, bs_ref, wo_ref, wr_ref,
                   x3_ref, h_ref, gates_ref, rank_ref, cnt_ref, us_sc):
    bb, tl, d = x_ref.shape
    tm = bb * tl
    cw = u_ref.shape[-1]
    ngroups, cm, _ = ws_ref.shape
    gd = cw // ngroups
    u = u_ref[...].reshape(tm, cw)
    v = v_ref[...].reshape(tm, cw)
    bs = bs_ref[...]
    mm_dtype = wo_ref.dtype
    r = lax.broadcasted_iota(jnp.int32, (cm, cm), 0)
    c = lax.broadcasted_iota(jnp.int32, (cm, cm), 1)
    for g in range(ngroups):
        wsg = jnp.where(r >= c, ws_ref[g], 0.0).astype(mm_dtype)
        cols = slice(g * gd, (g + 1) * gd)
        for sb in range(tm // cm):
            rows = slice(sb * cm, (sb + 1) * cm)
            s = _mm(wsg, v[rows, cols].astype(mm_dtype)) + bs[rows, g:g + 1]
            us_sc[rows, cols] = (u[rows, cols] * s).astype(mm_dtype)
    y = _mm(us_sc[...], wo_ref[...])
    m = mod_ref[...]
    x3 = x_ref[...] + m[:, 2:3, :] * y.reshape(bb, tl, d)
    x3_ref[...] = x3
    h = (_rms(x3, g_ref[...]) * (1.0 + m[:, 4:5, :]) + m[:, 3:4, :]).reshape(tm, d)
    h_ref[...] = h.astype(BF16)
    wr = wr_ref[...]
    if mm_dtype == BF16:
        h_hi = h.astype(BF16)
        h_lo = (h - h_hi.astype(F32)).astype(BF16)
        wr_hi = wr.astype(BF16)
        wr_lo = (wr - wr_hi.astype(F32)).astype(BF16)
        both = jnp.dot(jnp.concatenate([h_hi, h_lo], axis=0), wr_hi, preferred_element_type=F32)
        logits = both[0:tm] + both[tm:2 * tm] + jnp.dot(h_hi, wr_lo, preferred_element_type=F32)
    else:
        logits = jnp.dot(h, wr, precision=HIGHEST, preferred_element_type=F32)
    ne = logits.shape[1]
    col = lax.broadcasted_iota(jnp.int32, (tm, ne), 1)
    m1 = jnp.max(logits, axis=1, keepdims=True)
    i1 = jnp.min(jnp.where(logits == m1, col, ne), axis=1, keepdims=True)
    rest = jnp.where(col == i1, -jnp.inf, logits)
    m2 = jnp.max(rest, axis=1, keepdims=True)
    i2 = jnp.min(jnp.where(rest == m2, col, ne), axis=1, keepdims=True)
    e2 = jnp.exp(m2 - m1)
    gates = jnp.where(col == i1, 1.0 / (1.0 + e2), 0.0) + jnp.where(col == i2, e2 / (1.0 + e2), 0.0)
    gates_ref[...] = gates
    routed = gates > 0.0
    tr = lax.broadcasted_iota(jnp.int32, (tm, tm), 0)
    tc = lax.broadcasted_iota(jnp.int32, (tm, tm), 1)
    upto = jnp.dot(jnp.where(tr >= tc, 1.0, 0.0).astype(BF16), jnp.where(routed, 1.0, 0.0).astype(BF16),
                   preferred_element_type=F32)
    rank_ref[...] = jnp.where(routed, upto - 1.0, -1.0)
    cnt_ref[0] = upto[tm - 1:tm, :]


def _cm_out(x, u, v, mod, g, ws_mix, bs_rows, wo, wr, cm):
    b, l, d = x.shape
    cw = u.shape[-1]
    ne = wr.shape[1]
    bb, tl = _token_tiles(b, l, 256)
    tm = bb * tl
    tok = lambda i, t: (i, t, 0)
    flat = lambda i, t: (i * (l // tl) + t, 0)
    const2 = lambda i, t: (0, 0)
    return pl.pallas_call(
        _cm_out_kernel,
        grid=(b // bb, l // tl),
        in_specs=[
            pl.BlockSpec((bb, tl, d), tok),
            pl.BlockSpec((bb, tl, cw), tok),
            pl.BlockSpec((bb, tl, cw), tok),
            pl.BlockSpec((bb, 6, d), lambda i, t: (i, 0, 0)),
            pl.BlockSpec((1, d), const2),
            pl.BlockSpec(ws_mix.shape, lambda i, t: (0, 0, 0)),
            pl.BlockSpec((tm, bs_rows.shape[1]), const2),
            pl.BlockSpec(wo.shape, const2),
            pl.BlockSpec(wr.shape, const2),
        ],
        out_specs=[pl.BlockSpec((bb, tl, d), tok), pl.BlockSpec((tm, d), flat), pl.BlockSpec((tm, ne), flat),
                   pl.BlockSpec((tm, ne), flat),
                   pl.BlockSpec((1, 1, ne), lambda i, t: (i * (l // tl) + t, 0, 0))],
        out_shape=[jax.ShapeDtypeStruct((b, l, d), F32), jax.ShapeDtypeStruct((b * l, d), BF16),
                   jax.ShapeDtypeStruct((b * l, ne), F32), jax.ShapeDtypeStruct((b * l, ne), F32),
                   jax.ShapeDtypeStruct((b * l // tm, 1, ne), F32)],
        scratch_shapes=[pltpu.VMEM((tm, cw), wo.dtype)],
        compiler_params=_params("arbitrary", "arbitrary"),
        name="cm_out_router",
    )(x, u, v, mod, g, ws_mix, bs_rows, wo, wr)


MOE_SUB = 256
MOE_ALIGN = 16
MOE_WIN_SMALL = 128
MOE_WIN_FULL = MOE_SUB + MOE_ALIGN
MOE_ROW_CHUNK = 512
MOE_FFN_ROWS = 640
MOE_BLOCK_SUBS = 10
MOE_HIDDEN_CHUNK = 512


def _moe_routing(rank, cnt, nb, nsub):
    ne = rank.shape[1]
    rank = rank.astype(jnp.int32).reshape(nb, nsub, MOE_SUB, ne)
    cnt = cnt.astype(jnp.int32).reshape(nb, nsub, ne)
    off = jnp.cumsum(cnt, axis=1) - cnt
    total = jnp.sum(cnt, axis=1)
    start = (off // MOE_ALIGN) * MOE_ALIGN
    shift = off - start
    win_row = jnp.where(rank >= 0, rank + shift[:, :, None, :], -1)
    win_row = win_row.transpose(0, 3, 1, 2).reshape(nb, ne, nsub * MOE_SUB)
    used = jnp.where(cnt > 0, shift + cnt, 0)
    scalars = jnp.concatenate([start.transpose(0, 2, 1).reshape(-1), used.transpose(0, 2, 1).reshape(-1),
                               total.reshape(-1)]).astype(jnp.int32)
    return win_row, scalars


def _moe_kernel(sc_ref, h_ref, gates_ref, row_ref, wg_ref, wu_ref, wd_ref, o_ref, x_sc, y_sc):
    b = pl.program_id(0)
    e = pl.program_id(1)
    j = pl.program_id(2)
    ne = pl.num_programs(1)
    tb = h_ref.shape[0]
    nsub = tb // MOE_SUB
    cap = x_sc.shape[0]
    n_be = pl.num_programs(0) * ne
    seg = (b * ne + e) * nsub
    total = sc_ref[2 * n_be * nsub + b * ne + e]

    def windows(fn):
        for s in range(nsub):
            start = pl.multiple_of(sc_ref[seg + s], MOE_ALIGN)
            used = sc_ref[n_be * nsub + seg + s]
            tok = slice(s * MOE_SUB, (s + 1) * MOE_SUB)
            win_row = row_ref[0, pl.ds(e, 1), tok]

            def run(rows, start=start, tok=tok, win_row=win_row):
                ridx = lax.broadcasted_iota(jnp.int32, (rows, MOE_SUB), 0)
                fn(pl.ds(start, rows), tok, jnp.where(ridx == win_row, 1.0, 0.0).astype(BF16))

            pl.when((used > 0) & (used <= MOE_WIN_SMALL))(functools.partial(run, MOE_WIN_SMALL))
            pl.when(used > MOE_WIN_SMALL)(functools.partial(run, MOE_WIN_FULL))

    @pl.when((e == 0) & (j == 0))
    def _():
        o_ref[...] = jnp.zeros_like(o_ref)

    @pl.when(j == 0)
    def _():
        def clear(c, carry):
            rows = pl.ds(pl.multiple_of(c * MOE_ROW_CHUNK, MOE_ROW_CHUNK), MOE_ROW_CHUNK)
            x_sc[rows, :] = jnp.zeros((MOE_ROW_CHUNK, x_sc.shape[1]), BF16)
            y_sc[rows, :] = jnp.zeros((MOE_ROW_CHUNK, y_sc.shape[1]), F32)
            return carry

        lax.fori_loop(0, jnp.minimum((total + MOE_WIN_FULL + MOE_ROW_CHUNK - 1) // MOE_ROW_CHUNK,
                                     cap // MOE_ROW_CHUNK), clear, 0)

        def dispatch(rows, tok, onehot):
            picked = jnp.dot(onehot, h_ref[tok, :], preferred_element_type=F32).astype(BF16)
            x_sc[rows, :] = x_sc[rows, :] + picked

        windows(dispatch)

    def ffn(r0, nrows):
        rows = pl.ds(pl.multiple_of(r0, MOE_WIN_SMALL), nrows)
        x = x_sc[rows, :]
        gate = jnp.dot(x, wg_ref[0], preferred_element_type=F32)
        up = jnp.dot(x, wu_ref[0], preferred_element_type=F32)
        act = (_silu(gate) * up).astype(BF16)
        y_sc[rows, :] += jnp.dot(act, wd_ref[0], preferred_element_type=F32)

    nfull = (jnp.maximum(total - 512, 0) + MOE_FFN_ROWS - 1) // MOE_FFN_ROWS
    rem = jnp.maximum(total - nfull * MOE_FFN_ROWS, 0)
    tail = nfull * MOE_FFN_ROWS

    def full_chunk(c, carry):
        ffn(c * MOE_FFN_ROWS, MOE_FFN_ROWS)
        return carry

    lax.fori_loop(0, nfull, full_chunk, 0)
    take_256 = (rem > 128) & (rem <= 384)
    pl.when(rem > 384)(lambda: ffn(tail, 512))
    pl.when(take_256)(lambda: ffn(tail, 256))
    pl.when((rem > 0) & (rem <= 128))(lambda: ffn(tail, 128))
    pl.when((rem > 256) & (rem <= 384))(lambda: ffn(tail + 256, 128))

    @pl.when(j == pl.num_programs(2) - 1)
    def _():
        gates = gates_ref[...]
        col = lax.broadcasted_iota(jnp.int32, gates.shape, 1)
        ge = jnp.sum(jnp.where(col == e, gates, 0.0), axis=1, keepdims=True)

        def combine(rows, tok, onehot):
            back = lax.dot_general(onehot, y_sc[rows, :].astype(BF16), TN_DIMS, preferred_element_type=F32)
            o_ref[tok, :] = (o_ref[tok, :].astype(F32) + ge[tok] * back).astype(o_ref.dtype)

        windows(combine)


def _moe(h, gates, rank, cnt, wg, wu, wd, tb):
    t, d = h.shape
    ne, _, f = wg.shape
    fc = MOE_HIDDEN_CHUNK
    assert t % tb == 0 and tb % MOE_SUB == 0 and f % fc == 0
    nb, nsub = t // tb, tb // MOE_SUB
    win_row, scalars = _moe_routing(rank, cnt, nb, nsub)
    cap = -(-(tb + MOE_WIN_FULL) // MOE_ROW_CHUNK) * MOE_ROW_CHUNK
    grid_spec = pltpu.PrefetchScalarGridSpec(
        num_scalar_prefetch=1,
        grid=(nb, ne, f // fc),
        in_specs=[
            pl.BlockSpec((tb, d), lambda i, e, j, sc: (i, 0)),
            pl.BlockSpec((tb, ne), lambda i, e, j, sc: (i, 0)),
            pl.BlockSpec((1, ne, tb), lambda i, e, j, sc: (i, 0, 0)),
            pl.BlockSpec((1, d, fc), lambda i, e, j, sc: (e, 0, j)),
            pl.BlockSpec((1, d, fc), lambda i, e, j, sc: (e, 0, j)),
            pl.BlockSpec((1, fc, d), lambda i, e, j, sc: (e, j, 0)),
        ],
        out_specs=pl.BlockSpec((tb, d), lambda i, e, j, sc: (i, 0)),
        scratch_shapes=[pltpu.VMEM((cap, d), BF16), pltpu.VMEM((cap, d), F32)],
    )
    return pl.pallas_call(
        _moe_kernel,
        grid_spec=grid_spec,
        out_shape=jax.ShapeDtypeStruct((t, d), BF16),
        compiler_params=_params("arbitrary", "arbitrary", "arbitrary"),
        name="moe_experts",
    )(scalars, h, gates, win_row, wg, wu, wd)


def _final_kernel(x_ref, moe_ref, mod_ref, g_ref, o_ref):
    bb, tl, d = x_ref.shape
    g2 = mod_ref[...][:, 5:6, :]
    x4 = x_ref[...] + g2 * moe_ref[...].reshape(bb, tl, d)
    o_ref[...] = _rms(x4, g_ref[...])


def _final(x, moe, row0, mod, g):
    b, l, d = x.shape
    bb, tl = _token_tiles(b, l, 512 if l >= 512 else 256)
    tm = bb * tl
    assert row0 % tm == 0
    tok = lambda i, t: (i, t, 0)
    return pl.pallas_call(
        _final_kernel,
        grid=(b // bb, l // tl),
        in_specs=[
            pl.BlockSpec((bb, tl, d), tok),
            pl.BlockSpec((tm, d), lambda i, t: (row0 // tm + i * (l // tl) + t, 0)),
            pl.BlockSpec((bb, 6, d), lambda i, t: (i, 0, 0)),
            pl.BlockSpec((1, d), lambda i, t: (0, 0)),
        ],
        out_specs=pl.BlockSpec((bb, tl, d), tok),
        out_shape=jax.ShapeDtypeStruct((b, l, d), F32),
        compiler_params=_params("arbitrary", "arbitrary"),
        name="final_norm",
    )(x, moe, mod, g)


def _layer0(x, mod, pos0, past, w):
    b, l, d = x.shape
    hw = HEAD_WIDTH
    outs = _in_proj(x, mod, w["norm_mix_g0"], w["w_in_main"], w["w_in_lr"], w["wa2_hi"], w["wa2_lo"], w["ba2"], pos0)
    qa, ka, va, qg, kg, vg, gs, la = outs
    if past is None:
        oa = _moba_prompt(qa, ka, va)
        s0_t = jnp.zeros((b, HEADS_PER_GROUP, HEAD_DIM, HEAD_DIM), F32)
    else:
        cache_k, cache_v, state, page_table = past
        oa = _moba_sample(qa, ka, va, cache_k, cache_v, page_table)
        s0_t = jnp.swapaxes(state, -1, -2)
    chunk = GLA_CHUNK if l % GLA_CHUNK == 0 else l
    mm_dtype = BF16 if chunk % 16 == 0 else F32
    og, s_t = _gla(qg, kg, vg, la, gs, w["gla_norm_pair"], s0_t, chunk, mm_dtype)
    x2 = _mix_ffn(x, oa, og, mod, w["norm_ffn_g0"], w["wo_a"], w["wo_g"], w["ffn_wg"], w["ffn_wu"], w["ffn_wd"])
    k_rows = ka.reshape(b, l, HEADS_PER_GROUP, HEAD_DIM)
    v_rows = va.reshape(b, l, HEADS_PER_GROUP, HEAD_DIM)
    return x2, k_rows, v_rows, jnp.swapaxes(s_t, -1, -2)


def _layer1_front(x, mod, w):
    b, l, d = x.shape
    u, v = _cm_in(x, mod, w["norm_mix_g1"], w["cm_w_in"], w["cm_ln_g"], w["cm_ln_b"])
    bb, tl = _token_tiles(b, l, 256)
    tm = bb * tl
    if l >= CM_CHUNK:
        cm = CM_CHUNK
        ws_mix = w["cm_w_s"]
        bs_rows = jnp.tile(w["cm_b_s"].T, (tm // CM_CHUNK, 1))
    else:
        cm = tm
        pos = jnp.arange(tm, dtype=jnp.int32)
        same_seq = (pos[:, None] // l == pos[None, :] // l).astype(F32)
        place = (pos[:, None] % l == jnp.arange(l, dtype=jnp.int32)[None, :]).astype(F32)
        ws_mix = jnp.einsum("ra,gab,cb->grc", place, w["cm_w_s"][:, :l, :l], place,
                            precision=HIGHEST) * same_seq[None]
        bs_rows = jnp.tile(w["cm_b_s"][:, :l].T, (tm // l, 1))
    assert tm == MOE_SUB
    x3, h, gates, rank, cnt = _cm_out(x, u, v, mod, w["norm_ffn_g1"], ws_mix, bs_rows, w["cm_w_out"],
                                      w["moe_w_router"], cm)
    start = ((l - 1) // CM_CHUNK) * CM_CHUNK
    return x3, (h, gates, rank, cnt), v[:, start:]


def kernel(x_prompt, x_sample, cache_k, cache_v, state_gla, page_table, c_prompt, c_sample, ada_w, ada_b, norm_mix_g, norm_ffn_g, ab_w_in, gla_w_a2, gla_b_a2, gla_norm_g, ab_w_out, ffn_w_gate, ffn_w_up, ffn_w_down, cm_w_in, cm_ln_g, cm_ln_b, cm_w_s, cm_b_s, cm_w_out, moe_w_router, moe_w_gate, moe_w_up, moe_w_down, final_norm_g):
    bp, lp, d = x_prompt.shape
    bs, ls, _ = x_sample.shape
    hw = HEAD_WIDTH
    n_main = 7 * hw
    past_len = page_table.shape[1] * cache_k.shape[2]

    wa2 = jnp.pad(gla_w_a2[0], ((0, LANES - GLA_LOWRANK), (0, 0)))
    wa2_hi = wa2.astype(BF16)
    shared = {
        "norm_mix_g0": norm_mix_g[0][None], "norm_mix_g1": norm_mix_g[1][None],
        "norm_ffn_g0": norm_ffn_g[0][None], "norm_ffn_g1": norm_ffn_g[1][None],
        "wa2_hi": wa2_hi, "wa2_lo": (wa2 - wa2_hi.astype(F32)).astype(BF16),
        "ba2": gla_b_a2[0][None],
        "gla_norm_pair": jnp.tile(gla_norm_g[0], LANES // HEAD_DIM)[None],
        "cm_ln_g": cm_ln_g[0][None], "cm_ln_b": cm_ln_b[0][None],
        "cm_w_s": cm_w_s[0], "cm_b_s": cm_b_s[0],
        "moe_w_router": moe_w_router[0],
    }
    mats = {
        "w_in_main": ab_w_in[0],
        "w_in_lr": jnp.pad(ab_w_in[0][:, n_main:], ((0, 0), (0, LANES - GLA_LOWRANK))),
        "wo_a": ab_w_out[0][:hw], "wo_g": ab_w_out[0][hw:],
        "ffn_wg": ffn_w_gate[0], "ffn_wu": ffn_w_up[0], "ffn_wd": ffn_w_down[0],
        "cm_w_in": cm_w_in[0], "cm_w_out": cm_w_out[0],
    }
    w = dict(shared, **{k: v.astype(BF16) for k, v in mats.items()})
    w_sample = dict(shared, **mats)

    rows = bp + bs
    rows_pad = -(-rows // 16) * 16
    c_all = jnp.pad(jnp.concatenate([c_prompt, c_sample], axis=0), ((0, rows_pad - rows), (0, 0)))
    mod_all = _modulation(c_all, ada_w, ada_b)
    mod_p = [mod_all[l, :bp].reshape(bp, 6, d) for l in range(2)]
    mod_s = [mod_all[l, bp:rows].reshape(bs, 6, d) for l in range(2)]

    x2_p, k_p, v_p, gla_p = _layer0(x_prompt, mod_p[0], 0, None, w)
    x2_s, k_s, v_s, gla_s = _layer0(x_sample, mod_s[0], past_len,
                                    (cache_k[0], cache_v[0], state_gla[0], page_table), w_sample)

    x3_p, route_p, cmv_p = _layer1_front(x2_p, mod_p[1], w)
    x3_s, route_s, cmv_s = _layer1_front(x2_s, mod_s[1], w_sample)

    t_all = bp * lp + bs * ls
    subs = -(-t_all // MOE_SUB)
    n_blocks = -(-subs // MOE_BLOCK_SUBS)
    tb = -(-subs // n_blocks) * MOE_SUB
    pad = n_blocks * tb - t_all
    ne = moe_w_gate.shape[1]
    pads = (jnp.zeros((pad, d), BF16), jnp.zeros((pad, ne), F32), jnp.full((pad, ne), -1.0, F32),
            jnp.zeros((pad // MOE_SUB, 1, ne), F32))
    h_all, gates_all, rank_all, cnt_all = (jnp.concatenate(parts, axis=0) for parts in zip(route_p, route_s, pads))
    moe = _moe(h_all, gates_all, rank_all, cnt_all, moe_w_gate[0].astype(BF16), moe_w_up[0].astype(BF16),
               moe_w_down[0].astype(BF16), tb)

    fg = final_norm_g[None]
    y_p = _final(x3_p, moe, 0, mod_p[1], fg)
    y_s = _final(x3_s, moe, bp * lp, mod_s[1], fg)

    return (y_p, y_s, k_p[None], v_p[None], k_s[None], v_s[None], gla_p[None], gla_s[None],
            cmv_p[None], cmv_s[None])
```

```python
import functools

import jax
import jax.numpy as jnp
from jax import lax
from jax.experimental import pallas as pl
from jax.experimental.pallas import tpu as pltpu

F32 = jnp.float32
BF16 = jnp.bfloat16
HIGHEST = lax.Precision.HIGHEST

HEAD_DIM = 64
HEADS_PER_GROUP = 8
HEAD_WIDTH = HEADS_PER_GROUP * HEAD_DIM
MOBA_BLOCK = 256
MOBA_TOPK = 3
ROPE_THETA = 10000.0
GLA_LOWRANK = 16
GLA_GATE_NORM = 16.0
GLA_CHUNK = 64
CM_CHUNK = 128
CM_GROUPS = 8
MOE_TOP_K = 2
EPS = 1e-6
NEG = -1e30

LANES = 128
FFN_COL_CHUNK = 1024
VMEM_LIMIT_BYTES = 56 * 1024 * 1024

NT_DIMS = (((1,), (1,)), ((), ()))
TN_DIMS = (((0,), (0,)), ((), ()))


def _params(*sem):
    return pltpu.CompilerParams(dimension_semantics=sem, vmem_limit_bytes=VMEM_LIMIT_BYTES)


def _rms(x, g):
    return x * lax.rsqrt(jnp.mean(x * x, axis=-1, keepdims=True) + EPS) * g


def _silu(x):
    return x * jax.nn.sigmoid(x)


def _mm(a, w):
    if w.dtype == F32:
        return jnp.dot(a.astype(F32), w, precision=HIGHEST, preferred_element_type=F32)
    return jnp.dot(a.astype(BF16), w, preferred_element_type=F32)


def _gelu_tanh(x):
    c = 0.7978845608028654
    return x * (0.5 + 0.5 * jnp.tanh(x * (c + (c * 0.044715) * (x * x))))


def _token_tiles(b, l, target):
    if l >= target:
        assert l % target == 0
        return 1, target
    bb = min(b, target // l)
    assert target % l == 0 and b % bb == 0
    return bb, l


def _mod_kernel(c_ref, w_ref, b_ref, o_ref):
    s = _silu(c_ref[...])
    r = s.shape[0]
    s_hi = s.astype(BF16)
    s_lo = (s - s_hi.astype(F32)).astype(BF16)
    w = w_ref[0]
    w_hi = w.astype(BF16)
    w_lo = (w - w_hi.astype(F32)).astype(BF16)
    both = jnp.dot(jnp.concatenate([s_hi, s_lo], axis=0), w_hi, preferred_element_type=F32)
    o_ref[0] = both[0:r] + both[r:2 * r] + jnp.dot(s_hi, w_lo, preferred_element_type=F32) + b_ref[0]


def _modulation(c_all, ada_w, ada_b):
    r, d = c_all.shape
    depth, _, n = ada_w.shape
    tn = 1536
    return pl.pallas_call(
        _mod_kernel,
        grid=(depth, n // tn),
        in_specs=[
            pl.BlockSpec((r, d), lambda l, j: (0, 0)),
            pl.BlockSpec((1, d, tn), lambda l, j: (l, 0, j)),
            pl.BlockSpec((1, 1, tn), lambda l, j: (l, 0, j)),
        ],
        out_specs=pl.BlockSpec((1, r, tn), lambda l, j: (l, 0, j)),
        out_shape=jax.ShapeDtypeStruct((depth, r, n), F32),
        compiler_params=_params("arbitrary", "arbitrary"),
        name="adaln_modulation",
    )(c_all, ada_w, ada_b.reshape(depth, 1, n))


def _in_proj_kernel(x_ref, mod_ref, g_ref, w_ref, wlr_ref, wa2h_ref, wa2l_ref, ba2_ref, cos_ref, sin_ref,
                    qa_ref, ka_ref, va_ref, qg_ref, kg_ref, vg_ref, gs_ref, la_ref):
    bb, tl, d = x_ref.shape
    tm = bb * tl
    hw = HEAD_WIDTH
    m = mod_ref[...]
    h = _rms(x_ref[...], g_ref[...]) * (1.0 + m[:, 1:2, :]) + m[:, 0:1, :]
    hb = h.reshape(tm, d).astype(w_ref.dtype)

    def proj(c):
        return _mm(hb, w_ref[:, c * hw:(c + 1) * hw])

    cos = jnp.concatenate([cos_ref[...]] * (hw // LANES), axis=-1)[None]
    sin = jnp.concatenate([sin_ref[...]] * (hw // LANES), axis=-1)[None]
    lane = lax.broadcasted_iota(jnp.int32, (tm, hw), 1)
    first_half = (lane % HEAD_DIM) < (HEAD_DIM // 2)

    def rope(z):
        partner = jnp.where(first_half, pltpu.roll(z, hw - HEAD_DIM // 2, 1), pltpu.roll(z, HEAD_DIM // 2, 1))
        return z.reshape(bb, tl, hw) * cos + partner.reshape(bb, tl, hw) * sin

    scale = HEAD_DIM ** -0.5
    qa_ref[...] = rope(proj(0)) * scale
    ka_ref[...] = rope(proj(1))
    va_ref[...] = proj(2).reshape(bb, tl, hw)
    qg_ref[...] = (proj(3) * scale).reshape(bb, tl, hw)
    kg_ref[...] = proj(4).reshape(bb, tl, hw)
    vg_ref[...] = proj(5).reshape(bb, tl, hw)
    gs_ref[...] = _silu(proj(6)).reshape(bb, tl, hw)
    zlr = _mm(hb, wlr_ref[...])
    hi = zlr.astype(BF16)
    lo = (zlr - hi.astype(F32)).astype(BF16)
    pre = (jnp.dot(hi, wa2h_ref[...], preferred_element_type=F32)
           + jnp.dot(lo, wa2h_ref[...], preferred_element_type=F32)
           + jnp.dot(hi, wa2l_ref[...], preferred_element_type=F32)) + ba2_ref[...]
    log_sig = jnp.minimum(pre, 0.0) - jnp.log(1.0 + jnp.exp(-jnp.abs(pre)))
    la_ref[...] = (log_sig * (1.0 / GLA_GATE_NORM)).reshape(bb, tl, hw)


def _rope_tables(pos0, l):
    half = HEAD_DIM // 2
    inv = 1.0 / (ROPE_THETA ** (jnp.arange(half, dtype=F32) / half))
    ang = (pos0 + jnp.arange(l, dtype=jnp.int32)).astype(F32)[:, None] * inv[None, :]
    cos, sin = jnp.cos(ang), jnp.sin(ang)
    reps = LANES // HEAD_DIM
    cos_t = jnp.tile(jnp.concatenate([cos, cos], axis=-1), (1, reps))
    sin_t = jnp.tile(jnp.concatenate([-sin, sin], axis=-1), (1, reps))
    return cos_t, sin_t


def _in_proj(x, mod, g, w_main, w_lr, wa2_hi, wa2_lo, ba2, pos0):
    b, l, d = x.shape
    bb, tl = _token_tiles(b, l, 512 if l >= 512 else 256)
    hw = HEAD_WIDTH
    cos_t, sin_t = _rope_tables(pos0, l)
    tok = lambda i, j: (i, j, 0)
    const2 = lambda i, j: (0, 0)
    out_spec = pl.BlockSpec((bb, tl, hw), tok)
    out_sds = jax.ShapeDtypeStruct((b, l, hw), F32)
    return pl.pallas_call(
        _in_proj_kernel,
        grid=(b // bb, l // tl),
        in_specs=[
            pl.BlockSpec((bb, tl, d), tok),
            pl.BlockSpec((bb, 6, d), lambda i, j: (i, 0, 0)),
            pl.BlockSpec((1, d), const2),
            pl.BlockSpec((d, 7 * hw), const2),
            pl.BlockSpec(w_lr.shape, const2),
            pl.BlockSpec(wa2_hi.shape, const2),
            pl.BlockSpec(wa2_lo.shape, const2),
            pl.BlockSpec((1, hw), const2),
            pl.BlockSpec((tl, LANES), lambda i, j: (j, 0)),
            pl.BlockSpec((tl, LANES), lambda i, j: (j, 0)),
        ],
        out_specs=[out_spec] * 8,
        out_shape=[out_sds] * 8,
        compiler_params=_params("arbitrary", "arbitrary"),
        name="in_proj_rope",
    )(x, mod, g, w_main, w_lr, wa2_hi, wa2_lo, ba2, cos_t, sin_t)


def _moba_prompt_kernel(q_ref, k_ref, v_ref, o_ref, kb_sc, vb_sc, km_sc, qs_sc, s_sc, p_sc, m_sc, acc_sc):
    ti = pl.program_id(2)
    lk = k_ref.shape[1]
    nblk = lk // MOBA_BLOCK
    nblk8 = -(-nblk // 8) * 8
    tq = q_ref.shape[1]
    nsub = tq // MOBA_BLOCK

    @pl.when(ti == 0)
    def _():
        k = k_ref[0]
        v = v_ref[0]
        row = lax.broadcasted_iota(jnp.int32, (lk, LANES), 0)
        col = lax.broadcasted_iota(jnp.int32, (lk, LANES), 1)
        kb_sc[:, 0:LANES] = k.astype(BF16)
        kb_sc[:, LANES:2 * LANES] = jnp.where(row // MOBA_BLOCK == col, 1.0, 0.0).astype(BF16)
        vb_sc[0] = jnp.where(col < HEAD_DIM, v, 1.0).astype(BF16)
        vb_sc[1] = jnp.where(col >= HEAD_DIM, v, 1.0).astype(BF16)
        km_sc[...] = jnp.zeros_like(km_sc)
        km_sc[0:nblk, :] = jnp.mean(k.reshape(nblk, MOBA_BLOCK, LANES), axis=1)

    q = q_ref[0]
    lane = lax.broadcasted_iota(jnp.int32, (tq, LANES), 1)
    blk = lax.broadcasted_iota(jnp.int32, (nblk8, tq), 0)
    own = ti * nsub + lax.broadcasted_iota(jnp.int32, (nblk8, tq), 1) // MOBA_BLOCK
    past_blk = blk < own
    km = km_sc[...]
    for h in range(2):
        head_lanes = (lane < HEAD_DIM) if h == 0 else (lane >= HEAD_DIM)
        qh = jnp.where(head_lanes, q, 0.0)
        gate = lax.dot_general(km, qh, NT_DIMS, precision=HIGHEST, preferred_element_type=F32)[0:nblk8]
        gate = jnp.where(past_blk, gate, -jnp.inf)
        cnt = jnp.zeros((nblk8, tq), F32)
        for mrow in range(nblk):
            gm = gate[mrow:mrow + 1, :]
            beats = (gm > gate) | ((gm == gate) & (mrow < blk))
            cnt = cnt + jnp.where(beats, 1.0, 0.0)
        sel = (past_blk & (cnt < float(MOBA_TOPK))) | (blk == own)
        bias_t = jnp.concatenate(
            [jnp.where(sel, 0.0, NEG), jnp.full((LANES - nblk8, tq), NEG, F32)], axis=0)
        qs_sc[h, :, 0:LANES] = qh.astype(BF16)
        qs_sc[h, :, LANES:2 * LANES] = bias_t.T.astype(BF16)

    m_sc[...] = jnp.full_like(m_sc, NEG)
    acc_sc[...] = jnp.zeros_like(acc_sc)
    rc = 128
    rowi = lax.broadcasted_iota(jnp.int32, (rc, MOBA_BLOCK), 0)
    coli = lax.broadcasted_iota(jnp.int32, (rc, MOBA_BLOCK), 1)

    def scores(n, slot):
        start = pl.multiple_of(n * MOBA_BLOCK, MOBA_BLOCK)
        kblk = kb_sc[pl.ds(start, MOBA_BLOCK), :]
        for h in range(2):
            s_sc[slot, h] = lax.dot_general(qs_sc[h], kblk, NT_DIMS, preferred_element_type=F32)

    def consume(n, slot, own_sub):
        start = pl.multiple_of(n * MOBA_BLOCK, MOBA_BLOCK)
        for h in range(2):
            for r0 in range(0, tq, rc):
                rows = slice(r0, r0 + rc)
                s = s_sc[slot, h, rows, :]
                if own_sub is not None and r0 // MOBA_BLOCK == own_sub:
                    s = jnp.where(rowi + (r0 - own_sub * MOBA_BLOCK) < coli, NEG, s)
                m_old = m_sc[h, rows, :]
                row_max = jnp.max(jnp.maximum(s[:, 0:LANES], s[:, LANES:2 * LANES]), axis=1, keepdims=True)
                m_new = jnp.maximum(m_old, row_max)
                p_sc[h, rows, :] = jnp.exp(s - jnp.concatenate([m_new, m_new], axis=1)).astype(BF16)
                acc_sc[h, rows, :] = jnp.exp(m_old - m_new) * acc_sc[h, rows, :]
                m_sc[h, rows, :] = m_new
            acc_sc[h] += jnp.dot(p_sc[h], vb_sc[h, pl.ds(start, MOBA_BLOCK), :], preferred_element_type=F32)

    def past_pair(i, carry):
        scores(2 * i + 1, 1)
        consume(2 * i, 0, None)
        scores(2 * i + 2, 0)
        consume(2 * i + 1, 1, None)
        return carry

    scores(0, 0)
    lax.fori_loop(0, ti, past_pair, 0)
    scores(2 * ti + 1, 1)
    consume(2 * ti, 0, 0)
    consume(2 * ti + 1, 1, 1)
    a0 = acc_sc[0]
    a1 = acc_sc[1]
    o_ref[0] = jnp.where(lane < HEAD_DIM, a0 / pltpu.roll(a0, HEAD_DIM, 1), a1 / pltpu.roll(a1, HEAD_DIM, 1))


def _moba_prompt(qa, ka, va):
    b, l, hw = qa.shape
    tq = 2 * MOBA_BLOCK
    assert l % tq == 0
    pairs = hw // LANES
    return pl.pallas_call(
        _moba_prompt_kernel,
        grid=(b, pairs, l // tq),
        in_specs=[
            pl.BlockSpec((1, tq, LANES), lambda i, p, t: (i, t, p)),
            pl.BlockSpec((1, l, LANES), lambda i, p, t: (i, 0, p)),
            pl.BlockSpec((1, l, LANES), lambda i, p, t: (i, 0, p)),
        ],
        out_specs=pl.BlockSpec((1, tq, LANES), lambda i, p, t: (i, t, p)),
        out_shape=jax.ShapeDtypeStruct((b, l, hw), F32),
        scratch_shapes=[
            pltpu.VMEM((l, 2 * LANES), BF16),
            pltpu.VMEM((2, l, LANES), BF16),
            pltpu.VMEM((LANES, LANES), F32),
            pltpu.VMEM((2, tq, 2 * LANES), BF16),
            pltpu.VMEM((2, 2, tq, MOBA_BLOCK), F32),
            pltpu.VMEM((2, tq, MOBA_BLOCK), BF16),
            pltpu.VMEM((2, tq, LANES), F32),
            pltpu.VMEM((2, tq, LANES), F32),
        ],
        compiler_params=_params("arbitrary", "arbitrary", "arbitrary"),
        name="moba_prompt",
    )(qa, ka, va)


def _moba_sample_kernel(pt_ref, q_ref, kn_ref, vn_ref, *refs, blocks_per_step):
    del pt_ref
    npages = 2 * blocks_per_step
    k_refs, v_refs = refs[0:npages], refs[npages:2 * npages]
    o_ref, q_sc, m_sc, l_sc, g_sc, acc_sc = refs[2 * npages:]
    j = pl.program_id(1)
    last = pl.num_programs(1) - 1
    nb = acc_sc.shape[0]
    nq = q_ref.shape[1]
    nh = HEADS_PER_GROUP
    nr = nh * nq
    hw = HEAD_WIDTH
    keys = 2 * k_refs[0].shape[3]
    row_w = lax.broadcasted_iota(jnp.int32, (nr, hw), 0)
    lane_w = lax.broadcasted_iota(jnp.int32, (nr, hw), 1)
    own_head = (row_w // nq) == (lane_w // HEAD_DIM)
    lane = lax.broadcasted_iota(jnp.int32, (nr, LANES), 1)

    def to_queries(t):
        return jnp.sum(jnp.where(own_head, t, 0.0).reshape(nh, nq, hw), axis=0)

    @pl.when(j == 0)
    def _():
        qbd = jnp.where(own_head, jnp.concatenate([q_ref[0]] * nh, axis=0), 0.0)
        hi = qbd.astype(BF16)
        q_sc[0:nr] = hi
        q_sc[nr:2 * nr] = (qbd - hi.astype(F32)).astype(BF16)
        m_sc[...] = jnp.zeros_like(m_sc)
        l_sc[...] = jnp.zeros_like(l_sc)
        g_sc[...] = jnp.zeros_like(g_sc)

    def scores(kt):
        s2 = jnp.dot(q_sc[...], kt, preferred_element_type=F32)
        return s2[0:nr] + s2[nr:2 * nr]

    @pl.when(j < last)
    def _():
        m_all, l_all, g_all = m_sc[...], l_sc[...], g_sc[...]
        for i in range(blocks_per_step):
            n = j * blocks_per_step + i
            kt = jnp.concatenate([k_refs[2 * i][0, 0], k_refs[2 * i + 1][0, 0]], axis=1).astype(BF16)
            vt = jnp.concatenate([v_refs[2 * i][0, 0], v_refs[2 * i + 1][0, 0]], axis=1).astype(BF16)
            s = scores(kt)
            gate = jnp.sum(s, axis=1, keepdims=True) * (1.0 / keys)
            mx = jnp.max(s, axis=1, keepdims=True)
            p = jnp.exp(s - mx)
            den = jnp.sum(p, axis=1, keepdims=True)
            pv = lax.dot_general(p.astype(BF16), vt, NT_DIMS, preferred_element_type=F32)
            acc_sc[n] = to_queries(pv)
            m_all = jnp.where(lane == n, mx, m_all)
            l_all = jnp.where(lane == n, den, l_all)
            g_all = jnp.where(lane == n, gate, g_all)
        m_sc[...], l_sc[...], g_sc[...] = m_all, l_all, g_all

    @pl.when(j == last)
    def _():
        cur = jnp.where(lane < nb, g_sc[...], -jnp.inf)
        sel = jnp.zeros((nr, LANES), jnp.bool_)
        for _ in range(MOBA_TOPK):
            top = jnp.max(cur, axis=1, keepdims=True)
            first = jnp.min(jnp.where(cur == top, lane, LANES), axis=1, keepdims=True)
            pick = lane == first
            sel = sel | pick
            cur = jnp.where(pick, -jnp.inf, cur)
        pad = jnp.zeros((nq, hw), F32)
        kn = jnp.concatenate([kn_ref[0], pad], axis=0)
        vn = jnp.concatenate([vn_ref[0], pad], axis=0)
        s2 = lax.dot_general(q_sc[...], kn.astype(BF16), NT_DIMS, preferred_element_type=F32)
        s_own = s2[0:nr] + s2[nr:2 * nr]
        qpos = lax.broadcasted_iota(jnp.int32, (nr, 2 * nq), 0) % nq
        kpos = lax.broadcasted_iota(jnp.int32, (nr, 2 * nq), 1)
        s_own = jnp.where(kpos <= qpos, s_own, NEG)
        m_own = jnp.max(s_own, axis=1, keepdims=True)
        p_own = jnp.exp(s_own - m_own)
        l_own = jnp.sum(p_own, axis=1, keepdims=True)
        acc_own = to_queries(jnp.dot(p_own, vn, preferred_element_type=F32))
        m_all = m_sc[...]
        m_tot = jnp.maximum(jnp.max(jnp.where(sel, m_all, NEG), axis=1, keepdims=True), m_own)
        w = jnp.where(sel, jnp.exp(m_all - m_tot), 0.0)
        w_own = jnp.exp(m_own - m_tot)
        l_tot = jnp.sum(w * l_sc[...], axis=1, keepdims=True) + w_own * l_own

        def widen(col):
            return to_queries(jnp.broadcast_to(col, (nr, hw)))

        num = widen(w_own) * acc_own
        for n in range(nb):
            num = num + widen(w[:, n:n + 1]) * acc_sc[n]
        o_ref[0] = num / widen(l_tot)


def _moba_sample(qa, ka, va, cache_k, cache_v, page_table):
    b, nq, hw = qa.shape
    nh = HEADS_PER_GROUP
    n_pool, page = cache_k.shape[0], cache_k.shape[1]
    pages_per_blk = MOBA_BLOCK // page
    assert pages_per_blk == 2 and page == LANES
    nb = page_table.shape[1] // pages_per_blk
    assert MOBA_TOPK <= nb <= LANES
    ck = cache_k.transpose(0, 2, 3, 1).reshape(1, n_pool, hw, page)
    cv = cache_v.transpose(0, 2, 3, 1).reshape(1, n_pool, hw, page)
    tok = lambda i, j, pt: (i, 0, 0)
    per_step = next(g for g in (4, 2, 1) if nb % g == 0)
    nsteps = nb // per_step

    def page_spec(p):
        return pl.BlockSpec((1, 1, hw, page),
                            lambda i, j, pt: (0, pt[i, jnp.minimum(j, nsteps - 1) * 2 * per_step + p], 0, 0))

    pages = [page_spec(p) for p in range(2 * per_step)]
    grid_spec = pltpu.PrefetchScalarGridSpec(
        num_scalar_prefetch=1,
        grid=(b, nsteps + 1),
        in_specs=[
            pl.BlockSpec((1, nq, hw), tok),
            pl.BlockSpec((1, nq, hw), tok),
            pl.BlockSpec((1, nq, hw), tok),
        ] + pages + pages,
        out_specs=pl.BlockSpec((1, nq, hw), tok),
        scratch_shapes=[
            pltpu.VMEM((2 * nh * nq, hw), BF16),
            pltpu.VMEM((nh * nq, LANES), F32),
            pltpu.VMEM((nh * nq, LANES), F32),
            pltpu.VMEM((nh * nq, LANES), F32),
            pltpu.VMEM((nb, nq, hw), F32),
        ],
    )
    return pl.pallas_call(
        functools.partial(_moba_sample_kernel, blocks_per_step=per_step),
        grid_spec=grid_spec,
        out_shape=jax.ShapeDtypeStruct((b, nq, hw), F32),
        compiler_params=_params("arbitrary", "arbitrary"),
        name="moba_sample",
    )(page_table, qa, ka, va, *([ck] * (2 * per_step)), *([cv] * (2 * per_step)))


def _gla_kernel(q_ref, k_ref, v_ref, la_ref, gs_ref, gn_ref, s0_ref, o_ref, sout_ref, st_sc, *, chunk, mm_dtype):
    c = chunk
    n = q_ref.shape[1] // c
    npair = q_ref.shape[2] // LANES
    hd = HEAD_DIM
    lane = lax.broadcasted_iota(jnp.int32, (c, LANES), 1)
    head0 = lane < hd
    r = lax.broadcasted_iota(jnp.int32, (c, c), 0)
    cc = lax.broadcasted_iota(jnp.int32, (c, c), 1)
    causal = r >= cc
    tri = jnp.where(causal, 1.0, 0.0)
    br = lax.broadcasted_iota(jnp.int32, (LANES, LANES), 0) // hd
    bc = lax.broadcasted_iota(jnp.int32, (LANES, LANES), 1) // hd
    same_head = br == bc
    head_mean = jnp.where(same_head, 1.0 / hd, 0.0)
    gn = gn_ref[...]

    zero = jnp.zeros((hd, hd), F32)
    for p in range(npair):
        st_sc[p] = jnp.concatenate([jnp.concatenate([s0_ref[0, 2 * p], zero], axis=1),
                                    jnp.concatenate([zero, s0_ref[0, 2 * p + 1]], axis=1)], axis=0)

    def split_dot(a, b_mat, stack_axis):
        hi = a.astype(BF16)
        lo = (a - hi.astype(F32)).astype(BF16)
        both = jnp.concatenate([hi, lo], axis=stack_axis)
        if stack_axis == 1:
            r = jnp.dot(b_mat, both, preferred_element_type=F32)
            return r[:, 0:LANES] + r[:, LANES:2 * LANES]
        r = jnp.dot(both, b_mat, preferred_element_type=F32)
        return r[0:c] + r[c:2 * c]

    def chunk(ci, p):
        rows = pl.ds(pl.multiple_of(ci * c, c), c)
        lanes = slice(p * LANES, (p + 1) * LANES)
        la = la_ref[0, rows, lanes]
        q = q_ref[0, rows, lanes]
        k = k_ref[0, rows, lanes]
        v = v_ref[0, rows, lanes]
        if mm_dtype == BF16:
            b = split_dot(la, tri.astype(BF16), 1)
        else:
            b = jnp.dot(tri, la, precision=HIGHEST, preferred_element_type=F32)
        st = st_sc[p]
        b_last = b[c - 1:c, :]
        b_mid = b[c // 2 - 1:c // 2, :]
        qt = q * jnp.exp(b - b_mid)
        kt = k * jnp.exp(b_mid - b)
        qts = jnp.concatenate([jnp.where(head0, qt, 0.0), jnp.where(head0, 0.0, qt)], axis=0)
        a = lax.dot_general(qts.astype(mm_dtype), kt.astype(mm_dtype), NT_DIMS, preferred_element_type=F32)
        a0 = jnp.where(causal, a[0:c], 0.0).astype(mm_dtype)
        a1 = jnp.where(causal, a[c:2 * c], 0.0).astype(mm_dtype)
        vm = v.astype(mm_dtype)
        o = jnp.where(head0, jnp.dot(a0, vm, preferred_element_type=F32),
                      jnp.dot(a1, vm, preferred_element_type=F32))
        qe = q * jnp.exp(b)
        o = o + lax.dot_general(qe.astype(mm_dtype), st.astype(mm_dtype), NT_DIMS, preferred_element_type=F32)
        kd = k * jnp.exp(b_last - b)
        upd = lax.dot_general(vm, kd.astype(mm_dtype), TN_DIMS, preferred_element_type=F32)
        st_sc[p] = st * jnp.exp(b_last) + jnp.where(same_head, upd, 0.0)
        if mm_dtype == BF16:
            ms = split_dot(o * o, head_mean.astype(BF16), 0)
        else:
            ms = jnp.dot(o * o, head_mean, precision=HIGHEST, preferred_element_type=F32)
        o_ref[0, rows, lanes] = o * lax.rsqrt(ms + EPS) * gn * gs_ref[0, rows, lanes]

    unroll = next(u for u in (8, 4, 2, 1) if n % u == 0)

    def body(i, carry):
        for u in range(unroll):
            for p in range(npair):
                chunk(i * unroll + u, p)
        return carry

    lax.fori_loop(0, n // unroll, body, 0)
    for p in range(npair):
        st = st_sc[p]
        sout_ref[0, 2 * p] = st[0:hd, 0:hd]
        sout_ref[0, 2 * p + 1] = st[hd:2 * hd, hd:2 * hd]


def _gla(qg, kg, vg, la, gs, gn_pair, s0_t, chunk, mm_dtype):
    b, l, hw = qg.shape
    pairs = hw // LANES
    npair = pairs if l <= GLA_CHUNK else 1
    seq = pl.BlockSpec((1, l, npair * LANES), lambda i, p: (i, 0, p))
    st_spec = pl.BlockSpec((1, 2 * npair, HEAD_DIM, HEAD_DIM), lambda i, p: (i, p, 0, 0))
    return pl.pallas_call(
        functools.partial(_gla_kernel, chunk=chunk, mm_dtype=mm_dtype),
        grid=(b, pairs // npair),
        in_specs=[seq, seq, seq, seq, seq, pl.BlockSpec((1, LANES), lambda i, p: (0, 0)), st_spec],
        out_specs=[seq, st_spec],
        out_shape=[jax.ShapeDtypeStruct((b, l, hw), F32),
                   jax.ShapeDtypeStruct((b, HEADS_PER_GROUP, HEAD_DIM, HEAD_DIM), F32)],
        scratch_shapes=[pltpu.VMEM((npair, LANES, LANES), F32)],
        compiler_params=_params("arbitrary", "arbitrary"),
        name="gla_chunked",
    )(qg, kg, vg, la, gs, gn_pair, s0_t)


def _mix_ffn_kernel(x_ref, oa_ref, og_ref, mod_ref, g_ref, wo_a_ref, wo_g_ref, wg_ref, wu_ref, wd_ref, o_ref,
                    x1_sc, h_sc, acc_sc):
    j = pl.program_id(2)
    bb, tl, d = x_ref.shape
    tm = bb * tl

    @pl.when(j == 0)
    def _():
        m = mod_ref[...]
        mix = (_mm(oa_ref[...].reshape(tm, -1), wo_a_ref[...])
               + _mm(og_ref[...].reshape(tm, -1), wo_g_ref[...]))
        x1 = x_ref[...] + m[:, 2:3, :] * mix.reshape(bb, tl, d)
        h = _rms(x1, g_ref[...]) * (1.0 + m[:, 4:5, :]) + m[:, 3:4, :]
        x1_sc[...] = x1.reshape(tm, d)
        h_sc[...] = h.reshape(tm, d).astype(h_sc.dtype)
        acc_sc[...] = jnp.zeros_like(acc_sc)

    h = h_sc[...]
    fc = wg_ref.shape[1]
    for c0 in range(0, fc, FFN_COL_CHUNK):
        cols = slice(c0, min(c0 + FFN_COL_CHUNK, fc))
        acc_sc[...] += _mm(_silu(_mm(h, wg_ref[:, cols])) * _mm(h, wu_ref[:, cols]), wd_ref[cols, :])

    @pl.when(j == pl.num_programs(2) - 1)
    def _():
        g2 = mod_ref[...][:, 5:6, :]
        o_ref[...] = x1_sc[...].reshape(bb, tl, d) + g2 * acc_sc[...].reshape(bb, tl, d)


def _mix_ffn(x, oa, og, mod, g, wo_a, wo_g, wg, wu, wd):
    b, l, d = x.shape
    hw = oa.shape[-1]
    f = wg.shape[1]
    bb, tl = _token_tiles(b, l, 512 if l >= 512 else 256)
    tm = bb * tl
    fc = f if wg.dtype == BF16 else 256
    once = dict(pipeline_mode=pl.Buffered(1)) if fc == f else {}
    tok = lambda i, t, j: (i, t, 0)
    const2 = lambda i, t, j: (0, 0)
    return pl.pallas_call(
        _mix_ffn_kernel,
        grid=(b // bb, l // tl, f // fc),
        in_specs=[
            pl.BlockSpec((bb, tl, d), tok),
            pl.BlockSpec((bb, tl, hw), tok),
            pl.BlockSpec((bb, tl, hw), tok),
            pl.BlockSpec((bb, 6, d), lambda i, t, j: (i, 0, 0)),
            pl.BlockSpec((1, d), const2),
            pl.BlockSpec(wo_a.shape, const2),
            pl.BlockSpec(wo_g.shape, const2),
            pl.BlockSpec((d, fc), lambda i, t, j: (0, j), **once),
            pl.BlockSpec((d, fc), lambda i, t, j: (0, j), **once),
            pl.BlockSpec((fc, d), lambda i, t, j: (j, 0), **once),
        ],
        out_specs=pl.BlockSpec((bb, tl, d), tok),
        out_shape=jax.ShapeDtypeStruct((b, l, d), F32),
        scratch_shapes=[pltpu.VMEM((tm, d), F32), pltpu.VMEM((tm, d), wg.dtype), pltpu.VMEM((tm, d), F32)],
        compiler_params=_params("arbitrary", "arbitrary", "arbitrary"),
        name="outproj_ffn",
    )(x, oa, og, mod, g, wo_a, wo_g, wg, wu, wd)


def _cm_in_kernel(x_ref, mod_ref, g_ref, w_ref, lng_ref, lnb_ref, u_ref, v_ref):
    bb, tl, d = x_ref.shape
    tm = bb * tl
    cw = u_ref.shape[-1]
    m = mod_ref[...]
    h = _rms(x_ref[...], g_ref[...]) * (1.0 + m[:, 1:2, :]) + m[:, 0:1, :]
    hb = h.reshape(tm, d).astype(w_ref.dtype)
    u_ref[...] = _gelu_tanh(_mm(hb, w_ref[:, 0:cw])).reshape(bb, tl, cw)
    zv = _gelu_tanh(_mm(hb, w_ref[:, cw:2 * cw]))
    mu = jnp.mean(zv, axis=-1, keepdims=True)
    zc = zv - mu
    var = jnp.mean(zc * zc, axis=-1, keepdims=True)
    v_ref[...] = (zc * lax.rsqrt(var + EPS) * lng_ref[...] + lnb_ref[...]).reshape(bb, tl, cw)


def _cm_in(x, mod, g, w, ln_g, ln_b):
    b, l, d = x.shape
    cw = w.shape[1] // 2
    bb, tl = _token_tiles(b, l, 512 if l >= 512 else 256)
    tok = lambda i, t: (i, t, 0)
    const2 = lambda i, t: (0, 0)
    out_spec = pl.BlockSpec((bb, tl, cw), tok)
    out_sds = jax.ShapeDtypeStruct((b, l, cw), F32)
    return pl.pallas_call(
        _cm_in_kernel,
        grid=(b // bb, l // tl),
        in_specs=[
            pl.BlockSpec((bb, tl, d), tok),
            pl.BlockSpec((bb, 6, d), lambda i, t: (i, 0, 0)),
            pl.BlockSpec((1, d), const2),
            pl.BlockSpec(w.shape, const2),
            pl.BlockSpec((1, cw), const2),
            pl.BlockSpec((1, cw), const2),
        ],
        out_specs=[out_spec, out_spec],
        out_shape=[out_sds, out_sds],
        compiler_params=_params("arbitrary", "arbitrary"),
        name="cm_in_gelu_ln",
    )(x, mod, g, w, ln_g, ln_b)


def _cm_out_kernel(x_ref, u_ref, v_ref, mod_ref, g_ref, ws_ref, bs_ref, wo_ref, wr_ref,
                   x3_ref, h_ref, gates_ref, rank_ref, cnt_ref, us_sc):
    bb, tl, d = x_ref.shape
    tm = bb * tl
    cw = u_ref.shape[-1]
    ngroups, cm, _ = ws_ref.shape
    gd = cw // ngroups
    u = u_ref[...].reshape(tm, cw)
    v = v_ref[...].reshape(tm, cw)
    bs = bs_ref[...]
    mm_dtype = wo_ref.dtype
    r = lax.broadcasted_iota(jnp.int32, (cm, cm), 0)
    c = lax.broadcasted_iota(jnp.int32, (cm, cm), 1)
    for g in range(ngroups):
        wsg = jnp.where(r >= c, ws_ref[g], 0.0).astype(mm_dtype)
        cols = slice(g * gd, (g + 1) * gd)
        for sb in range(tm // cm):
            rows = slice(sb * cm, (sb + 1) * cm)
            s = _mm(wsg, v[rows, cols].astype(mm_dtype)) + bs[rows, g:g + 1]
            us_sc[rows, cols] = (u[rows, cols] * s).astype(mm_dtype)
    y = _mm(us_sc[...], wo_ref[...])
    m = mod_ref[...]
    x3 = x_ref[...] + m[:, 2:3, :] * y.reshape(bb, tl, d)
    x3_ref[...] = x3
    h = (_rms(x3, g_ref[...]) * (1.0 + m[:, 4:5, :]) + m[:, 3:4, :]).reshape(tm, d)
    h_ref[...] = h.astype(BF16)
    wr = wr_ref[...]
    if mm_dtype == BF16:
        h_hi = h.astype(BF16)
        h_lo = (h - h_hi.astype(F32)).astype(BF16)
        wr_hi = wr.astype(BF16)
        wr_lo = (wr - wr_hi.astype(F32)).astype(BF16)
        both = jnp.dot(jnp.concatenate([h_hi, h_lo], axis=0), wr_hi, preferred_element_type=F32)
        logits = both[0:tm] + both[tm:2 * tm] + jnp.dot(h_hi, wr_lo, preferred_element_type=F32)
    else:
        logits = jnp.dot(h, wr, precision=HIGHEST, preferred_element_type=F32)
    ne = logits.shape[1]
    col = lax.broadcasted_iota(jnp.int32, (tm, ne), 1)
    m1 = jnp.max(logits, axis=1, keepdims=True)
    i1 = jnp.min(jnp.where(logits == m1, col, ne), axis=1, keepdims=True)
    rest = jnp.where(col == i1, -jnp.inf, logits)
    m2 = jnp.max(rest, axis=1, keepdims=True)
    i2 = jnp.min(jnp.where(rest == m2, col, ne), axis=1, keepdims=True)
    e2 = jnp.exp(m2 - m1)
    gates = jnp.where(col == i1, 1.0 / (1.0 + e2), 0.0) + jnp.where(col == i2, e2 / (1.0 + e2), 0.0)
    gates_ref[...] = gates
    routed = gates > 0.0
    tr = lax.broadcasted_iota(jnp.int32, (tm, tm), 0)
    tc = lax.broadcasted_iota(jnp.int32, (tm, tm), 1)
    upto = jnp.dot(jnp.where(tr >= tc, 1.0, 0.0).astype(BF16), jnp.where(routed, 1.0, 0.0).astype(BF16),
                   preferred_element_type=F32)
    rank_ref[...] = jnp.where(routed, upto - 1.0, -1.0)
    cnt_ref[0] = upto[tm - 1:tm, :]


def _cm_out(x, u, v, mod, g, ws_mix, bs_rows, wo, wr, cm):
    b, l, d = x.shape
    cw = u.shape[-1]
    ne = wr.shape[1]
    bb, tl = _token_tiles(b, l, 256)
    tm = bb * tl
    tok = lambda i, t: (i, t, 0)
    flat = lambda i, t: (i * (l // tl) + t, 0)
    const2 = lambda i, t: (0, 0)
    return pl.pallas_call(
        _cm_out_kernel,
        grid=(b // bb, l // tl),
        in_specs=[
            pl.BlockSpec((bb, tl, d), tok),
            pl.BlockSpec((bb, tl, cw), tok),
            pl.BlockSpec((bb, tl, cw), tok),
            pl.BlockSpec((bb, 6, d), lambda i, t: (i, 0, 0)),
            pl.BlockSpec((1, d), const2),
            pl.BlockSpec(ws_mix.shape, lambda i, t: (0, 0, 0)),
            pl.BlockSpec((tm, bs_rows.shape[1]), const2),
            pl.BlockSpec(wo.shape, const2),
            pl.BlockSpec(wr.shape, const2),
        ],
        out_specs=[pl.BlockSpec((bb, tl, d), tok), pl.BlockSpec((tm, d), flat), pl.BlockSpec((tm, ne), flat),
                   pl.BlockSpec((tm, ne), flat),
                   pl.BlockSpec((1, 1, ne), lambda i, t: (i * (l // tl) + t, 0, 0))],
        out_shape=[jax.ShapeDtypeStruct((b, l, d), F32), jax.ShapeDtypeStruct((b * l, d), BF16),
                   jax.ShapeDtypeStruct((b * l, ne), F32), jax.ShapeDtypeStruct((b * l, ne), F32),
                   jax.ShapeDtypeStruct((b * l // tm, 1, ne), F32)],
        scratch_shapes=[pltpu.VMEM((tm, cw), wo.dtype)],
        compiler_params=_params("arbitrary", "arbitrary"),
        name="cm_out_router",
    )(x, u, v, mod, g, ws_mix, bs_rows, wo, wr)


MOE_SUB = 256
MOE_ALIGN = 16
MOE_WIN_SMALL = 128
MOE_WIN_FULL = MOE_SUB + MOE_ALIGN
MOE_ROW_CHUNK = 512
MOE_FFN_ROWS = 896
MOE_BLOCK_SUBS = 13
MOE_HIDDEN_CHUNK = 512


def _moe_routing(rank, cnt, nb, nsub):
    ne = rank.shape[1]
    rank = rank.astype(jnp.int32).reshape(nb, nsub, MOE_SUB, ne)
    cnt = cnt.astype(jnp.int32).reshape(nb, nsub, ne)
    off = jnp.cumsum(cnt, axis=1) - cnt
    total = jnp.sum(cnt, axis=1)
    start = (off // MOE_ALIGN) * MOE_ALIGN
    shift = off - start
    win_row = jnp.where(rank >= 0, rank + shift[:, :, None, :], -1)
    win_row = win_row.transpose(0, 3, 1, 2).reshape(nb, ne, nsub * MOE_SUB)
    used = jnp.where(cnt > 0, shift + cnt, 0)
    scalars = jnp.concatenate([start.transpose(0, 2, 1).reshape(-1), used.transpose(0, 2, 1).reshape(-1),
                               total.reshape(-1)]).astype(jnp.int32)
    return win_row, scalars


def _moe_kernel(sc_ref, h_ref, gates_ref, row_ref, wg_ref, wu_ref, wd_ref, o_ref, x_sc, y_sc):
    b = pl.program_id(0)
    e = pl.program_id(1)
    j = pl.program_id(2)
    ne = pl.num_programs(1)
    tb = h_ref.shape[0]
    nsub = tb // MOE_SUB
    cap = x_sc.shape[0]
    n_be = pl.num_programs(0) * ne
    seg = (b * ne + e) * nsub
    total = sc_ref[2 * n_be * nsub + b * ne + e]

    def windows(fn):
        for s in range(nsub):
            start = pl.multiple_of(sc_ref[seg + s], MOE_ALIGN)
            used = sc_ref[n_be * nsub + seg + s]
            tok = slice(s * MOE_SUB, (s + 1) * MOE_SUB)
            win_row = row_ref[0, pl.ds(e, 1), tok]

            def run(rows, start=start, tok=tok, win_row=win_row):
                ridx = lax.broadcasted_iota(jnp.int32, (rows, MOE_SUB), 0)
                fn(pl.ds(start, rows), tok, jnp.where(ridx == win_row, 1.0, 0.0).astype(BF16))

            pl.when((used > 0) & (used <= MOE_WIN_SMALL))(functools.partial(run, MOE_WIN_SMALL))
            pl.when(used > MOE_WIN_SMALL)(functools.partial(run, MOE_WIN_FULL))

    @pl.when((e == 0) & (j == 0))
    def _():
        o_ref[...] = jnp.zeros_like(o_ref)

    @pl.when(j == 0)
    def _():
        def clear(c, carry):
            rows = pl.ds(pl.multiple_of(c * MOE_ROW_CHUNK, MOE_ROW_CHUNK), MOE_ROW_CHUNK)
            x_sc[rows, :] = jnp.zeros((MOE_ROW_CHUNK, x_sc.shape[1]), BF16)
            y_sc[rows, :] = jnp.zeros((MOE_ROW_CHUNK, y_sc.shape[1]), F32)
            return carry

        lax.fori_loop(0, jnp.minimum((total + MOE_WIN_FULL + MOE_ROW_CHUNK - 1) // MOE_ROW_CHUNK,
                                     cap // MOE_ROW_CHUNK), clear, 0)

        def dispatch(rows, tok, onehot):
            picked = jnp.dot(onehot, h_ref[tok, :], preferred_element_type=F32).astype(BF16)
            x_sc[rows, :] = x_sc[rows, :] + picked

        windows(dispatch)

    def ffn(r0, nrows):
        rows = pl.ds(pl.multiple_of(r0, MOE_WIN_SMALL), nrows)
        x = x_sc[rows, :]
        gate = jnp.dot(x, wg_ref[0], preferred_element_type=F32)
        up = jnp.dot(x, wu_ref[0], preferred_element_type=F32)
        act = (_silu(gate) * up).astype(BF16)
        y_sc[rows, :] += jnp.dot(act, wd_ref[0], preferred_element_type=F32)

    nfull = (jnp.maximum(total - 512, 0) + MOE_FFN_ROWS - 1) // MOE_FFN_ROWS
    rem = jnp.maximum(total - nfull * MOE_FFN_ROWS, 0)
    tail = nfull * MOE_FFN_ROWS

    def full_chunk(c, carry):
        ffn(c * MOE_FFN_ROWS, MOE_FFN_ROWS)
        return carry

    lax.fori_loop(0, nfull, full_chunk, 0)
    take_256 = (rem > 128) & (rem <= 384)
    pl.when(rem > 384)(lambda: ffn(tail, 512))
    pl.when(take_256)(lambda: ffn(tail, 256))
    pl.when((rem > 0) & (rem <= 128))(lambda: ffn(tail, 128))
    pl.when((rem > 256) & (rem <= 384))(lambda: ffn(tail + 256, 128))

    @pl.when(j == pl.num_programs(2) - 1)
    def _():
        gates = gates_ref[...]
        col = lax.broadcasted_iota(jnp.int32, gates.shape, 1)
        ge = jnp.sum(jnp.where(col == e, gates, 0.0), axis=1, keepdims=True)

        def combine(rows, tok, onehot):
            back = lax.dot_general(onehot, y_sc[rows, :].astype(BF16), TN_DIMS, preferred_element_type=F32)
            o_ref[tok, :] = (o_ref[tok, :].astype(F32) + ge[tok] * back).astype(o_ref.dtype)

        windows(combine)


def _moe(h, gates, rank, cnt, wg, wu, wd, tb):
    t, d = h.shape
    ne, _, f = wg.shape
    fc = MOE_HIDDEN_CHUNK
    assert t % tb == 0 and tb % MOE_SUB == 0 and f % fc == 0
    nb, nsub = t // tb, tb // MOE_SUB
    win_row, scalars = _moe_routing(rank, cnt, nb, nsub)
    cap = -(-(tb + MOE_WIN_FULL) // MOE_ROW_CHUNK) * MOE_ROW_CHUNK
    once = dict(pipeline_mode=pl.Buffered(1))
    grid_spec = pltpu.PrefetchScalarGridSpec(
        num_scalar_prefetch=1,
        grid=(nb, ne, f // fc),
        in_specs=[
            pl.BlockSpec((tb, d), lambda i, e, j, sc: (i, 0), **once),
            pl.BlockSpec((tb, ne), lambda i, e, j, sc: (i, 0), **once),
            pl.BlockSpec((1, ne, tb), lambda i, e, j, sc: (i, 0, 0), **once),
            pl.BlockSpec((1, d, fc), lambda i, e, j, sc: (e, 0, j)),
            pl.BlockSpec((1, d, fc), lambda i, e, j, sc: (e, 0, j)),
            pl.BlockSpec((1, fc, d), lambda i, e, j, sc: (e, j, 0)),
        ],
        out_specs=pl.BlockSpec((tb, d), lambda i, e, j, sc: (i, 0), **once),
        scratch_shapes=[pltpu.VMEM((cap, d), BF16), pltpu.VMEM((cap, d), F32)],
    )
    return pl.pallas_call(
        _moe_kernel,
        grid_spec=grid_spec,
        out_shape=jax.ShapeDtypeStruct((t, d), BF16),
        compiler_params=_params("arbitrary", "arbitrary", "arbitrary"),
        name="moe_experts",
    )(scalars, h, gates, win_row, wg, wu, wd)


def _final_kernel(x_ref, moe_ref, mod_ref, g_ref, o_ref):
    bb, tl, d = x_ref.shape
    g2 = mod_ref[...][:, 5:6, :]
    x4 = x_ref[...] + g2 * moe_ref[...].reshape(bb, tl, d)
    o_ref[...] = _rms(x4, g_ref[...])


def _final(x, moe, row0, mod, g):
    b, l, d = x.shape
    bb, tl = _token_tiles(b, l, 512 if l >= 512 else 256)
    tm = bb * tl
    assert row0 % tm == 0
    tok = lambda i, t: (i, t, 0)
    return pl.pallas_call(
        _final_kernel,
        grid=(b // bb, l // tl),
        in_specs=[
            pl.BlockSpec((bb, tl, d), tok),
            pl.BlockSpec((tm, d), lambda i, t: (row0 // tm + i * (l // tl) + t, 0)),
            pl.BlockSpec((bb, 6, d), lambda i, t: (i, 0, 0)),
            pl.BlockSpec((1, d), lambda i, t: (0, 0)),
        ],
        out_specs=pl.BlockSpec((bb, tl, d), tok),
        out_shape=jax.ShapeDtypeStruct((b, l, d), F32),
        compiler_params=_params("arbitrary", "arbitrary"),
        name="final_norm",
    )(x, moe, mod, g)


def _layer0(x, mod, pos0, past, w):
    b, l, d = x.shape
    hw = HEAD_WIDTH
    outs = _in_proj(x, mod, w["norm_mix_g0"], w["w_in_main"], w["w_in_lr"], w["wa2_hi"], w["wa2_lo"], w["ba2"], pos0)
    qa, ka, va, qg, kg, vg, gs, la = outs
    if past is None:
        oa = _moba_prompt(qa, ka, va)
        s0_t = jnp.zeros((b, HEADS_PER_GROUP, HEAD_DIM, HEAD_DIM), F32)
    else:
        cache_k, cache_v, state, page_table = past
        oa = _moba_sample(qa, ka, va, cache_k, cache_v, page_table)
        s0_t = jnp.swapaxes(state, -1, -2)
    chunk = GLA_CHUNK if l % GLA_CHUNK == 0 else l
    mm_dtype = BF16 if chunk % 16 == 0 else F32
    og, s_t = _gla(qg, kg, vg, la, gs, w["gla_norm_pair"], s0_t, chunk, mm_dtype)
    x2 = _mix_ffn(x, oa, og, mod, w["norm_ffn_g0"], w["wo_a"], w["wo_g"], w["ffn_wg"], w["ffn_wu"], w["ffn_wd"])
    k_rows = ka.reshape(b, l, HEADS_PER_GROUP, HEAD_DIM)
    v_rows = va.reshape(b, l, HEADS_PER_GROUP, HEAD_DIM)
    return x2, k_rows, v_rows, jnp.swapaxes(s_t, -1, -2)


def _layer1_front(x, mod, w):
    b, l, d = x.shape
    u, v = _cm_in(x, mod, w["norm_mix_g1"], w["cm_w_in"], w["cm_ln_g"], w["cm_ln_b"])
    bb, tl = _token_tiles(b, l, 256)
    tm = bb * tl
    if l >= CM_CHUNK:
        cm = CM_CHUNK
        ws_mix = w["cm_w_s"]
        bs_rows = jnp.tile(w["cm_b_s"].T, (tm // CM_CHUNK, 1))
    else:
        cm = tm
        pos = jnp.arange(tm, dtype=jnp.int32)
        same_seq = (pos[:, None] // l == pos[None, :] // l).astype(F32)
        place = (pos[:, None] % l == jnp.arange(l, dtype=jnp.int32)[None, :]).astype(F32)
        ws_mix = jnp.einsum("ra,gab,cb->grc", place, w["cm_w_s"][:, :l, :l], place,
                            precision=HIGHEST) * same_seq[None]
        bs_rows = jnp.tile(w["cm_b_s"][:, :l].T, (tm // l, 1))
    assert tm == MOE_SUB
    x3, h, gates, rank, cnt = _cm_out(x, u, v, mod, w["norm_ffn_g1"], ws_mix, bs_rows, w["cm_w_out"],
                                      w["moe_w_router"], cm)
    start = ((l - 1) // CM_CHUNK) * CM_CHUNK
    return x3, (h, gates, rank, cnt), v[:, start:]


def kernel(x_prompt, x_sample, cache_k, cache_v, state_gla, page_table, c_prompt, c_sample, ada_w, ada_b, norm_mix_g, norm_ffn_g, ab_w_in, gla_w_a2, gla_b_a2, gla_norm_g, ab_w_out, ffn_w_gate, ffn_w_up, ffn_w_down, cm_w_in, cm_ln_g, cm_ln_b, cm_w_s, cm_b_s, cm_w_out, moe_w_router, moe_w_gate, moe_w_up, moe_w_down, final_norm_g):
    bp, lp, d = x_prompt.shape
    bs, ls, _ = x_sample.shape
    hw = HEAD_WIDTH
    n_main = 7 * hw
    past_len = page_table.shape[1] * cache_k.shape[2]

    wa2 = jnp.pad(gla_w_a2[0], ((0, LANES - GLA_LOWRANK), (0, 0)))
    wa2_hi = wa2.astype(BF16)
    shared = {
        "norm_mix_g0": norm_mix_g[0][None], "norm_mix_g1": norm_mix_g[1][None],
        "norm_ffn_g0": norm_ffn_g[0][None], "norm_ffn_g1": norm_ffn_g[1][None],
        "wa2_hi": wa2_hi, "wa2_lo": (wa2 - wa2_hi.astype(F32)).astype(BF16),
        "ba2": gla_b_a2[0][None],
        "gla_norm_pair": jnp.tile(gla_norm_g[0], LANES // HEAD_DIM)[None],
        "cm_ln_g": cm_ln_g[0][None], "cm_ln_b": cm_ln_b[0][None],
        "cm_w_s": cm_w_s[0], "cm_b_s": cm_b_s[0],
        "moe_w_router": moe_w_router[0],
    }
    mats = {
        "w_in_main": ab_w_in[0],
        "w_in_lr": jnp.pad(ab_w_in[0][:, n_main:], ((0, 0), (0, LANES - GLA_LOWRANK))),
        "wo_a": ab_w_out[0][:hw], "wo_g": ab_w_out[0][hw:],
        "ffn_wg": ffn_w_gate[0], "ffn_wu": ffn_w_up[0], "ffn_wd": ffn_w_down[0],
        "cm_w_in": cm_w_in[0], "cm_w_out": cm_w_out[0],
    }
    w = dict(shared, **{k: v.astype(BF16) for k, v in mats.items()})
    w_sample = dict(shared, **mats)

    rows = bp + bs
    rows_pad = -(-rows // 16) * 16
    c_all = jnp.pad(jnp.concatenate([c_prompt, c_sample], axis=0), ((0, rows_pad - rows), (0, 0)))
    mod_all = _modulation(c_all, ada_w, ada_b)
    mod_p = [mod_all[l, :bp].reshape(bp, 6, d) for l in range(2)]
    mod_s = [mod_all[l, bp:rows].reshape(bs, 6, d) for l in range(2)]

    x2_p, k_p, v_p, gla_p = _layer0(x_prompt, mod_p[0], 0, None, w)
    x2_s, k_s, v_s, gla_s = _layer0(x_sample, mod_s[0], past_len,
                                    (cache_k[0], cache_v[0], state_gla[0], page_table), w_sample)

    x3_p, route_p, cmv_p = _layer1_front(x2_p, mod_p[1], w)
    x3_s, route_s, cmv_s = _layer1_front(x2_s, mod_s[1], w_sample)

    t_all = bp * lp + bs * ls
    subs = -(-t_all // MOE_SUB)
    n_blocks = -(-subs // MOE_BLOCK_SUBS)
    tb = -(-subs // n_blocks) * MOE_SUB
    pad = n_blocks * tb - t_all
    ne = moe_w_gate.shape[1]
    pads = (jnp.zeros((pad, d), BF16), jnp.zeros((pad, ne), F32), jnp.full((pad, ne), -1.0, F32),
            jnp.zeros((pad // MOE_SUB, 1, ne), F32))
    h_all, gates_all, rank_all, cnt_all = (jnp.concatenate(parts, axis=0) for parts in zip(route_p, route_s, pads))
    moe = _moe(h_all, gates_all, rank_all, cnt_all, moe_w_gate[0].astype(BF16), moe_w_up[0].astype(BF16),
               moe_w_down[0].astype(BF16), tb)

    fg = final_norm_g[None]
    y_p = _final(x3_p, moe, 0, mod_p[1], fg)
    y_s = _final(x3_s, moe, bp * lp, mod_s[1], fg)

    return (y_p, y_s, k_p[None], v_p[None], k_s[None], v_s[None], gla_p[None], gla_s[None],
            cmv_p[None], cmv_s[None])
```

```python
import functools

import jax
import jax.numpy as jnp
from jax import lax
from jax.experimental import pallas as pl
from jax.experimental.pallas import tpu as pltpu

F32 = jnp.float32
BF16 = jnp.bfloat16
HIGHEST = lax.Precision.HIGHEST

HEAD_DIM = 64
HEADS_PER_GROUP = 8
HEAD_WIDTH = HEADS_PER_GROUP * HEAD_DIM
MOBA_BLOCK = 256
MOBA_TOPK = 3
ROPE_THETA = 10000.0
GLA_LOWRANK = 16
GLA_GATE_NORM = 16.0
GLA_CHUNK = 64
CM_CHUNK = 128
CM_GROUPS = 8
MOE_TOP_K = 2
EPS = 1e-6
NEG = -1e30

LANES = 128
FFN_COL_CHUNK = 1024
VMEM_LIMIT_BYTES = 56 * 1024 * 1024

NT_DIMS = (((1,), (1,)), ((), ()))
TN_DIMS = (((0,), (0,)), ((), ()))


def _params(*sem):
    return pltpu.CompilerParams(dimension_semantics=sem, vmem_limit_bytes=VMEM_LIMIT_BYTES)


def _rms(x, g):
    return x * lax.rsqrt(jnp.mean(x * x, axis=-1, keepdims=True) + EPS) * g


def _silu(x):
    return x * jax.nn.sigmoid(x)


def _split(x):
    hi = x.astype(BF16)
    return hi, (x - hi.astype(F32)).astype(BF16)


def _mm(a, w):
    if w.dtype == F32:
        m = a.shape[0]
        a_hi, a_lo = _split(a.astype(F32))
        w_hi, w_lo = _split(w)
        both = jnp.dot(jnp.concatenate([a_hi, a_lo], axis=0), w_hi, preferred_element_type=F32)
        return both[0:m] + both[m:2 * m] + jnp.dot(a_hi, w_lo, preferred_element_type=F32)
    return jnp.dot(a.astype(BF16), w, preferred_element_type=F32)


def _gelu_tanh(x):
    c = 0.7978845608028654
    return x * (0.5 + 0.5 * jnp.tanh(x * (c + (c * 0.044715) * (x * x))))


def _token_tiles(b, l, target):
    if l >= target:
        assert l % target == 0
        return 1, target
    bb = min(b, target // l)
    assert target % l == 0 and b % bb == 0
    return bb, l


def _mod_kernel(c_ref, w_ref, b_ref, o_ref):
    s = _silu(c_ref[...])
    r = s.shape[0]
    s_hi = s.astype(BF16)
    s_lo = (s - s_hi.astype(F32)).astype(BF16)
    w = w_ref[0]
    w_hi = w.astype(BF16)
    w_lo = (w - w_hi.astype(F32)).astype(BF16)
    both = jnp.dot(jnp.concatenate([s_hi, s_lo], axis=0), w_hi, preferred_element_type=F32)
    o_ref[0] = both[0:r] + both[r:2 * r] + jnp.dot(s_hi, w_lo, preferred_element_type=F32) + b_ref[0]


def _modulation(c_all, ada_w, ada_b):
    r, d = c_all.shape
    depth, _, n = ada_w.shape
    tn = 1536
    return pl.pallas_call(
        _mod_kernel,
        grid=(depth, n // tn),
        in_specs=[
            pl.BlockSpec((r, d), lambda l, j: (0, 0)),
            pl.BlockSpec((1, d, tn), lambda l, j: (l, 0, j)),
            pl.BlockSpec((1, 1, tn), lambda l, j: (l, 0, j)),
        ],
        out_specs=pl.BlockSpec((1, r, tn), lambda l, j: (l, 0, j)),
        out_shape=jax.ShapeDtypeStruct((depth, r, n), F32),
        compiler_params=_params("arbitrary", "arbitrary"),
        name="adaln_modulation",
    )(c_all, ada_w, ada_b.reshape(depth, 1, n))


def _in_proj_kernel(x_ref, mod_ref, g_ref, w_ref, wlr_ref, wa2h_ref, wa2l_ref, ba2_ref, cos_ref, sin_ref,
                    qa_ref, ka_ref, va_ref, qg_ref, kg_ref, vg_ref, gs_ref, la_ref):
    bb, tl, d = x_ref.shape
    tm = bb * tl
    hw = HEAD_WIDTH
    m = mod_ref[...]
    h = _rms(x_ref[...], g_ref[...]) * (1.0 + m[:, 1:2, :]) + m[:, 0:1, :]
    hb = h.reshape(tm, d).astype(w_ref.dtype)

    def proj(c):
        return _mm(hb, w_ref[:, c * hw:(c + 1) * hw])

    cos = jnp.concatenate([cos_ref[...]] * (hw // LANES), axis=-1)[None]
    sin = jnp.concatenate([sin_ref[...]] * (hw // LANES), axis=-1)[None]
    lane = lax.broadcasted_iota(jnp.int32, (tm, hw), 1)
    first_half = (lane % HEAD_DIM) < (HEAD_DIM // 2)

    def rope(z):
        partner = jnp.where(first_half, pltpu.roll(z, hw - HEAD_DIM // 2, 1), pltpu.roll(z, HEAD_DIM // 2, 1))
        return z.reshape(bb, tl, hw) * cos + partner.reshape(bb, tl, hw) * sin

    scale = HEAD_DIM ** -0.5
    qa_ref[...] = rope(proj(0)) * scale
    ka_ref[...] = rope(proj(1))
    va_ref[...] = proj(2).reshape(bb, tl, hw)
    qg_ref[...] = (proj(3) * scale).reshape(bb, tl, hw)
    kg_ref[...] = proj(4).reshape(bb, tl, hw)
    vg_ref[...] = proj(5).reshape(bb, tl, hw)
    gs_ref[...] = _silu(proj(6)).reshape(bb, tl, hw)
    zlr = _mm(hb, wlr_ref[...])
    hi = zlr.astype(BF16)
    lo = (zlr - hi.astype(F32)).astype(BF16)
    pre = (jnp.dot(hi, wa2h_ref[...], preferred_element_type=F32)
           + jnp.dot(lo, wa2h_ref[...], preferred_element_type=F32)
           + jnp.dot(hi, wa2l_ref[...], preferred_element_type=F32)) + ba2_ref[...]
    log_sig = jnp.minimum(pre, 0.0) - jnp.log(1.0 + jnp.exp(-jnp.abs(pre)))
    la_ref[...] = (log_sig * (1.0 / GLA_GATE_NORM)).reshape(bb, tl, hw)


def _rope_tables(pos0, l):
    half = HEAD_DIM // 2
    inv = 1.0 / (ROPE_THETA ** (jnp.arange(half, dtype=F32) / half))
    ang = (pos0 + jnp.arange(l, dtype=jnp.int32)).astype(F32)[:, None] * inv[None, :]
    cos, sin = jnp.cos(ang), jnp.sin(ang)
    reps = LANES // HEAD_DIM
    cos_t = jnp.tile(jnp.concatenate([cos, cos], axis=-1), (1, reps))
    sin_t = jnp.tile(jnp.concatenate([-sin, sin], axis=-1), (1, reps))
    return cos_t, sin_t


def _in_proj(x, mod, g, w_main, w_lr, wa2_hi, wa2_lo, ba2, pos0):
    b, l, d = x.shape
    bb, tl = _token_tiles(b, l, 512 if l >= 512 else 256)
    hw = HEAD_WIDTH
    cos_t, sin_t = _rope_tables(pos0, l)
    tok = lambda i, j: (i, j, 0)
    const2 = lambda i, j: (0, 0)
    out_spec = pl.BlockSpec((bb, tl, hw), tok)
    out_sds = jax.ShapeDtypeStruct((b, l, hw), F32)
    return pl.pallas_call(
        _in_proj_kernel,
        grid=(b // bb, l // tl),
        in_specs=[
            pl.BlockSpec((bb, tl, d), tok),
            pl.BlockSpec((bb, 6, d), lambda i, j: (i, 0, 0)),
            pl.BlockSpec((1, d), const2),
            pl.BlockSpec((d, 7 * hw), const2),
            pl.BlockSpec(w_lr.shape, const2),
            pl.BlockSpec(wa2_hi.shape, const2),
            pl.BlockSpec(wa2_lo.shape, const2),
            pl.BlockSpec((1, hw), const2),
            pl.BlockSpec((tl, LANES), lambda i, j: (j, 0)),
            pl.BlockSpec((tl, LANES), lambda i, j: (j, 0)),
        ],
        out_specs=[out_spec] * 8,
        out_shape=[out_sds] * 8,
        compiler_params=_params("arbitrary", "arbitrary"),
        name="in_proj_rope",
    )(x, mod, g, w_main, w_lr, wa2_hi, wa2_lo, ba2, cos_t, sin_t)


def _moba_prompt_kernel(q_ref, k_ref, v_ref, o_ref, kb_sc, vb_sc, km_sc, qs_sc, s_sc, p_sc, m_sc, acc_sc):
    ti = pl.program_id(2)
    lk = k_ref.shape[1]
    nblk = lk // MOBA_BLOCK
    nblk8 = -(-nblk // 8) * 8
    tq = q_ref.shape[1]
    nsub = tq // MOBA_BLOCK

    @pl.when(ti == 0)
    def _():
        k = k_ref[0]
        v = v_ref[0]
        row = lax.broadcasted_iota(jnp.int32, (lk, LANES), 0)
        col = lax.broadcasted_iota(jnp.int32, (lk, LANES), 1)
        kb_sc[:, 0:LANES] = k.astype(BF16)
        kb_sc[:, LANES:2 * LANES] = jnp.where(row // MOBA_BLOCK == col, 1.0, 0.0).astype(BF16)
        vb_sc[0] = jnp.where(col < HEAD_DIM, v, 1.0).astype(BF16)
        vb_sc[1] = jnp.where(col >= HEAD_DIM, v, 1.0).astype(BF16)
        km_sc[...] = jnp.zeros_like(km_sc)
        km_sc[0:nblk, :] = jnp.mean(k.reshape(nblk, MOBA_BLOCK, LANES), axis=1)

    q = q_ref[0]
    lane = lax.broadcasted_iota(jnp.int32, (tq, LANES), 1)
    blk = lax.broadcasted_iota(jnp.int32, (nblk8, tq), 0)
    own = ti * nsub + lax.broadcasted_iota(jnp.int32, (nblk8, tq), 1) // MOBA_BLOCK
    past_blk = blk < own
    km = km_sc[...]
    for h in range(2):
        head_lanes = (lane < HEAD_DIM) if h == 0 else (lane >= HEAD_DIM)
        qh = jnp.where(head_lanes, q, 0.0)
        gate = lax.dot_general(km, qh, NT_DIMS, precision=HIGHEST, preferred_element_type=F32)[0:nblk8]
        gate = jnp.where(past_blk, gate, -jnp.inf)
        cnt = jnp.zeros((nblk8, tq), F32)
        for mrow in range(nblk):
            gm = gate[mrow:mrow + 1, :]
            beats = (gm > gate) | ((gm == gate) & (mrow < blk))
            cnt = cnt + jnp.where(beats, 1.0, 0.0)
        sel = (past_blk & (cnt < float(MOBA_TOPK))) | (blk == own)
        bias_t = jnp.concatenate(
            [jnp.where(sel, 0.0, NEG), jnp.full((LANES - nblk8, tq), NEG, F32)], axis=0)
        qs_sc[h, :, 0:LANES] = qh.astype(BF16)
        qs_sc[h, :, LANES:2 * LANES] = bias_t.T.astype(BF16)

    m_sc[...] = jnp.full_like(m_sc, NEG)
    acc_sc[...] = jnp.zeros_like(acc_sc)
    rc = 128
    rowi = lax.broadcasted_iota(jnp.int32, (rc, MOBA_BLOCK), 0)
    coli = lax.broadcasted_iota(jnp.int32, (rc, MOBA_BLOCK), 1)

    def scores(n, slot):
        start = pl.multiple_of(n * MOBA_BLOCK, MOBA_BLOCK)
        kblk = kb_sc[pl.ds(start, MOBA_BLOCK), :]
        for h in range(2):
            s_sc[slot, h] = lax.dot_general(qs_sc[h], kblk, NT_DIMS, preferred_element_type=F32)

    def consume(n, slot, own_sub):
        start = pl.multiple_of(n * MOBA_BLOCK, MOBA_BLOCK)
        for h in range(2):
            for r0 in range(0, tq, rc):
                rows = slice(r0, r0 + rc)
                s = s_sc[slot, h, rows, :]
                if own_sub is not None and r0 // MOBA_BLOCK == own_sub:
                    s = jnp.where(rowi + (r0 - own_sub * MOBA_BLOCK) < coli, NEG, s)
                m_old = m_sc[h, rows, :]
                row_max = jnp.max(jnp.maximum(s[:, 0:LANES], s[:, LANES:2 * LANES]), axis=1, keepdims=True)
                m_new = jnp.maximum(m_old, row_max)
                p_sc[h, rows, :] = jnp.exp(s - jnp.concatenate([m_new, m_new], axis=1)).astype(BF16)
                acc_sc[h, rows, :] = jnp.exp(m_old - m_new) * acc_sc[h, rows, :]
                m_sc[h, rows, :] = m_new
            acc_sc[h] += jnp.dot(p_sc[h], vb_sc[h, pl.ds(start, MOBA_BLOCK), :], preferred_element_type=F32)

    def past_pair(i, carry):
        scores(2 * i + 1, 1)
        consume(2 * i, 0, None)
        scores(2 * i + 2, 0)
        consume(2 * i + 1, 1, None)
        return carry

    scores(0, 0)
    lax.fori_loop(0, ti, past_pair, 0)
    scores(2 * ti + 1, 1)
    consume(2 * ti, 0, 0)
    consume(2 * ti + 1, 1, 1)
    a0 = acc_sc[0]
    a1 = acc_sc[1]
    o_ref[0] = jnp.where(lane < HEAD_DIM, a0 / pltpu.roll(a0, HEAD_DIM, 1), a1 / pltpu.roll(a1, HEAD_DIM, 1))


def _moba_prompt(qa, ka, va):
    b, l, hw = qa.shape
    tq = 2 * MOBA_BLOCK
    assert l % tq == 0
    pairs = hw // LANES
    return pl.pallas_call(
        _moba_prompt_kernel,
        grid=(b, pairs, l // tq),
        in_specs=[
            pl.BlockSpec((1, tq, LANES), lambda i, p, t: (i, t, p)),
            pl.BlockSpec((1, l, LANES), lambda i, p, t: (i, 0, p)),
            pl.BlockSpec((1, l, LANES), lambda i, p, t: (i, 0, p)),
        ],
        out_specs=pl.BlockSpec((1, tq, LANES), lambda i, p, t: (i, t, p)),
        out_shape=jax.ShapeDtypeStruct((b, l, hw), F32),
        scratch_shapes=[
            pltpu.VMEM((l, 2 * LANES), BF16),
            pltpu.VMEM((2, l, LANES), BF16),
            pltpu.VMEM((LANES, LANES), F32),
            pltpu.VMEM((2, tq, 2 * LANES), BF16),
            pltpu.VMEM((2, 2, tq, MOBA_BLOCK), F32),
            pltpu.VMEM((2, tq, MOBA_BLOCK), BF16),
            pltpu.VMEM((2, tq, LANES), F32),
            pltpu.VMEM((2, tq, LANES), F32),
        ],
        compiler_params=_params("arbitrary", "arbitrary", "arbitrary"),
        name="moba_prompt",
    )(qa, ka, va)


def _moba_sample_kernel(pt_ref, q_ref, kn_ref, vn_ref, *refs, blocks_per_step):
    del pt_ref
    npages = 2 * blocks_per_step
    k_refs, v_refs = refs[0:npages], refs[npages:2 * npages]
    o_ref, q_sc, m_sc, l_sc, g_sc, acc_sc = refs[2 * npages:]
    j = pl.program_id(1)
    last = pl.num_programs(1) - 1
    nb = acc_sc.shape[0]
    nq = q_ref.shape[1]
    nh = HEADS_PER_GROUP
    nr = nh * nq
    hw = HEAD_WIDTH
    keys = 2 * k_refs[0].shape[3]
    row_w = lax.broadcasted_iota(jnp.int32, (nr, hw), 0)
    lane_w = lax.broadcasted_iota(jnp.int32, (nr, hw), 1)
    own_head = (row_w // nq) == (lane_w // HEAD_DIM)
    lane = lax.broadcasted_iota(jnp.int32, (nr, LANES), 1)

    def to_queries(t):
        return jnp.sum(jnp.where(own_head, t, 0.0).reshape(nh, nq, hw), axis=0)

    @pl.when(j == 0)
    def _():
        qbd = jnp.where(own_head, jnp.concatenate([q_ref[0]] * nh, axis=0), 0.0)
        hi = qbd.astype(BF16)
        q_sc[0:nr] = hi
        q_sc[nr:2 * nr] = (qbd - hi.astype(F32)).astype(BF16)
        m_sc[...] = jnp.zeros_like(m_sc)
        l_sc[...] = jnp.zeros_like(l_sc)
        g_sc[...] = jnp.zeros_like(g_sc)

    def scores(kt):
        s2 = jnp.dot(q_sc[...], kt, preferred_element_type=F32)
        return s2[0:nr] + s2[nr:2 * nr]

    @pl.when(j < last)
    def _():
        m_all, l_all, g_all = m_sc[...], l_sc[...], g_sc[...]
        for i in range(blocks_per_step):
            n = j * blocks_per_step + i
            kt = jnp.concatenate([k_refs[2 * i][0, 0], k_refs[2 * i + 1][0, 0]], axis=1).astype(BF16)
            vt = jnp.concatenate([v_refs[2 * i][0, 0], v_refs[2 * i + 1][0, 0]], axis=1).astype(BF16)
            s = scores(kt)
            gate = jnp.sum(s, axis=1, keepdims=True) * (1.0 / keys)
            mx = jnp.max(s, axis=1, keepdims=True)
            p = jnp.exp(s - mx)
            den = jnp.sum(p, axis=1, keepdims=True)
            pv = lax.dot_general(p.astype(BF16), vt, NT_DIMS, preferred_element_type=F32)
            acc_sc[n] = to_queries(pv)
            m_all = jnp.where(lane == n, mx, m_all)
            l_all = jnp.where(lane == n, den, l_all)
            g_all = jnp.where(lane == n, gate, g_all)
        m_sc[...], l_sc[...], g_sc[...] = m_all, l_all, g_all

    @pl.when(j == last)
    def _():
        cur = jnp.where(lane < nb, g_sc[...], -jnp.inf)
        sel = jnp.zeros((nr, LANES), jnp.bool_)
        for _ in range(MOBA_TOPK):
            top = jnp.max(cur, axis=1, keepdims=True)
            first = jnp.min(jnp.where(cur == top, lane, LANES), axis=1, keepdims=True)
            pick = lane == first
            sel = sel | pick
            cur = jnp.where(pick, -jnp.inf, cur)
        pad = jnp.zeros((nq, hw), F32)
        kn = jnp.concatenate([kn_ref[0], pad], axis=0)
        vn = jnp.concatenate([vn_ref[0], pad], axis=0)
        s2 = lax.dot_general(q_sc[...], kn.astype(BF16), NT_DIMS, preferred_element_type=F32)
        s_own = s2[0:nr] + s2[nr:2 * nr]
        qpos = lax.broadcasted_iota(jnp.int32, (nr, 2 * nq), 0) % nq
        kpos = lax.broadcasted_iota(jnp.int32, (nr, 2 * nq), 1)
        s_own = jnp.where(kpos <= qpos, s_own, NEG)
        m_own = jnp.max(s_own, axis=1, keepdims=True)
        p_own = jnp.exp(s_own - m_own)
        l_own = jnp.sum(p_own, axis=1, keepdims=True)
        acc_own = to_queries(jnp.dot(p_own, vn, preferred_element_type=F32))
        m_all = m_sc[...]
        m_tot = jnp.maximum(jnp.max(jnp.where(sel, m_all, NEG), axis=1, keepdims=True), m_own)
        w = jnp.where(sel, jnp.exp(m_all - m_tot), 0.0)
        w_own = jnp.exp(m_own - m_tot)
        l_tot = jnp.sum(w * l_sc[...], axis=1, keepdims=True) + w_own * l_own

        def widen(col):
            return to_queries(jnp.broadcast_to(col, (nr, hw)))

        num = widen(w_own) * acc_own
        for n in range(nb):
            num = num + widen(w[:, n:n + 1]) * acc_sc[n]
        o_ref[0] = num / widen(l_tot)


def _moba_sample(qa, ka, va, cache_k, cache_v, page_table):
    b, nq, hw = qa.shape
    nh = HEADS_PER_GROUP
    n_pool, page = cache_k.shape[0], cache_k.shape[1]
    pages_per_blk = MOBA_BLOCK // page
    assert pages_per_blk == 2 and page == LANES
    nb = page_table.shape[1] // pages_per_blk
    assert MOBA_TOPK <= nb <= LANES
    ck = cache_k.transpose(0, 2, 3, 1).reshape(1, n_pool, hw, page)
    cv = cache_v.transpose(0, 2, 3, 1).reshape(1, n_pool, hw, page)
    tok = lambda i, j, pt: (i, 0, 0)
    per_step = next(g for g in (4, 2, 1) if nb % g == 0)
    nsteps = nb // per_step

    def page_spec(p):
        return pl.BlockSpec((1, 1, hw, page),
                            lambda i, j, pt: (0, pt[i, jnp.minimum(j, nsteps - 1) * 2 * per_step + p], 0, 0))

    pages = [page_spec(p) for p in range(2 * per_step)]
    grid_spec = pltpu.PrefetchScalarGridSpec(
        num_scalar_prefetch=1,
        grid=(b, nsteps + 1),
        in_specs=[
            pl.BlockSpec((1, nq, hw), tok),
            pl.BlockSpec((1, nq, hw), tok),
            pl.BlockSpec((1, nq, hw), tok),
        ] + pages + pages,
        out_specs=pl.BlockSpec((1, nq, hw), tok),
        scratch_shapes=[
            pltpu.VMEM((2 * nh * nq, hw), BF16),
            pltpu.VMEM((nh * nq, LANES), F32),
            pltpu.VMEM((nh * nq, LANES), F32),
            pltpu.VMEM((nh * nq, LANES), F32),
            pltpu.VMEM((nb, nq, hw), F32),
        ],
    )
    return pl.pallas_call(
        functools.partial(_moba_sample_kernel, blocks_per_step=per_step),
        grid_spec=grid_spec,
        out_shape=jax.ShapeDtypeStruct((b, nq, hw), F32),
        compiler_params=_params("arbitrary", "arbitrary"),
        name="moba_sample",
    )(page_table, qa, ka, va, *([ck] * (2 * per_step)), *([cv] * (2 * per_step)))


def _gla_kernel(q_ref, k_ref, v_ref, la_ref, gs_ref, gn_ref, s0_ref, o_ref, sout_ref, st_sc, *, chunk, mm_dtype):
    c = chunk
    n = q_ref.shape[1] // c
    npair = q_ref.shape[2] // LANES
    hd = HEAD_DIM
    lane = lax.broadcasted_iota(jnp.int32, (c, LANES), 1)
    head0 = lane < hd
    r = lax.broadcasted_iota(jnp.int32, (c, c), 0)
    cc = lax.broadcasted_iota(jnp.int32, (c, c), 1)
    causal = r >= cc
    tri = jnp.where(causal, 1.0, 0.0)
    br = lax.broadcasted_iota(jnp.int32, (LANES, LANES), 0) // hd
    bc = lax.broadcasted_iota(jnp.int32, (LANES, LANES), 1) // hd
    same_head = br == bc
    head_mean = jnp.where(same_head, 1.0 / hd, 0.0)
    gn = gn_ref[...]

    zero = jnp.zeros((hd, hd), F32)
    for p in range(npair):
        st_sc[p] = jnp.concatenate([jnp.concatenate([s0_ref[0, 2 * p], zero], axis=1),
                                    jnp.concatenate([zero, s0_ref[0, 2 * p + 1]], axis=1)], axis=0)

    def split_dot(a, b_mat, stack_axis):
        hi = a.astype(BF16)
        lo = (a - hi.astype(F32)).astype(BF16)
        both = jnp.concatenate([hi, lo], axis=stack_axis)
        if stack_axis == 1:
            r = jnp.dot(b_mat, both, preferred_element_type=F32)
            return r[:, 0:LANES] + r[:, LANES:2 * LANES]
        r = jnp.dot(both, b_mat, preferred_element_type=F32)
        return r[0:c] + r[c:2 * c]

    def chunk(ci, p):
        rows = pl.ds(pl.multiple_of(ci * c, c), c)
        lanes = slice(p * LANES, (p + 1) * LANES)
        la = la_ref[0, rows, lanes]
        q = q_ref[0, rows, lanes]
        k = k_ref[0, rows, lanes]
        v = v_ref[0, rows, lanes]
        if mm_dtype == BF16:
            b = split_dot(la, tri.astype(BF16), 1)
        else:
            b = jnp.dot(tri, la, precision=HIGHEST, preferred_element_type=F32)
        st = st_sc[p]
        b_last = b[c - 1:c, :]
        b_mid = b[c // 2 - 1:c // 2, :]
        qt = q * jnp.exp(b - b_mid)
        kt = k * jnp.exp(b_mid - b)
        qts = jnp.concatenate([jnp.where(head0, qt, 0.0), jnp.where(head0, 0.0, qt)], axis=0)
        a = lax.dot_general(qts.astype(mm_dtype), kt.astype(mm_dtype), NT_DIMS, preferred_element_type=F32)
        a0 = jnp.where(causal, a[0:c], 0.0).astype(mm_dtype)
        a1 = jnp.where(causal, a[c:2 * c], 0.0).astype(mm_dtype)
        vm = v.astype(mm_dtype)
        o = jnp.where(head0, jnp.dot(a0, vm, preferred_element_type=F32),
                      jnp.dot(a1, vm, preferred_element_type=F32))
        qe = q * jnp.exp(b)
        o = o + lax.dot_general(qe.astype(mm_dtype), st.astype(mm_dtype), NT_DIMS, preferred_element_type=F32)
        kd = k * jnp.exp(b_last - b)
        upd = lax.dot_general(vm, kd.astype(mm_dtype), TN_DIMS, preferred_element_type=F32)
        st_sc[p] = st * jnp.exp(b_last) + jnp.where(same_head, upd, 0.0)
        if mm_dtype == BF16:
            ms = split_dot(o * o, head_mean.astype(BF16), 0)
        else:
            ms = jnp.dot(o * o, head_mean, precision=HIGHEST, preferred_element_type=F32)
        o_ref[0, rows, lanes] = o * lax.rsqrt(ms + EPS) * gn * gs_ref[0, rows, lanes]

    unroll = next(u for u in (8, 4, 2, 1) if n % u == 0)

    def body(i, carry):
        for u in range(unroll):
            for p in range(npair):
                chunk(i * unroll + u, p)
        return carry

    lax.fori_loop(0, n // unroll, body, 0)
    for p in range(npair):
        st = st_sc[p]
        sout_ref[0, 2 * p] = st[0:hd, 0:hd]
        sout_ref[0, 2 * p + 1] = st[hd:2 * hd, hd:2 * hd]


def _gla(qg, kg, vg, la, gs, gn_pair, s0_t, chunk, mm_dtype):
    b, l, hw = qg.shape
    pairs = hw // LANES
    npair = pairs if l <= GLA_CHUNK else 1
    seq = pl.BlockSpec((1, l, npair * LANES), lambda i, p: (i, 0, p))
    st_spec = pl.BlockSpec((1, 2 * npair, HEAD_DIM, HEAD_DIM), lambda i, p: (i, p, 0, 0))
    return pl.pallas_call(
        functools.partial(_gla_kernel, chunk=chunk, mm_dtype=mm_dtype),
        grid=(b, pairs // npair),
        in_specs=[seq, seq, seq, seq, seq, pl.BlockSpec((1, LANES), lambda i, p: (0, 0)), st_spec],
        out_specs=[seq, st_spec],
        out_shape=[jax.ShapeDtypeStruct((b, l, hw), F32),
                   jax.ShapeDtypeStruct((b, HEADS_PER_GROUP, HEAD_DIM, HEAD_DIM), F32)],
        scratch_shapes=[pltpu.VMEM((npair, LANES, LANES), F32)],
        compiler_params=_params("arbitrary", "arbitrary"),
        name="gla_chunked",
    )(qg, kg, vg, la, gs, gn_pair, s0_t)


def _mix_ffn_kernel(x_ref, oa_ref, og_ref, mod_ref, g_ref, wo_a_ref, wo_g_ref, wg_ref, wu_ref, wd_ref, o_ref,
                    x1_sc, h_sc, acc_sc):
    j = pl.program_id(2)
    bb, tl, d = x_ref.shape
    tm = bb * tl

    @pl.when(j == 0)
    def _():
        m = mod_ref[...]
        mix = (_mm(oa_ref[...].reshape(tm, -1), wo_a_ref[...])
               + _mm(og_ref[...].reshape(tm, -1), wo_g_ref[...]))
        x1 = x_ref[...] + m[:, 2:3, :] * mix.reshape(bb, tl, d)
        h = _rms(x1, g_ref[...]) * (1.0 + m[:, 4:5, :]) + m[:, 3:4, :]
        x1_sc[...] = x1.reshape(tm, d)
        h_sc[...] = h.reshape(tm, d).astype(h_sc.dtype)
        acc_sc[...] = jnp.zeros_like(acc_sc)

    h = h_sc[...]
    fc = wg_ref.shape[1]
    for c0 in range(0, fc, FFN_COL_CHUNK):
        cols = slice(c0, min(c0 + FFN_COL_CHUNK, fc))
        acc_sc[...] += _mm(_silu(_mm(h, wg_ref[:, cols])) * _mm(h, wu_ref[:, cols]), wd_ref[cols, :])

    @pl.when(j == pl.num_programs(2) - 1)
    def _():
        g2 = mod_ref[...][:, 5:6, :]
        o_ref[...] = x1_sc[...].reshape(bb, tl, d) + g2 * acc_sc[...].reshape(bb, tl, d)


def _mix_ffn(x, oa, og, mod, g, wo_a, wo_g, wg, wu, wd):
    b, l, d = x.shape
    hw = oa.shape[-1]
    f = wg.shape[1]
    bb, tl = _token_tiles(b, l, 512 if l >= 512 else 256)
    tm = bb * tl
    fc = f if wg.dtype == BF16 else 256
    once = dict(pipeline_mode=pl.Buffered(1)) if fc == f else {}
    tok = lambda i, t, j: (i, t, 0)
    const2 = lambda i, t, j: (0, 0)
    return pl.pallas_call(
        _mix_ffn_kernel,
        grid=(b // bb, l // tl, f // fc),
        in_specs=[
            pl.BlockSpec((bb, tl, d), tok),
            pl.BlockSpec((bb, tl, hw), tok),
            pl.BlockSpec((bb, tl, hw), tok),
            pl.BlockSpec((bb, 6, d), lambda i, t, j: (i, 0, 0)),
            pl.BlockSpec((1, d), const2),
            pl.BlockSpec(wo_a.shape, const2),
            pl.BlockSpec(wo_g.shape, const2),
            pl.BlockSpec((d, fc), lambda i, t, j: (0, j), **once),
            pl.BlockSpec((d, fc), lambda i, t, j: (0, j), **once),
            pl.BlockSpec((fc, d), lambda i, t, j: (j, 0), **once),
        ],
        out_specs=pl.BlockSpec((bb, tl, d), tok),
        out_shape=jax.ShapeDtypeStruct((b, l, d), F32),
        scratch_shapes=[pltpu.VMEM((tm, d), F32), pltpu.VMEM((tm, d), wg.dtype), pltpu.VMEM((tm, d), F32)],
        compiler_params=_params("arbitrary", "arbitrary", "arbitrary"),
        name="outproj_ffn",
    )(x, oa, og, mod, g, wo_a, wo_g, wg, wu, wd)


def _cm_in_kernel(x_ref, mod_ref, g_ref, w_ref, lng_ref, lnb_ref, u_ref, v_ref):
    bb, tl, d = x_ref.shape
    tm = bb * tl
    cw = u_ref.shape[-1]
    m = mod_ref[...]
    h = _rms(x_ref[...], g_ref[...]) * (1.0 + m[:, 1:2, :]) + m[:, 0:1, :]
    hb = h.reshape(tm, d).astype(w_ref.dtype)
    u_ref[...] = _gelu_tanh(_mm(hb, w_ref[:, 0:cw])).reshape(bb, tl, cw)
    zv = _gelu_tanh(_mm(hb, w_ref[:, cw:2 * cw]))
    mu = jnp.mean(zv, axis=-1, keepdims=True)
    zc = zv - mu
    var = jnp.mean(zc * zc, axis=-1, keepdims=True)
    v_ref[...] = (zc * lax.rsqrt(var + EPS) * lng_ref[...] + lnb_ref[...]).reshape(bb, tl, cw)


def _cm_in(x, mod, g, w, ln_g, ln_b):
    b, l, d = x.shape
    cw = w.shape[1] // 2
    bb, tl = _token_tiles(b, l, 512 if l >= 512 else 256)
    tok = lambda i, t: (i, t, 0)
    const2 = lambda i, t: (0, 0)
    out_spec = pl.BlockSpec((bb, tl, cw), tok)
    out_sds = jax.ShapeDtypeStruct((b, l, cw), F32)
    return pl.pallas_call(
        _cm_in_kernel,
        grid=(b // bb, l // tl),
        in_specs=[
            pl.BlockSpec((bb, tl, d), tok),
            pl.BlockSpec((bb, 6, d), lambda i, t: (i, 0, 0)),
            pl.BlockSpec((1, d), const2),
            pl.BlockSpec(w.shape, const2),
            pl.BlockSpec((1, cw), const2),
            pl.BlockSpec((1, cw), const2),
        ],
        out_specs=[out_spec, out_spec],
        out_shape=[out_sds, out_sds],
        compiler_params=_params("arbitrary", "arbitrary"),
        name="cm_in_gelu_ln",
    )(x, mod, g, w, ln_g, ln_b)


def _cm_out_kernel(x_ref, u_ref, v_ref, mod_ref, g_ref, ws_ref, bs_ref, wo_ref, wr_ref,
                   x3_ref, h_ref, gates_ref, rank_ref, cnt_ref, us_sc):
    bb, tl, d = x_ref.shape
    tm = bb * tl
    cw = u_ref.shape[-1]
    ngroups, cm, _ = ws_ref.shape
    gd = cw // ngroups
    u = u_ref[...].reshape(tm, cw)
    v = v_ref[...].reshape(tm, cw)
    bs = bs_ref[...]
    mm_dtype = wo_ref.dtype
    r = lax.broadcasted_iota(jnp.int32, (cm, cm), 0)
    c = lax.broadcasted_iota(jnp.int32, (cm, cm), 1)
    for g in range(ngroups):
        wsg = jnp.where(r >= c, ws_ref[g], 0.0).astype(mm_dtype)
        cols = slice(g * gd, (g + 1) * gd)
        for sb in range(tm // cm):
            rows = slice(sb * cm, (sb + 1) * cm)
            s = _mm(wsg, v[rows, cols].astype(mm_dtype)) + bs[rows, g:g + 1]
            us_sc[rows, cols] = (u[rows, cols] * s).astype(mm_dtype)
    y = _mm(us_sc[...], wo_ref[...])
    m = mod_ref[...]
    x3 = x_ref[...] + m[:, 2:3, :] * y.reshape(bb, tl, d)
    x3_ref[...] = x3
    h = (_rms(x3, g_ref[...]) * (1.0 + m[:, 4:5, :]) + m[:, 3:4, :]).reshape(tm, d)
    h_ref[...] = h.astype(BF16)
    wr = wr_ref[...]
    if mm_dtype == BF16:
        h_hi = h.astype(BF16)
        h_lo = (h - h_hi.astype(F32)).astype(BF16)
        wr_hi = wr.astype(BF16)
        wr_lo = (wr - wr_hi.astype(F32)).astype(BF16)
        both = jnp.dot(jnp.concatenate([h_hi, h_lo], axis=0), wr_hi, preferred_element_type=F32)
        logits = both[0:tm] + both[tm:2 * tm] + jnp.dot(h_hi, wr_lo, preferred_element_type=F32)
    else:
        logits = jnp.dot(h, wr, precision=HIGHEST, preferred_element_type=F32)
    ne = logits.shape[1]
    col = lax.broadcasted_iota(jnp.int32, (tm, ne), 1)
    m1 = jnp.max(logits, axis=1, keepdims=True)
    i1 = jnp.min(jnp.where(logits == m1, col, ne), axis=1, keepdims=True)
    rest = jnp.where(col == i1, -jnp.inf, logits)
    m2 = jnp.max(rest, axis=1, keepdims=True)
    i2 = jnp.min(jnp.where(rest == m2, col, ne), axis=1, keepdims=True)
    e2 = jnp.exp(m2 - m1)
    gates = jnp.where(col == i1, 1.0 / (1.0 + e2), 0.0) + jnp.where(col == i2, e2 / (1.0 + e2), 0.0)
    gates_ref[...] = gates
    routed = gates > 0.0
    tr = lax.broadcasted_iota(jnp.int32, (tm, tm), 0)
    tc = lax.broadcasted_iota(jnp.int32, (tm, tm), 1)
    upto = jnp.dot(jnp.where(tr >= tc, 1.0, 0.0).astype(BF16), jnp.where(routed, 1.0, 0.0).astype(BF16),
                   preferred_element_type=F32)
    rank_ref[...] = jnp.where(routed, upto - 1.0, -1.0)
    cnt_ref[0] = upto[tm - 1:tm, :]


def _cm_out(x, u, v, mod, g, ws_mix, bs_rows, wo, wr, cm):
    b, l, d = x.shape
    cw = u.shape[-1]
    ne = wr.shape[1]
    bb, tl = _token_tiles(b, l, 256)
    tm = bb * tl
    tok = lambda i, t: (i, t, 0)
    flat = lambda i, t: (i * (l // tl) + t, 0)
    const2 = lambda i, t: (0, 0)
    return pl.pallas_call(
        _cm_out_kernel,
        grid=(b // bb, l // tl),
        in_specs=[
            pl.BlockSpec((bb, tl, d), tok),
            pl.BlockSpec((bb, tl, cw), tok),
            pl.BlockSpec((bb, tl, cw), tok),
            pl.BlockSpec((bb, 6, d), lambda i, t: (i, 0, 0)),
            pl.BlockSpec((1, d), const2),
            pl.BlockSpec(ws_mix.shape, lambda i, t: (0, 0, 0)),
            pl.BlockSpec((tm, bs_rows.shape[1]), const2),
            pl.BlockSpec(wo.shape, const2),
            pl.BlockSpec(wr.shape, const2),
        ],
        out_specs=[pl.BlockSpec((bb, tl, d), tok), pl.BlockSpec((tm, d), flat), pl.BlockSpec((tm, ne), flat),
                   pl.BlockSpec((tm, ne), flat),
                   pl.BlockSpec((1, 1, ne), lambda i, t: (i * (l // tl) + t, 0, 0))],
        out_shape=[jax.ShapeDtypeStruct((b, l, d), F32), jax.ShapeDtypeStruct((b * l, d), BF16),
                   jax.ShapeDtypeStruct((b * l, ne), F32), jax.ShapeDtypeStruct((b * l, ne), F32),
                   jax.ShapeDtypeStruct((b * l // tm, 1, ne), F32)],
        scratch_shapes=[pltpu.VMEM((tm, cw), wo.dtype)],
        compiler_params=_params("arbitrary", "arbitrary"),
        name="cm_out_router",
    )(x, u, v, mod, g, ws_mix, bs_rows, wo, wr)


MOE_SUB = 256
MOE_ALIGN = 16
MOE_WIN_SMALL = 128
MOE_WIN_FULL = MOE_SUB + MOE_ALIGN
MOE_ROW_CHUNK = 512
MOE_FFN_ROWS = 640
MOE_BLOCK_SUBS = 10
MOE_HIDDEN_CHUNK = 512


def _moe_routing(rank, cnt, nb, nsub):
    ne = rank.shape[1]
    rank = rank.astype(jnp.int32).reshape(nb, nsub, MOE_SUB, ne)
    cnt = cnt.astype(jnp.int32).reshape(nb, nsub, ne)
    off = jnp.cumsum(cnt, axis=1) - cnt
    total = jnp.sum(cnt, axis=1)
    start = (off // MOE_ALIGN) * MOE_ALIGN
    shift = off - start
    win_row = jnp.where(rank >= 0, rank + shift[:, :, None, :], -1)
    win_row = win_row.transpose(0, 3, 1, 2).reshape(nb, ne, nsub * MOE_SUB)
    used = jnp.where(cnt > 0, shift + cnt, 0)
    scalars = jnp.concatenate([start.transpose(0, 2, 1).reshape(-1), used.transpose(0, 2, 1).reshape(-1),
                               total.reshape(-1)]).astype(jnp.int32)
    return win_row, scalars


def _moe_kernel(sc_ref, h_ref, gates_ref, row_ref, wg_ref, wu_ref, wd_ref, o_ref, x_sc, y_sc):
    b = pl.program_id(0)
    e = pl.program_id(1)
    j = pl.program_id(2)
    ne = pl.num_programs(1)
    tb = h_ref.shape[0]
    nsub = tb // MOE_SUB
    cap = x_sc.shape[0]
    n_be = pl.num_programs(0) * ne
    seg = (b * ne + e) * nsub
    total = sc_ref[2 * n_be * nsub + b * ne + e]

    def windows(fn):
        for s in range(nsub):
            start = pl.multiple_of(sc_ref[seg + s], MOE_ALIGN)
            used = sc_ref[n_be * nsub + seg + s]
            tok = slice(s * MOE_SUB, (s + 1) * MOE_SUB)
            win_row = row_ref[0, pl.ds(e, 1), tok]

            def run(rows, start=start, tok=tok, win_row=win_row):
                ridx = lax.broadcasted_iota(jnp.int32, (rows, MOE_SUB), 0)
                fn(pl.ds(start, rows), tok, jnp.where(ridx == win_row, 1.0, 0.0).astype(BF16))

            pl.when((used > 0) & (used <= MOE_WIN_SMALL))(functools.partial(run, MOE_WIN_SMALL))
            pl.when(used > MOE_WIN_SMALL)(functools.partial(run, MOE_WIN_FULL))

    @pl.when((e == 0) & (j == 0))
    def _():
        o_ref[...] = jnp.zeros_like(o_ref)

    @pl.when(j == 0)
    def _():
        def clear(c, carry):
            rows = pl.ds(pl.multiple_of(c * MOE_ROW_CHUNK, MOE_ROW_CHUNK), MOE_ROW_CHUNK)
            x_sc[rows, :] = jnp.zeros((MOE_ROW_CHUNK, x_sc.shape[1]), BF16)
            y_sc[rows, :] = jnp.zeros((MOE_ROW_CHUNK, y_sc.shape[1]), F32)
            return carry

        lax.fori_loop(0, jnp.minimum((total + MOE_WIN_FULL + MOE_ROW_CHUNK - 1) // MOE_ROW_CHUNK,
                                     cap // MOE_ROW_CHUNK), clear, 0)

        def dispatch(rows, tok, onehot):
            picked = jnp.dot(onehot, h_ref[tok, :], preferred_element_type=F32).astype(BF16)
            x_sc[rows, :] = x_sc[rows, :] + picked

        windows(dispatch)

    def ffn(r0, nrows):
        rows = pl.ds(pl.multiple_of(r0, MOE_WIN_SMALL), nrows)
        x = x_sc[rows, :]
        gate = jnp.dot(x, wg_ref[0], preferred_element_type=F32)
        up = jnp.dot(x, wu_ref[0], preferred_element_type=F32)
        act = (_silu(gate) * up).astype(BF16)
        y_sc[rows, :] += jnp.dot(act, wd_ref[0], preferred_element_type=F32)

    nfull = (jnp.maximum(total - 512, 0) + MOE_FFN_ROWS - 1) // MOE_FFN_ROWS
    rem = jnp.maximum(total - nfull * MOE_FFN_ROWS, 0)
    tail = nfull * MOE_FFN_ROWS

    def full_chunk(c, carry):
        ffn(c * MOE_FFN_ROWS, MOE_FFN_ROWS)
        return carry

    lax.fori_loop(0, nfull, full_chunk, 0)
    take_256 = (rem > 128) & (rem <= 384)
    pl.when(rem > 384)(lambda: ffn(tail, 512))
    pl.when(take_256)(lambda: ffn(tail, 256))
    pl.when((rem > 0) & (rem <= 128))(lambda: ffn(tail, 128))
    pl.when((rem > 256) & (rem <= 384))(lambda: ffn(tail + 256, 128))

    @pl.when(j == pl.num_programs(2) - 1)
    def _():
        gates = gates_ref[...]
        col = lax.broadcasted_iota(jnp.int32, gates.shape, 1)
        ge = jnp.sum(jnp.where(col == e, gates, 0.0), axis=1, keepdims=True)

        def combine(rows, tok, onehot):
            back = lax.dot_general(onehot, y_sc[rows, :].astype(BF16), TN_DIMS, preferred_element_type=F32)
            o_ref[tok, :] = (o_ref[tok, :].astype(F32) + ge[tok] * back).astype(o_ref.dtype)

        windows(combine)


def _moe(h, gates, rank, cnt, wg, wu, wd, tb):
    t, d = h.shape
    ne, _, f = wg.shape
    fc = MOE_HIDDEN_CHUNK
    assert t % tb == 0 and tb % MOE_SUB == 0 and f % fc == 0
    nb, nsub = t // tb, tb // MOE_SUB
    win_row, scalars = _moe_routing(rank, cnt, nb, nsub)
    cap = -(-(tb + MOE_WIN_FULL) // MOE_ROW_CHUNK) * MOE_ROW_CHUNK
    once = dict(pipeline_mode=pl.Buffered(1))
    grid_spec = pltpu.PrefetchScalarGridSpec(
        num_scalar_prefetch=1,
        grid=(nb, ne, f // fc),
        in_specs=[
            pl.BlockSpec((tb, d), lambda i, e, j, sc: (i, 0), **once),
            pl.BlockSpec((tb, ne), lambda i, e, j, sc: (i, 0), **once),
            pl.BlockSpec((1, ne, tb), lambda i, e, j, sc: (i, 0, 0), **once),
            pl.BlockSpec((1, d, fc), lambda i, e, j, sc: (e, 0, j)),
            pl.BlockSpec((1, d, fc), lambda i, e, j, sc: (e, 0, j)),
            pl.BlockSpec((1, fc, d), lambda i, e, j, sc: (e, j, 0)),
        ],
        out_specs=pl.BlockSpec((tb, d), lambda i, e, j, sc: (i, 0), **once),
        scratch_shapes=[pltpu.VMEM((cap, d), BF16), pltpu.VMEM((cap, d), F32)],
    )
    return pl.pallas_call(
        _moe_kernel,
        grid_spec=grid_spec,
        out_shape=jax.ShapeDtypeStruct((t, d), BF16),
        compiler_params=_params("arbitrary", "arbitrary", "arbitrary"),
        name="moe_experts",
    )(scalars, h, gates, win_row, wg, wu, wd)


def _final_kernel(x_ref, moe_ref, mod_ref, g_ref, o_ref):
    bb, tl, d = x_ref.shape
    g2 = mod_ref[...][:, 5:6, :]
    x4 = x_ref[...] + g2 * moe_ref[...].reshape(bb, tl, d)
    o_ref[...] = _rms(x4, g_ref[...])


def _final(x, moe, row0, mod, g):
    b, l, d = x.shape
    bb, tl = _token_tiles(b, l, 512 if l >= 512 else 256)
    tm = bb * tl
    assert row0 % tm == 0
    tok = lambda i, t: (i, t, 0)
    return pl.pallas_call(
        _final_kernel,
        grid=(b // bb, l // tl),
        in_specs=[
            pl.BlockSpec((bb, tl, d), tok),
            pl.BlockSpec((tm, d), lambda i, t: (row0 // tm + i * (l // tl) + t, 0)),
            pl.BlockSpec((bb, 6, d), lambda i, t: (i, 0, 0)),
            pl.BlockSpec((1, d), lambda i, t: (0, 0)),
        ],
        out_specs=pl.BlockSpec((bb, tl, d), tok),
        out_shape=jax.ShapeDtypeStruct((b, l, d), F32),
        compiler_params=_params("arbitrary", "arbitrary"),
        name="final_norm",
    )(x, moe, mod, g)


def _layer0(x, mod, pos0, past, w):
    b, l, d = x.shape
    hw = HEAD_WIDTH
    outs = _in_proj(x, mod, w["norm_mix_g0"], w["w_in_main"], w["w_in_lr"], w["wa2_hi"], w["wa2_lo"], w["ba2"], pos0)
    qa, ka, va, qg, kg, vg, gs, la = outs
    if past is None:
        oa = _moba_prompt(qa, ka, va)
        s0_t = jnp.zeros((b, HEADS_PER_GROUP, HEAD_DIM, HEAD_DIM), F32)
    else:
        cache_k, cache_v, state, page_table = past
        oa = _moba_sample(qa, ka, va, cache_k, cache_v, page_table)
        s0_t = jnp.swapaxes(state, -1, -2)
    chunk = GLA_CHUNK if l % GLA_CHUNK == 0 else l
    mm_dtype = BF16 if chunk % 16 == 0 else F32
    og, s_t = _gla(qg, kg, vg, la, gs, w["gla_norm_pair"], s0_t, chunk, mm_dtype)
    x2 = _mix_ffn(x, oa, og, mod, w["norm_ffn_g0"], w["wo_a"], w["wo_g"], w["ffn_wg"], w["ffn_wu"], w["ffn_wd"])
    k_rows = ka.reshape(b, l, HEADS_PER_GROUP, HEAD_DIM)
    v_rows = va.reshape(b, l, HEADS_PER_GROUP, HEAD_DIM)
    return x2, k_rows, v_rows, jnp.swapaxes(s_t, -1, -2)


def _layer1_front(x, mod, w):
    b, l, d = x.shape
    u, v = _cm_in(x, mod, w["norm_mix_g1"], w["cm_w_in"], w["cm_ln_g"], w["cm_ln_b"])
    bb, tl = _token_tiles(b, l, 256)
    tm = bb * tl
    if l >= CM_CHUNK:
        cm = CM_CHUNK
        ws_mix = w["cm_w_s"]
        bs_rows = jnp.tile(w["cm_b_s"].T, (tm // CM_CHUNK, 1))
    else:
        cm = tm
        pos = jnp.arange(tm, dtype=jnp.int32)
        same_seq = (pos[:, None] // l == pos[None, :] // l).astype(F32)
        place = (pos[:, None] % l == jnp.arange(l, dtype=jnp.int32)[None, :]).astype(F32)
        ws_mix = jnp.einsum("ra,gab,cb->grc", place, w["cm_w_s"][:, :l, :l], place,
                            precision=HIGHEST) * same_seq[None]
        bs_rows = jnp.tile(w["cm_b_s"][:, :l].T, (tm // l, 1))
    assert tm == MOE_SUB
    x3, h, gates, rank, cnt = _cm_out(x, u, v, mod, w["norm_ffn_g1"], ws_mix, bs_rows, w["cm_w_out"],
                                      w["moe_w_router"], cm)
    start = ((l - 1) // CM_CHUNK) * CM_CHUNK
    return x3, (h, gates, rank, cnt), v[:, start:]


def kernel(x_prompt, x_sample, cache_k, cache_v, state_gla, page_table, c_prompt, c_sample, ada_w, ada_b, norm_mix_g, norm_ffn_g, ab_w_in, gla_w_a2, gla_b_a2, gla_norm_g, ab_w_out, ffn_w_gate, ffn_w_up, ffn_w_down, cm_w_in, cm_ln_g, cm_ln_b, cm_w_s, cm_b_s, cm_w_out, moe_w_router, moe_w_gate, moe_w_up, moe_w_down, final_norm_g):
    bp, lp, d = x_prompt.shape
    bs, ls, _ = x_sample.shape
    hw = HEAD_WIDTH
    n_main = 7 * hw
    past_len = page_table.shape[1] * cache_k.shape[2]

    wa2 = jnp.pad(gla_w_a2[0], ((0, LANES - GLA_LOWRANK), (0, 0)))
    wa2_hi = wa2.astype(BF16)
    shared = {
        "norm_mix_g0": norm_mix_g[0][None], "norm_mix_g1": norm_mix_g[1][None],
        "norm_ffn_g0": norm_ffn_g[0][None], "norm_ffn_g1": norm_ffn_g[1][None],
        "wa2_hi": wa2_hi, "wa2_lo": (wa2 - wa2_hi.astype(F32)).astype(BF16),
        "ba2": gla_b_a2[0][None],
        "gla_norm_pair": jnp.tile(gla_norm_g[0], LANES // HEAD_DIM)[None],
        "cm_ln_g": cm_ln_g[0][None], "cm_ln_b": cm_ln_b[0][None],
        "cm_w_s": cm_w_s[0], "cm_b_s": cm_b_s[0],
        "moe_w_router": moe_w_router[0],
    }
    mats = {
        "w_in_main": ab_w_in[0],
        "w_in_lr": jnp.pad(ab_w_in[0][:, n_main:], ((0, 0), (0, LANES - GLA_LOWRANK))),
        "wo_a": ab_w_out[0][:hw], "wo_g": ab_w_out[0][hw:],
        "ffn_wg": ffn_w_gate[0], "ffn_wu": ffn_w_up[0], "ffn_wd": ffn_w_down[0],
        "cm_w_in": cm_w_in[0], "cm_w_out": cm_w_out[0],
    }
    w = dict(shared, **{k: v.astype(BF16) for k, v in mats.items()})
    w_sample = dict(shared, **mats)

    rows = bp + bs
    rows_pad = -(-rows // 16) * 16
    c_all = jnp.pad(jnp.concatenate([c_prompt, c_sample], axis=0), ((0, rows_pad - rows), (0, 0)))
    mod_all = _modulation(c_all, ada_w, ada_b)
    mod_p = [mod_all[l, :bp].reshape(bp, 6, d) for l in range(2)]
    mod_s = [mod_all[l, bp:rows].reshape(bs, 6, d) for l in range(2)]

    x2_p, k_p, v_p, gla_p = _layer0(x_prompt, mod_p[0], 0, None, w)
    x2_s, k_s, v_s, gla_s = _layer0(x_sample, mod_s[0], past_len,
                                    (cache_k[0], cache_v[0], state_gla[0], page_table), w_sample)

    x3_p, route_p, cmv_p = _layer1_front(x2_p, mod_p[1], w)
    x3_s, route_s, cmv_s = _layer1_front(x2_s, mod_s[1], w_sample)

    t_all = bp * lp + bs * ls
    subs = -(-t_all // MOE_SUB)
    n_blocks = -(-subs // MOE_BLOCK_SUBS)
    tb = -(-subs // n_blocks) * MOE_SUB
    pad = n_blocks * tb - t_all
    ne = moe_w_gate.shape[1]
    pads = (jnp.zeros((pad, d), BF16), jnp.zeros((pad, ne), F32), jnp.full((pad, ne), -1.0, F32),
            jnp.zeros((pad // MOE_SUB, 1, ne), F32))
    h_all, gates_all, rank_all, cnt_all = (jnp.concatenate(parts, axis=0) for parts in zip(route_p, route_s, pads))
    moe = _moe(h_all, gates_all, rank_all, cnt_all, moe_w_gate[0].astype(BF16), moe_w_up[0].astype(BF16),
               moe_w_down[0].astype(BF16), tb)

    fg = final_norm_g[None]
    y_p = _final(x3_p, moe, 0, mod_p[1], fg)
    y_s = _final(x3_s, moe, bp * lp, mod_s[1], fg)

    return (y_p, y_s, k_p[None], v_p[None], k_s[None], v_s[None], gla_p[None], gla_s[None],
            cmv_p[None], cmv_s[None])
```

```python
import functools

import jax
import jax.numpy as jnp
from jax import lax
from jax.experimental import pallas as pl
from jax.experimental.pallas import tpu as pltpu

F32 = jnp.float32
BF16 = jnp.bfloat16
HIGHEST = lax.Precision.HIGHEST

HEAD_DIM = 64
HEADS_PER_GROUP = 8
HEAD_WIDTH = HEADS_PER_GROUP * HEAD_DIM
MOBA_BLOCK = 256
MOBA_TOPK = 3
ROPE_THETA = 10000.0
GLA_LOWRANK = 16
GLA_GATE_NORM = 16.0
GLA_CHUNK = 64
CM_CHUNK = 128
CM_GROUPS = 8
MOE_TOP_K = 2
EPS = 1e-6
NEG = -1e30

LANES = 128
FFN_COL_CHUNK = 1024
VMEM_LIMIT_BYTES = 56 * 1024 * 1024

NT_DIMS = (((1,), (1,)), ((), ()))
TN_DIMS = (((0,), (0,)), ((), ()))


def _params(*sem):
    return pltpu.CompilerParams(dimension_semantics=sem, vmem_limit_bytes=VMEM_LIMIT_BYTES)


def _rms(x, g):
    return x * lax.rsqrt(jnp.mean(x * x, axis=-1, keepdims=True) + EPS) * g


def _silu(x):
    return x * jax.nn.sigmoid(x)


def _split(x):
    hi = x.astype(BF16)
    return hi, (x - hi.astype(F32)).astype(BF16)


def _mm(a, w):
    if w.dtype == F32:
        m = a.shape[0]
        a_hi, a_lo = _split(a.astype(F32))
        w_hi, w_lo = _split(w)
        both = jnp.dot(jnp.concatenate([a_hi, a_lo], axis=0), w_hi, preferred_element_type=F32)
        return both[0:m] + both[m:2 * m] + jnp.dot(a_hi, w_lo, preferred_element_type=F32)
    return jnp.dot(a.astype(BF16), w, preferred_element_type=F32)


def _gelu_tanh(x):
    c = 0.7978845608028654
    return x * (0.5 + 0.5 * jnp.tanh(x * (c + (c * 0.044715) * (x * x))))


def _token_tiles(b, l, target):
    if l >= target:
        assert l % target == 0
        return 1, target
    bb = min(b, target // l)
    assert target % l == 0 and b % bb == 0
    return bb, l


def _mod_kernel(c_ref, w_ref, b_ref, o_ref):
    s = _silu(c_ref[...])
    r = s.shape[0]
    s_hi = s.astype(BF16)
    s_lo = (s - s_hi.astype(F32)).astype(BF16)
    w = w_ref[0]
    w_hi = w.astype(BF16)
    w_lo = (w - w_hi.astype(F32)).astype(BF16)
    both = jnp.dot(jnp.concatenate([s_hi, s_lo], axis=0), w_hi, preferred_element_type=F32)
    o_ref[0] = both[0:r] + both[r:2 * r] + jnp.dot(s_hi, w_lo, preferred_element_type=F32) + b_ref[0]


def _modulation(c_all, ada_w, ada_b):
    r, d = c_all.shape
    depth, _, n = ada_w.shape
    tn = 1536
    return pl.pallas_call(
        _mod_kernel,
        grid=(depth, n // tn),
        in_specs=[
            pl.BlockSpec((r, d), lambda l, j: (0, 0)),
            pl.BlockSpec((1, d, tn), lambda l, j: (l, 0, j)),
            pl.BlockSpec((1, 1, tn), lambda l, j: (l, 0, j)),
        ],
        out_specs=pl.BlockSpec((1, r, tn), lambda l, j: (l, 0, j)),
        out_shape=jax.ShapeDtypeStruct((depth, r, n), F32),
        compiler_params=_params("arbitrary", "arbitrary"),
        name="adaln_modulation",
    )(c_all, ada_w, ada_b.reshape(depth, 1, n))


def _in_proj_kernel(x_ref, mod_ref, g_ref, w_ref, wlr_ref, wa2h_ref, wa2l_ref, ba2_ref, cos_ref, sin_ref,
                    qa_ref, ka_ref, va_ref, qg_ref, kg_ref, vg_ref, gs_ref, la_ref):
    bb, tl, d = x_ref.shape
    tm = bb * tl
    hw = HEAD_WIDTH
    m = mod_ref[...]
    h = _rms(x_ref[...], g_ref[...]) * (1.0 + m[:, 1:2, :]) + m[:, 0:1, :]
    hb = h.reshape(tm, d).astype(w_ref.dtype)

    def proj(c):
        return _mm(hb, w_ref[:, c * hw:(c + 1) * hw])

    cos = jnp.concatenate([cos_ref[...]] * (hw // LANES), axis=-1)[None]
    sin = jnp.concatenate([sin_ref[...]] * (hw // LANES), axis=-1)[None]
    lane = lax.broadcasted_iota(jnp.int32, (tm, hw), 1)
    first_half = (lane % HEAD_DIM) < (HEAD_DIM // 2)

    def rope(z):
        partner = jnp.where(first_half, pltpu.roll(z, hw - HEAD_DIM // 2, 1), pltpu.roll(z, HEAD_DIM // 2, 1))
        return z.reshape(bb, tl, hw) * cos + partner.reshape(bb, tl, hw) * sin

    scale = HEAD_DIM ** -0.5
    qa_ref[...] = rope(proj(0)) * scale
    ka_ref[...] = rope(proj(1))
    va_ref[...] = proj(2).reshape(bb, tl, hw)
    qg_ref[...] = (proj(3) * scale).reshape(bb, tl, hw)
    kg_ref[...] = proj(4).reshape(bb, tl, hw)
    vg_ref[...] = proj(5).reshape(bb, tl, hw)
    gs_ref[...] = _silu(proj(6)).reshape(bb, tl, hw)
    zlr = _mm(hb, wlr_ref[...])
    hi = zlr.astype(BF16)
    lo = (zlr - hi.astype(F32)).astype(BF16)
    pre = (jnp.dot(hi, wa2h_ref[...], preferred_element_type=F32)
           + jnp.dot(lo, wa2h_ref[...], preferred_element_type=F32)
           + jnp.dot(hi, wa2l_ref[...], preferred_element_type=F32)) + ba2_ref[...]
    log_sig = jnp.minimum(pre, 0.0) - jnp.log(1.0 + jnp.exp(-jnp.abs(pre)))
    la_ref[...] = (log_sig * (1.0 / GLA_GATE_NORM)).reshape(bb, tl, hw)


def _rope_tables(pos0, l):
    half = HEAD_DIM // 2
    inv = 1.0 / (ROPE_THETA ** (jnp.arange(half, dtype=F32) / half))
    ang = (pos0 + jnp.arange(l, dtype=jnp.int32)).astype(F32)[:, None] * inv[None, :]
    cos, sin = jnp.cos(ang), jnp.sin(ang)
    reps = LANES // HEAD_DIM
    cos_t = jnp.tile(jnp.concatenate([cos, cos], axis=-1), (1, reps))
    sin_t = jnp.tile(jnp.concatenate([-sin, sin], axis=-1), (1, reps))
    return cos_t, sin_t


def _in_proj(x, mod, g, w_main, w_lr, wa2_hi, wa2_lo, ba2, pos0):
    b, l, d = x.shape
    bb, tl = _token_tiles(b, l, 512 if l >= 512 else 256)
    hw = HEAD_WIDTH
    cos_t, sin_t = _rope_tables(pos0, l)
    tok = lambda i, j: (i, j, 0)
    const2 = lambda i, j: (0, 0)
    out_spec = pl.BlockSpec((bb, tl, hw), tok)
    out_sds = jax.ShapeDtypeStruct((b, l, hw), F32)
    return pl.pallas_call(
        _in_proj_kernel,
        grid=(b // bb, l // tl),
        in_specs=[
            pl.BlockSpec((bb, tl, d), tok),
            pl.BlockSpec((bb, 6, d), lambda i, j: (i, 0, 0)),
            pl.BlockSpec((1, d), const2),
            pl.BlockSpec((d, 7 * hw), const2),
            pl.BlockSpec(w_lr.shape, const2),
            pl.BlockSpec(wa2_hi.shape, const2),
            pl.BlockSpec(wa2_lo.shape, const2),
            pl.BlockSpec((1, hw), const2),
            pl.BlockSpec((tl, LANES), lambda i, j: (j, 0)),
            pl.BlockSpec((tl, LANES), lambda i, j: (j, 0)),
        ],
        out_specs=[out_spec] * 8,
        out_shape=[out_sds] * 8,
        compiler_params=_params("arbitrary", "arbitrary"),
        name="in_proj_rope",
    )(x, mod, g, w_main, w_lr, wa2_hi, wa2_lo, ba2, cos_t, sin_t)


def _moba_prompt_kernel(q_ref, k_ref, v_ref, o_ref, kb_sc, vb_sc, km_sc, qs_sc, s_sc, p_sc, m_sc, acc_sc):
    ti = pl.program_id(2)
    lk = k_ref.shape[1]
    nblk = lk // MOBA_BLOCK
    nblk8 = -(-nblk // 8) * 8
    tq = q_ref.shape[1]
    nsub = tq // MOBA_BLOCK

    @pl.when(ti == 0)
    def _():
        k = k_ref[0]
        v = v_ref[0]
        row = lax.broadcasted_iota(jnp.int32, (lk, LANES), 0)
        col = lax.broadcasted_iota(jnp.int32, (lk, LANES), 1)
        kb_sc[:, 0:LANES] = k.astype(BF16)
        kb_sc[:, LANES:2 * LANES] = jnp.where(row // MOBA_BLOCK == col, 1.0, 0.0).astype(BF16)
        vb_sc[0] = jnp.where(col < HEAD_DIM, v, 1.0).astype(BF16)
        vb_sc[1] = jnp.where(col >= HEAD_DIM, v, 1.0).astype(BF16)
        km_sc[...] = jnp.zeros_like(km_sc)
        km_sc[0:nblk, :] = jnp.mean(k.reshape(nblk, MOBA_BLOCK, LANES), axis=1)

    q = q_ref[0]
    lane = lax.broadcasted_iota(jnp.int32, (tq, LANES), 1)
    blk = lax.broadcasted_iota(jnp.int32, (nblk8, tq), 0)
    own = ti * nsub + lax.broadcasted_iota(jnp.int32, (nblk8, tq), 1) // MOBA_BLOCK
    past_blk = blk < own
    km = km_sc[...]
    for h in range(2):
        head_lanes = (lane < HEAD_DIM) if h == 0 else (lane >= HEAD_DIM)
        qh = jnp.where(head_lanes, q, 0.0)
        gate = lax.dot_general(km, qh, NT_DIMS, precision=HIGHEST, preferred_element_type=F32)[0:nblk8]
        gate = jnp.where(past_blk, gate, -jnp.inf)
        cnt = jnp.zeros((nblk8, tq), F32)
        for mrow in range(nblk):
            gm = gate[mrow:mrow + 1, :]
            beats = (gm > gate) | ((gm == gate) & (mrow < blk))
            cnt = cnt + jnp.where(beats, 1.0, 0.0)
        sel = (past_blk & (cnt < float(MOBA_TOPK))) | (blk == own)
        bias_t = jnp.concatenate(
            [jnp.where(sel, 0.0, NEG), jnp.full((LANES - nblk8, tq), NEG, F32)], axis=0)
        qs_sc[h, :, 0:LANES] = qh.astype(BF16)
        qs_sc[h, :, LANES:2 * LANES] = bias_t.T.astype(BF16)

    m_sc[...] = jnp.full_like(m_sc, NEG)
    acc_sc[...] = jnp.zeros_like(acc_sc)
    rc = 128
    rowi = lax.broadcasted_iota(jnp.int32, (rc, MOBA_BLOCK), 0)
    coli = lax.broadcasted_iota(jnp.int32, (rc, MOBA_BLOCK), 1)

    def scores(n, slot):
        start = pl.multiple_of(n * MOBA_BLOCK, MOBA_BLOCK)
        kblk = kb_sc[pl.ds(start, MOBA_BLOCK), :]
        for h in range(2):
            s_sc[slot, h] = lax.dot_general(qs_sc[h], kblk, NT_DIMS, preferred_element_type=F32)

    def consume(n, slot, own_sub):
        start = pl.multiple_of(n * MOBA_BLOCK, MOBA_BLOCK)
        for h in range(2):
            for r0 in range(0, tq, rc):
                rows = slice(r0, r0 + rc)
                s = s_sc[slot, h, rows, :]
                if own_sub is not None and r0 // MOBA_BLOCK == own_sub:
                    s = jnp.where(rowi + (r0 - own_sub * MOBA_BLOCK) < coli, NEG, s)
                m_old = m_sc[h, rows, :]
                row_max = jnp.max(jnp.maximum(s[:, 0:LANES], s[:, LANES:2 * LANES]), axis=1, keepdims=True)
                m_new = jnp.maximum(m_old, row_max)
                p_sc[h, rows, :] = jnp.exp(s - jnp.concatenate([m_new, m_new], axis=1)).astype(BF16)
                acc_sc[h, rows, :] = jnp.exp(m_old - m_new) * acc_sc[h, rows, :]
                m_sc[h, rows, :] = m_new
            acc_sc[h] += jnp.dot(p_sc[h], vb_sc[h, pl.ds(start, MOBA_BLOCK), :], preferred_element_type=F32)

    def past_pair(i, carry):
        scores(2 * i + 1, 1)
        consume(2 * i, 0, None)
        scores(2 * i + 2, 0)
        consume(2 * i + 1, 1, None)
        return carry

    scores(0, 0)
    lax.fori_loop(0, ti, past_pair, 0)
    scores(2 * ti + 1, 1)
    consume(2 * ti, 0, 0)
    consume(2 * ti + 1, 1, 1)
    a0 = acc_sc[0]
    a1 = acc_sc[1]
    o_ref[0] = jnp.where(lane < HEAD_DIM, a0 / pltpu.roll(a0, HEAD_DIM, 1), a1 / pltpu.roll(a1, HEAD_DIM, 1))


def _moba_prompt(qa, ka, va):
    b, l, hw = qa.shape
    tq = 2 * MOBA_BLOCK
    assert l % tq == 0
    pairs = hw // LANES
    return pl.pallas_call(
        _moba_prompt_kernel,
        grid=(b, pairs, l // tq),
        in_specs=[
            pl.BlockSpec((1, tq, LANES), lambda i, p, t: (i, t, p)),
            pl.BlockSpec((1, l, LANES), lambda i, p, t: (i, 0, p)),
            pl.BlockSpec((1, l, LANES), lambda i, p, t: (i, 0, p)),
        ],
        out_specs=pl.BlockSpec((1, tq, LANES), lambda i, p, t: (i, t, p)),
        out_shape=jax.ShapeDtypeStruct((b, l, hw), F32),
        scratch_shapes=[
            pltpu.VMEM((l, 2 * LANES), BF16),
            pltpu.VMEM((2, l, LANES), BF16),
            pltpu.VMEM((LANES, LANES), F32),
            pltpu.VMEM((2, tq, 2 * LANES), BF16),
            pltpu.VMEM((2, 2, tq, MOBA_BLOCK), F32),
            pltpu.VMEM((2, tq, MOBA_BLOCK), BF16),
            pltpu.VMEM((2, tq, LANES), F32),
            pltpu.VMEM((2, tq, LANES), F32),
        ],
        compiler_params=_params("arbitrary", "arbitrary", "arbitrary"),
        name="moba_prompt",
    )(qa, ka, va)


def _moba_sample_kernel(pt_ref, q_ref, kn_ref, vn_ref, *refs, blocks_per_step):
    del pt_ref
    npages = 2 * blocks_per_step
    k_refs, v_refs = refs[0:npages], refs[npages:2 * npages]
    o_ref, q_sc, m_sc, l_sc, g_sc, acc_sc = refs[2 * npages:]
    j = pl.program_id(1)
    last = pl.num_programs(1) - 1
    nb = acc_sc.shape[0]
    nq = q_ref.shape[1]
    nh = HEADS_PER_GROUP
    nr = nh * nq
    hw = HEAD_WIDTH
    keys = 2 * k_refs[0].shape[3]
    row_w = lax.broadcasted_iota(jnp.int32, (nr, hw), 0)
    lane_w = lax.broadcasted_iota(jnp.int32, (nr, hw), 1)
    own_head = (row_w // nq) == (lane_w // HEAD_DIM)
    lane = lax.broadcasted_iota(jnp.int32, (nr, LANES), 1)

    def to_queries(t):
        return jnp.sum(jnp.where(own_head, t, 0.0).reshape(nh, nq, hw), axis=0)

    @pl.when(j == 0)
    def _():
        qbd = jnp.where(own_head, jnp.concatenate([q_ref[0]] * nh, axis=0), 0.0)
        hi = qbd.astype(BF16)
        q_sc[0:nr] = hi
        q_sc[nr:2 * nr] = (qbd - hi.astype(F32)).astype(BF16)
        m_sc[...] = jnp.zeros_like(m_sc)
        l_sc[...] = jnp.zeros_like(l_sc)
        g_sc[...] = jnp.zeros_like(g_sc)

    def scores(kt):
        s2 = jnp.dot(q_sc[...], kt, preferred_element_type=F32)
        return s2[0:nr] + s2[nr:2 * nr]

    @pl.when(j < last)
    def _():
        m_all, l_all, g_all = m_sc[...], l_sc[...], g_sc[...]
        for i in range(blocks_per_step):
            n = j * blocks_per_step + i
            kt = jnp.concatenate([k_refs[2 * i][0, 0], k_refs[2 * i + 1][0, 0]], axis=1).astype(BF16)
            vt = jnp.concatenate([v_refs[2 * i][0, 0], v_refs[2 * i + 1][0, 0]], axis=1).astype(BF16)
            s = scores(kt)
            gate = jnp.sum(s, axis=1, keepdims=True) * (1.0 / keys)
            mx = jnp.max(s, axis=1, keepdims=True)
            p = jnp.exp(s - mx)
            den = jnp.sum(p, axis=1, keepdims=True)
            pv = lax.dot_general(p.astype(BF16), vt, NT_DIMS, preferred_element_type=F32)
            acc_sc[n] = to_queries(pv)
            m_all = jnp.where(lane == n, mx, m_all)
            l_all = jnp.where(lane == n, den, l_all)
            g_all = jnp.where(lane == n, gate, g_all)
        m_sc[...], l_sc[...], g_sc[...] = m_all, l_all, g_all

    @pl.when(j == last)
    def _():
        cur = jnp.where(lane < nb, g_sc[...], -jnp.inf)
        sel = jnp.zeros((nr, LANES), jnp.bool_)
        for _ in range(MOBA_TOPK):
            top = jnp.max(cur, axis=1, keepdims=True)
            first = jnp.min(jnp.where(cur == top, lane, LANES), axis=1, keepdims=True)
            pick = lane == first
            sel = sel | pick
            cur = jnp.where(pick, -jnp.inf, cur)
        pad = jnp.zeros((nq, hw), F32)
        kn = jnp.concatenate([kn_ref[0], pad], axis=0)
        vn = jnp.concatenate([vn_ref[0], pad], axis=0)
        s2 = lax.dot_general(q_sc[...], kn.astype(BF16), NT_DIMS, preferred_element_type=F32)
        s_own = s2[0:nr] + s2[nr:2 * nr]
        qpos = lax.broadcasted_iota(jnp.int32, (nr, 2 * nq), 0) % nq
        kpos = lax.broadcasted_iota(jnp.int32, (nr, 2 * nq), 1)
        s_own = jnp.where(kpos <= qpos, s_own, NEG)
        m_own = jnp.max(s_own, axis=1, keepdims=True)
        p_own = jnp.exp(s_own - m_own)
        l_own = jnp.sum(p_own, axis=1, keepdims=True)
        acc_own = to_queries(jnp.dot(p_own, vn, preferred_element_type=F32))
        m_all = m_sc[...]
        m_tot = jnp.maximum(jnp.max(jnp.where(sel, m_all, NEG), axis=1, keepdims=True), m_own)
        w = jnp.where(sel, jnp.exp(m_all - m_tot), 0.0)
        w_own = jnp.exp(m_own - m_tot)
        l_tot = jnp.sum(w * l_sc[...], axis=1, keepdims=True) + w_own * l_own

        def widen(col):
            return to_queries(jnp.broadcast_to(col, (nr, hw)))

        num = widen(w_own) * acc_own
        for n in range(nb):
            num = num + widen(w[:, n:n + 1]) * acc_sc[n]
        o_ref[0] = num / widen(l_tot)


def _moba_sample(qa, ka, va, cache_k, cache_v, page_table):
    b, nq, hw = qa.shape
    nh = HEADS_PER_GROUP
    n_pool, page = cache_k.shape[0], cache_k.shape[1]
    pages_per_blk = MOBA_BLOCK // page
    assert pages_per_blk == 2 and page == LANES
    nb = page_table.shape[1] // pages_per_blk
    assert MOBA_TOPK <= nb <= LANES
    ck = cache_k.transpose(0, 2, 3, 1).reshape(1, n_pool, hw, page)
    cv = cache_v.transpose(0, 2, 3, 1).reshape(1, n_pool, hw, page)
    tok = lambda i, j, pt: (i, 0, 0)
    per_step = next(g for g in (4, 2, 1) if nb % g == 0)
    nsteps = nb // per_step

    def page_spec(p):
        return pl.BlockSpec((1, 1, hw, page),
                            lambda i, j, pt: (0, pt[i, jnp.minimum(j, nsteps - 1) * 2 * per_step + p], 0, 0))

    pages = [page_spec(p) for p in range(2 * per_step)]
    grid_spec = pltpu.PrefetchScalarGridSpec(
        num_scalar_prefetch=1,
        grid=(b, nsteps + 1),
        in_specs=[
            pl.BlockSpec((1, nq, hw), tok),
            pl.BlockSpec((1, nq, hw), tok),
            pl.BlockSpec((1, nq, hw), tok),
        ] + pages + pages,
        out_specs=pl.BlockSpec((1, nq, hw), tok),
        scratch_shapes=[
            pltpu.VMEM((2 * nh * nq, hw), BF16),
            pltpu.VMEM((nh * nq, LANES), F32),
            pltpu.VMEM((nh * nq, LANES), F32),
            pltpu.VMEM((nh * nq, LANES), F32),
            pltpu.VMEM((nb, nq, hw), F32),
        ],
    )
    return pl.pallas_call(
        functools.partial(_moba_sample_kernel, blocks_per_step=per_step),
        grid_spec=grid_spec,
        out_shape=jax.ShapeDtypeStruct((b, nq, hw), F32),
        compiler_params=_params("arbitrary", "arbitrary"),
        name="moba_sample",
    )(page_table, qa, ka, va, *([ck] * (2 * per_step)), *([cv] * (2 * per_step)))


def _gla_kernel(q_ref, k_ref, v_ref, la_ref, gs_ref, gn_ref, s0_ref, o_ref, sout_ref, st_sc, *, chunk, mm_dtype):
    c = chunk
    n = q_ref.shape[1] // c
    npair = q_ref.shape[2] // LANES
    hd = HEAD_DIM
    lane = lax.broadcasted_iota(jnp.int32, (c, LANES), 1)
    head0 = lane < hd
    r = lax.broadcasted_iota(jnp.int32, (c, c), 0)
    cc = lax.broadcasted_iota(jnp.int32, (c, c), 1)
    causal = r >= cc
    tri = jnp.where(causal, 1.0, 0.0)
    br = lax.broadcasted_iota(jnp.int32, (LANES, LANES), 0) // hd
    bc = lax.broadcasted_iota(jnp.int32, (LANES, LANES), 1) // hd
    same_head = br == bc
    head_mean = jnp.where(same_head, 1.0 / hd, 0.0)
    gn = gn_ref[...]

    zero = jnp.zeros((hd, hd), F32)
    for p in range(npair):
        st_sc[p] = jnp.concatenate([jnp.concatenate([s0_ref[0, 2 * p], zero], axis=1),
                                    jnp.concatenate([zero, s0_ref[0, 2 * p + 1]], axis=1)], axis=0)

    def split_dot(a, b_mat, stack_axis):
        hi = a.astype(BF16)
        lo = (a - hi.astype(F32)).astype(BF16)
        both = jnp.concatenate([hi, lo], axis=stack_axis)
        if stack_axis == 1:
            r = jnp.dot(b_mat, both, preferred_element_type=F32)
            return r[:, 0:LANES] + r[:, LANES:2 * LANES]
        r = jnp.dot(both, b_mat, preferred_element_type=F32)
        return r[0:c] + r[c:2 * c]

    def chunk(ci, p):
        rows = pl.ds(pl.multiple_of(ci * c, c), c)
        lanes = slice(p * LANES, (p + 1) * LANES)
        la = la_ref[0, rows, lanes]
        q = q_ref[0, rows, lanes]
        k = k_ref[0, rows, lanes]
        v = v_ref[0, rows, lanes]
        if mm_dtype == BF16:
            b = split_dot(la, tri.astype(BF16), 1)
        else:
            b = jnp.dot(tri, la, precision=HIGHEST, preferred_element_type=F32)
        st = st_sc[p]
        b_last = b[c - 1:c, :]
        b_mid = b[c // 2 - 1:c // 2, :]
        qt = q * jnp.exp(b - b_mid)
        kt = k * jnp.exp(b_mid - b)
        qts = jnp.concatenate([jnp.where(head0, qt, 0.0), jnp.where(head0, 0.0, qt)], axis=0)
        a = lax.dot_general(qts.astype(mm_dtype), kt.astype(mm_dtype), NT_DIMS, preferred_element_type=F32)
        a0 = jnp.where(causal, a[0:c], 0.0).astype(mm_dtype)
        a1 = jnp.where(causal, a[c:2 * c], 0.0).astype(mm_dtype)
        vm = v.astype(mm_dtype)
        o = jnp.where(head0, jnp.dot(a0, vm, preferred_element_type=F32),
                      jnp.dot(a1, vm, preferred_element_type=F32))
        qe = q * jnp.exp(b)
        o = o + lax.dot_general(qe.astype(mm_dtype), st.astype(mm_dtype), NT_DIMS, preferred_element_type=F32)
        kd = k * jnp.exp(b_last - b)
        upd = lax.dot_general(vm, kd.astype(mm_dtype), TN_DIMS, preferred_element_type=F32)
        st_sc[p] = st * jnp.exp(b_last) + jnp.where(same_head, upd, 0.0)
        if mm_dtype == BF16:
            ms = split_dot(o * o, head_mean.astype(BF16), 0)
        else:
            ms = jnp.dot(o * o, head_mean, precision=HIGHEST, preferred_element_type=F32)
        o_ref[0, rows, lanes] = o * lax.rsqrt(ms + EPS) * gn * gs_ref[0, rows, lanes]

    unroll = next(u for u in (8, 4, 2, 1) if n % u == 0)

    def body(i, carry):
        for u in range(unroll):
            for p in range(npair):
                chunk(i * unroll + u, p)
        return carry

    lax.fori_loop(0, n // unroll, body, 0)
    for p in range(npair):
        st = st_sc[p]
        sout_ref[0, 2 * p] = st[0:hd, 0:hd]
        sout_ref[0, 2 * p + 1] = st[hd:2 * hd, hd:2 * hd]


def _gla(qg, kg, vg, la, gs, gn_pair, s0_t, chunk, mm_dtype):
    b, l, hw = qg.shape
    pairs = hw // LANES
    npair = pairs if l <= GLA_CHUNK else 1
    seq = pl.BlockSpec((1, l, npair * LANES), lambda i, p: (i, 0, p))
    st_spec = pl.BlockSpec((1, 2 * npair, HEAD_DIM, HEAD_DIM), lambda i, p: (i, p, 0, 0))
    return pl.pallas_call(
        functools.partial(_gla_kernel, chunk=chunk, mm_dtype=mm_dtype),
        grid=(b, pairs // npair),
        in_specs=[seq, seq, seq, seq, seq, pl.BlockSpec((1, LANES), lambda i, p: (0, 0)), st_spec],
        out_specs=[seq, st_spec],
        out_shape=[jax.ShapeDtypeStruct((b, l, hw), F32),
                   jax.ShapeDtypeStruct((b, HEADS_PER_GROUP, HEAD_DIM, HEAD_DIM), F32)],
        scratch_shapes=[pltpu.VMEM((npair, LANES, LANES), F32)],
        compiler_params=_params("arbitrary", "arbitrary"),
        name="gla_chunked",
    )(qg, kg, vg, la, gs, gn_pair, s0_t)


def _mix_ffn_kernel(x_ref, oa_ref, og_ref, mod_ref, g_ref, wo_a_ref, wo_g_ref, wg_ref, wu_ref, wd_ref, o_ref,
                    x1_sc, h_sc, acc_sc):
    j = pl.program_id(2)
    bb, tl, d = x_ref.shape
    tm = bb * tl

    @pl.when(j == 0)
    def _():
        m = mod_ref[...]
        mix = (_mm(oa_ref[...].reshape(tm, -1), wo_a_ref[...])
               + _mm(og_ref[...].reshape(tm, -1), wo_g_ref[...]))
        x1 = x_ref[...] + m[:, 2:3, :] * mix.reshape(bb, tl, d)
        h = _rms(x1, g_ref[...]) * (1.0 + m[:, 4:5, :]) + m[:, 3:4, :]
        x1_sc[...] = x1.reshape(tm, d)
        h_sc[...] = h.reshape(tm, d).astype(h_sc.dtype)
        acc_sc[...] = jnp.zeros_like(acc_sc)

    h = h_sc[...]
    fc = wg_ref.shape[1]
    for c0 in range(0, fc, FFN_COL_CHUNK):
        cols = slice(c0, min(c0 + FFN_COL_CHUNK, fc))
        acc_sc[...] += _mm(_silu(_mm(h, wg_ref[:, cols])) * _mm(h, wu_ref[:, cols]), wd_ref[cols, :])

    @pl.when(j == pl.num_programs(2) - 1)
    def _():
        g2 = mod_ref[...][:, 5:6, :]
        o_ref[...] = x1_sc[...].reshape(bb, tl, d) + g2 * acc_sc[...].reshape(bb, tl, d)


def _mix_ffn(x, oa, og, mod, g, wo_a, wo_g, wg, wu, wd):
    b, l, d = x.shape
    hw = oa.shape[-1]
    f = wg.shape[1]
    bb, tl = _token_tiles(b, l, 512 if l >= 512 else 256)
    tm = bb * tl
    fc = f if wg.dtype == BF16 else 256
    once = dict(pipeline_mode=pl.Buffered(1)) if fc == f else {}
    tok = lambda i, t, j: (i, t, 0)
    const2 = lambda i, t, j: (0, 0)
    return pl.pallas_call(
        _mix_ffn_kernel,
        grid=(b // bb, l // tl, f // fc),
        in_specs=[
            pl.BlockSpec((bb, tl, d), tok),
            pl.BlockSpec((bb, tl, hw), tok),
            pl.BlockSpec((bb, tl, hw), tok),
            pl.BlockSpec((bb, 6, d), lambda i, t, j: (i, 0, 0)),
            pl.BlockSpec((1, d), const2),
            pl.BlockSpec(wo_a.shape, const2),
            pl.BlockSpec(wo_g.shape, const2),
            pl.BlockSpec((d, fc), lambda i, t, j: (0, j), **once),
            pl.BlockSpec((d, fc), lambda i, t, j: (0, j), **once),
            pl.BlockSpec((fc, d), lambda i, t, j: (j, 0), **once),
        ],
        out_specs=pl.BlockSpec((bb, tl, d), tok),
        out_shape=jax.ShapeDtypeStruct((b, l, d), F32),
        scratch_shapes=[pltpu.VMEM((tm, d), F32), pltpu.VMEM((tm, d), wg.dtype), pltpu.VMEM((tm, d), F32)],
        compiler_params=_params("arbitrary", "arbitrary", "arbitrary"),
        name="outproj_ffn",
    )(x, oa, og, mod, g, wo_a, wo_g, wg, wu, wd)


def _cm_in_kernel(x_ref, mod_ref, g_ref, w_ref, lng_ref, lnb_ref, u_ref, v_ref):
    bb, tl, d = x_ref.shape
    tm = bb * tl
    cw = u_ref.shape[-1]
    m = mod_ref[...]
    h = _rms(x_ref[...], g_ref[...]) * (1.0 + m[:, 1:2, :]) + m[:, 0:1, :]
    hb = h.reshape(tm, d).astype(w_ref.dtype)
    u_ref[...] = _gelu_tanh(_mm(hb, w_ref[:, 0:cw])).reshape(bb, tl, cw)
    zv = _gelu_tanh(_mm(hb, w_ref[:, cw:2 * cw]))
    mu = jnp.mean(zv, axis=-1, keepdims=True)
    zc = zv - mu
    var = jnp.mean(zc * zc, axis=-1, keepdims=True)
    v_ref[...] = (zc * lax.rsqrt(var + EPS) * lng_ref[...] + lnb_ref[...]).reshape(bb, tl, cw)


def _cm_in(x, mod, g, w, ln_g, ln_b):
    b, l, d = x.shape
    cw = w.shape[1] // 2
    bb, tl = _token_tiles(b, l, 512 if l >= 512 else 256)
    tok = lambda i, t: (i, t, 0)
    const2 = lambda i, t: (0, 0)
    out_spec = pl.BlockSpec((bb, tl, cw), tok)
    out_sds = jax.ShapeDtypeStruct((b, l, cw), F32)
    return pl.pallas_call(
        _cm_in_kernel,
        grid=(b // bb, l // tl),
        in_specs=[
            pl.BlockSpec((bb, tl, d), tok),
            pl.BlockSpec((bb, 6, d), lambda i, t: (i, 0, 0)),
            pl.BlockSpec((1, d), const2),
            pl.BlockSpec(w.shape, const2),
            pl.BlockSpec((1, cw), const2),
            pl.BlockSpec((1, cw), const2),
        ],
        out_specs=[out_spec, out_spec],
        out_shape=[out_sds, out_sds],
        compiler_params=_params("arbitrary", "arbitrary"),
        name="cm_in_gelu_ln",
    )(x, mod, g, w, ln_g, ln_b)


def _cm_out_kernel(x_ref, u_ref, v_ref, mod_ref, g_ref, ws_ref, bs_ref, wo_ref, wr_ref,
                   x3_ref, h_ref, gates_ref, rank_ref, cnt_ref, us_sc):
    bb, tl, d = x_ref.shape
    tm = bb * tl
    cw = u_ref.shape[-1]
    ngroups, cm, _ = ws_ref.shape
    gd = cw // ngroups
    u = u_ref[...].reshape(tm, cw)
    v = v_ref[...].reshape(tm, cw)
    bs = bs_ref[...]
    mm_dtype = wo_ref.dtype
    r = lax.broadcasted_iota(jnp.int32, (cm, cm), 0)
    c = lax.broadcasted_iota(jnp.int32, (cm, cm), 1)
    for g in range(ngroups):
        wsg = jnp.where(r >= c, ws_ref[g], 0.0).astype(mm_dtype)
        cols = slice(g * gd, (g + 1) * gd)
        for sb in range(tm // cm):
            rows = slice(sb * cm, (sb + 1) * cm)
            s = _mm(wsg, v[rows, cols].astype(mm_dtype)) + bs[rows, g:g + 1]
            us_sc[rows, cols] = (u[rows, cols] * s).astype(mm_dtype)
    y = _mm(us_sc[...], wo_ref[...])
    m = mod_ref[...]
    x3 = x_ref[...] + m[:, 2:3, :] * y.reshape(bb, tl, d)
    x3_ref[...] = x3
    h = (_rms(x3, g_ref[...]) * (1.0 + m[:, 4:5, :]) + m[:, 3:4, :]).reshape(tm, d)
    h_ref[...] = h.astype(BF16)
    wr = wr_ref[...]
    if mm_dtype == BF16:
        h_hi = h.astype(BF16)
        h_lo = (h - h_hi.astype(F32)).astype(BF16)
        wr_hi = wr.astype(BF16)
        wr_lo = (wr - wr_hi.astype(F32)).astype(BF16)
        both = jnp.dot(jnp.concatenate([h_hi, h_lo], axis=0), wr_hi, preferred_element_type=F32)
        logits = both[0:tm] + both[tm:2 * tm] + jnp.dot(h_hi, wr_lo, preferred_element_type=F32)
    else:
        logits = jnp.dot(h, wr, precision=HIGHEST, preferred_element_type=F32)
    ne = logits.shape[1]
    col = lax.broadcasted_iota(jnp.int32, (tm, ne), 1)
    m1 = jnp.max(logits, axis=1, keepdims=True)
    i1 = jnp.min(jnp.where(logits == m1, col, ne), axis=1, keepdims=True)
    rest = jnp.where(col == i1, -jnp.inf, logits)
    m2 = jnp.max(rest, axis=1, keepdims=True)
    i2 = jnp.min(jnp.where(rest == m2, col, ne), axis=1, keepdims=True)
    e2 = jnp.exp(m2 - m1)
    gates = jnp.where(col == i1, 1.0 / (1.0 + e2), 0.0) + jnp.where(col == i2, e2 / (1.0 + e2), 0.0)
    gates_ref[...] = gates
    routed = gates > 0.0
    tr = lax.broadcasted_iota(jnp.int32, (tm, tm), 0)
    tc = lax.broadcasted_iota(jnp.int32, (tm, tm), 1)
    upto = jnp.dot(jnp.where(tr >= tc, 1.0, 0.0).astype(BF16), jnp.where(routed, 1.0, 0.0).astype(BF16),
                   preferred_element_type=F32)
    rank_ref[...] = jnp.where(routed, upto - 1.0, -1.0)
    cnt_ref[0] = upto[tm - 1:tm, :]


def _cm_out(x, u, v, mod, g, ws_mix, bs_rows, wo, wr, cm):
    b, l, d = x.shape
    cw = u.shape[-1]
    ne = wr.shape[1]
    bb, tl = _token_tiles(b, l, 256)
    tm = bb * tl
    tok = lambda i, t: (i, t, 0)
    flat = lambda i, t: (i * (l // tl) + t, 0)
    const2 = lambda i, t: (0, 0)
    return pl.pallas_call(
        _cm_out_kernel,
        grid=(b // bb, l // tl),
        in_specs=[
            pl.BlockSpec((bb, tl, d), tok),
            pl.BlockSpec((bb, tl, cw), tok),
            pl.BlockSpec((bb, tl, cw), tok),
            pl.BlockSpec((bb, 6, d), lambda i, t: (i, 0, 0)),
            pl.BlockSpec((1, d), const2),
            pl.BlockSpec(ws_mix.shape, lambda i, t: (0, 0, 0)),
            pl.BlockSpec((tm, bs_rows.shape[1]), const2),
            pl.BlockSpec(wo.shape, const2),
            pl.BlockSpec(wr.shape, const2),
        ],
        out_specs=[pl.BlockSpec((bb, tl, d), tok), pl.BlockSpec((tm, d), flat), pl.BlockSpec((tm, ne), flat),
                   pl.BlockSpec((tm, ne), flat),
                   pl.BlockSpec((1, 1, ne), lambda i, t: (i * (l // tl) + t, 0, 0))],
        out_shape=[jax.ShapeDtypeStruct((b, l, d), F32), jax.ShapeDtypeStruct((b * l, d), BF16),
                   jax.ShapeDtypeStruct((b * l, ne), F32), jax.ShapeDtypeStruct((b * l, ne), F32),
                   jax.ShapeDtypeStruct((b * l // tm, 1, ne), F32)],
        scratch_shapes=[pltpu.VMEM((tm, cw), wo.dtype)],
        compiler_params=_params("arbitrary", "arbitrary"),
        name="cm_out_router",
    )(x, u, v, mod, g, ws_mix, bs_rows, wo, wr)


MOE_SUB = 256
MOE_ALIGN = 16
MOE_WIN_SMALL = 128
MOE_WIN_FULL = MOE_SUB + MOE_ALIGN
MOE_ROW_CHUNK = 512
MOE_FFN_ROWS = 640
MOE_BLOCK_SUBS = 10
MOE_HIDDEN_CHUNK = 512


def _moe_routing(rank, cnt, nb, nsub):
    ne = rank.shape[1]
    rank = rank.astype(jnp.int32).reshape(nb, nsub, MOE_SUB, ne)
    cnt = cnt.astype(jnp.int32).reshape(nb, nsub, ne)
    off = jnp.cumsum(cnt, axis=1) - cnt
    total = jnp.sum(cnt, axis=1)
    start = (off // MOE_ALIGN) * MOE_ALIGN
    shift = off - start
    win_row = jnp.where(rank >= 0, rank + shift[:, :, None, :], -1)
    win_row = win_row.transpose(0, 3, 1, 2).reshape(nb, ne, nsub * MOE_SUB)
    used = jnp.where(cnt > 0, shift + cnt, 0)
    scalars = jnp.concatenate([start.transpose(0, 2, 1).reshape(-1), used.transpose(0, 2, 1).reshape(-1),
                               total.reshape(-1)]).astype(jnp.int32)
    return win_row, scalars


def _moe_kernel(sc_ref, h_ref, gates_ref, row_ref, wg_ref, wu_ref, wd_ref, o_ref, x_sc, y_sc):
    b = pl.program_id(0)
    e = pl.program_id(1)
    j = pl.program_id(2)
    ne = pl.num_programs(1)
    tb = h_ref.shape[0]
    nsub = tb // MOE_SUB
    cap = x_sc.shape[0]
    n_be = pl.num_programs(0) * ne
    seg = (b * ne + e) * nsub
    total = sc_ref[2 * n_be * nsub + b * ne + e]

    def windows(fn):
        for s in range(nsub):
            start = pl.multiple_of(sc_ref[seg + s], MOE_ALIGN)
            used = sc_ref[n_be * nsub + seg + s]
            tok = slice(s * MOE_SUB, (s + 1) * MOE_SUB)
            win_row = row_ref[0, pl.ds(e, 1), tok]

            def run(rows, start=start, tok=tok, win_row=win_row):
                ridx = lax.broadcasted_iota(jnp.int32, (rows, MOE_SUB), 0)
                fn(pl.ds(start, rows), tok, jnp.where(ridx == win_row, 1.0, 0.0).astype(BF16))

            pl.when((used > 0) & (used <= MOE_WIN_SMALL))(functools.partial(run, MOE_WIN_SMALL))
            pl.when(used > MOE_WIN_SMALL)(functools.partial(run, MOE_WIN_FULL))

    @pl.when((e == 0) & (j == 0))
    def _():
        o_ref[...] = jnp.zeros_like(o_ref)

    @pl.when(j == 0)
    def _():
        def clear(c, carry):
            rows = pl.ds(pl.multiple_of(c * MOE_ROW_CHUNK, MOE_ROW_CHUNK), MOE_ROW_CHUNK)
            x_sc[rows, :] = jnp.zeros((MOE_ROW_CHUNK, x_sc.shape[1]), BF16)
            y_sc[rows, :] = jnp.zeros((MOE_ROW_CHUNK, y_sc.shape[1]), F32)
            return carry

        lax.fori_loop(0, jnp.minimum((total + MOE_WIN_FULL + MOE_ROW_CHUNK - 1) // MOE_ROW_CHUNK,
                                     cap // MOE_ROW_CHUNK), clear, 0)

        def dispatch(rows, tok, onehot):
            picked = jnp.dot(onehot, h_ref[tok, :], preferred_element_type=F32).astype(BF16)
            x_sc[rows, :] = x_sc[rows, :] + picked

        windows(dispatch)

    def ffn(r0, nrows):
        rows = pl.ds(pl.multiple_of(r0, MOE_WIN_SMALL), nrows)
        x = x_sc[rows, :]
        gate = jnp.dot(x, wg_ref[0], preferred_element_type=F32)
        up = jnp.dot(x, wu_ref[0], preferred_element_type=F32)
        act = (_silu(gate) * up).astype(BF16)
        y_sc[rows, :] += jnp.dot(act, wd_ref[0], preferred_element_type=F32)

    nfull = (jnp.maximum(total - 512, 0) + MOE_FFN_ROWS - 1) // MOE_FFN_ROWS
    rem = jnp.maximum(total - nfull * MOE_FFN_ROWS, 0)
    tail = nfull * MOE_FFN_ROWS

    def full_chunk(c, carry):
        ffn(c * MOE_FFN_ROWS, MOE_FFN_ROWS)
        return carry

    lax.fori_loop(0, nfull, full_chunk, 0)
    take_256 = (rem > 128) & (rem <= 384)
    pl.when(rem > 384)(lambda: ffn(tail, 512))
    pl.when(take_256)(lambda: ffn(tail, 256))
    pl.when((rem > 0) & (rem <= 128))(lambda: ffn(tail, 128))
    pl.when((rem > 256) & (rem <= 384))(lambda: ffn(tail + 256, 128))

    @pl.when(j == pl.num_programs(2) - 1)
    def _():
        gates = gates_ref[...]
        col = lax.broadcasted_iota(jnp.int32, gates.shape, 1)
        ge = jnp.sum(jnp.where(col == e, gates, 0.0), axis=1, keepdims=True)

        def combine(rows, tok, onehot):
            back = lax.dot_general(onehot, y_sc[rows, :].astype(BF16), TN_DIMS, preferred_element_type=F32)
            o_ref[tok, :] = (o_ref[tok, :].astype(F32) + ge[tok] * back).astype(o_ref.dtype)

        windows(combine)


def _moe(h, gates, rank, cnt, wg, wu, wd, tb):
    t, d = h.shape
    ne, _, f = wg.shape
    fc = MOE_HIDDEN_CHUNK
    assert t % tb == 0 and tb % MOE_SUB == 0 and f % fc == 0
    nb, nsub = t // tb, tb // MOE_SUB
    win_row, scalars = _moe_routing(rank, cnt, nb, nsub)
    cap = -(-(tb + MOE_WIN_FULL) // MOE_ROW_CHUNK) * MOE_ROW_CHUNK
    grid_spec = pltpu.PrefetchScalarGridSpec(
        num_scalar_prefetch=1,
        grid=(nb, ne, f // fc),
        in_specs=[
            pl.BlockSpec((tb, d), lambda i, e, j, sc: (i, 0)),
            pl.BlockSpec((tb, ne), lambda i, e, j, sc: (i, 0)),
            pl.BlockSpec((1, ne, tb), lambda i, e, j, sc: (i, 0, 0)),
            pl.BlockSpec((1, d, fc), lambda i, e, j, sc: (e, 0, j)),
            pl.BlockSpec((1, d, fc), lambda i, e, j, sc: (e, 0, j)),
            pl.BlockSpec((1, fc, d), lambda i, e, j, sc: (e, j, 0)),
        ],
        out_specs=pl.BlockSpec((tb, d), lambda i, e, j, sc: (i, 0)),
        scratch_shapes=[pltpu.VMEM((cap, d), BF16), pltpu.VMEM((cap, d), F32)],
    )
    return pl.pallas_call(
        _moe_kernel,
        grid_spec=grid_spec,
        out_shape=jax.ShapeDtypeStruct((t, d), BF16),
        compiler_params=_params("arbitrary", "arbitrary", "arbitrary"),
        name="moe_experts",
    )(scalars, h, gates, win_row, wg, wu, wd)


def _final_kernel(x_ref, moe_ref, mod_ref, g_ref, o_ref):
    bb, tl, d = x_ref.shape
    g2 = mod_ref[...][:, 5:6, :]
    x4 = x_ref[...] + g2 * moe_ref[...].reshape(bb, tl, d)
    o_ref[...] = _rms(x4, g_ref[...])


def _final(x, moe, row0, mod, g):
    b, l, d = x.shape
    bb, tl = _token_tiles(b, l, 512 if l >= 512 else 256)
    tm = bb * tl
    assert row0 % tm == 0
    tok = lambda i, t: (i, t, 0)
    return pl.pallas_call(
        _final_kernel,
        grid=(b // bb, l // tl),
        in_specs=[
            pl.BlockSpec((bb, tl, d), tok),
            pl.BlockSpec((tm, d), lambda i, t: (row0 // tm + i * (l // tl) + t, 0)),
            pl.BlockSpec((bb, 6, d), lambda i, t: (i, 0, 0)),
            pl.BlockSpec((1, d), lambda i, t: (0, 0)),
        ],
        out_specs=pl.BlockSpec((bb, tl, d), tok),
        out_shape=jax.ShapeDtypeStruct((b, l, d), F32),
        compiler_params=_params("arbitrary", "arbitrary"),
        name="final_norm",
    )(x, moe, mod, g)


def _layer0(x, mod, pos0, past, w):
    b, l, d = x.shape
    hw = HEAD_WIDTH
    outs = _in_proj(x, mod, w["norm_mix_g0"], w["w_in_main"], w["w_in_lr"], w["wa2_hi"], w["wa2_lo"], w["ba2"], pos0)
    qa, ka, va, qg, kg, vg, gs, la = outs
    if past is None:
        oa = _moba_prompt(qa, ka, va)
        s0_t = jnp.zeros((b, HEADS_PER_GROUP, HEAD_DIM, HEAD_DIM), F32)
    else:
        cache_k, cache_v, state, page_table = past
        oa = _moba_sample(qa, ka, va, cache_k, cache_v, page_table)
        s0_t = jnp.swapaxes(state, -1, -2)
    chunk = GLA_CHUNK if l % GLA_CHUNK == 0 else l
    mm_dtype = BF16 if chunk % 16 == 0 else F32
    og, s_t = _gla(qg, kg, vg, la, gs, w["gla_norm_pair"], s0_t, chunk, mm_dtype)
    x2 = _mix_ffn(x, oa, og, mod, w["norm_ffn_g0"], w["wo_a"], w["wo_g"], w["ffn_wg"], w["ffn_wu"], w["ffn_wd"])
    k_rows = ka.reshape(b, l, HEADS_PER_GROUP, HEAD_DIM)
    v_rows = va.reshape(b, l, HEADS_PER_GROUP, HEAD_DIM)
    return x2, k_rows, v_rows, jnp.swapaxes(s_t, -1, -2)


def _layer1_front(x, mod, w):
    b, l, d = x.shape
    u, v = _cm_in(x, mod, w["norm_mix_g1"], w["cm_w_in"], w["cm_ln_g"], w["cm_ln_b"])
    bb, tl = _token_tiles(b, l, 256)
    tm = bb * tl
    if l >= CM_CHUNK:
        cm = CM_CHUNK
        ws_mix = w["cm_w_s"]
        bs_rows = jnp.tile(w["cm_b_s"].T, (tm // CM_CHUNK, 1))
    else:
        cm = tm
        pos = jnp.arange(tm, dtype=jnp.int32)
        same_seq = (pos[:, None] // l == pos[None, :] // l).astype(F32)
        place = (pos[:, None] % l == jnp.arange(l, dtype=jnp.int32)[None, :]).astype(F32)
        ws_mix = jnp.einsum("ra,gab,cb->grc", place, w["cm_w_s"][:, :l, :l], place,
                            precision=HIGHEST) * same_seq[None]
        bs_rows = jnp.tile(w["cm_b_s"][:, :l].T, (tm // l, 1))
    assert tm == MOE_SUB
    x3, h, gates, rank, cnt = _cm_out(x, u, v, mod, w["norm_ffn_g1"], ws_mix, bs_rows, w["cm_w_out"],
                                      w["moe_w_router"], cm)
    start = ((l - 1) // CM_CHUNK) * CM_CHUNK
    return x3, (h, gates, rank, cnt), v[:, start:]


def kernel(x_prompt, x_sample, cache_k, cache_v, state_gla, page_table, c_prompt, c_sample, ada_w, ada_b, norm_mix_g, norm_ffn_g, ab_w_in, gla_w_a2, gla_b_a2, gla_norm_g, ab_w_out, ffn_w_gate, ffn_w_up, ffn_w_down, cm_w_in, cm_ln_g, cm_ln_b, cm_w_s, cm_b_s, cm_w_out, moe_w_router, moe_w_gate, moe_w_up, moe_w_down, final_norm_g):
    bp, lp, d = x_prompt.shape
    bs, ls, _ = x_sample.shape
    hw = HEAD_WIDTH
    n_main = 7 * hw
    past_len = page_table.shape[1] * cache_k.shape[2]

    wa2 = jnp.pad(gla_w_a2[0], ((0, LANES - GLA_LOWRANK), (0, 0)))
    wa2_hi = wa2.astype(BF16)
    shared = {
        "norm_mix_g0": norm_mix_g[0][None], "norm_mix_g1": norm_mix_g[1][None],
        "norm_ffn_g0": norm_ffn_g[0][None], "norm_ffn_g1": norm_ffn_g[1][None],
        "wa2_hi": wa2_hi, "wa2_lo": (wa2 - wa2_hi.astype(F32)).astype(BF16),
        "ba2": gla_b_a2[0][None],
        "gla_norm_pair": jnp.tile(gla_norm_g[0], LANES // HEAD_DIM)[None],
        "cm_ln_g": cm_ln_g[0][None], "cm_ln_b": cm_ln_b[0][None],
        "cm_w_s": cm_w_s[0], "cm_b_s": cm_b_s[0],
        "moe_w_router": moe_w_router[0],
    }
    mats = {
        "w_in_main": ab_w_in[0],
        "w_in_lr": jnp.pad(ab_w_in[0][:, n_main:], ((0, 0), (0, LANES - GLA_LOWRANK))),
        "wo_a": ab_w_out[0][:hw], "wo_g": ab_w_out[0][hw:],
        "ffn_wg": ffn_w_gate[0], "ffn_wu": ffn_w_up[0], "ffn_wd": ffn_w_down[0],
        "cm_w_in": cm_w_in[0], "cm_w_out": cm_w_out[0],
    }
    w = dict(shared, **{k: v.astype(BF16) for k, v in mats.items()})
    w_sample = dict(shared, **mats)

    rows = bp + bs
    rows_pad = -(-rows // 16) * 16
    c_all = jnp.pad(jnp.concatenate([c_prompt, c_sample], axis=0), ((0, rows_pad - rows), (0, 0)))
    mod_all = _modulation(c_all, ada_w, ada_b)
    mod_p = [mod_all[l, :bp].reshape(bp, 6, d) for l in range(2)]
    mod_s = [mod_all[l, bp:rows].reshape(bs, 6, d) for l in range(2)]

    x2_p, k_p, v_p, gla_p = _layer0(x_prompt, mod_p[0], 0, None, w)
    x2_s, k_s, v_s, gla_s = _layer0(x_sample, mod_s[0], past_len,
                                    (cache_k[0], cache_v[0], state_gla[0], page_table), w_sample)

    x3_p, route_p, cmv_p = _layer1_front(x2_p, mod_p[1], w)
    x3_s, route_s, cmv_s = _layer1_front(x2_s, mod_s[1], w_sample)

    t_all = bp * lp + bs * ls
    subs = -(-t_all // MOE_SUB)
    n_blocks = -(-subs // MOE_BLOCK_SUBS)
    tb = -(-subs // n_blocks) * MOE_SUB
    pad = n_blocks * tb - t_all
    ne = moe_w_gate.shape[1]
    pads = (jnp.zeros((pad, d), BF16), jnp.zeros((pad, ne), F32), jnp.full((pad, ne), -1.0, F32),
            jnp.zeros((pad // MOE_SUB, 1, ne), F32))
    h_all, gates_all, rank_all, cnt_all = (jnp.concatenate(parts, axis=0) for parts in zip(route_p, route_s, pads))
    moe = _moe(h_all, gates_all, rank_all, cnt_all, moe_w_gate[0].astype(BF16), moe_w_up[0].astype(BF16),
               moe_w_down[0].astype(BF16), tb)

    fg = final_norm_g[None]
    y_p = _final(x3_p, moe, 0, mod_p[1], fg)
    y_s = _final(x3_s, moe, bp * lp, mod_s[1], fg)

    return (y_p, y_s, k_p[None], v_p[None], k_s[None], v_s[None], gla_p[None], gla_s[None],
            cmv_p[None], cmv_s[None])
```

```python
import functools

import jax
import jax.numpy as jnp
from jax import lax
from jax.experimental import pallas as pl
from jax.experimental.pallas import tpu as pltpu

F32 = jnp.float32
BF16 = jnp.bfloat16
HIGHEST = lax.Precision.HIGHEST

HEAD_DIM = 64
HEADS_PER_GROUP = 8
HEAD_WIDTH = HEADS_PER_GROUP * HEAD_DIM
MOBA_BLOCK = 256
MOBA_TOPK = 3
ROPE_THETA = 10000.0
GLA_LOWRANK = 16
GLA_GATE_NORM = 16.0
GLA_CHUNK = 64
CM_CHUNK = 128
CM_GROUPS = 8
MOE_TOP_K = 2
EPS = 1e-6
NEG = -1e30

LANES = 128
FFN_COL_CHUNK = 1024
VMEM_LIMIT_BYTES = 56 * 1024 * 1024

NT_DIMS = (((1,), (1,)), ((), ()))
TN_DIMS = (((0,), (0,)), ((), ()))


def _params(*sem):
    return pltpu.CompilerParams(dimension_semantics=sem, vmem_limit_bytes=VMEM_LIMIT_BYTES)


def _rms(x, g):
    return x * lax.rsqrt(jnp.mean(x * x, axis=-1, keepdims=True) + EPS) * g


def _silu(x):
    return x * jax.nn.sigmoid(x)


def _split(x):
    hi = x.astype(BF16)
    return hi, (x - hi.astype(F32)).astype(BF16)


def _mm(a, w):
    if w.dtype == F32:
        m = a.shape[0]
        a_hi, a_lo = _split(a.astype(F32))
        w_hi, w_lo = _split(w)
        both = jnp.dot(jnp.concatenate([a_hi, a_lo], axis=0), w_hi, preferred_element_type=F32)
        return both[0:m] + both[m:2 * m] + jnp.dot(a_hi, w_lo, preferred_element_type=F32)
    return jnp.dot(a.astype(BF16), w, preferred_element_type=F32)


def _gelu_tanh(x):
    c = 0.7978845608028654
    return x * (0.5 + 0.5 * jnp.tanh(x * (c + (c * 0.044715) * (x * x))))


def _token_tiles(b, l, target):
    if l >= target:
        assert l % target == 0
        return 1, target
    bb = min(b, target // l)
    assert target % l == 0 and b % bb == 0
    return bb, l


def _mod_kernel(c_ref, w_ref, b_ref, o_ref):
    s = _silu(c_ref[...])
    r = s.shape[0]
    s_hi = s.astype(BF16)
    s_lo = (s - s_hi.astype(F32)).astype(BF16)
    w = w_ref[0]
    w_hi = w.astype(BF16)
    w_lo = (w - w_hi.astype(F32)).astype(BF16)
    both = jnp.dot(jnp.concatenate([s_hi, s_lo], axis=0), w_hi, preferred_element_type=F32)
    o_ref[0] = both[0:r] + both[r:2 * r] + jnp.dot(s_hi, w_lo, preferred_element_type=F32) + b_ref[0]


def _modulation(c_all, ada_w, ada_b):
    r, d = c_all.shape
    depth, _, n = ada_w.shape
    tn = 1536
    return pl.pallas_call(
        _mod_kernel,
        grid=(depth, n // tn),
        in_specs=[
            pl.BlockSpec((r, d), lambda l, j: (0, 0)),
            pl.BlockSpec((1, d, tn), lambda l, j: (l, 0, j)),
            pl.BlockSpec((1, 1, tn), lambda l, j: (l, 0, j)),
        ],
        out_specs=pl.BlockSpec((1, r, tn), lambda l, j: (l, 0, j)),
        out_shape=jax.ShapeDtypeStruct((depth, r, n), F32),
        compiler_params=_params("arbitrary", "arbitrary"),
        name="adaln_modulation",
    )(c_all, ada_w, ada_b.reshape(depth, 1, n))


def _in_proj_kernel(x_ref, mod_ref, g_ref, w_ref, wlr_ref, wa2h_ref, wa2l_ref, ba2_ref, cos_ref, sin_ref,
                    qa_ref, ka_ref, va_ref, qg_ref, kg_ref, vg_ref, gs_ref, la_ref):
    bb, tl, d = x_ref.shape
    tm = bb * tl
    hw = HEAD_WIDTH
    m = mod_ref[...]
    h = _rms(x_ref[...], g_ref[...]) * (1.0 + m[:, 1:2, :]) + m[:, 0:1, :]
    hb = h.reshape(tm, d).astype(w_ref.dtype)

    def proj(c):
        return _mm(hb, w_ref[:, c * hw:(c + 1) * hw])

    cos = jnp.concatenate([cos_ref[...]] * (hw // LANES), axis=-1)[None]
    sin = jnp.concatenate([sin_ref[...]] * (hw // LANES), axis=-1)[None]
    lane = lax.broadcasted_iota(jnp.int32, (tm, hw), 1)
    first_half = (lane % HEAD_DIM) < (HEAD_DIM // 2)

    def rope(z):
        partner = jnp.where(first_half, pltpu.roll(z, hw - HEAD_DIM // 2, 1), pltpu.roll(z, HEAD_DIM // 2, 1))
        return z.reshape(bb, tl, hw) * cos + partner.reshape(bb, tl, hw) * sin

    scale = HEAD_DIM ** -0.5
    qa_ref[...] = rope(proj(0)) * scale
    ka_ref[...] = rope(proj(1))
    va_ref[...] = proj(2).reshape(bb, tl, hw)
    qg_ref[...] = (proj(3) * scale).reshape(bb, tl, hw)
    kg_ref[...] = proj(4).reshape(bb, tl, hw)
    vg_ref[...] = proj(5).reshape(bb, tl, hw)
    gs_ref[...] = _silu(proj(6)).reshape(bb, tl, hw)
    zlr = _mm(hb, wlr_ref[...])
    hi = zlr.astype(BF16)
    lo = (zlr - hi.astype(F32)).astype(BF16)
    pre = (jnp.dot(hi, wa2h_ref[...], preferred_element_type=F32)
           + jnp.dot(lo, wa2h_ref[...], preferred_element_type=F32)
           + jnp.dot(hi, wa2l_ref[...], preferred_element_type=F32)) + ba2_ref[...]
    log_sig = jnp.minimum(pre, 0.0) - jnp.log(1.0 + jnp.exp(-jnp.abs(pre)))
    la_ref[...] = (log_sig * (1.0 / GLA_GATE_NORM)).reshape(bb, tl, hw)


def _rope_tables(pos0, l):
    half = HEAD_DIM // 2
    inv = 1.0 / (ROPE_THETA ** (jnp.arange(half, dtype=F32) / half))
    ang = (pos0 + jnp.arange(l, dtype=jnp.int32)).astype(F32)[:, None] * inv[None, :]
    cos, sin = jnp.cos(ang), jnp.sin(ang)
    reps = LANES // HEAD_DIM
    cos_t = jnp.tile(jnp.concatenate([cos, cos], axis=-1), (1, reps))
    sin_t = jnp.tile(jnp.concatenate([-sin, sin], axis=-1), (1, reps))
    return cos_t, sin_t


def _in_proj(x, mod, g, w_main, w_lr, wa2_hi, wa2_lo, ba2, pos0):
    b, l, d = x.shape
    bb, tl = _token_tiles(b, l, 512 if l >= 512 else 256)
    hw = HEAD_WIDTH
    cos_t, sin_t = _rope_tables(pos0, l)
    tok = lambda i, j: (i, j, 0)
    const2 = lambda i, j: (0, 0)
    out_spec = pl.BlockSpec((bb, tl, hw), tok)
    out_sds = jax.ShapeDtypeStruct((b, l, hw), F32)
    return pl.pallas_call(
        _in_proj_kernel,
        grid=(b // bb, l // tl),
        in_specs=[
            pl.BlockSpec((bb, tl, d), tok),
            pl.BlockSpec((bb, 6, d), lambda i, j: (i, 0, 0)),
            pl.BlockSpec((1, d), const2),
            pl.BlockSpec((d, 7 * hw), const2),
            pl.BlockSpec(w_lr.shape, const2),
            pl.BlockSpec(wa2_hi.shape, const2),
            pl.BlockSpec(wa2_lo.shape, const2),
            pl.BlockSpec((1, hw), const2),
            pl.BlockSpec((tl, LANES), lambda i, j: (j, 0)),
            pl.BlockSpec((tl, LANES), lambda i, j: (j, 0)),
        ],
        out_specs=[out_spec] * 8,
        out_shape=[out_sds] * 8,
        compiler_params=_params("arbitrary", "arbitrary"),
        name="in_proj_rope",
    )(x, mod, g, w_main, w_lr, wa2_hi, wa2_lo, ba2, cos_t, sin_t)


def _moba_prompt_kernel(q_ref, k_ref, v_ref, o_ref, kb_sc, vb_sc, km_sc, qs_sc, s_sc, p_sc, m_sc, acc_sc):
    ti = pl.program_id(2)
    lk = k_ref.shape[1]
    nblk = lk // MOBA_BLOCK
    nblk8 = -(-nblk // 8) * 8
    tq = q_ref.shape[1]
    nsub = tq // MOBA_BLOCK

    @pl.when(ti == 0)
    def _():
        k = k_ref[0]
        v = v_ref[0]
        row = lax.broadcasted_iota(jnp.int32, (lk, LANES), 0)
        col = lax.broadcasted_iota(jnp.int32, (lk, LANES), 1)
        kb_sc[:, 0:LANES] = k.astype(BF16)
        kb_sc[:, LANES:2 * LANES] = jnp.where(row // MOBA_BLOCK == col, 1.0, 0.0).astype(BF16)
        vb_sc[0] = jnp.where(col < HEAD_DIM, v, 1.0).astype(BF16)
        vb_sc[1] = jnp.where(col >= HEAD_DIM, v, 1.0).astype(BF16)
        km_sc[...] = jnp.zeros_like(km_sc)
        km_sc[0:nblk, :] = jnp.mean(k.reshape(nblk, MOBA_BLOCK, LANES), axis=1)

    q = q_ref[0]
    lane = lax.broadcasted_iota(jnp.int32, (tq, LANES), 1)
    blk = lax.broadcasted_iota(jnp.int32, (nblk8, tq), 0)
    own = ti * nsub + lax.broadcasted_iota(jnp.int32, (nblk8, tq), 1) // MOBA_BLOCK
    past_blk = blk < own
    km = km_sc[...]
    for h in range(2):
        head_lanes = (lane < HEAD_DIM) if h == 0 else (lane >= HEAD_DIM)
        qh = jnp.where(head_lanes, q, 0.0)
        gate = lax.dot_general(km, qh, NT_DIMS, precision=HIGHEST, preferred_element_type=F32)[0:nblk8]
        gate = jnp.where(past_blk, gate, -jnp.inf)
        cnt = jnp.zeros((nblk8, tq), F32)
        for mrow in range(nblk):
            gm = gate[mrow:mrow + 1, :]
            beats = (gm > gate) | ((gm == gate) & (mrow < blk))
            cnt = cnt + jnp.where(beats, 1.0, 0.0)
        sel = (past_blk & (cnt < float(MOBA_TOPK))) | (blk == own)
        bias_t = jnp.concatenate(
            [jnp.where(sel, 0.0, NEG), jnp.full((LANES - nblk8, tq), NEG, F32)], axis=0)
        qs_sc[h, :, 0:LANES] = qh.astype(BF16)
        qs_sc[h, :, LANES:2 * LANES] = bias_t.T.astype(BF16)

    m_sc[...] = jnp.full_like(m_sc, NEG)
    acc_sc[...] = jnp.zeros_like(acc_sc)
    rc = 128
    rowi = lax.broadcasted_iota(jnp.int32, (rc, MOBA_BLOCK), 0)
    coli = lax.broadcasted_iota(jnp.int32, (rc, MOBA_BLOCK), 1)

    def scores(n, slot):
        start = pl.multiple_of(n * MOBA_BLOCK, MOBA_BLOCK)
        kblk = kb_sc[pl.ds(start, MOBA_BLOCK), :]
        for h in range(2):
            s_sc[slot, h] = lax.dot_general(qs_sc[h], kblk, NT_DIMS, preferred_element_type=F32)

    def consume(n, slot, own_sub):
        start = pl.multiple_of(n * MOBA_BLOCK, MOBA_BLOCK)
        for h in range(2):
            for r0 in range(0, tq, rc):
                rows = slice(r0, r0 + rc)
                s = s_sc[slot, h, rows, :]
                if own_sub is not None and r0 // MOBA_BLOCK == own_sub:
                    s = jnp.where(rowi + (r0 - own_sub * MOBA_BLOCK) < coli, NEG, s)
                m_old = m_sc[h, rows, :]
                row_max = jnp.max(jnp.maximum(s[:, 0:LANES], s[:, LANES:2 * LANES]), axis=1, keepdims=True)
                m_new = jnp.maximum(m_old, row_max)
                p_sc[h, rows, :] = jnp.exp(s - jnp.concatenate([m_new, m_new], axis=1)).astype(BF16)
                acc_sc[h, rows, :] = jnp.exp(m_old - m_new) * acc_sc[h, rows, :]
                m_sc[h, rows, :] = m_new
            acc_sc[h] += jnp.dot(p_sc[h], vb_sc[h, pl.ds(start, MOBA_BLOCK), :], preferred_element_type=F32)

    def past_pair(i, carry):
        scores(2 * i + 1, 1)
        consume(2 * i, 0, None)
        scores(2 * i + 2, 0)
        consume(2 * i + 1, 1, None)
        return carry

    scores(0, 0)
    lax.fori_loop(0, ti, past_pair, 0)
    scores(2 * ti + 1, 1)
    consume(2 * ti, 0, 0)
    consume(2 * ti + 1, 1, 1)
    a0 = acc_sc[0]
    a1 = acc_sc[1]
    o_ref[0] = jnp.where(lane < HEAD_DIM, a0 / pltpu.roll(a0, HEAD_DIM, 1), a1 / pltpu.roll(a1, HEAD_DIM, 1))


def _moba_prompt(qa, ka, va):
    b, l, hw = qa.shape
    tq = 2 * MOBA_BLOCK
    assert l % tq == 0
    pairs = hw // LANES
    return pl.pallas_call(
        _moba_prompt_kernel,
        grid=(b, pairs, l // tq),
        in_specs=[
            pl.BlockSpec((1, tq, LANES), lambda i, p, t: (i, t, p)),
            pl.BlockSpec((1, l, LANES), lambda i, p, t: (i, 0, p)),
            pl.BlockSpec((1, l, LANES), lambda i, p, t: (i, 0, p)),
        ],
        out_specs=pl.BlockSpec((1, tq, LANES), lambda i, p, t: (i, t, p)),
        out_shape=jax.ShapeDtypeStruct((b, l, hw), F32),
        scratch_shapes=[
            pltpu.VMEM((l, 2 * LANES), BF16),
            pltpu.VMEM((2, l, LANES), BF16),
            pltpu.VMEM((LANES, LANES), F32),
            pltpu.VMEM((2, tq, 2 * LANES), BF16),
            pltpu.VMEM((2, 2, tq, MOBA_BLOCK), F32),
            pltpu.VMEM((2, tq, MOBA_BLOCK), BF16),
            pltpu.VMEM((2, tq, LANES), F32),
            pltpu.VMEM((2, tq, LANES), F32),
        ],
        compiler_params=_params("arbitrary", "arbitrary", "arbitrary"),
        name="moba_prompt",
    )(qa, ka, va)


def _moba_sample_kernel(pt_ref, q_ref, kn_ref, vn_ref, *refs, blocks_per_step):
    del pt_ref
    npages = 2 * blocks_per_step
    k_refs, v_refs = refs[0:npages], refs[npages:2 * npages]
    o_ref, q_sc, m_sc, l_sc, g_sc, acc_sc = refs[2 * npages:]
    j = pl.program_id(1)
    last = pl.num_programs(1) - 1
    nb = acc_sc.shape[0]
    nq = q_ref.shape[1]
    nh = HEADS_PER_GROUP
    nr = nh * nq
    hw = HEAD_WIDTH
    keys = 2 * k_refs[0].shape[3]
    row_w = lax.broadcasted_iota(jnp.int32, (nr, hw), 0)
    lane_w = lax.broadcasted_iota(jnp.int32, (nr, hw), 1)
    own_head = (row_w // nq) == (lane_w // HEAD_DIM)
    lane = lax.broadcasted_iota(jnp.int32, (nr, LANES), 1)

    def to_queries(t):
        return jnp.sum(jnp.where(own_head, t, 0.0).reshape(nh, nq, hw), axis=0)

    @pl.when(j == 0)
    def _():
        qbd = jnp.where(own_head, jnp.concatenate([q_ref[0]] * nh, axis=0), 0.0)
        hi = qbd.astype(BF16)
        q_sc[0:nr] = hi
        q_sc[nr:2 * nr] = (qbd - hi.astype(F32)).astype(BF16)
        m_sc[...] = jnp.zeros_like(m_sc)
        l_sc[...] = jnp.zeros_like(l_sc)
        g_sc[...] = jnp.zeros_like(g_sc)

    def scores(kt):
        s2 = jnp.dot(q_sc[...], kt, preferred_element_type=F32)
        return s2[0:nr] + s2[nr:2 * nr]

    @pl.when(j < last)
    def _():
        m_all, l_all, g_all = m_sc[...], l_sc[...], g_sc[...]
        for i in range(blocks_per_step):
            n = j * blocks_per_step + i
            kt = jnp.concatenate([k_refs[2 * i][0, 0], k_refs[2 * i + 1][0, 0]], axis=1).astype(BF16)
            vt = jnp.concatenate([v_refs[2 * i][0, 0], v_refs[2 * i + 1][0, 0]], axis=1).astype(BF16)
            s = scores(kt)
            gate = jnp.sum(s, axis=1, keepdims=True) * (1.0 / keys)
            mx = jnp.max(s, axis=1, keepdims=True)
            p = jnp.exp(s - mx)
            den = jnp.sum(p, axis=1, keepdims=True)
            pv = lax.dot_general(p.astype(BF16), vt, NT_DIMS, preferred_element_type=F32)
            acc_sc[n] = to_queries(pv)
            m_all = jnp.where(lane == n, mx, m_all)
            l_all = jnp.where(lane == n, den, l_all)
            g_all = jnp.where(lane == n, gate, g_all)
        m_sc[...], l_sc[...], g_sc[...] = m_all, l_all, g_all

    @pl.when(j == last)
    def _():
        cur = jnp.where(lane < nb, g_sc[...], -jnp.inf)
        sel = jnp.zeros((nr, LANES), jnp.bool_)
        for _ in range(MOBA_TOPK):
            top = jnp.max(cur, axis=1, keepdims=True)
            first = jnp.min(jnp.where(cur == top, lane, LANES), axis=1, keepdims=True)
            pick = lane == first
            sel = sel | pick
            cur = jnp.where(pick, -jnp.inf, cur)
        pad = jnp.zeros((nq, hw), F32)
        kn = jnp.concatenate([kn_ref[0], pad], axis=0)
        vn = jnp.concatenate([vn_ref[0], pad], axis=0)
        s2 = lax.dot_general(q_sc[...], kn.astype(BF16), NT_DIMS, preferred_element_type=F32)
        s_own = s2[0:nr] + s2[nr:2 * nr]
        qpos = lax.broadcasted_iota(jnp.int32, (nr, 2 * nq), 0) % nq
        kpos = lax.broadcasted_iota(jnp.int32, (nr, 2 * nq), 1)
        s_own = jnp.where(kpos <= qpos, s_own, NEG)
        m_own = jnp.max(s_own, axis=1, keepdims=True)
        p_own = jnp.exp(s_own - m_own)
        l_own = jnp.sum(p_own, axis=1, keepdims=True)
        acc_own = to_queries(jnp.dot(p_own, vn, preferred_element_type=F32))
        m_all = m_sc[...]
        m_tot = jnp.maximum(jnp.max(jnp.where(sel, m_all, NEG), axis=1, keepdims=True), m_own)
        w = jnp.where(sel, jnp.exp(m_all - m_tot), 0.0)
        w_own = jnp.exp(m_own - m_tot)
        l_tot = jnp.sum(w * l_sc[...], axis=1, keepdims=True) + w_own * l_own

        def widen(col):
            return to_queries(jnp.broadcast_to(col, (nr, hw)))

        num = widen(w_own) * acc_own
        for n in range(nb):
            num = num + widen(w[:, n:n + 1]) * acc_sc[n]
        o_ref[0] = num / widen(l_tot)


def _moba_sample(qa, ka, va, cache_k, cache_v, page_table):
    b, nq, hw = qa.shape
    nh = HEADS_PER_GROUP
    n_pool, page = cache_k.shape[0], cache_k.shape[1]
    pages_per_blk = MOBA_BLOCK // page
    assert pages_per_blk == 2 and page == LANES
    nb = page_table.shape[1] // pages_per_blk
    assert MOBA_TOPK <= nb <= LANES
    ck = cache_k.transpose(0, 2, 3, 1).reshape(1, n_pool, hw, page)
    cv = cache_v.transpose(0, 2, 3, 1).reshape(1, n_pool, hw, page)
    tok = lambda i, j, pt: (i, 0, 0)
    per_step = next(g for g in (4, 2, 1) if nb % g == 0)
    nsteps = nb // per_step

    def page_spec(p):
        return pl.BlockSpec((1, 1, hw, page),
                            lambda i, j, pt: (0, pt[i, jnp.minimum(j, nsteps - 1) * 2 * per_step + p], 0, 0))

    pages = [page_spec(p) for p in range(2 * per_step)]
    grid_spec = pltpu.PrefetchScalarGridSpec(
        num_scalar_prefetch=1,
        grid=(b, nsteps + 1),
        in_specs=[
            pl.BlockSpec((1, nq, hw), tok),
            pl.BlockSpec((1, nq, hw), tok),
            pl.BlockSpec((1, nq, hw), tok),
        ] + pages + pages,
        out_specs=pl.BlockSpec((1, nq, hw), tok),
        scratch_shapes=[
            pltpu.VMEM((2 * nh * nq, hw), BF16),
            pltpu.VMEM((nh * nq, LANES), F32),
            pltpu.VMEM((nh * nq, LANES), F32),
            pltpu.VMEM((nh * nq, LANES), F32),
            pltpu.VMEM((nb, nq, hw), F32),
        ],
    )
    return pl.pallas_call(
        functools.partial(_moba_sample_kernel, blocks_per_step=per_step),
        grid_spec=grid_spec,
        out_shape=jax.ShapeDtypeStruct((b, nq, hw), F32),
        compiler_params=_params("arbitrary", "arbitrary"),
        name="moba_sample",
    )(page_table, qa, ka, va, *([ck] * (2 * per_step)), *([cv] * (2 * per_step)))


def _gla_kernel(q_ref, k_ref, v_ref, la_ref, gs_ref, gn_ref, s0_ref, o_ref, sout_ref, st_sc, *, chunk, mm_dtype):
    c = chunk
    n = q_ref.shape[1] // c
    npair = q_ref.shape[2] // LANES
    hd = HEAD_DIM
    lane = lax.broadcasted_iota(jnp.int32, (c, LANES), 1)
    head0 = lane < hd
    r = lax.broadcasted_iota(jnp.int32, (c, c), 0)
    cc = lax.broadcasted_iota(jnp.int32, (c, c), 1)
    causal = r >= cc
    tri = jnp.where(causal, 1.0, 0.0)
    br = lax.broadcasted_iota(jnp.int32, (LANES, LANES), 0) // hd
    bc = lax.broadcasted_iota(jnp.int32, (LANES, LANES), 1) // hd
    same_head = br == bc
    head_mean = jnp.where(same_head, 1.0 / hd, 0.0)
    gn = gn_ref[...]

    zero = jnp.zeros((hd, hd), F32)
    for p in range(npair):
        st_sc[p] = jnp.concatenate([jnp.concatenate([s0_ref[0, 2 * p], zero], axis=1),
                                    jnp.concatenate([zero, s0_ref[0, 2 * p + 1]], axis=1)], axis=0)

    def split_dot(a, b_mat, stack_axis):
        hi = a.astype(BF16)
        lo = (a - hi.astype(F32)).astype(BF16)
        both = jnp.concatenate([hi, lo], axis=stack_axis)
        if stack_axis == 1:
            r = jnp.dot(b_mat, both, preferred_element_type=F32)
            return r[:, 0:LANES] + r[:, LANES:2 * LANES]
        r = jnp.dot(both, b_mat, preferred_element_type=F32)
        return r[0:c] + r[c:2 * c]

    def chunk(ci, p):
        rows = pl.ds(pl.multiple_of(ci * c, c), c)
        lanes = slice(p * LANES, (p + 1) * LANES)
        la = la_ref[0, rows, lanes]
        q = q_ref[0, rows, lanes]
        k = k_ref[0, rows, lanes]
        v = v_ref[0, rows, lanes]
        if mm_dtype == BF16:
            b = split_dot(la, tri.astype(BF16), 1)
        else:
            b = jnp.dot(tri, la, precision=HIGHEST, preferred_element_type=F32)
        st = st_sc[p]
        b_last = b[c - 1:c, :]
        b_mid = b[c // 2 - 1:c // 2, :]
        qt = q * jnp.exp(b - b_mid)
        kt = k * jnp.exp(b_mid - b)
        qts = jnp.concatenate([jnp.where(head0, qt, 0.0), jnp.where(head0, 0.0, qt)], axis=0)
        a = lax.dot_general(qts.astype(mm_dtype), kt.astype(mm_dtype), NT_DIMS, preferred_element_type=F32)
        a0 = jnp.where(causal, a[0:c], 0.0).astype(mm_dtype)
        a1 = jnp.where(causal, a[c:2 * c], 0.0).astype(mm_dtype)
        vm = v.astype(mm_dtype)
        o = jnp.where(head0, jnp.dot(a0, vm, preferred_element_type=F32),
                      jnp.dot(a1, vm, preferred_element_type=F32))
        qe = q * jnp.exp(b)
        o = o + lax.dot_general(qe.astype(mm_dtype), st.astype(mm_dtype), NT_DIMS, preferred_element_type=F32)
        kd = k * jnp.exp(b_last - b)
        upd = lax.dot_general(vm, kd.astype(mm_dtype), TN_DIMS, preferred_element_type=F32)
        st_sc[p] = st * jnp.exp(b_last) + jnp.where(same_head, upd, 0.0)
        if mm_dtype == BF16:
            ms = split_dot(o * o, head_mean.astype(BF16), 0)
        else:
            ms = jnp.dot(o * o, head_mean, precision=HIGHEST, preferred_element_type=F32)
        o_ref[0, rows, lanes] = o * lax.rsqrt(ms + EPS) * gn * gs_ref[0, rows, lanes]

    unroll = next(u for u in (8, 4, 2, 1) if n % u == 0)

    def body(i, carry):
        for u in range(unroll):
            for p in range(npair):
                chunk(i * unroll + u, p)
        return carry

    lax.fori_loop(0, n // unroll, body, 0)
    for p in range(npair):
        st = st_sc[p]
        sout_ref[0, 2 * p] = st[0:hd, 0:hd]
        sout_ref[0, 2 * p + 1] = st[hd:2 * hd, hd:2 * hd]


def _gla(qg, kg, vg, la, gs, gn_pair, s0_t, chunk, mm_dtype):
    b, l, hw = qg.shape
    pairs = hw // LANES
    npair = pairs if l <= GLA_CHUNK else 1
    seq = pl.BlockSpec((1, l, npair * LANES), lambda i, p: (i, 0, p))
    st_spec = pl.BlockSpec((1, 2 * npair, HEAD_DIM, HEAD_DIM), lambda i, p: (i, p, 0, 0))
    return pl.pallas_call(
        functools.partial(_gla_kernel, chunk=chunk, mm_dtype=mm_dtype),
        grid=(b, pairs // npair),
        in_specs=[seq, seq, seq, seq, seq, pl.BlockSpec((1, LANES), lambda i, p: (0, 0)), st_spec],
        out_specs=[seq, st_spec],
        out_shape=[jax.ShapeDtypeStruct((b, l, hw), F32),
                   jax.ShapeDtypeStruct((b, HEADS_PER_GROUP, HEAD_DIM, HEAD_DIM), F32)],
        scratch_shapes=[pltpu.VMEM((npair, LANES, LANES), F32)],
        compiler_params=_params("arbitrary", "arbitrary"),
        name="gla_chunked",
    )(qg, kg, vg, la, gs, gn_pair, s0_t)


def _mix_ffn_kernel(x_ref, oa_ref, og_ref, mod_ref, g_ref, wo_a_ref, wo_g_ref, wg_ref, wu_ref, wd_ref, o_ref,
                    x1_sc, h_sc, acc_sc):
    j = pl.program_id(2)
    bb, tl, d = x_ref.shape
    tm = bb * tl

    @pl.when(j == 0)
    def _():
        m = mod_ref[...]
        mix = (_mm(oa_ref[...].reshape(tm, -1), wo_a_ref[...])
               + _mm(og_ref[...].reshape(tm, -1), wo_g_ref[...]))
        x1 = x_ref[...] + m[:, 2:3, :] * mix.reshape(bb, tl, d)
        h = _rms(x1, g_ref[...]) * (1.0 + m[:, 4:5, :]) + m[:, 3:4, :]
        x1_sc[...] = x1.reshape(tm, d)
        h_sc[...] = h.reshape(tm, d).astype(h_sc.dtype)
        acc_sc[...] = jnp.zeros_like(acc_sc)

    h = h_sc[...]
    fc = wg_ref.shape[1]
    for c0 in range(0, fc, FFN_COL_CHUNK):
        cols = slice(c0, min(c0 + FFN_COL_CHUNK, fc))
        acc_sc[...] += _mm(_silu(_mm(h, wg_ref[:, cols])) * _mm(h, wu_ref[:, cols]), wd_ref[cols, :])

    @pl.when(j == pl.num_programs(2) - 1)
    def _():
        g2 = mod_ref[...][:, 5:6, :]
        o_ref[...] = x1_sc[...].reshape(bb, tl, d) + g2 * acc_sc[...].reshape(bb, tl, d)


def _mix_ffn(x, oa, og, mod, g, wo_a, wo_g, wg, wu, wd):
    b, l, d = x.shape
    hw = oa.shape[-1]
    f = wg.shape[1]
    bb, tl = _token_tiles(b, l, 512 if l >= 512 else 256)
    tm = bb * tl
    fc = f if wg.dtype == BF16 else 256
    once = dict(pipeline_mode=pl.Buffered(1)) if fc == f else {}
    tok = lambda i, t, j: (i, t, 0)
    const2 = lambda i, t, j: (0, 0)
    return pl.pallas_call(
        _mix_ffn_kernel,
        grid=(b // bb, l // tl, f // fc),
        in_specs=[
            pl.BlockSpec((bb, tl, d), tok),
            pl.BlockSpec((bb, tl, hw), tok),
            pl.BlockSpec((bb, tl, hw), tok),
            pl.BlockSpec((bb, 6, d), lambda i, t, j: (i, 0, 0)),
            pl.BlockSpec((1, d), const2),
            pl.BlockSpec(wo_a.shape, const2),
            pl.BlockSpec(wo_g.shape, const2),
            pl.BlockSpec((d, fc), lambda i, t, j: (0, j), **once),
            pl.BlockSpec((d, fc), lambda i, t, j: (0, j), **once),
            pl.BlockSpec((fc, d), lambda i, t, j: (j, 0), **once),
        ],
        out_specs=pl.BlockSpec((bb, tl, d), tok),
        out_shape=jax.ShapeDtypeStruct((b, l, d), F32),
        scratch_shapes=[pltpu.VMEM((tm, d), F32), pltpu.VMEM((tm, d), wg.dtype), pltpu.VMEM((tm, d), F32)],
        compiler_params=_params("arbitrary", "arbitrary", "arbitrary"),
        name="outproj_ffn",
    )(x, oa, og, mod, g, wo_a, wo_g, wg, wu, wd)


def _cm_in_kernel(x_ref, mod_ref, g_ref, w_ref, lng_ref, lnb_ref, u_ref, v_ref):
    bb, tl, d = x_ref.shape
    tm = bb * tl
    cw = u_ref.shape[-1]
    m = mod_ref[...]
    h = _rms(x_ref[...], g_ref[...]) * (1.0 + m[:, 1:2, :]) + m[:, 0:1, :]
    hb = h.reshape(tm, d).astype(w_ref.dtype)
    u_ref[...] = _gelu_tanh(_mm(hb, w_ref[:, 0:cw])).reshape(bb, tl, cw)
    zv = _gelu_tanh(_mm(hb, w_ref[:, cw:2 * cw]))
    mu = jnp.mean(zv, axis=-1, keepdims=True)
    zc = zv - mu
    var = jnp.mean(zc * zc, axis=-1, keepdims=True)
    v_ref[...] = (zc * lax.rsqrt(var + EPS) * lng_ref[...] + lnb_ref[...]).reshape(bb, tl, cw)


def _cm_in(x, mod, g, w, ln_g, ln_b):
    b, l, d = x.shape
    cw = w.shape[1] // 2
    bb, tl = _token_tiles(b, l, 512 if l >= 512 else 256)
    tok = lambda i, t: (i, t, 0)
    const2 = lambda i, t: (0, 0)
    out_spec = pl.BlockSpec((bb, tl, cw), tok)
    out_sds = jax.ShapeDtypeStruct((b, l, cw), F32)
    return pl.pallas_call(
        _cm_in_kernel,
        grid=(b // bb, l // tl),
        in_specs=[
            pl.BlockSpec((bb, tl, d), tok),
            pl.BlockSpec((bb, 6, d), lambda i, t: (i, 0, 0)),
            pl.BlockSpec((1, d), const2),
            pl.BlockSpec(w.shape, const2),
            pl.BlockSpec((1, cw), const2),
            pl.BlockSpec((1, cw), const2),
        ],
        out_specs=[out_spec, out_spec],
        out_shape=[out_sds, out_sds],
        compiler_params=_params("arbitrary", "arbitrary"),
        name="cm_in_gelu_ln",
    )(x, mod, g, w, ln_g, ln_b)


def _cm_out_kernel(x_ref, u_ref, v_ref, mod_ref, g_ref, ws_ref, bs_ref, wo_ref, wr_ref,
                   x3_ref, h_ref, gates_ref, rank_ref, cnt_ref, us_sc):
    bb, tl, d = x_ref.shape
    tm = bb * tl
    cw = u_ref.shape[-1]
    ngroups, cm, _ = ws_ref.shape
    gd = cw // ngroups
    u = u_ref[...].reshape(tm, cw)
    v = v_ref[...].reshape(tm, cw)
    bs = bs_ref[...]
    mm_dtype = wo_ref.dtype
    r = lax.broadcasted_iota(jnp.int32, (cm, cm), 0)
    c = lax.broadcasted_iota(jnp.int32, (cm, cm), 1)
    for g in range(ngroups):
        wsg = jnp.where(r >= c, ws_ref[g], 0.0).astype(mm_dtype)
        cols = slice(g * gd, (g + 1) * gd)
        for sb in range(tm // cm):
            rows = slice(sb * cm, (sb + 1) * cm)
            s = _mm(wsg, v[rows, cols].astype(mm_dtype)) + bs[rows, g:g + 1]
            us_sc[rows, cols] = (u[rows, cols] * s).astype(mm_dtype)
    y = _mm(us_sc[...], wo_ref[...])
    m = mod_ref[...]
    x3 = x_ref[...] + m[:, 2:3, :] * y.reshape(bb, tl, d)
    x3_ref[...] = x3
    h = (_rms(x3, g_ref[...]) * (1.0 + m[:, 4:5, :]) + m[:, 3:4, :]).reshape(tm, d)
    h_ref[...] = h.astype(BF16)
    wr = wr_ref[...]
    if mm_dtype == BF16:
        h_hi = h.astype(BF16)
        h_lo = (h - h_hi.astype(F32)).astype(BF16)
        wr_hi = wr.astype(BF16)
        wr_lo = (wr - wr_hi.astype(F32)).astype(BF16)
        both = jnp.dot(jnp.concatenate([h_hi, h_lo], axis=0), wr_hi, preferred_element_type=F32)
        logits = both[0:tm] + both[tm:2 * tm] + jnp.dot(h_hi, wr_lo, preferred_element_type=F32)
    else:
        logits = jnp.dot(h, wr, precision=HIGHEST, preferred_element_type=F32)
    ne = logits.shape[1]
    col = lax.broadcasted_iota(jnp.int32, (tm, ne), 1)
    m1 = jnp.max(logits, axis=1, keepdims=True)
    i1 = jnp.min(jnp.where(logits == m1, col, ne), axis=1, keepdims=True)
    rest = jnp.where(col == i1, -jnp.inf, logits)
    m2 = jnp.max(rest, axis=1, keepdims=True)
    i2 = jnp.min(jnp.where(rest == m2, col, ne), axis=1, keepdims=True)
    e2 = jnp.exp(m2 - m1)
    gates = jnp.where(col == i1, 1.0 / (1.0 + e2), 0.0) + jnp.where(col == i2, e2 / (1.0 + e2), 0.0)
    gates_ref[...] = gates
    routed = gates > 0.0
    tr = lax.broadcasted_iota(jnp.int32, (tm, tm), 0)
    tc = lax.broadcasted_iota(jnp.int32, (tm, tm), 1)
    upto = jnp.dot(jnp.where(tr >= tc, 1.0, 0.0).astype(BF16), jnp.where(routed, 1.0, 0.0).astype(BF16),
                   preferred_element_type=F32)
    rank_ref[...] = jnp.where(routed, upto - 1.0, -1.0)
    cnt_ref[0] = upto[tm - 1:tm, :]


def _cm_out(x, u, v, mod, g, ws_mix, bs_rows, wo, wr, cm):
    b, l, d = x.shape
    cw = u.shape[-1]
    ne = wr.shape[1]
    bb, tl = _token_tiles(b, l, 256)
    tm = bb * tl
    tok = lambda i, t: (i, t, 0)
    flat = lambda i, t: (i * (l // tl) + t, 0)
    const2 = lambda i, t: (0, 0)
    return pl.pallas_call(
        _cm_out_kernel,
        grid=(b // bb, l // tl),
        in_specs=[
            pl.BlockSpec((bb, tl, d), tok),
            pl.BlockSpec((bb, tl, cw), tok),
            pl.BlockSpec((bb, tl, cw), tok),
            pl.BlockSpec((bb, 6, d), lambda i, t: (i, 0, 0)),
            pl.BlockSpec((1, d), const2),
            pl.BlockSpec(ws_mix.shape, lambda i, t: (0, 0, 0)),
            pl.BlockSpec((tm, bs_rows.shape[1]), const2),
            pl.BlockSpec(wo.shape, const2),
            pl.BlockSpec(wr.shape, const2),
        ],
        out_specs=[pl.BlockSpec((bb, tl, d), tok), pl.BlockSpec((tm, d), flat), pl.BlockSpec((tm, ne), flat),
                   pl.BlockSpec((tm, ne), flat),
                   pl.BlockSpec((1, 1, ne), lambda i, t: (i * (l // tl) + t, 0, 0))],
        out_shape=[jax.ShapeDtypeStruct((b, l, d), F32), jax.ShapeDtypeStruct((b * l, d), BF16),
                   jax.ShapeDtypeStruct((b * l, ne), F32), jax.ShapeDtypeStruct((b * l, ne), F32),
                   jax.ShapeDtypeStruct((b * l // tm, 1, ne), F32)],
        scratch_shapes=[pltpu.VMEM((tm, cw), wo.dtype)],
        compiler_params=_params("arbitrary", "arbitrary"),
        name="cm_out_router",
    )(x, u, v, mod, g, ws_mix, bs_rows, wo, wr)


MOE_SUB = 256
MOE_ALIGN = 16
MOE_WIN_SMALL = 128
MOE_WIN_FULL = MOE_SUB + MOE_ALIGN
MOE_ROW_CHUNK = 512
MOE_FFN_ROWS = 640
MOE_BLOCK_SUBS = 10
MOE_HIDDEN_CHUNK = 512


def _moe_routing(rank, cnt, nb, nsub):
    ne = rank.shape[1]
    rank = rank.astype(jnp.int32).reshape(nb, nsub, MOE_SUB, ne)
    cnt = cnt.astype(jnp.int32).reshape(nb, nsub, ne)
    off = jnp.cumsum(cnt, axis=1) - cnt
    total = jnp.sum(cnt, axis=1)
    start = (off // MOE_ALIGN) * MOE_ALIGN
    shift = off - start
    win_row = jnp.where(rank >= 0, rank + shift[:, :, None, :], -1)
    win_row = win_row.transpose(0, 3, 1, 2).reshape(nb, ne, nsub * MOE_SUB)
    used = jnp.where(cnt > 0, shift + cnt, 0)
    scalars = jnp.concatenate([start.transpose(0, 2, 1).reshape(-1), used.transpose(0, 2, 1).reshape(-1),
                               total.reshape(-1)]).astype(jnp.int32)
    return win_row, scalars


def _moe_kernel(sc_ref, h_ref, gates_ref, row_ref, wg_ref, wu_ref, wd_ref, o_ref, x_sc, y_sc):
    b = pl.program_id(0)
    e = pl.program_id(1)
    j = pl.program_id(2)
    ne = pl.num_programs(1)
    tb = h_ref.shape[0]
    nsub = tb // MOE_SUB
    cap = x_sc.shape[0]
    n_be = pl.num_programs(0) * ne
    seg = (b * ne + e) * nsub
    total = sc_ref[2 * n_be * nsub + b * ne + e]

    def windows(fn):
        for s in range(nsub):
            start = pl.multiple_of(sc_ref[seg + s], MOE_ALIGN)
            used = sc_ref[n_be * nsub + seg + s]
            tok = slice(s * MOE_SUB, (s + 1) * MOE_SUB)
            win_row = row_ref[0, pl.ds(e, 1), tok]

            def run(rows, start=start, tok=tok, win_row=win_row):
                ridx = lax.broadcasted_iota(jnp.int32, (rows, MOE_SUB), 0)
                fn(pl.ds(start, rows), tok, jnp.where(ridx == win_row, 1.0, 0.0).astype(BF16))

            pl.when((used > 0) & (used <= MOE_WIN_SMALL))(functools.partial(run, MOE_WIN_SMALL))
            pl.when(used > MOE_WIN_SMALL)(functools.partial(run, MOE_WIN_FULL))

    @pl.when((e == 0) & (j == 0))
    def _():
        o_ref[...] = jnp.zeros_like(o_ref)

    @pl.when(j == 0)
    def _():
        def clear(c, carry):
            rows = pl.ds(pl.multiple_of(c * MOE_ROW_CHUNK, MOE_ROW_CHUNK), MOE_ROW_CHUNK)
            x_sc[rows, :] = jnp.zeros((MOE_ROW_CHUNK, x_sc.shape[1]), BF16)
            y_sc[rows, :] = jnp.zeros((MOE_ROW_CHUNK, y_sc.shape[1]), F32)
            return carry

        lax.fori_loop(0, jnp.minimum((total + MOE_WIN_FULL + MOE_ROW_CHUNK - 1) // MOE_ROW_CHUNK,
                                     cap // MOE_ROW_CHUNK), clear, 0)

        def dispatch(rows, tok, onehot):
            picked = jnp.dot(onehot, h_ref[tok, :], preferred_element_type=F32).astype(BF16)
            x_sc[rows, :] = x_sc[rows, :] + picked

        windows(dispatch)

    def ffn(r0, nrows):
        rows = pl.ds(pl.multiple_of(r0, MOE_WIN_SMALL), nrows)
        x = x_sc[rows, :]
        gate = jnp.dot(x, wg_ref[0], preferred_element_type=F32)
        up = jnp.dot(x, wu_ref[0], preferred_element_type=F32)
        act = (_silu(gate) * up).astype(BF16)
        y_sc[rows, :] += jnp.dot(act, wd_ref[0], preferred_element_type=F32)

    nfull = (jnp.maximum(total - 512, 0) + MOE_FFN_ROWS - 1) // MOE_FFN_ROWS
    rem = jnp.maximum(total - nfull * MOE_FFN_ROWS, 0)
    tail = nfull * MOE_FFN_ROWS

    def full_chunk(c, carry):
        ffn(c * MOE_FFN_ROWS, MOE_FFN_ROWS)
        return carry

    lax.fori_loop(0, nfull, full_chunk, 0)
    take_256 = (rem > 128) & (rem <= 384)
    pl.when(rem > 384)(lambda: ffn(tail, 512))
    pl.when(take_256)(lambda: ffn(tail, 256))
    pl.when((rem > 0) & (rem <= 128))(lambda: ffn(tail, 128))
    pl.when((rem > 256) & (rem <= 384))(lambda: ffn(tail + 256, 128))

    @pl.when(j == pl.num_programs(2) - 1)
    def _():
        gates = gates_ref[...]
        col = lax.broadcasted_iota(jnp.int32, gates.shape, 1)
        ge = jnp.sum(jnp.where(col == e, gates, 0.0), axis=1, keepdims=True)

        def combine(rows, tok, onehot):
            back = lax.dot_general(onehot, y_sc[rows, :].astype(BF16), TN_DIMS, preferred_element_type=F32)
            o_ref[tok, :] = (o_ref[tok, :].astype(F32) + ge[tok] * back).astype(o_ref.dtype)

        windows(combine)


def _moe(h, gates, rank, cnt, wg, wu, wd, tb):
    t, d = h.shape
    ne, _, f = wg.shape
    fc = MOE_HIDDEN_CHUNK
    assert t % tb == 0 and tb % MOE_SUB == 0 and f % fc == 0
    nb, nsub = t // tb, tb // MOE_SUB
    win_row, scalars = _moe_routing(rank, cnt, nb, nsub)
    cap = -(-(tb + MOE_WIN_FULL) // MOE_ROW_CHUNK) * MOE_ROW_CHUNK
    grid_spec = pltpu.PrefetchScalarGridSpec(
        num_scalar_prefetch=1,
        grid=(nb, ne, f // fc),
        in_specs=[
            pl.BlockSpec((tb, d), lambda i, e, j, sc: (i, 0)),
            pl.BlockSpec((tb, ne), lambda i, e, j, sc: (i, 0)),
            pl.BlockSpec((1, ne, tb), lambda i, e, j, sc: (i, 0, 0)),
            pl.BlockSpec((1, d, fc), lambda i, e, j, sc: (e, 0, j)),
            pl.BlockSpec((1, d, fc), lambda i, e, j, sc: (e, 0, j)),
            pl.BlockSpec((1, fc, d), lambda i, e, j, sc: (e, j, 0)),
        ],
        out_specs=pl.BlockSpec((tb, d), lambda i, e, j, sc: (i, 0)),
        scratch_shapes=[pltpu.VMEM((cap, d), BF16), pltpu.VMEM((cap, d), F32)],
    )
    return pl.pallas_call(
        _moe_kernel,
        grid_spec=grid_spec,
        out_shape=jax.ShapeDtypeStruct((t, d), BF16),
        compiler_params=_params("arbitrary", "arbitrary", "arbitrary"),
        name="moe_experts",
    )(scalars, h, gates, win_row, wg, wu, wd)


def _final_kernel(x_ref, moe_ref, mod_ref, g_ref, o_ref):
    bb, tl, d = x_ref.shape
    g2 = mod_ref[...][:, 5:6, :]
    x4 = x_ref[...] + g2 * moe_ref[...].reshape(bb, tl, d)
    o_ref[...] = _rms(x4, g_ref[...])


def _final(x, moe, row0, mod, g):
    b, l, d = x.shape
    bb, tl = _token_tiles(b, l, 512 if l >= 512 else 256)
    tm = bb * tl
    assert row0 % tm == 0
    tok = lambda i, t: (i, t, 0)
    return pl.pallas_call(
        _final_kernel,
        grid=(b // bb, l // tl),
        in_specs=[
            pl.BlockSpec((bb, tl, d), tok),
            pl.BlockSpec((tm, d), lambda i, t: (row0 // tm + i * (l // tl) + t, 0)),
            pl.BlockSpec((bb, 6, d), lambda i, t: (i, 0, 0)),
            pl.BlockSpec((1, d), lambda i, t: (0, 0)),
        ],
        out_specs=pl.BlockSpec((bb, tl, d), tok),
        out_shape=jax.ShapeDtypeStruct((b, l, d), F32),
        compiler_params=_params("arbitrary", "arbitrary"),
        name="final_norm",
    )(x, moe, mod, g)


def _layer0(x, mod, pos0, past, w):
    b, l, d = x.shape
    hw = HEAD_WIDTH
    outs = _in_proj(x, mod, w["norm_mix_g0"], w["w_in_main"], w["w_in_lr"], w["wa2_hi"], w["wa2_lo"], w["ba2"], pos0)
    qa, ka, va, qg, kg, vg, gs, la = outs
    if past is None:
        oa = _moba_prompt(qa, ka, va)
        s0_t = jnp.zeros((b, HEADS_PER_GROUP, HEAD_DIM, HEAD_DIM), F32)
    else:
        cache_k, cache_v, state, page_table = past
        oa = _moba_sample(qa, ka, va, cache_k, cache_v, page_table)
        s0_t = jnp.swapaxes(state, -1, -2)
    chunk = GLA_CHUNK if l % GLA_CHUNK == 0 else l
    mm_dtype = BF16 if chunk % 16 == 0 else F32
    og, s_t = _gla(qg, kg, vg, la, gs, w["gla_norm_pair"], s0_t, chunk, mm_dtype)
    x2 = _mix_ffn(x, oa, og, mod, w["norm_ffn_g0"], w["wo_a"], w["wo_g"], w["ffn_wg"], w["ffn_wu"], w["ffn_wd"])
    k_rows = ka.reshape(b, l, HEADS_PER_GROUP, HEAD_DIM)
    v_rows = va.reshape(b, l, HEADS_PER_GROUP, HEAD_DIM)
    return x2, k_rows, v_rows, jnp.swapaxes(s_t, -1, -2)


def _layer1_front(x, mod, w):
    b, l, d = x.shape
    u, v = _cm_in(x, mod, w["norm_mix_g1"], w["cm_w_in"], w["cm_ln_g"], w["cm_ln_b"])
    bb, tl = _token_tiles(b, l, 256)
    tm = bb * tl
    c = min(l, CM_CHUNK)
    cm = tm
    pos = jnp.arange(tm, dtype=jnp.int32)
    same_chunk = (pos[:, None] // c == pos[None, :] // c).astype(F32)
    place = (pos[:, None] % c == jnp.arange(c, dtype=jnp.int32)[None, :]).astype(F32)
    ws_mix = jnp.einsum("ra,gab,cb->grc", place, w["cm_w_s"][:, :c, :c], place,
                        precision=HIGHEST) * same_chunk[None]
    bs_rows = jnp.tile(w["cm_b_s"][:, :c].T, (tm // c, 1))
    assert tm == MOE_SUB
    x3, h, gates, rank, cnt = _cm_out(x, u, v, mod, w["norm_ffn_g1"], ws_mix, bs_rows, w["cm_w_out"],
                                      w["moe_w_router"], cm)
    start = ((l - 1) // CM_CHUNK) * CM_CHUNK
    return x3, (h, gates, rank, cnt), v[:, start:]


def kernel(x_prompt, x_sample, cache_k, cache_v, state_gla, page_table, c_prompt, c_sample, ada_w, ada_b, norm_mix_g, norm_ffn_g, ab_w_in, gla_w_a2, gla_b_a2, gla_norm_g, ab_w_out, ffn_w_gate, ffn_w_up, ffn_w_down, cm_w_in, cm_ln_g, cm_ln_b, cm_w_s, cm_b_s, cm_w_out, moe_w_router, moe_w_gate, moe_w_up, moe_w_down, final_norm_g):
    bp, lp, d = x_prompt.shape
    bs, ls, _ = x_sample.shape
    hw = HEAD_WIDTH
    n_main = 7 * hw
    past_len = page_table.shape[1] * cache_k.shape[2]

    wa2 = jnp.pad(gla_w_a2[0], ((0, LANES - GLA_LOWRANK), (0, 0)))
    wa2_hi = wa2.astype(BF16)
    shared = {
        "norm_mix_g0": norm_mix_g[0][None], "norm_mix_g1": norm_mix_g[1][None],
        "norm_ffn_g0": norm_ffn_g[0][None], "norm_ffn_g1": norm_ffn_g[1][None],
        "wa2_hi": wa2_hi, "wa2_lo": (wa2 - wa2_hi.astype(F32)).astype(BF16),
        "ba2": gla_b_a2[0][None],
        "gla_norm_pair": jnp.tile(gla_norm_g[0], LANES // HEAD_DIM)[None],
        "cm_ln_g": cm_ln_g[0][None], "cm_ln_b": cm_ln_b[0][None],
        "cm_w_s": cm_w_s[0], "cm_b_s": cm_b_s[0],
        "moe_w_router": moe_w_router[0],
    }
    mats = {
        "w_in_main": ab_w_in[0],
        "w_in_lr": jnp.pad(ab_w_in[0][:, n_main:], ((0, 0), (0, LANES - GLA_LOWRANK))),
        "wo_a": ab_w_out[0][:hw], "wo_g": ab_w_out[0][hw:],
        "ffn_wg": ffn_w_gate[0], "ffn_wu": ffn_w_up[0], "ffn_wd": ffn_w_down[0],
        "cm_w_in": cm_w_in[0], "cm_w_out": cm_w_out[0],
    }
    w = dict(shared, **{k: v.astype(BF16) for k, v in mats.items()})
    w_sample = dict(shared, **mats)

    rows = bp + bs
    rows_pad = -(-rows // 16) * 16
    c_all = jnp.pad(jnp.concatenate([c_prompt, c_sample], axis=0), ((0, rows_pad - rows), (0, 0)))
    mod_all = _modulation(c_all, ada_w, ada_b)
    mod_p = [mod_all[l, :bp].reshape(bp, 6, d) for l in range(2)]
    mod_s = [mod_all[l, bp:rows].reshape(bs, 6, d) for l in range(2)]

    x2_p, k_p, v_p, gla_p = _layer0(x_prompt, mod_p[0], 0, None, w)
    x2_s, k_s, v_s, gla_s = _layer0(x_sample, mod_s[0], past_len,
                                    (cache_k[0], cache_v[0], state_gla[0], page_table), w_sample)

    x3_p, route_p, cmv_p = _layer1_front(x2_p, mod_p[1], w)
    x3_s, route_s, cmv_s = _layer1_front(x2_s, mod_s[1], w_sample)

    t_all = bp * lp + bs * ls
    subs = -(-t_all // MOE_SUB)
    n_blocks = -(-subs // MOE_BLOCK_SUBS)
    tb = -(-subs // n_blocks) * MOE_SUB
    pad = n_blocks * tb - t_all
    ne = moe_w_gate.shape[1]
    pads = (jnp.zeros((pad, d), BF16), jnp.zeros((pad, ne), F32), jnp.full((pad, ne), -1.0, F32),
            jnp.zeros((pad // MOE_SUB, 1, ne), F32))
    h_all, gates_all, rank_all, cnt_all = (jnp.concatenate(parts, axis=0) for parts in zip(route_p, route_s, pads))
    moe = _moe(h_all, gates_all, rank_all, cnt_all, moe_w_gate[0].astype(BF16), moe_w_up[0].astype(BF16),
               moe_w_down[0].astype(BF16), tb)

    fg = final_norm_g[None]
    y_p = _final(x3_p, moe, 0, mod_p[1], fg)
    y_s = _final(x3_s, moe, bp * lp, mod_s[1], fg)

    return (y_p, y_s, k_p[None], v_p[None], k_s[None], v_s[None], gla_p[None], gla_s[None],
            cmv_p[None], cmv_s[None])
```
